```python
import jax
import jax.numpy as jnp
from jax import lax
import numpy as np

D_MODEL = 1024
BATCH = 2
SEQ = 16384
DEPTH = 1

GRID_W = 64
CTX_LEN = 256
CHUNK = 64
GLA_HEADS = 4
GLA_DK = 64
GLA_DV = 128
GLA_GATE_RANK = 16
GLA_GATE_TAU = 16.0
RET_HEADS = 4
RET_DK = 128
RET_DV = 128
RET_GAMMA_EXP0 = 5
ROPE_BASE = 10000.0
N_EXPERTS = 16
EC_CAPACITY_FACTOR = 2
EXPERT_FF = 1408
N_ADA = 6
EPS = 1e-6

GLA_QK = GLA_HEADS * GLA_DK
GLA_V = GLA_HEADS * GLA_DV
RET_QK = RET_HEADS * RET_DK
RET_V = RET_HEADS * RET_DV
MIX_WIDTH = GLA_V + RET_V
IN_WIDTHS = (GLA_QK, GLA_QK, GLA_V, 2 * GLA_GATE_RANK, GLA_V, RET_QK, RET_QK, RET_V, RET_V)
IN_WIDTH = sum(IN_WIDTHS)
IN_SPLIT_POINTS = tuple(int(p) for p in np.cumsum(IN_WIDTHS)[:-1])

kernel_name = 'hybrid_gla_retention_ec_moe_dit_block'

F32 = jnp.float32


def _rmsnorm(x, w):
    xf = x.astype(F32)
    y = xf * lax.rsqrt(jnp.mean(xf * xf, axis=-1, keepdims=True) + EPS)
    return (y * w).astype(x.dtype)


def _head_rmsnorm(o, w):
    B, T, H, d = o.shape
    of = o.astype(F32)
    y = of * lax.rsqrt(jnp.mean(of * of, axis=-1, keepdims=True) + EPS)
    return y.reshape(B, T, H * d) * w


def _head_layernorm(o, w):
    B, T, H, d = o.shape
    of = o.astype(F32)
    mu = jnp.mean(of, axis=-1, keepdims=True)
    var = jnp.mean(jnp.square(of - mu), axis=-1, keepdims=True)
    return ((of - mu) * lax.rsqrt(var + EPS)).reshape(B, T, H * d) * w


def _flip(a):
    return jnp.flip(a, axis=1)


def _to_chunks(a):
    B, T, H, d = a.shape
    return a.reshape(B, T // CHUNK, CHUNK, H, d).transpose(1, 0, 3, 2, 4)


def _from_chunks(a):
    N, B, H, C, d = a.shape
    return a.transpose(1, 0, 3, 2, 4).reshape(B, N * C, H, d)


def _grid_rope_angles(n_tok):
    rows = n_tok // GRID_W
    row = jnp.broadcast_to(jnp.arange(rows)[:, None], (rows, GRID_W)).reshape(-1).astype(F32)
    col = jnp.broadcast_to(jnp.arange(GRID_W)[None, :], (rows, GRID_W)).reshape(-1).astype(F32)
    n_freq = RET_DK // 4
    inv = ROPE_BASE ** (-jnp.arange(n_freq, dtype=F32) / n_freq)
    return jnp.concatenate([row[:, None] * inv, col[:, None] * inv], axis=-1)


def _apply_rope(a, ang):
    half = a.shape[-1] // 2
    cos = jnp.cos(ang)[None, :, None, :]
    sin = jnp.sin(ang)[None, :, None, :]
    a1, a2 = a[..., :half], a[..., half:]
    return jnp.concatenate([a1 * cos - a2 * sin, a1 * sin + a2 * cos], axis=-1)


def _gla_chunked(q, k, v, log_a, s0):
    mask = jnp.tril(jnp.ones((CHUNK, CHUNK), dtype=bool))[:, :, None]

    def step(s, xs):
        qc, kc, vc, gc = xs
        b = jnp.cumsum(gc, axis=2)
        b_end = b[:, :, -1:, :]
        decay = jnp.exp(jnp.where(mask, b[:, :, :, None, :] - b[:, :, None, :, :], -jnp.inf))
        scores = jnp.einsum('bhik,bhjk,bhijk->bhij', qc, kc, decay)
        o = jnp.einsum('bhij,bhjv->bhiv', scores, vc) + jnp.einsum('bhik,bhkv->bhiv', qc * jnp.exp(b), s)
        s = jnp.exp(b_end)[:, :, 0, :, None] * s + jnp.einsum('bhjk,bhjv->bhkv', kc * jnp.exp(b_end - b), vc)
        return s, o

    s_end, o = lax.scan(step, s0, (_to_chunks(q), _to_chunks(k), _to_chunks(v), _to_chunks(log_a)))
    return _from_chunks(o), s_end


def _retention_chunked(q, k, v, log_gamma, s0):
    pos = jnp.arange(CHUNK, dtype=F32)
    lg = log_gamma.astype(F32)[:, None, None]
    rel = pos[:, None] - pos[None, :]
    d_intra = jnp.where(rel >= 0, jnp.exp(lg * jnp.maximum(rel, 0.0)), 0.0)
    q_decay = jnp.exp(lg * (pos + 1.0)[:, None])
    k_decay = jnp.exp(lg * (CHUNK - 1.0 - pos)[:, None])
    c_decay = jnp.exp(lg * CHUNK)

    def step(s, xs):
        qc, kc, vc = xs
        scores = jnp.einsum('bhik,bhjk->bhij', qc, kc) * d_intra
        o = jnp.einsum('bhij,bhjv->bhiv', scores, vc) + jnp.einsum('bhik,bhkv->bhiv', qc * q_decay, s)
        s = c_decay * s + jnp.einsum('bhjk,bhjv->bhkv', kc * k_decay, vc)
        return s, o

    s_end, o = lax.scan(step, s0, (_to_chunks(q), _to_chunks(k), _to_chunks(v)))
    return _from_chunks(o), s_end


def _gla_final_state(k, v, log_a):
    b = jnp.cumsum(log_a, axis=1)
    w = jnp.exp(b[:, -1:] - b)
    return jnp.einsum('bthk,bthv->bhkv', k * w, v)


def _ret_final_state(k, v, log_gamma):
    T = k.shape[1]
    w = jnp.exp(log_gamma.astype(F32)[None, :] * jnp.arange(T - 1, -1, -1, dtype=F32)[:, None])
    return jnp.einsum('bthk,bthv->bhkv', k * w[None, :, :, None], v)


def _project(h, w_in, gla_gate_w, gla_gate_b, rope):
    B, T, _ = h.shape
    gq, gk, gv, gz, gg, rq, rk, rv, rg = jnp.split(h @ w_in, IN_SPLIT_POINTS, axis=-1)
    gq = gq.reshape(B, T, GLA_HEADS, GLA_DK) * (GLA_DK ** -0.5)
    gk = gk.reshape(B, T, GLA_HEADS, GLA_DK)
    gv = gv.reshape(B, T, GLA_HEADS, GLA_DV)
    pre = jnp.einsum('btdr,drk->btdk', gz.reshape(B, T, 2, GLA_GATE_RANK), gla_gate_w) + gla_gate_b
    log_a = (jax.nn.log_sigmoid(pre.astype(F32)) / GLA_GATE_TAU).reshape(B, T, 2, GLA_HEADS, GLA_DK)
    rq = rq.reshape(B, T, RET_HEADS, RET_DK)
    rk = rk.reshape(B, T, RET_HEADS, RET_DK) * (RET_DK ** -0.5)
    if rope is not None:
        rq = _apply_rope(rq, rope)
        rk = _apply_rope(rk, rope)
    rv = rv.reshape(B, T, RET_HEADS, RET_DV)
    return gq, gk, gv, log_a, gg, rq, rk, rv, rg


def _zero_states(B):
    g = jnp.zeros((B, GLA_HEADS, GLA_DK, GLA_DV), F32)
    r = jnp.zeros((B, RET_HEADS, RET_DK, RET_DV), F32)
    return (g, g, r, r)


def _token_mixer(h, w_in, gla_gate_w, gla_gate_b, ret_decay_logit, gla_norm_w, ret_norm_w, w_out, init_states, rope):
    gq, gk, gv, log_a, gg, rq, rk, rv, rg = _project(h, w_in, gla_gate_w, gla_gate_b, rope)
    log_gamma = jax.nn.log_sigmoid(ret_decay_logit.astype(F32))
    s_gf, s_gb, s_rf, s_rb = init_states
    o_gf, s_gf = _gla_chunked(gq, gk, gv, log_a[:, :, 0], s_gf)
    o_gb, s_gb = _gla_chunked(_flip(gq), _flip(gk), _flip(gv), _flip(log_a[:, :, 1]), s_gb)
    o_rf, s_rf = _retention_chunked(rq, rk, rv, log_gamma[0], s_rf)
    o_rb, s_rb = _retention_chunked(_flip(rq), _flip(rk), _flip(rv), log_gamma[1], s_rb)
    o_g = _head_rmsnorm(o_gf + _flip(o_gb), gla_norm_w) * jax.nn.silu(gg)
    o_r = _head_layernorm(o_rf + _flip(o_rb), ret_norm_w) * jax.nn.silu(rg)
    out = jnp.concatenate([o_g, o_r], axis=-1) @ w_out
    return out, (s_gf, s_gb, s_rf, s_rb)


def _context_states(hc, w_in, gla_gate_w, gla_gate_b, ret_decay_logit):
    _, gk, gv, log_a, _, _, rk, rv, _ = _project(hc, w_in, gla_gate_w, gla_gate_b, None)
    log_gamma = jax.nn.log_sigmoid(ret_decay_logit.astype(F32))
    s_gf = _gla_final_state(gk, gv, log_a[:, :, 0])
    s_gb = _gla_final_state(_flip(gk), _flip(gv), _flip(log_a[:, :, 1]))
    s_rf = _ret_final_state(rk, rv, log_gamma[0])
    s_rb = _ret_final_state(_flip(rk), _flip(rv), log_gamma[1])
    return (s_gf, s_gb, s_rf, s_rb)


def _expert_choice_ffn(h, w_router, w_gate, w_up, w_down):
    B, T, _ = h.shape
    cap = EC_CAPACITY_FACTOR * T // N_EXPERTS
    aff = jax.nn.softmax((h @ w_router).astype(F32), axis=-1)
    gate, idx = lax.top_k(jnp.swapaxes(aff, 1, 2), cap)
    bidx = jnp.arange(B)[:, None, None]
    xe = h[bidx, idx]
    a = jnp.einsum('becd,edf->becf', xe, w_gate)
    u = jnp.einsum('becd,edf->becf', xe, w_up)
    ye = jnp.einsum('becf,efd->becd', jax.nn.silu(a) * u, w_down) * gate[..., None]
    return jnp.zeros_like(h).at[bidx, idx].add(ye.astype(h.dtype))


def setup_inputs(seed: int = 0) -> dict:
    key = jax.random.key(seed)
    ks = jax.random.split(key, 24)
    n = lambda k, shape, s: jax.random.normal(k, shape, F32) * s
    logit0 = jnp.asarray(np.log(2.0 ** (RET_GAMMA_EXP0 + np.arange(RET_HEADS)) - 1.0), F32)
    return {
        'x': n(ks[0], (BATCH, SEQ, D_MODEL), 1.0),
        'c': n(ks[1], (BATCH, D_MODEL), 1.0),
        'ctx': n(ks[2], (BATCH, CTX_LEN, D_MODEL), 1.0),
        'c_ctx': n(ks[3], (D_MODEL,), 1.0),
        'w_ada': n(ks[4], (DEPTH, D_MODEL, N_ADA * D_MODEL), 0.02),
        'b_ada': n(ks[5], (DEPTH, N_ADA * D_MODEL), 0.02),
        'norm1_w': 1.0 + n(ks[6], (DEPTH, D_MODEL), 0.02),
        'w_in': n(ks[7], (DEPTH, D_MODEL, IN_WIDTH), D_MODEL ** -0.5),
        'gla_gate_w': n(ks[8], (DEPTH, 2, GLA_GATE_RANK, GLA_QK), GLA_GATE_RANK ** -0.5),
        'gla_gate_b': n(ks[9], (DEPTH, 2, GLA_QK), 0.1),
        'ret_decay_logit': logit0[None, None, :] + n(ks[10], (DEPTH, 2, RET_HEADS), 0.05),
        'gla_norm_w': 1.0 + n(ks[11], (DEPTH, GLA_V), 0.02),
        'ret_norm_w': 1.0 + n(ks[12], (DEPTH, RET_V), 0.02),
        'w_out': n(ks[13], (DEPTH, MIX_WIDTH, D_MODEL), MIX_WIDTH ** -0.5),
        'norm2_w': 1.0 + n(ks[14], (DEPTH, D_MODEL), 0.02),
        'w_router': n(ks[15], (DEPTH, D_MODEL, N_EXPERTS), D_MODEL ** -0.5),
        'w_exp_gate': n(ks[16], (DEPTH, N_EXPERTS, D_MODEL, EXPERT_FF), D_MODEL ** -0.5),
        'w_exp_up': n(ks[17], (DEPTH, N_EXPERTS, D_MODEL, EXPERT_FF), D_MODEL ** -0.5),
        'w_exp_down': n(ks[18], (DEPTH, N_EXPERTS, EXPERT_FF, D_MODEL), EXPERT_FF ** -0.5),
        'final_norm_w': 1.0 + n(ks[19], (D_MODEL,), 0.02),
    }


def reference(x, c, ctx, c_ctx, w_ada, b_ada, norm1_w, w_in, gla_gate_w, gla_gate_b, ret_decay_logit,
              gla_norm_w, ret_norm_w, w_out, norm2_w, w_router, w_exp_gate, w_exp_up, w_exp_down, final_norm_w):
    B, n_lat, D = x.shape
    rope = _grid_rope_angles(n_lat)
    for i in range(DEPTH):
        last = i == DEPTH - 1
        mod = (jax.nn.silu(c) @ w_ada[i] + b_ada[i]).reshape(B, N_ADA, D)[:, :, None, :]
        mod_c = (jax.nn.silu(c_ctx) @ w_ada[i] + b_ada[i]).reshape(N_ADA, D)
        hc = _rmsnorm(ctx, norm1_w[i]) * (1.0 + mod_c[1]) + mod_c[0]
        if last:
            ctx_states = _context_states(hc, w_in[i], gla_gate_w[i], gla_gate_b[i], ret_decay_logit[i])
        else:
            ctx_out, ctx_states = _token_mixer(hc, w_in[i], gla_gate_w[i], gla_gate_b[i], ret_decay_logit[i],
                                               gla_norm_w[i], ret_norm_w[i], w_out[i], _zero_states(B), None)
        h = _rmsnorm(x, norm1_w[i]) * (1.0 + mod[:, 1]) + mod[:, 0]
        x_out, _ = _token_mixer(h, w_in[i], gla_gate_w[i], gla_gate_b[i], ret_decay_logit[i],
                                gla_norm_w[i], ret_norm_w[i], w_out[i], ctx_states, rope)
        x = x + mod[:, 2] * x_out
        h2 = _rmsnorm(x, norm2_w[i]) * (1.0 + mod[:, 4]) + mod[:, 3]
        x = x + mod[:, 5] * _expert_choice_ffn(h2, w_router[i], w_exp_gate[i], w_exp_up[i], w_exp_down[i])
        if not last:
            ctx = ctx + mod_c[2] * ctx_out
            hc2 = _rmsnorm(ctx, norm2_w[i]) * (1.0 + mod_c[4]) + mod_c[3]
            ctx = ctx + mod_c[5] * _expert_choice_ffn(hc2, w_router[i], w_exp_gate[i], w_exp_up[i], w_exp_down[i])
    return _rmsnorm(x, final_norm_w)
```

```python
import functools

import numpy as np
import jax
import jax.numpy as jnp
from jax import lax
from jax.experimental import pallas as pl
from jax.experimental.pallas import tpu as pltpu

F32 = jnp.float32
BF16 = jnp.bfloat16
I32 = jnp.int32

GLA_HEADS = 4
GLA_DK = 64
GLA_DV = 128
GLA_RANK = 16
GLA_TAU = 16.0
RET_HEADS = 4
RET_DK = 128
RET_DV = 128
GRID_W = 64
ROPE_BASE = 10000.0
N_EXPERTS = 16
EC_CAPACITY_FACTOR = 2
N_ADA = 6
EPS = 1e-6

GLA_QK = GLA_HEADS * GLA_DK
GLA_V = GLA_HEADS * GLA_DV
RET_QK = RET_HEADS * RET_DK
RET_V = RET_HEADS * RET_DV
IN_WIDTHS = (GLA_QK, GLA_QK, GLA_V, 2 * GLA_RANK, GLA_V, RET_QK, RET_QK, RET_V, RET_V)

LANES = 128
TILE = 256
N_LEVELS = 8
WIN = 64
WROWS = WIN + 8
GZ_PAD = LANES
VMEM_LIMIT = 56 * 1024 * 1024

_NT = (((1,), (1,)), ((), ()))
_TN = (((0,), (0,)), ((), ()))


def _dot(a, b):
    return jnp.dot(a, b, preferred_element_type=F32)


def _dg(a, b, dims):
    return lax.dot_general(a, b, dims, preferred_element_type=F32)


def _split(a):
    hi = a.astype(BF16)
    lo = (a - hi.astype(F32)).astype(BF16)
    return hi, lo


def _logsig(x):
    return jnp.minimum(x, 0.0) - jnp.log(1.0 + jnp.exp(-jnp.abs(x)))


def _silu(x):
    return x / (1.0 + jnp.exp(-x))


def _rms(x, w):
    return x * lax.rsqrt(jnp.mean(x * x, axis=-1, keepdims=True) + EPS) * w


def _level_matrices(c, reverse):
    nl = int(np.log2(c))
    t = np.arange(c)[None, :]
    i = np.arange(c)[:, None]
    mats = []
    for l in range(nl):
        m = 1 << l
        blk = (i // (2 * m)) * (2 * m)
        mid = blk + m
        upper = (i - blk) >= m
        if not reverse:
            sel = np.where(upper, (t >= mid) & (t <= i), (t > i) & (t <= mid - 1))
        else:
            sel = np.where(upper, (t >= mid) & (t < i), (t >= i) & (t <= mid - 1))
        mats.append(sel)
    if not reverse:
        mats.append(t <= i)
        mats.append(t > i)
    else:
        mats.append(t >= i)
        mats.append(t < i)
    return np.concatenate(mats, axis=0).astype(np.float32)


def _level_index(c, reverse):
    i = np.arange(c)[:, None]
    j = np.arange(c)[None, :]
    x = i ^ j
    lvl = np.where(x > 0, np.floor(np.log2(np.maximum(x, 1))), -1).astype(np.int32)
    bad = (j > i) if not reverse else (j < i)
    return np.where(bad, 99, lvl).astype(np.int32)


def _project(xn, wall_ref, ghi_ref, glo_ref, gb_ref):
    proj = _dot(xn.astype(BF16), wall_ref[...])
    o = 0
    out = []
    for w in (GLA_QK, GLA_QK, GLA_V, GLA_V, RET_QK, RET_QK, RET_V, RET_V, GZ_PAD):
        out.append(proj[:, o:o + w])
        o += w
    gq, gk, gv, gg, rq, rk, rv, rg, gz = out
    z_hi, z_lo = _split(gz)
    pre = _dot(z_hi, ghi_ref[...]) + _dot(z_lo, ghi_ref[...]) + _dot(z_hi, glo_ref[...]) + gb_ref[...]
    log_a = _logsig(pre) * (1.0 / GLA_TAU)
    return gq * (GLA_DK ** -0.5), gk, gv, gg, rq, rk * (RET_DK ** -0.5), rv, rg, log_a


def _rope(a, cos, sin):
    outs = []
    for h in range(RET_HEADS):
        ah = a[:, h * RET_DK:(h + 1) * RET_DK]
        outs.append(ah * cos + pltpu.roll(ah, RET_DK // 2, 1) * sin)
    return jnp.concatenate(outs, axis=1)


def _stack_heads(a):
    head = lax.broadcasted_iota(I32, a.shape, 1) >> 6
    zero = jnp.zeros_like(a)
    return jnp.concatenate([jnp.where(head == h, a, zero) for h in range(GLA_HEADS)], axis=0)


def _gate_sums(g):
    hi, lo = _split(g)
    return jnp.concatenate([hi, lo], axis=1)


def _level_exp(mall_ref, g2, level, c):
    r = _dot(mall_ref[level * c:(level + 1) * c, :], g2)
    return jnp.exp(r[:, :GLA_QK] + r[:, GLA_QK:])


def _gla_state_update(k, vb, g2, e_k, s_prev, c):
    kk = (k * e_k).astype(BF16)
    kv = _dg(kk, vb, _TN)
    cs = _dg(g2, jnp.ones((c, LANES), BF16), _TN)
    e_col = jnp.exp(cs[:GLA_QK] + cs[GLA_QK:])
    new = []
    for h in range(GLA_HEADS):
        r = slice(h * GLA_DK, (h + 1) * GLA_DK)
        new.append(e_col[r] * s_prev[r] + kv[r, h * GLA_DV:(h + 1) * GLA_DV])
    return jnp.concatenate(new, axis=0)


def _gla_tile(q, k, vb, g, mall_ref, lvl_ref, s_ref, c):
    g2 = _gate_sums(g)
    lvl = lvl_ref[...]
    lvl4 = jnp.concatenate([lvl] * GLA_HEADS, axis=0)
    scores = jnp.zeros((GLA_HEADS * c, c), F32)
    for level in range(N_LEVELS):
        e = _level_exp(mall_ref, g2, level, c)
        p = _dg(_stack_heads((q * e).astype(BF16)), (k * e).astype(BF16), _NT)
        scores = jnp.where(lvl4 == level, p, scores)
    p = _dg(_stack_heads(q.astype(BF16)), k.astype(BF16), _NT)
    scores = jnp.where(lvl4 == -1, p, scores)
    e_b = _level_exp(mall_ref, g2, N_LEVELS, c)
    e_k = _level_exp(mall_ref, g2, N_LEVELS + 1, c)
    s_prev = s_ref[...]
    inter = _dot(_stack_heads((q * e_b).astype(BF16)), s_prev.astype(BF16))
    sb = scores.astype(BF16)
    outs = []
    for h in range(GLA_HEADS):
        r = slice(h * c, (h + 1) * c)
        outs.append(_dot(sb[r], vb[:, h * GLA_DV:(h + 1) * GLA_DV]) + inter[r])
    s_ref[...] = _gla_state_update(k, vb, g2, e_k, s_prev, c)
    return jnp.concatenate(outs, axis=1)


def _ret_decays(rlog_ref, c, reverse):
    lg = _logsig(rlog_ref[0])
    ii = lax.broadcasted_iota(I32, (c, c), 0)
    jj = lax.broadcasted_iota(I32, (c, c), 1)
    rel = ((jj - ii) if reverse else (ii - jj)).astype(F32)
    pos = lax.broadcasted_iota(I32, (c, RET_DK), 0).astype(F32)
    dmats, qd, kd, cd = [], [], [], []
    for h in range(RET_HEADS):
        lh = lg[h:h + 1, :]
        dmats.append(jnp.where(rel >= 0, jnp.exp(lh * jnp.maximum(rel, 0.0)), 0.0))
        l1 = lh[:, :RET_DK]
        qd.append(jnp.exp(l1 * ((c - pos) if reverse else (pos + 1.0))))
        kd.append(jnp.exp(l1 * (pos if reverse else (c - 1.0 - pos))))
        cd.append(jnp.exp(l1 * float(c)))
    return dmats, jnp.concatenate(qd, axis=1), jnp.concatenate(kd, axis=1), jnp.concatenate(cd, axis=1)


def _ret_tile(q, k, vb, dmat_ref, qdec, kdec, cdec, s_ref):
    outs = []
    for h in range(RET_HEADS):
        sl = slice(h * RET_DK, (h + 1) * RET_DK)
        vh = vb[:, h * RET_DV:(h + 1) * RET_DV]
        qh = q[:, sl]
        kh = k[:, sl]
        sc = _dg(qh.astype(BF16), kh.astype(BF16), _NT) * dmat_ref[h]
        s = s_ref[h]
        outs.append(_dot(sc.astype(BF16), vh) + _dot((qh * qdec[:, sl]).astype(BF16), s.astype(BF16)))
        s_ref[h] = cdec[:, sl] * s + _dg((kh * kdec[:, sl]).astype(BF16), vh, _TN)
    return jnp.concatenate(outs, axis=1)


def _ada_kernel(c_ref, w_ref, b_ref, o_ref):
    s_hi, s_lo = _split(_silu(c_ref[...]))
    w_hi, w_lo = _split(w_ref[...])
    o_ref[...] = _dot(s_hi, w_hi) + _dot(s_lo, w_hi) + _dot(s_hi, w_lo) + b_ref[...]


def _ada(cs, w, b):
    rows, d = cs.shape
    n = w.shape[1]
    tn = 1536
    return pl.pallas_call(
        _ada_kernel,
        out_shape=jax.ShapeDtypeStruct((rows, n), F32),
        grid=(n // tn,),
        in_specs=[pl.BlockSpec((rows, d), lambda i: (0, 0)),
                  pl.BlockSpec((d, tn), lambda i: (0, i)),
                  pl.BlockSpec((1, tn), lambda i: (0, i))],
        out_specs=pl.BlockSpec((rows, tn), lambda i: (0, i)),
        compiler_params=pltpu.CompilerParams(dimension_semantics=("arbitrary",), vmem_limit_bytes=VMEM_LIMIT),
        name="ada",
    )(cs, w, b)


def _ctx_kernel(ctx_ref, mod_ref, n1w_ref, wall_ref, ghi_ref, glo_ref, gb_ref, rlog_ref, mf_ref, mb_ref,
                sgf_ref, sgb_ref, srf_ref, srb_ref, *, c):
    mod = mod_ref[0]
    hc = _rms(ctx_ref[0], n1w_ref[...]) * (1.0 + mod[1:2]) + mod[0:1]
    _, gk, gv, _, _, rk, rv, _, log_a = _project(hc, wall_ref, ghi_ref, glo_ref, gb_ref)
    gvb = gv.astype(BF16)
    rvb = rv.astype(BF16)
    zero = jnp.zeros((GLA_QK, GLA_DV), F32)
    for d, (m_ref, out_g, out_r) in enumerate(((mf_ref, sgf_ref, srf_ref), (mb_ref, sgb_ref, srb_ref))):
        g2 = _gate_sums(log_a[:, d * GLA_QK:(d + 1) * GLA_QK])
        e_k = _level_exp(m_ref, g2, N_LEVELS + 1, c)
        out_g[0] = _gla_state_update(gk, gvb, g2, e_k, zero, c)
        _, _, kdec, _ = _ret_decays(rlog_ref.at[d:d + 1], c, reverse=bool(d))
        for h in range(RET_HEADS):
            sl = slice(h * RET_DK, (h + 1) * RET_DK)
            out_r[0, h] = _dg((rk[:, sl] * kdec[:, sl]).astype(BF16), rvb[:, h * RET_DV:(h + 1) * RET_DV], _TN)


def _ctx_states(ctx, modc, n1w, wall, ghi, glo, gb, rlog, mall_f, mall_b):
    bsz, c, d = ctx.shape
    const = lambda shape: pl.BlockSpec(shape, lambda b: (0,) * len(shape))
    return pl.pallas_call(
        functools.partial(_ctx_kernel, c=c),
        out_shape=(jax.ShapeDtypeStruct((bsz, GLA_QK, GLA_DV), F32),
                   jax.ShapeDtypeStruct((bsz, GLA_QK, GLA_DV), F32),
                   jax.ShapeDtypeStruct((bsz, RET_HEADS, RET_DK, RET_DV), F32),
                   jax.ShapeDtypeStruct((bsz, RET_HEADS, RET_DK, RET_DV), F32)),
        grid=(bsz,),
        in_specs=[pl.BlockSpec((1, c, d), lambda b: (b, 0, 0)),
                  const(modc.shape), const(n1w.shape), const(wall.shape), const(ghi.shape), const(glo.shape),
                  const(gb.shape), const(rlog.shape), const(mall_f.shape), const(mall_b.shape)],
        out_specs=(pl.BlockSpec((1, GLA_QK, GLA_DV), lambda b: (b, 0, 0)),
                   pl.BlockSpec((1, GLA_QK, GLA_DV), lambda b: (b, 0, 0)),
                   pl.BlockSpec((1, RET_HEADS, RET_DK, RET_DV), lambda b: (b, 0, 0, 0)),
                   pl.BlockSpec((1, RET_HEADS, RET_DK, RET_DV), lambda b: (b, 0, 0, 0))),
        compiler_params=pltpu.CompilerParams(dimension_semantics=("arbitrary",), vmem_limit_bytes=VMEM_LIMIT),
        name="ctx_states",
    )(ctx, modc, n1w, wall, ghi, glo, gb, rlog, mall_f, mall_b)


def _fwd_kernel(x_ref, mod_ref, n1w_ref, wall_ref, ghi_ref, glo_ref, gb_ref, cos_ref, sin_ref, rlog_ref,
                mall_ref, lvl_ref, sg0_ref, sr0_ref,
                of_ref, gqkv_ref, gates_ref, rqkv_ref, lab_ref,
                sg_scr, sr_scr, dmat_scr, qdec_scr, kdec_scr, cdec_scr, *, c):
    @pl.when(pl.program_id(1) == 0)
    def _init():
        sg_scr[...] = sg0_ref[0]
        sr_scr[...] = sr0_ref[0]
        dmats, qd, kd, cd = _ret_decays(rlog_ref, c, reverse=False)
        for h in range(RET_HEADS):
            dmat_scr[h] = dmats[h]
        qdec_scr[...] = qd
        kdec_scr[...] = kd
        cdec_scr[...] = cd

    mod = mod_ref[0]
    h = _rms(x_ref[0], n1w_ref[...]) * (1.0 + mod[1:2]) + mod[0:1]
    gq, gk, gv, gg, rq, rk, rv, rg, log_a = _project(h, wall_ref, ghi_ref, glo_ref, gb_ref)
    cos = cos_ref[...]
    sin = sin_ref[...]
    rq = _rope(rq, cos, sin)
    rk = _rope(rk, cos, sin)
    gvb = gv.astype(BF16)
    rvb = rv.astype(BF16)
    gqkv_ref[0] = jnp.concatenate([gq.astype(BF16), gk.astype(BF16), gvb], axis=1)
    gates_ref[0] = jnp.concatenate([gg, rg], axis=1).astype(BF16)
    rqkv_ref[0] = jnp.concatenate([rq.astype(BF16), rk.astype(BF16), rvb], axis=1)
    lab_ref[0] = log_a[:, GLA_QK:]
    o_g = _gla_tile(gq, gk, gvb, log_a[:, :GLA_QK], mall_ref, lvl_ref, sg_scr, c)
    o_r = _ret_tile(rq, rk, rvb, dmat_scr, qdec_scr[...], kdec_scr[...], cdec_scr[...], sr_scr)
    of_ref[0] = jnp.concatenate([o_g, o_r], axis=1)


def _fwd(x, modb, n1w, wall, ghi, glo, gb, cosf, sins, rlog, mall_f, lvl_f, sgf, srf):
    bsz, t, d = x.shape
    c = TILE
    nt = t // c
    const = lambda shape: pl.BlockSpec(shape, lambda b, j: (0,) * len(shape))
    tile = lambda w: pl.BlockSpec((1, c, w), lambda b, j: (b, j, 0))
    mixw = GLA_V + RET_V
    return pl.pallas_call(
        functools.partial(_fwd_kernel, c=c),
        out_shape=(jax.ShapeDtypeStruct((bsz, t, mixw), F32),
                   jax.ShapeDtypeStruct((bsz, t, 2 * GLA_QK + GLA_V), BF16),
                   jax.ShapeDtypeStruct((bsz, t, GLA_V + RET_V), BF16),
                   jax.ShapeDtypeStruct((bsz, t, 2 * RET_QK + RET_V), BF16),
                   jax.ShapeDtypeStruct((bsz, t, GLA_QK), F32)),
        grid=(bsz, nt),
        in_specs=[tile(d),
                  pl.BlockSpec((1,) + modb.shape[1:], lambda b, j: (b, 0, 0)),
                  const(n1w.shape), const(wall.shape), const(ghi.shape), const(glo.shape), const(gb.shape),
                  pl.BlockSpec((c, RET_DK), lambda b, j: (j, 0)),
                  pl.BlockSpec((c, RET_DK), lambda b, j: (j, 0)),
                  pl.BlockSpec((1,) + rlog.shape[1:], lambda b, j: (0, 0, 0)),
                  const(mall_f.shape), const(lvl_f.shape),
                  pl.BlockSpec((1, GLA_QK, GLA_DV), lambda b, j: (b, 0, 0)),
                  pl.BlockSpec((1, RET_HEADS, RET_DK, RET_DV), lambda b, j: (b, 0, 0, 0))],
        out_specs=(tile(mixw), tile(2 * GLA_QK + GLA_V), tile(GLA_V + RET_V), tile(2 * RET_QK + RET_V),
                   tile(GLA_QK)),
        scratch_shapes=[pltpu.VMEM((GLA_QK, GLA_DV), F32),
                        pltpu.VMEM((RET_HEADS, RET_DK, RET_DV), F32),
                        pltpu.VMEM((RET_HEADS, c, c), F32),
                        pltpu.VMEM((c, RET_QK), F32),
                        pltpu.VMEM((c, RET_QK), F32),
                        pltpu.VMEM((1, RET_QK), F32)],
        compiler_params=pltpu.CompilerParams(dimension_semantics=("arbitrary", "arbitrary"),
                                             vmem_limit_bytes=VMEM_LIMIT),
        name="mixer_fwd",
    )(x, modb, n1w, wall, ghi, glo, gb, cosf, sins, rlog, mall_f, lvl_f, sgf, srf)


def _bwd_kernel(x_ref, of_ref, gqkv_ref, gates_ref, rqkv_ref, lab_ref, mod_ref, rlog_ref, mall_ref, lvl_ref,
                sg0_ref, sr0_ref, gnw_ref, rnw_ref, wout_ref, n2w_ref, wrh_ref, wrl_ref,
                x1_ref, h2_ref, aff_ref,
                sg_scr, sr_scr, dmat_scr, qdec_scr, kdec_scr, cdec_scr, *, c):
    @pl.when(pl.program_id(1) == 0)
    def _init():
        sg_scr[...] = sg0_ref[0]
        sr_scr[...] = sr0_ref[0]
        dmats, qd, kd, cd = _ret_decays(rlog_ref, c, reverse=True)
        for h in range(RET_HEADS):
            dmat_scr[h] = dmats[h]
        qdec_scr[...] = qd
        kdec_scr[...] = kd
        cdec_scr[...] = cd

    gqkv = gqkv_ref[0]
    rqkv = rqkv_ref[0]
    gq = gqkv[:, :GLA_QK].astype(F32)
    gk = gqkv[:, GLA_QK:2 * GLA_QK].astype(F32)
    gvb = gqkv[:, 2 * GLA_QK:]
    rq = rqkv[:, :RET_QK].astype(F32)
    rk = rqkv[:, RET_QK:2 * RET_QK].astype(F32)
    rvb = rqkv[:, 2 * RET_QK:]
    o_f = of_ref[0]
    o_g = o_f[:, :GLA_V] + _gla_tile(gq, gk, gvb, lab_ref[0], mall_ref, lvl_ref, sg_scr, c)
    o_r = o_f[:, GLA_V:] + _ret_tile(rq, rk, rvb, dmat_scr, qdec_scr[...], kdec_scr[...], cdec_scr[...], sr_scr)

    gates = gates_ref[0].astype(F32)
    gnw = gnw_ref[...]
    rnw = rnw_ref[...]
    parts = []
    for h in range(GLA_HEADS):
        sl = slice(h * GLA_DV, (h + 1) * GLA_DV)
        oh = o_g[:, sl]
        parts.append(oh * lax.rsqrt(jnp.mean(oh * oh, axis=-1, keepdims=True) + EPS) * gnw[:, sl])
    for h in range(RET_HEADS):
        sl = slice(h * RET_DV, (h + 1) * RET_DV)
        oh = o_r[:, sl]
        mu = jnp.mean(oh, axis=-1, keepdims=True)
        dv = oh - mu
        parts.append(dv * lax.rsqrt(jnp.mean(dv * dv, axis=-1, keepdims=True) + EPS) * rnw[:, sl])
    mix = jnp.concatenate(parts, axis=1) * _silu(gates)
    mod = mod_ref[0]
    x1 = x_ref[0] + mod[2:3] * _dot(mix.astype(BF16), wout_ref[...])
    x1_ref[0] = x1
    h2 = _rms(x1, n2w_ref[...]) * (1.0 + mod[4:5]) + mod[3:4]
    h_hi, h_lo = _split(h2)
    h2_ref[0] = h_hi
    wrh = wrh_ref[...]
    logit = _dg(wrh, h_hi, _NT) + _dg(wrh, h_lo, _NT) + _dg(wrl_ref[...], h_hi, _NT)
    ex = jnp.exp(logit - jnp.max(logit, axis=0, keepdims=True))
    aff_ref[0] = ex / jnp.sum(ex, axis=0, keepdims=True)


def _bwd(x, o_f, gqkv, gates, rqkv, lab, modb, rlog, mall_b, lvl_b, sgb, srb, gnw, rnw, wout, n2w, wrh, wrl):
    bsz, t, d = x.shape
    c = TILE
    nt = t // c
    ne = wrh.shape[0]
    const = lambda shape: pl.BlockSpec(shape, lambda b, j: (0,) * len(shape))
    tile = lambda w: pl.BlockSpec((1, c, w), lambda b, j: (b, nt - 1 - j, 0))
    return pl.pallas_call(
        functools.partial(_bwd_kernel, c=c),
        out_shape=(jax.ShapeDtypeStruct((bsz, t, d), F32),
                   jax.ShapeDtypeStruct((bsz, t, d), BF16),
                   jax.ShapeDtypeStruct((bsz, ne, t), F32)),
        grid=(bsz, nt),
        in_specs=[tile(d), tile(o_f.shape[2]), tile(gqkv.shape[2]), tile(gates.shape[2]), tile(rqkv.shape[2]),
                  tile(lab.shape[2]),
                  pl.BlockSpec((1,) + modb.shape[1:], lambda b, j: (b, 0, 0)),
                  pl.BlockSpec((1,) + rlog.shape[1:], lambda b, j: (1, 0, 0)),
                  const(mall_b.shape), const(lvl_b.shape),
                  pl.BlockSpec((1, GLA_QK, GLA_DV), lambda b, j: (b, 0, 0)),
                  pl.BlockSpec((1, RET_HEADS, RET_DK, RET_DV), lambda b, j: (b, 0, 0, 0)),
                  const(gnw.shape), const(rnw.shape), const(wout.shape), const(n2w.shape),
                  const(wrh.shape), const(wrl.shape)],
        out_specs=(tile(d), tile(d), pl.BlockSpec((1, ne, c), lambda b, j: (b, 0, nt - 1 - j))),
        scratch_shapes=[pltpu.VMEM((GLA_QK, GLA_DV), F32),
                        pltpu.VMEM((RET_HEADS, RET_DK, RET_DV), F32),
                        pltpu.VMEM((RET_HEADS, c, c), F32),
                        pltpu.VMEM((c, RET_QK), F32),
                        pltpu.VMEM((c, RET_QK), F32),
                        pltpu.VMEM((1, RET_QK), F32)],
        compiler_params=pltpu.CompilerParams(dimension_semantics=("arbitrary", "arbitrary"),
                                             vmem_limit_bytes=VMEM_LIMIT),
        name="mixer_bwd",
    )(x, o_f, gqkv, gates, rqkv, lab, modb, rlog, mall_b, lvl_b, sgb, srb, gnw, rnw, wout, n2w, wrh, wrl)


def _route_kernel(aff_ref, pos_ref, off_ref, *, cap, nb):
    a = aff_ref[0]
    ne = a.shape[0]
    bits = lax.bitcast_convert_type(a, I32)
    kf = float(cap)

    def count(mask):
        return jnp.sum(jnp.sum(jnp.where(mask, 1.0, 0.0), axis=2, keepdims=True), axis=1, keepdims=True)

    def search(i, cur):
        cand = cur | jnp.left_shift(jnp.int32(1), 30 - i)
        return jnp.where(count(bits >= cand) >= kf, cand, cur)

    kth = lax.fori_loop(0, 31, search, jnp.zeros((ne, 1, 1), I32))
    gt = bits > kth
    eq = bits == kth
    need = kf - count(gt)

    upper = (lax.broadcasted_iota(I32, (LANES, LANES), 0) <= lax.broadcasted_iota(I32, (LANES, LANES), 1))
    upper = jnp.where(upper, 1.0, 0.0).astype(BF16)
    ones = jnp.ones((LANES, LANES), BF16)
    lower = (lax.broadcasted_iota(I32, (ne, nb, nb), 2) < lax.broadcasted_iota(I32, (ne, nb, nb), 1))
    lower = jnp.where(lower, 1.0, 0.0).astype(BF16)

    def excl_prefix(mask):
        m = jnp.where(mask, 1.0, 0.0)
        mb = m.astype(BF16).reshape(ne * nb, LANES)
        inc = _dot(mb, upper).reshape(ne, nb, LANES)
        tot = _dot(mb, ones).reshape(ne, nb, LANES)
        offs = lax.dot_general(lower, tot.astype(BF16), (((2,), (1,)), ((0,), (0,))), preferred_element_type=F32)
        return inc - m + offs, offs

    eq_rank, _ = excl_prefix(eq)
    sel = gt | (eq & (eq_rank < need))
    rank, offs = excl_prefix(sel)
    pos_ref[0] = jnp.where(sel, rank, -1.0).astype(I32)
    off_ref[0] = offs.astype(I32)


def _route(aff4, cap):
    bsz, ne, nb, _ = aff4.shape
    spec = pl.BlockSpec((1, ne, nb, LANES), lambda b: (b, 0, 0, 0))
    return pl.pallas_call(
        functools.partial(_route_kernel, cap=cap, nb=nb),
        out_shape=(jax.ShapeDtypeStruct(aff4.shape, I32), jax.ShapeDtypeStruct(aff4.shape, I32)),
        grid=(bsz,),
        in_specs=[spec],
        out_specs=(spec, spec),
        compiler_params=pltpu.CompilerParams(dimension_semantics=("arbitrary",), vmem_limit_bytes=VMEM_LIMIT),
        name="route",
    )(aff4)


def _tile_counts(cnt_ref, b, j, ne):
    m = cnt_ref[b, j, 0]
    for e in range(1, ne):
        m = jnp.maximum(m, cnt_ref[b, j, e])
    return m


def _window_select(rel, valid, val, ne):
    c = rel.shape[1]
    w = lax.broadcasted_iota(I32, (ne, WROWS, c), 1)
    relm = jnp.where(valid, rel, -1)
    sel = jnp.where(relm[:, None, :] == w, jnp.broadcast_to(val[:, None, :], (ne, WROWS, c)), 0.0)
    return sel.reshape(ne * WROWS, c)


def _round_slots(basev, cntv, r):
    start = basev + jnp.minimum(r * WIN, cntv)
    num = jnp.clip(cntv - r * WIN, 0, WIN)
    return start, num


def _round_slots_scalar(base, cnt, r):
    return base + jnp.minimum(r * WIN, cnt), jnp.clip(cnt - r * WIN, 0, WIN)


def _align_down(v):
    return (v >> 3) << 3


def _gather_kernel(base_ref, cnt_ref, pos_ref, basev_ref, cntv_ref, h2_ref, xe_ref, xbuf, carry, sem, *, cap, ne):
    b = pl.program_id(0)
    j = pl.program_id(1)

    def window_copies(row0):
        return [pltpu.make_async_copy(xbuf.at[pl.ds(e * WROWS, WROWS)],
                                      xe_ref.at[b, e, pl.ds(row0(e), WROWS)], sem.at[e]) for e in range(ne)]

    @pl.when(j == 0)
    def _start_sample():
        carry[...] = jnp.zeros(carry.shape, F32)
        xbuf[...] = jnp.zeros(xbuf.shape, F32)
        cps = window_copies(lambda e: cap)
        for cp in cps:
            cp.start()
        for cp in cps:
            cp.wait()

    pos = pos_ref[0]
    basev = basev_ref[0, 0]
    cntv = cntv_ref[0, 0]
    h2 = h2_ref[0]
    ones = jnp.ones(pos.shape, F32)
    nrounds = (_tile_counts(cnt_ref, b, j, ne) + (WIN - 1)) // WIN

    def round_body(r, _):
        start, num = _round_slots(basev, cntv, r)
        valid = (pos >= start) & (pos < start + num)
        onehot = _window_select(pos - _align_down(start), valid, ones, ne).astype(BF16)
        xbuf[...] = _dot(onehot, h2)
        first, nxt = [], []
        for e in range(ne):
            s, n = _round_slots_scalar(base_ref[b, j, e], cnt_ref[b, j, e], r)
            first.append(pl.multiple_of(_align_down(s), 8))
            nxt.append(_align_down(s + n) - _align_down(s))
            xbuf[pl.ds(e * WROWS, 8), :] += carry[pl.ds(e * 8, 8), :]
        cps = window_copies(lambda e: first[e])
        for cp in cps:
            cp.start()
        for e in range(ne):
            carry[pl.ds(e * 8, 8), :] = xbuf[pl.ds(pl.multiple_of(e * WROWS + nxt[e], 8), 8), :]
        for cp in cps:
            cp.wait()
        return 0

    lax.fori_loop(0, nrounds, round_body, 0)


def _gather(base, cnt, pos, basev, cntv, h2, cap):
    bsz, t, d = h2.shape
    ne = pos.shape[1]
    c = TILE
    nt = t // c
    grid_spec = pltpu.PrefetchScalarGridSpec(
        num_scalar_prefetch=2,
        grid=(bsz, nt),
        in_specs=[pl.BlockSpec((1, ne, c), lambda b, j, *_: (b, 0, j)),
                  pl.BlockSpec((1, 1, ne, c), lambda b, j, *_: (b, j, 0, 0)),
                  pl.BlockSpec((1, 1, ne, c), lambda b, j, *_: (b, j, 0, 0)),
                  pl.BlockSpec((1, c, d), lambda b, j, *_: (b, j, 0))],
        out_specs=pl.BlockSpec(memory_space=pl.ANY),
        scratch_shapes=[pltpu.VMEM((ne * WROWS, d), F32), pltpu.VMEM((ne * 8, d), F32),
                        pltpu.SemaphoreType.DMA((ne,))],
    )
    return pl.pallas_call(
        functools.partial(_gather_kernel, cap=cap, ne=ne),
        out_shape=jax.ShapeDtypeStruct((bsz, ne, cap + WROWS, d), F32),
        grid_spec=grid_spec,
        compiler_params=pltpu.CompilerParams(dimension_semantics=("arbitrary", "arbitrary"),
                                             vmem_limit_bytes=VMEM_LIMIT),
        name="moe_gather",
    )(base, cnt, pos, basev, cntv, h2)


def _expert_kernel(xe_ref, wg_ref, wu_ref, wd_ref, ye_ref):
    xb = xe_ref[0, 0].astype(BF16)
    a = _dot(xb, wg_ref[0])
    u = _dot(xb, wu_ref[0])
    ye_ref[0, 0] = _dot((_silu(a) * u).astype(BF16), wd_ref[0])


def _experts(xe, wg, wu, wd, cap):
    bsz, ne, _, d = xe.shape
    ff = wg.shape[2]
    rows = min(512, cap)
    return pl.pallas_call(
        _expert_kernel,
        out_shape=jax.ShapeDtypeStruct((bsz, ne, cap, d), F32),
        grid=(ne, bsz, cap // rows),
        in_specs=[pl.BlockSpec((1, 1, rows, d), lambda e, b, r: (b, e, r, 0)),
                  pl.BlockSpec((1, d, ff), lambda e, b, r: (e, 0, 0)),
                  pl.BlockSpec((1, d, ff), lambda e, b, r: (e, 0, 0)),
                  pl.BlockSpec((1, ff, d), lambda e, b, r: (e, 0, 0))],
        out_specs=pl.BlockSpec((1, 1, rows, d), lambda e, b, r: (b, e, r, 0)),
        compiler_params=pltpu.CompilerParams(dimension_semantics=("arbitrary", "arbitrary", "arbitrary"),
                                             vmem_limit_bytes=VMEM_LIMIT),
        name="moe_experts",
    )(xe, wg, wu, wd)


def _combine_kernel(base_ref, cnt_ref, pos_ref, aff_ref, basev_ref, cntv_ref, x1_ref, mod_ref, fnw_ref, ye_ref,
                    out_ref, stage, acc, sem, *, cap, ne):
    b = pl.program_id(0)
    j = pl.program_id(1)
    pos = pos_ref[0]
    gate = aff_ref[0]
    basev = basev_ref[0, 0]
    cntv = cntv_ref[0, 0]
    last = cap - WROWS
    nrounds = (_tile_counts(cnt_ref, b, j, ne) + (WIN - 1)) // WIN
    acc[...] = jnp.zeros(acc.shape, F32)

    def round_body(r, _):
        cps = []
        for e in range(ne):
            s, _n = _round_slots_scalar(base_ref[b, j, e], cnt_ref[b, j, e], r)
            row0 = pl.multiple_of(jnp.minimum(_align_down(s), last), 8)
            cps.append(pltpu.make_async_copy(ye_ref.at[b, e, pl.ds(row0, WROWS)],
                                             stage.at[pl.ds(e * WROWS, WROWS)], sem.at[e]))
        for cp in cps:
            cp.start()
        start, num = _round_slots(basev, cntv, r)
        valid = (pos >= start) & (pos < start + num)
        w_hi, w_lo = _split(_window_select(pos - jnp.minimum(_align_down(start), last), valid, gate, ne))
        for cp in cps:
            cp.wait()
        rows = stage[...].astype(BF16)
        acc[...] += _dg(w_hi, rows, _TN) + _dg(w_lo, rows, _TN)
        return 0

    lax.fori_loop(0, nrounds, round_body, 0)
    mod = mod_ref[0]
    x2 = x1_ref[0] + mod[5:6] * acc[...]
    out_ref[0] = _rms(x2, fnw_ref[...])


def _combine(base, cnt, pos, aff, basev, cntv, x1, modb, fnw, ye, cap):
    bsz, t, d = x1.shape
    ne = pos.shape[1]
    c = TILE
    nt = t // c
    grid_spec = pltpu.PrefetchScalarGridSpec(
        num_scalar_prefetch=2,
        grid=(bsz, nt),
        in_specs=[pl.BlockSpec((1, ne, c), lambda b, j, *_: (b, 0, j)),
                  pl.BlockSpec((1, ne, c), lambda b, j, *_: (b, 0, j)),
                  pl.BlockSpec((1, 1, ne, c), lambda b, j, *_: (b, j, 0, 0)),
                  pl.BlockSpec((1, 1, ne, c), lambda b, j, *_: (b, j, 0, 0)),
                  pl.BlockSpec((1, c, d), lambda b, j, *_: (b, j, 0)),
                  pl.BlockSpec((1,) + modb.shape[1:], lambda b, j, *_: (b, 0, 0)),
                  pl.BlockSpec(fnw.shape, lambda b, j, *_: (0, 0)),
                  pl.BlockSpec(memory_space=pl.ANY)],
        out_specs=pl.BlockSpec((1, c, d), lambda b, j, *_: (b, j, 0)),
        scratch_shapes=[pltpu.VMEM((ne * WROWS, d), F32), pltpu.VMEM((c, d), F32),
                        pltpu.SemaphoreType.DMA((ne,))],
    )
    return pl.pallas_call(
        functools.partial(_combine_kernel, cap=cap, ne=ne),
        out_shape=jax.ShapeDtypeStruct((bsz, t, d), F32),
        grid_spec=grid_spec,
        compiler_params=pltpu.CompilerParams(dimension_semantics=("arbitrary", "arbitrary"),
                                             vmem_limit_bytes=VMEM_LIMIT),
        name="moe_combine",
    )(base, cnt, pos, aff, basev, cntv, x1, modb, fnw, ye)


def _rope_tables(t):
    rows = t // GRID_W
    row = jnp.broadcast_to(jnp.arange(rows)[:, None], (rows, GRID_W)).reshape(-1).astype(F32)
    col = jnp.broadcast_to(jnp.arange(GRID_W)[None, :], (rows, GRID_W)).reshape(-1).astype(F32)
    n_freq = RET_DK // 4
    inv = ROPE_BASE ** (-jnp.arange(n_freq, dtype=F32) / n_freq)
    ang = jnp.concatenate([row[:, None] * inv, col[:, None] * inv], axis=-1)
    cos = jnp.cos(ang)
    sin = jnp.sin(ang)
    return jnp.concatenate([cos, cos], axis=1), jnp.concatenate([-sin, sin], axis=1)


def _mixer_weights(w_in, gate_w, gate_b):
    pts = np.cumsum(IN_WIDTHS)[:-1]
    gq, gk, gv, gz, gg, rq, rk, rv, rg = jnp.split(w_in, [int(p) for p in pts], axis=1)
    gz = jnp.pad(gz, ((0, 0), (0, GZ_PAD - 2 * GLA_RANK)))
    wall = jnp.concatenate([gq, gk, gv, gg, rq, rk, rv, rg, gz], axis=1).astype(BF16)
    gmat = jnp.zeros((GZ_PAD, 2 * GLA_QK), F32)
    gmat = gmat.at[:GLA_RANK, :GLA_QK].set(gate_w[0]).at[GLA_RANK:2 * GLA_RANK, GLA_QK:].set(gate_w[1])
    ghi = gmat.astype(BF16)
    glo = (gmat - ghi.astype(F32)).astype(BF16)
    return wall, ghi, glo, gate_b.reshape(1, 2 * GLA_QK)


def kernel(x, c, ctx, c_ctx, w_ada, b_ada, norm1_w, w_in, gla_gate_w, gla_gate_b, ret_decay_logit, gla_norm_w,
           ret_norm_w, w_out, norm2_w, w_router, w_exp_gate, w_exp_up, w_exp_down, final_norm_w):
    bsz, t, d = x.shape
    depth = w_ada.shape[0]
    assert depth == 1 and t % TILE == 0 and ctx.shape[1] == TILE
    ne = w_router.shape[2]
    cap = EC_CAPACITY_FACTOR * t // ne
    assert cap >= WROWS and cap % 8 == 0 and cap % min(512, cap) == 0
    nt = t // TILE
    nb = t // LANES
    bpt = TILE // LANES

    cs = jnp.concatenate([c, c_ctx[None, :], jnp.zeros((8 - bsz - 1, d), F32)], axis=0)
    mod = _ada(cs, w_ada[0], b_ada[0][None, :])
    mod = jnp.pad(mod.reshape(8, N_ADA, d), ((0, 0), (0, 8 - N_ADA), (0, 0)))
    modb = mod[:bsz]
    modc = mod[bsz:bsz + 1]

    wall, ghi, glo, gb = _mixer_weights(w_in[0], gla_gate_w[0], gla_gate_b[0])
    n1w = norm1_w[0][None, :]
    rlog = jnp.broadcast_to(ret_decay_logit[0][:, :, None], (2, RET_HEADS, TILE)).astype(F32)
    mall_f = jnp.asarray(_level_matrices(TILE, False), BF16)
    mall_b = jnp.asarray(_level_matrices(TILE, True), BF16)
    lvl_f = jnp.asarray(_level_index(TILE, False))
    lvl_b = jnp.asarray(_level_index(TILE, True))
    cosf, sins = _rope_tables(t)

    sgf, sgb, srf, srb = _ctx_states(ctx, modc, n1w, wall, ghi, glo, gb, rlog, mall_f, mall_b)
    o_f, gqkv, gates, rqkv, lab = _fwd(x, modb, n1w, wall, ghi, glo, gb, cosf, sins, rlog, mall_f, lvl_f, sgf, srf)

    wr = w_router[0].T
    wrh = wr.astype(BF16)
    wrl = (wr - wrh.astype(F32)).astype(BF16)
    x1, h2, aff = _bwd(x, o_f, gqkv, gates, rqkv, lab, modb, rlog, mall_b, lvl_b, sgb, srb,
                       gla_norm_w[0][None, :], ret_norm_w[0][None, :], w_out[0].astype(BF16),
                       norm2_w[0][None, :], wrh, wrl)

    pos4, off4 = _route(aff.reshape(bsz, ne, nb, LANES), cap)
    pos = pos4.reshape(bsz, ne, t)
    boff = off4[:, :, :, 0]
    base = jnp.transpose(boff[:, :, ::bpt], (0, 2, 1))
    nxt = jnp.concatenate([base[:, 1:], jnp.full((bsz, 1, ne), cap, I32)], axis=1)
    cnt = nxt - base
    basev = jnp.broadcast_to(base[:, :, :, None], (bsz, nt, ne, TILE))
    cntv = jnp.broadcast_to(cnt[:, :, :, None], (bsz, nt, ne, TILE))

    xe = _gather(base, cnt, pos, basev, cntv, h2, cap)
    ye = _experts(xe, w_exp_gate[0].astype(BF16), w_exp_up[0].astype(BF16), w_exp_down[0].astype(BF16), cap)
    return _combine(base, cnt, pos, aff, basev, cntv, x1, modb, final_norm_w[None, :], ye, cap)
```

```python
import functools

import numpy as np
import jax
import jax.numpy as jnp
from jax import lax
from jax.experimental import pallas as pl
from jax.experimental.pallas import tpu as pltpu

F32 = jnp.float32
BF16 = jnp.bfloat16
I32 = jnp.int32

GLA_HEADS = 4
GLA_DK = 64
GLA_DV = 128
GLA_RANK = 16
GLA_TAU = 16.0
RET_HEADS = 4
RET_DK = 128
RET_DV = 128
GRID_W = 64
ROPE_BASE = 10000.0
N_EXPERTS = 16
EC_CAPACITY_FACTOR = 2
N_ADA = 6
EPS = 1e-6

GLA_QK = GLA_HEADS * GLA_DK
GLA_V = GLA_HEADS * GLA_DV
RET_QK = RET_HEADS * RET_DK
RET_V = RET_HEADS * RET_DV
IN_WIDTHS = (GLA_QK, GLA_QK, GLA_V, 2 * GLA_RANK, GLA_V, RET_QK, RET_QK, RET_V, RET_V)

LANES = 128
TILE = 256
N_LEVELS = 8
WIN = 48
ALIGN = 16
WROWS = WIN + ALIGN
GZ_PAD = LANES
VMEM_LIMIT = 56 * 1024 * 1024

_NT = (((1,), (1,)), ((), ()))
_TN = (((0,), (0,)), ((), ()))


def _dot(a, b):
    return jnp.dot(a, b, preferred_element_type=F32)


def _dg(a, b, dims):
    return lax.dot_general(a, b, dims, preferred_element_type=F32)


def _split(a):
    hi = a.astype(BF16)
    lo = (a - hi.astype(F32)).astype(BF16)
    return hi, lo


def _logsig(x):
    return jnp.minimum(x, 0.0) - jnp.log(1.0 + jnp.exp(-jnp.abs(x)))


def _silu(x):
    return x / (1.0 + jnp.exp(-x))


def _rms(x, w):
    return x * lax.rsqrt(jnp.mean(x * x, axis=-1, keepdims=True) + EPS) * w


def _level_matrices(c, reverse):
    nl = int(np.log2(c))
    t = np.arange(c)[None, :]
    i = np.arange(c)[:, None]
    mats = []
    for l in range(nl):
        m = 1 << l
        blk = (i // (2 * m)) * (2 * m)
        mid = blk + m
        upper = (i - blk) >= m
        if not reverse:
            sel = np.where(upper, (t >= mid) & (t <= i), (t > i) & (t <= mid - 1))
        else:
            sel = np.where(upper, (t >= mid) & (t < i), (t >= i) & (t <= mid - 1))
        mats.append(sel)
    if not reverse:
        mats.append(t <= i)
        mats.append(t > i)
    else:
        mats.append(t >= i)
        mats.append(t < i)
    return np.concatenate(mats, axis=0).astype(np.float32)


def _level_index(c, reverse):
    i = np.arange(c)[:, None]
    j = np.arange(c)[None, :]
    x = i ^ j
    lvl = np.where(x > 0, np.floor(np.log2(np.maximum(x, 1))), -1).astype(np.int32)
    bad = (j > i) if not reverse else (j < i)
    return np.where(bad, 99, lvl).astype(np.int32)


def _project(xn, wall_ref, ghi_ref, glo_ref, gb_ref):
    proj = _dot(xn.astype(BF16), wall_ref[...])
    o = 0
    out = []
    for w in (GLA_QK, GLA_QK, GLA_V, GLA_V, RET_QK, RET_QK, RET_V, RET_V, GZ_PAD):
        out.append(proj[:, o:o + w])
        o += w
    gq, gk, gv, gg, rq, rk, rv, rg, gz = out
    z_hi, z_lo = _split(gz)
    pre = _dot(z_hi, ghi_ref[...]) + _dot(z_lo, ghi_ref[...]) + _dot(z_hi, glo_ref[...]) + gb_ref[...]
    log_a = _logsig(pre) * (1.0 / GLA_TAU)
    return gq * (GLA_DK ** -0.5), gk, gv, gg, rq, rk * (RET_DK ** -0.5), rv, rg, log_a


def _rope(a, cos, sin):
    outs = []
    for h in range(RET_HEADS):
        ah = a[:, h * RET_DK:(h + 1) * RET_DK]
        outs.append(ah * cos + pltpu.roll(ah, RET_DK // 2, 1) * sin)
    return jnp.concatenate(outs, axis=1)


def _stack_heads(a):
    head = lax.broadcasted_iota(I32, a.shape, 1) >> 6
    zero = jnp.zeros_like(a)
    return jnp.concatenate([jnp.where(head == h, a, zero) for h in range(GLA_HEADS)], axis=0)


def _gate_sums(g):
    hi, lo = _split(g)
    return jnp.concatenate([hi, lo], axis=1)


def _level_exp(mall_ref, g2, level, c):
    r = _dot(mall_ref[level * c:(level + 1) * c, :], g2)
    return jnp.exp(r[:, :GLA_QK] + r[:, GLA_QK:])


def _gla_state_update(k, vb, g2, e_k, s_prev, c):
    kk = (k * e_k).astype(BF16)
    kv = _dg(kk, vb, _TN)
    cs = _dg(g2, jnp.ones((c, LANES), BF16), _TN)
    e_col = jnp.exp(cs[:GLA_QK] + cs[GLA_QK:])
    new = []
    for h in range(GLA_HEADS):
        r = slice(h * GLA_DK, (h + 1) * GLA_DK)
        new.append(e_col[r] * s_prev[r] + kv[r, h * GLA_DV:(h + 1) * GLA_DV])
    return jnp.concatenate(new, axis=0)


def _gla_tile(q, k, vb, g, mall_ref, lvl_ref, s_ref, c):
    g2 = _gate_sums(g)
    lvl = lvl_ref[...]
    lvl4 = jnp.concatenate([lvl] * GLA_HEADS, axis=0)
    scores = jnp.zeros((GLA_HEADS * c, c), F32)
    for level in range(N_LEVELS):
        e = _level_exp(mall_ref, g2, level, c)
        p = _dg(_stack_heads((q * e).astype(BF16)), (k * e).astype(BF16), _NT)
        scores = jnp.where(lvl4 == level, p, scores)
    p = _dg(_stack_heads(q.astype(BF16)), k.astype(BF16), _NT)
    scores = jnp.where(lvl4 == -1, p, scores)
    e_b = _level_exp(mall_ref, g2, N_LEVELS, c)
    e_k = _level_exp(mall_ref, g2, N_LEVELS + 1, c)
    s_prev = s_ref[...]
    inter = _dot(_stack_heads((q * e_b).astype(BF16)), s_prev.astype(BF16))
    sb = scores.astype(BF16)
    outs = []
    for h in range(GLA_HEADS):
        r = slice(h * c, (h + 1) * c)
        outs.append(_dot(sb[r], vb[:, h * GLA_DV:(h + 1) * GLA_DV]) + inter[r])
    s_ref[...] = _gla_state_update(k, vb, g2, e_k, s_prev, c)
    return jnp.concatenate(outs, axis=1)


def _ret_decays(rlog_ref, c, reverse):
    lg = _logsig(rlog_ref[0])
    ii = lax.broadcasted_iota(I32, (c, c), 0)
    jj = lax.broadcasted_iota(I32, (c, c), 1)
    rel = ((jj - ii) if reverse else (ii - jj)).astype(F32)
    pos = lax.broadcasted_iota(I32, (c, RET_DK), 0).astype(F32)
    dmats, qd, kd, cd = [], [], [], []
    for h in range(RET_HEADS):
        lh = lg[h:h + 1, :]
        dmats.append(jnp.where(rel >= 0, jnp.exp(lh * jnp.maximum(rel, 0.0)), 0.0))
        l1 = lh[:, :RET_DK]
        qd.append(jnp.exp(l1 * ((c - pos) if reverse else (pos + 1.0))))
        kd.append(jnp.exp(l1 * (pos if reverse else (c - 1.0 - pos))))
        cd.append(jnp.exp(l1 * float(c)))
    return dmats, jnp.concatenate(qd, axis=1), jnp.concatenate(kd, axis=1), jnp.concatenate(cd, axis=1)


def _ret_tile(q, k, vb, dmat_ref, qdec, kdec, cdec, s_ref):
    outs = []
    for h in range(RET_HEADS):
        sl = slice(h * RET_DK, (h + 1) * RET_DK)
        vh = vb[:, h * RET_DV:(h + 1) * RET_DV]
        qh = q[:, sl]
        kh = k[:, sl]
        sc = _dg(qh.astype(BF16), kh.astype(BF16), _NT) * dmat_ref[h]
        s = s_ref[h]
        outs.append(_dot(sc.astype(BF16), vh) + _dot((qh * qdec[:, sl]).astype(BF16), s.astype(BF16)))
        s_ref[h] = cdec[:, sl] * s + _dg((kh * kdec[:, sl]).astype(BF16), vh, _TN)
    return jnp.concatenate(outs, axis=1)


def _ada_kernel(c_ref, w_ref, b_ref, o_ref):
    s_hi, s_lo = _split(_silu(c_ref[...]))
    w_hi, w_lo = _split(w_ref[...])
    o_ref[...] = _dot(s_hi, w_hi) + _dot(s_lo, w_hi) + _dot(s_hi, w_lo) + b_ref[...]


def _ada(cs, w, b):
    rows, d = cs.shape
    n = w.shape[1]
    tn = 1536
    return pl.pallas_call(
        _ada_kernel,
        out_shape=jax.ShapeDtypeStruct((rows, n), F32),
        grid=(n // tn,),
        in_specs=[pl.BlockSpec((rows, d), lambda i: (0, 0)),
                  pl.BlockSpec((d, tn), lambda i: (0, i)),
                  pl.BlockSpec((1, tn), lambda i: (0, i))],
        out_specs=pl.BlockSpec((rows, tn), lambda i: (0, i)),
        compiler_params=pltpu.CompilerParams(dimension_semantics=("arbitrary",), vmem_limit_bytes=VMEM_LIMIT),
        name="ada",
    )(cs, w, b)


def _ctx_kernel(ctx_ref, mod_ref, n1w_ref, wall_ref, ghi_ref, glo_ref, gb_ref, rlog_ref, mf_ref, mb_ref,
                sgf_ref, sgb_ref, srf_ref, srb_ref, *, c):
    mod = mod_ref[0]
    hc = _rms(ctx_ref[0], n1w_ref[...]) * (1.0 + mod[1:2]) + mod[0:1]
    _, gk, gv, _, _, rk, rv, _, log_a = _project(hc, wall_ref, ghi_ref, glo_ref, gb_ref)
    gvb = gv.astype(BF16)
    rvb = rv.astype(BF16)
    zero = jnp.zeros((GLA_QK, GLA_DV), F32)
    for d, (m_ref, out_g, out_r) in enumerate(((mf_ref, sgf_ref, srf_ref), (mb_ref, sgb_ref, srb_ref))):
        g2 = _gate_sums(log_a[:, d * GLA_QK:(d + 1) * GLA_QK])
        e_k = _level_exp(m_ref, g2, N_LEVELS + 1, c)
        out_g[0] = _gla_state_update(gk, gvb, g2, e_k, zero, c)
        _, _, kdec, _ = _ret_decays(rlog_ref.at[d:d + 1], c, reverse=bool(d))
        for h in range(RET_HEADS):
            sl = slice(h * RET_DK, (h + 1) * RET_DK)
            out_r[0, h] = _dg((rk[:, sl] * kdec[:, sl]).astype(BF16), rvb[:, h * RET_DV:(h + 1) * RET_DV], _TN)


def _ctx_states(ctx, modc, n1w, wall, ghi, glo, gb, rlog, mall_f, mall_b):
    bsz, c, d = ctx.shape
    const = lambda shape: pl.BlockSpec(shape, lambda b: (0,) * len(shape))
    return pl.pallas_call(
        functools.partial(_ctx_kernel, c=c),
        out_shape=(jax.ShapeDtypeStruct((bsz, GLA_QK, GLA_DV), F32),
                   jax.ShapeDtypeStruct((bsz, GLA_QK, GLA_DV), F32),
                   jax.ShapeDtypeStruct((bsz, RET_HEADS, RET_DK, RET_DV), F32),
                   jax.ShapeDtypeStruct((bsz, RET_HEADS, RET_DK, RET_DV), F32)),
        grid=(bsz,),
        in_specs=[pl.BlockSpec((1, c, d), lambda b: (b, 0, 0)),
                  const(modc.shape), const(n1w.shape), const(wall.shape), const(ghi.shape), const(glo.shape),
                  const(gb.shape), const(rlog.shape), const(mall_f.shape), const(mall_b.shape)],
        out_specs=(pl.BlockSpec((1, GLA_QK, GLA_DV), lambda b: (b, 0, 0)),
                   pl.BlockSpec((1, GLA_QK, GLA_DV), lambda b: (b, 0, 0)),
                   pl.BlockSpec((1, RET_HEADS, RET_DK, RET_DV), lambda b: (b, 0, 0, 0)),
                   pl.BlockSpec((1, RET_HEADS, RET_DK, RET_DV), lambda b: (b, 0, 0, 0))),
        compiler_params=pltpu.CompilerParams(dimension_semantics=("arbitrary",), vmem_limit_bytes=VMEM_LIMIT),
        name="ctx_states",
    )(ctx, modc, n1w, wall, ghi, glo, gb, rlog, mall_f, mall_b)


def _fwd_kernel(x_ref, mod_ref, n1w_ref, wall_ref, ghi_ref, glo_ref, gb_ref, cos_ref, sin_ref, rlog_ref,
                mall_ref, lvl_ref, sg0_ref, sr0_ref,
                of_ref, gqkv_ref, gates_ref, rqkv_ref, lab_ref,
                sg_scr, sr_scr, dmat_scr, qdec_scr, kdec_scr, cdec_scr, *, c):
    @pl.when(pl.program_id(1) == 0)
    def _init():
        sg_scr[...] = sg0_ref[0]
        sr_scr[...] = sr0_ref[0]
        dmats, qd, kd, cd = _ret_decays(rlog_ref, c, reverse=False)
        for h in range(RET_HEADS):
            dmat_scr[h] = dmats[h]
        qdec_scr[...] = qd
        kdec_scr[...] = kd
        cdec_scr[...] = cd

    mod = mod_ref[0]
    h = _rms(x_ref[0], n1w_ref[...]) * (1.0 + mod[1:2]) + mod[0:1]
    gq, gk, gv, gg, rq, rk, rv, rg, log_a = _project(h, wall_ref, ghi_ref, glo_ref, gb_ref)
    cos = cos_ref[...]
    sin = sin_ref[...]
    rq = _rope(rq, cos, sin)
    rk = _rope(rk, cos, sin)
    gvb = gv.astype(BF16)
    rvb = rv.astype(BF16)
    gqkv_ref[0] = jnp.concatenate([gq.astype(BF16), gk.astype(BF16), gvb], axis=1)
    gates_ref[0] = jnp.concatenate([gg, rg], axis=1).astype(BF16)
    rqkv_ref[0] = jnp.concatenate([rq.astype(BF16), rk.astype(BF16), rvb], axis=1)
    lab_ref[0] = log_a[:, GLA_QK:]
    o_g = _gla_tile(gq, gk, gvb, log_a[:, :GLA_QK], mall_ref, lvl_ref, sg_scr, c)
    o_r = _ret_tile(rq, rk, rvb, dmat_scr, qdec_scr[...], kdec_scr[...], cdec_scr[...], sr_scr)
    of_ref[0] = jnp.concatenate([o_g, o_r], axis=1)


def _fwd(x, modb, n1w, wall, ghi, glo, gb, cosf, sins, rlog, mall_f, lvl_f, sgf, srf):
    bsz, t, d = x.shape
    c = TILE
    nt = t // c
    const = lambda shape: pl.BlockSpec(shape, lambda b, j: (0,) * len(shape))
    tile = lambda w: pl.BlockSpec((1, c, w), lambda b, j: (b, j, 0))
    mixw = GLA_V + RET_V
    return pl.pallas_call(
        functools.partial(_fwd_kernel, c=c),
        out_shape=(jax.ShapeDtypeStruct((bsz, t, mixw), F32),
                   jax.ShapeDtypeStruct((bsz, t, 2 * GLA_QK + GLA_V), BF16),
                   jax.ShapeDtypeStruct((bsz, t, GLA_V + RET_V), BF16),
                   jax.ShapeDtypeStruct((bsz, t, 2 * RET_QK + RET_V), BF16),
                   jax.ShapeDtypeStruct((bsz, t, GLA_QK), F32)),
        grid=(bsz, nt),
        in_specs=[tile(d),
                  pl.BlockSpec((1,) + modb.shape[1:], lambda b, j: (b, 0, 0)),
                  const(n1w.shape), const(wall.shape), const(ghi.shape), const(glo.shape), const(gb.shape),
                  pl.BlockSpec((c, RET_DK), lambda b, j: (j, 0)),
                  pl.BlockSpec((c, RET_DK), lambda b, j: (j, 0)),
                  pl.BlockSpec((1,) + rlog.shape[1:], lambda b, j: (0, 0, 0)),
                  const(mall_f.shape), const(lvl_f.shape),
                  pl.BlockSpec((1, GLA_QK, GLA_DV), lambda b, j: (b, 0, 0)),
                  pl.BlockSpec((1, RET_HEADS, RET_DK, RET_DV), lambda b, j: (b, 0, 0, 0))],
        out_specs=(tile(mixw), tile(2 * GLA_QK + GLA_V), tile(GLA_V + RET_V), tile(2 * RET_QK + RET_V),
                   tile(GLA_QK)),
        scratch_shapes=[pltpu.VMEM((GLA_QK, GLA_DV), F32),
                        pltpu.VMEM((RET_HEADS, RET_DK, RET_DV), F32),
                        pltpu.VMEM((RET_HEADS, c, c), F32),
                        pltpu.VMEM((c, RET_QK), F32),
                        pltpu.VMEM((c, RET_QK), F32),
                        pltpu.VMEM((1, RET_QK), F32)],
        compiler_params=pltpu.CompilerParams(dimension_semantics=("arbitrary", "arbitrary"),
                                             vmem_limit_bytes=VMEM_LIMIT),
        name="mixer_fwd",
    )(x, modb, n1w, wall, ghi, glo, gb, cosf, sins, rlog, mall_f, lvl_f, sgf, srf)


def _bwd_kernel(x_ref, of_ref, gqkv_ref, gates_ref, rqkv_ref, lab_ref, mod_ref, rlog_ref, mall_ref, lvl_ref,
                sg0_ref, sr0_ref, gnw_ref, rnw_ref, wout_ref, n2w_ref, wrh_ref, wrl_ref,
                x1_ref, h2_ref, aff_ref,
                sg_scr, sr_scr, dmat_scr, qdec_scr, kdec_scr, cdec_scr, *, c):
    @pl.when(pl.program_id(1) == 0)
    def _init():
        sg_scr[...] = sg0_ref[0]
        sr_scr[...] = sr0_ref[0]
        dmats, qd, kd, cd = _ret_decays(rlog_ref, c, reverse=True)
        for h in range(RET_HEADS):
            dmat_scr[h] = dmats[h]
        qdec_scr[...] = qd
        kdec_scr[...] = kd
        cdec_scr[...] = cd

    gqkv = gqkv_ref[0]
    rqkv = rqkv_ref[0]
    gq = gqkv[:, :GLA_QK].astype(F32)
    gk = gqkv[:, GLA_QK:2 * GLA_QK].astype(F32)
    gvb = gqkv[:, 2 * GLA_QK:]
    rq = rqkv[:, :RET_QK].astype(F32)
    rk = rqkv[:, RET_QK:2 * RET_QK].astype(F32)
    rvb = rqkv[:, 2 * RET_QK:]
    o_f = of_ref[0]
    o_g = o_f[:, :GLA_V] + _gla_tile(gq, gk, gvb, lab_ref[0], mall_ref, lvl_ref, sg_scr, c)
    o_r = o_f[:, GLA_V:] + _ret_tile(rq, rk, rvb, dmat_scr, qdec_scr[...], kdec_scr[...], cdec_scr[...], sr_scr)

    gates = gates_ref[0].astype(F32)
    gnw = gnw_ref[...]
    rnw = rnw_ref[...]
    parts = []
    for h in range(GLA_HEADS):
        sl = slice(h * GLA_DV, (h + 1) * GLA_DV)
        oh = o_g[:, sl]
        parts.append(oh * lax.rsqrt(jnp.mean(oh * oh, axis=-1, keepdims=True) + EPS) * gnw[:, sl])
    for h in range(RET_HEADS):
        sl = slice(h * RET_DV, (h + 1) * RET_DV)
        oh = o_r[:, sl]
        mu = jnp.mean(oh, axis=-1, keepdims=True)
        dv = oh - mu
        parts.append(dv * lax.rsqrt(jnp.mean(dv * dv, axis=-1, keepdims=True) + EPS) * rnw[:, sl])
    mix = jnp.concatenate(parts, axis=1) * _silu(gates)
    mod = mod_ref[0]
    x1 = x_ref[0] + mod[2:3] * _dot(mix.astype(BF16), wout_ref[...])
    x1_ref[0] = x1
    h2 = _rms(x1, n2w_ref[...]) * (1.0 + mod[4:5]) + mod[3:4]
    h_hi, h_lo = _split(h2)
    h2_ref[0] = h_hi
    wrh = wrh_ref[...]
    logit = _dg(wrh, h_hi, _NT) + _dg(wrh, h_lo, _NT) + _dg(wrl_ref[...], h_hi, _NT)
    ex = jnp.exp(logit - jnp.max(logit, axis=0, keepdims=True))
    aff_ref[0] = ex / jnp.sum(ex, axis=0, keepdims=True)


def _bwd(x, o_f, gqkv, gates, rqkv, lab, modb, rlog, mall_b, lvl_b, sgb, srb, gnw, rnw, wout, n2w, wrh, wrl):
    bsz, t, d = x.shape
    c = TILE
    nt = t // c
    ne = wrh.shape[0]
    const = lambda shape: pl.BlockSpec(shape, lambda b, j: (0,) * len(shape))
    tile = lambda w: pl.BlockSpec((1, c, w), lambda b, j: (b, nt - 1 - j, 0))
    return pl.pallas_call(
        functools.partial(_bwd_kernel, c=c),
        out_shape=(jax.ShapeDtypeStruct((bsz, t, d), F32),
                   jax.ShapeDtypeStruct((bsz, t, d), BF16),
                   jax.ShapeDtypeStruct((bsz, ne, t), F32)),
        grid=(bsz, nt),
        in_specs=[tile(d), tile(o_f.shape[2]), tile(gqkv.shape[2]), tile(gates.shape[2]), tile(rqkv.shape[2]),
                  tile(lab.shape[2]),
                  pl.BlockSpec((1,) + modb.shape[1:], lambda b, j: (b, 0, 0)),
                  pl.BlockSpec((1,) + rlog.shape[1:], lambda b, j: (1, 0, 0)),
                  const(mall_b.shape), const(lvl_b.shape),
                  pl.BlockSpec((1, GLA_QK, GLA_DV), lambda b, j: (b, 0, 0)),
                  pl.BlockSpec((1, RET_HEADS, RET_DK, RET_DV), lambda b, j: (b, 0, 0, 0)),
                  const(gnw.shape), const(rnw.shape), const(wout.shape), const(n2w.shape),
                  const(wrh.shape), const(wrl.shape)],
        out_specs=(tile(d), tile(d), pl.BlockSpec((1, ne, c), lambda b, j: (b, 0, nt - 1 - j))),
        scratch_shapes=[pltpu.VMEM((GLA_QK, GLA_DV), F32),
                        pltpu.VMEM((RET_HEADS, RET_DK, RET_DV), F32),
                        pltpu.VMEM((RET_HEADS, c, c), F32),
                        pltpu.VMEM((c, RET_QK), F32),
                        pltpu.VMEM((c, RET_QK), F32),
                        pltpu.VMEM((1, RET_QK), F32)],
        compiler_params=pltpu.CompilerParams(dimension_semantics=("arbitrary", "arbitrary"),
                                             vmem_limit_bytes=VMEM_LIMIT),
        name="mixer_bwd",
    )(x, o_f, gqkv, gates, rqkv, lab, modb, rlog, mall_b, lvl_b, sgb, srb, gnw, rnw, wout, n2w, wrh, wrl)


def _route_kernel(aff_ref, pos_ref, off_ref, *, cap, nb):
    a = aff_ref[0]
    ne = a.shape[0]
    bits = lax.bitcast_convert_type(a, I32)
    kf = float(cap)

    def count(mask):
        return jnp.sum(jnp.sum(jnp.where(mask, 1.0, 0.0), axis=2, keepdims=True), axis=1, keepdims=True)

    def search(i, cur):
        cand = cur | jnp.left_shift(jnp.int32(1), 30 - i)
        return jnp.where(count(bits >= cand) >= kf, cand, cur)

    kth = lax.fori_loop(0, 31, search, jnp.zeros((ne, 1, 1), I32))
    gt = bits > kth
    eq = bits == kth
    need = kf - count(gt)

    upper = (lax.broadcasted_iota(I32, (LANES, LANES), 0) <= lax.broadcasted_iota(I32, (LANES, LANES), 1))
    upper = jnp.where(upper, 1.0, 0.0).astype(BF16)
    ones = jnp.ones((LANES, LANES), BF16)
    lower = (lax.broadcasted_iota(I32, (ne, nb, nb), 2) < lax.broadcasted_iota(I32, (ne, nb, nb), 1))
    lower = jnp.where(lower, 1.0, 0.0).astype(BF16)

    def excl_prefix(mask):
        m = jnp.where(mask, 1.0, 0.0)
        mb = m.astype(BF16).reshape(ne * nb, LANES)
        inc = _dot(mb, upper).reshape(ne, nb, LANES)
        tot = _dot(mb, ones).reshape(ne, nb, LANES)
        offs = lax.dot_general(lower, tot.astype(BF16), (((2,), (1,)), ((0,), (0,))), preferred_element_type=F32)
        return inc - m + offs, offs

    eq_rank, _ = excl_prefix(eq)
    sel = gt | (eq & (eq_rank < need))
    rank, offs = excl_prefix(sel)
    pos_ref[0] = jnp.where(sel, rank, -1.0).astype(I32)
    off_ref[0] = offs.astype(I32)


def _route(aff4, cap):
    bsz, ne, nb, _ = aff4.shape
    spec = pl.BlockSpec((1, ne, nb, LANES), lambda b: (b, 0, 0, 0))
    return pl.pallas_call(
        functools.partial(_route_kernel, cap=cap, nb=nb),
        out_shape=(jax.ShapeDtypeStruct(aff4.shape, I32), jax.ShapeDtypeStruct(aff4.shape, I32)),
        grid=(bsz,),
        in_specs=[spec],
        out_specs=(spec, spec),
        compiler_params=pltpu.CompilerParams(dimension_semantics=("arbitrary",), vmem_limit_bytes=VMEM_LIMIT),
        name="route",
    )(aff4)


def _tile_counts(cnt_ref, b, j, ne):
    m = cnt_ref[b, j, 0]
    for e in range(1, ne):
        m = jnp.maximum(m, cnt_ref[b, j, e])
    return m


def _window_select(rel, valid, val, ne):
    c = rel.shape[1]
    w = lax.broadcasted_iota(I32, (ne, WROWS, c), 1)
    relm = jnp.where(valid, rel, -1)
    sel = jnp.where(relm[:, None, :] == w, jnp.broadcast_to(val[:, None, :], (ne, WROWS, c)), 0.0)
    return sel.reshape(ne * WROWS, c)


def _round_slots(basev, cntv, r):
    start = basev + jnp.minimum(r * WIN, cntv)
    num = jnp.clip(cntv - r * WIN, 0, WIN)
    return start, num


def _round_slots_scalar(base, cnt, r):
    return base + jnp.minimum(r * WIN, cnt), jnp.clip(cnt - r * WIN, 0, WIN)


def _align_down(v):
    shift = ALIGN.bit_length() - 1
    return (v >> shift) << shift


def _gather_kernel(base_ref, cnt_ref, pos_ref, basev_ref, cntv_ref, h2_ref, xe_ref,
                   xbuf, xrows, carry, zbuf, sem, zsem, nissued, *, cap, ne):
    b = pl.program_id(0)
    j = pl.program_id(1)
    last_step = (b == pl.num_programs(0) - 1) & (j == pl.num_programs(1) - 1)

    def window_copy(slot, e, row0):
        return pltpu.make_async_copy(xbuf.at[slot, pl.ds(e * WROWS, WROWS)],
                                     xe_ref.at[b, e, pl.ds(row0, WROWS)], sem.at[slot, e])

    def wait_round(g):
        @pl.when(g >= 0)
        def _():
            for e in range(ne):
                window_copy(g % 2, e, 0).wait()

    @pl.when((b == 0) & (j == 0))
    def _start():
        nissued[0] = 0
        zbuf[...] = jnp.zeros(zbuf.shape, BF16)

    @pl.when(j == 0)
    def _start_sample():
        carry[...] = jnp.zeros(carry.shape, F32)
        cps = [pltpu.make_async_copy(zbuf, xe_ref.at[b, e, pl.ds(cap, WROWS)], zsem.at[e]) for e in range(ne)]
        for cp in cps:
            cp.start()
        for cp in cps:
            cp.wait()

    pos = pos_ref[0]
    basev = basev_ref[0, 0]
    cntv = cntv_ref[0, 0]
    h2 = h2_ref[0]
    ones = jnp.ones(pos.shape, F32)
    nrounds = (_tile_counts(cnt_ref, b, j, ne) + (WIN - 1)) // WIN

    def round_body(r, _):
        g = nissued[0]
        slot = g % 2
        start, num = _round_slots(basev, cntv, r)
        valid = (pos >= start) & (pos < start + num)
        onehot = _window_select(pos - _align_down(start), valid, ones, ne).astype(BF16)
        first = []
        for e in range(ne):
            s, n = _round_slots_scalar(base_ref[b, j, e], cnt_ref[b, j, e], r)
            first.append(pl.multiple_of(_align_down(s), ALIGN))
            nxt = pl.multiple_of(_align_down(s + n) - _align_down(s), ALIGN)
            rows = pl.ds(e * WROWS, WROWS)
            xrows[...] = _dot(onehot[e * WROWS:(e + 1) * WROWS], h2)
            xrows[pl.ds(0, ALIGN), :] += carry[pl.ds(e * ALIGN, ALIGN), :]
            xbuf[slot, rows, :] = xrows[...].astype(BF16)
            carry[pl.ds(e * ALIGN, ALIGN), :] = xrows[pl.ds(nxt, ALIGN), :]
        wait_round(g - 1)
        for e in range(ne):
            window_copy(slot, e, first[e]).start()
        nissued[0] = g + 1
        return 0

    lax.fori_loop(0, nrounds, round_body, 0)

    @pl.when(last_step)
    def _drain():
        wait_round(nissued[0] - 1)


def _gather(base, cnt, pos, basev, cntv, h2, cap):
    bsz, t, d = h2.shape
    ne = pos.shape[1]
    c = TILE
    nt = t // c
    grid_spec = pltpu.PrefetchScalarGridSpec(
        num_scalar_prefetch=2,
        grid=(bsz, nt),
        in_specs=[pl.BlockSpec((1, ne, c), lambda b, j, *_: (b, 0, j)),
                  pl.BlockSpec((1, 1, ne, c), lambda b, j, *_: (b, j, 0, 0)),
                  pl.BlockSpec((1, 1, ne, c), lambda b, j, *_: (b, j, 0, 0)),
                  pl.BlockSpec((1, c, d), lambda b, j, *_: (b, j, 0))],
        out_specs=pl.BlockSpec(memory_space=pl.ANY),
        scratch_shapes=[pltpu.VMEM((2, ne * WROWS, d), BF16), pltpu.VMEM((WROWS, d), F32),
                        pltpu.VMEM((ne * ALIGN, d), F32), pltpu.VMEM((WROWS, d), BF16),
                        pltpu.SemaphoreType.DMA((2, ne)), pltpu.SemaphoreType.DMA((ne,)),
                        pltpu.SMEM((1,), I32)],
    )
    return pl.pallas_call(
        functools.partial(_gather_kernel, cap=cap, ne=ne),
        out_shape=jax.ShapeDtypeStruct((bsz, ne, cap + WROWS, d), BF16),
        grid_spec=grid_spec,
        compiler_params=pltpu.CompilerParams(dimension_semantics=("arbitrary", "arbitrary"),
                                             vmem_limit_bytes=VMEM_LIMIT),
        name="moe_gather",
    )(base, cnt, pos, basev, cntv, h2)


def _expert_kernel(xe_ref, wg_ref, wu_ref, wd_ref, ye_ref):
    xb = xe_ref[0, 0]
    a = _dot(xb, wg_ref[0])
    u = _dot(xb, wu_ref[0])
    ye_ref[0, 0] = _dot((_silu(a) * u).astype(BF16), wd_ref[0]).astype(BF16)


def _experts(xe, wg, wu, wd, cap):
    bsz, ne, _, d = xe.shape
    ff = wg.shape[2]
    rows = min(512, cap)
    return pl.pallas_call(
        _expert_kernel,
        out_shape=jax.ShapeDtypeStruct((bsz, ne, cap, d), BF16),
        grid=(ne, bsz, cap // rows),
        in_specs=[pl.BlockSpec((1, 1, rows, d), lambda e, b, r: (b, e, r, 0)),
                  pl.BlockSpec((1, d, ff), lambda e, b, r: (e, 0, 0)),
                  pl.BlockSpec((1, d, ff), lambda e, b, r: (e, 0, 0)),
                  pl.BlockSpec((1, ff, d), lambda e, b, r: (e, 0, 0))],
        out_specs=pl.BlockSpec((1, 1, rows, d), lambda e, b, r: (b, e, r, 0)),
        compiler_params=pltpu.CompilerParams(dimension_semantics=("arbitrary", "arbitrary", "arbitrary"),
                                             vmem_limit_bytes=VMEM_LIMIT),
        name="moe_experts",
    )(xe, wg, wu, wd)


def _combine_kernel(base_ref, cnt_ref, pos_ref, aff_ref, basev_ref, cntv_ref, x1_ref, mod_ref, fnw_ref, ye_ref,
                    out_ref, stage, acc, sem, *, cap, ne):
    b = pl.program_id(0)
    j = pl.program_id(1)
    nt = pl.num_programs(1)
    step = b * nt + j
    pos = pos_ref[0]
    gate = aff_ref[0]
    basev = basev_ref[0, 0]
    cntv = cntv_ref[0, 0]
    last = cap - WROWS

    def fetch(slot, bb, jj, r):
        cps = []
        for e in range(ne):
            s, _n = _round_slots_scalar(base_ref[bb, jj, e], cnt_ref[bb, jj, e], r)
            row0 = pl.multiple_of(jnp.minimum(_align_down(s), last), ALIGN)
            cps.append(pltpu.make_async_copy(ye_ref.at[bb, e, pl.ds(row0, WROWS)],
                                             stage.at[slot, pl.ds(e * WROWS, WROWS)], sem.at[slot, e]))
        return cps

    def weights(r):
        start, num = _round_slots(basev, cntv, r)
        valid = (pos >= start) & (pos < start + num)
        return _split(_window_select(pos - jnp.minimum(_align_down(start), last), valid, gate, ne))

    def expand(w, slot):
        rows = stage[slot]
        return _dg(w[0], rows, _TN) + _dg(w[1], rows, _TN)

    @pl.when(step == 0)
    def _first():
        for cp in fetch(0, b, j, 0):
            cp.start()

    @pl.when(step + 1 < pl.num_programs(0) * nt)
    def _prefetch():
        wrap = j + 1 == nt
        for cp in fetch((step + 1) % 2, jnp.where(wrap, b + 1, b), jnp.where(wrap, 0, j + 1), 0):
            cp.start()

    w0 = weights(0)
    slot = step % 2
    for cp in fetch(slot, b, j, 0):
        cp.wait()
    acc[...] = expand(w0, slot)

    def round_body(r, _):
        cps = fetch(2, b, j, r)
        for cp in cps:
            cp.start()
        w = weights(r)
        for cp in cps:
            cp.wait()
        acc[...] += expand(w, 2)
        return 0

    nrounds = (_tile_counts(cnt_ref, b, j, ne) + (WIN - 1)) // WIN
    lax.fori_loop(1, nrounds, round_body, 0)
    mod = mod_ref[0]
    x2 = x1_ref[0] + mod[5:6] * acc[...]
    out_ref[0] = _rms(x2, fnw_ref[...])


def _combine(base, cnt, pos, aff, basev, cntv, x1, modb, fnw, ye, cap):
    bsz, t, d = x1.shape
    ne = pos.shape[1]
    c = TILE
    nt = t // c
    grid_spec = pltpu.PrefetchScalarGridSpec(
        num_scalar_prefetch=2,
        grid=(bsz, nt),
        in_specs=[pl.BlockSpec((1, ne, c), lambda b, j, *_: (b, 0, j)),
                  pl.BlockSpec((1, ne, c), lambda b, j, *_: (b, 0, j)),
                  pl.BlockSpec((1, 1, ne, c), lambda b, j, *_: (b, j, 0, 0)),
                  pl.BlockSpec((1, 1, ne, c), lambda b, j, *_: (b, j, 0, 0)),
                  pl.BlockSpec((1, c, d), lambda b, j, *_: (b, j, 0)),
                  pl.BlockSpec((1,) + modb.shape[1:], lambda b, j, *_: (b, 0, 0)),
                  pl.BlockSpec(fnw.shape, lambda b, j, *_: (0, 0)),
                  pl.BlockSpec(memory_space=pl.ANY)],
        out_specs=pl.BlockSpec((1, c, d), lambda b, j, *_: (b, j, 0)),
        scratch_shapes=[pltpu.VMEM((3, ne * WROWS, d), BF16), pltpu.VMEM((c, d), F32),
                        pltpu.SemaphoreType.DMA((3, ne))],
    )
    return pl.pallas_call(
        functools.partial(_combine_kernel, cap=cap, ne=ne),
        out_shape=jax.ShapeDtypeStruct((bsz, t, d), F32),
        grid_spec=grid_spec,
        compiler_params=pltpu.CompilerParams(dimension_semantics=("arbitrary", "arbitrary"),
                                             vmem_limit_bytes=VMEM_LIMIT),
        name="moe_combine",
    )(base, cnt, pos, aff, basev, cntv, x1, modb, fnw, ye)


def _rope_tables(t):
    rows = t // GRID_W
    row = jnp.broadcast_to(jnp.arange(rows)[:, None], (rows, GRID_W)).reshape(-1).astype(F32)
    col = jnp.broadcast_to(jnp.arange(GRID_W)[None, :], (rows, GRID_W)).reshape(-1).astype(F32)
    n_freq = RET_DK // 4
    inv = ROPE_BASE ** (-jnp.arange(n_freq, dtype=F32) / n_freq)
    ang = jnp.concatenate([row[:, None] * inv, col[:, None] * inv], axis=-1)
    cos = jnp.cos(ang)
    sin = jnp.sin(ang)
    return jnp.concatenate([cos, cos], axis=1), jnp.concatenate([-sin, sin], axis=1)


def _mixer_weights(w_in, gate_w, gate_b):
    pts = np.cumsum(IN_WIDTHS)[:-1]
    gq, gk, gv, gz, gg, rq, rk, rv, rg = jnp.split(w_in, [int(p) for p in pts], axis=1)
    gz = jnp.pad(gz, ((0, 0), (0, GZ_PAD - 2 * GLA_RANK)))
    wall = jnp.concatenate([gq, gk, gv, gg, rq, rk, rv, rg, gz], axis=1).astype(BF16)
    gmat = jnp.zeros((GZ_PAD, 2 * GLA_QK), F32)
    gmat = gmat.at[:GLA_RANK, :GLA_QK].set(gate_w[0]).at[GLA_RANK:2 * GLA_RANK, GLA_QK:].set(gate_w[1])
    ghi = gmat.astype(BF16)
    glo = (gmat - ghi.astype(F32)).astype(BF16)
    return wall, ghi, glo, gate_b.reshape(1, 2 * GLA_QK)


def kernel(x, c, ctx, c_ctx, w_ada, b_ada, norm1_w, w_in, gla_gate_w, gla_gate_b, ret_decay_logit, gla_norm_w,
           ret_norm_w, w_out, norm2_w, w_router, w_exp_gate, w_exp_up, w_exp_down, final_norm_w):
    bsz, t, d = x.shape
    depth = w_ada.shape[0]
    assert depth == 1 and t % TILE == 0 and ctx.shape[1] == TILE
    ne = w_router.shape[2]
    cap = EC_CAPACITY_FACTOR * t // ne
    assert cap >= WROWS and cap % 8 == 0 and cap % min(512, cap) == 0
    nt = t // TILE
    nb = t // LANES
    bpt = TILE // LANES

    cs = jnp.concatenate([c, c_ctx[None, :], jnp.zeros((8 - bsz - 1, d), F32)], axis=0)
    mod = _ada(cs, w_ada[0], b_ada[0][None, :])
    mod = jnp.pad(mod.reshape(8, N_ADA, d), ((0, 0), (0, 8 - N_ADA), (0, 0)))
    modb = mod[:bsz]
    modc = mod[bsz:bsz + 1]

    wall, ghi, glo, gb = _mixer_weights(w_in[0], gla_gate_w[0], gla_gate_b[0])
    n1w = norm1_w[0][None, :]
    rlog = jnp.broadcast_to(ret_decay_logit[0][:, :, None], (2, RET_HEADS, TILE)).astype(F32)
    mall_f = jnp.asarray(_level_matrices(TILE, False), BF16)
    mall_b = jnp.asarray(_level_matrices(TILE, True), BF16)
    lvl_f = jnp.asarray(_level_index(TILE, False))
    lvl_b = jnp.asarray(_level_index(TILE, True))
    cosf, sins = _rope_tables(t)

    sgf, sgb, srf, srb = _ctx_states(ctx, modc, n1w, wall, ghi, glo, gb, rlog, mall_f, mall_b)
    o_f, gqkv, gates, rqkv, lab = _fwd(x, modb, n1w, wall, ghi, glo, gb, cosf, sins, rlog, mall_f, lvl_f, sgf, srf)

    wr = w_router[0].T
    wrh = wr.astype(BF16)
    wrl = (wr - wrh.astype(F32)).astype(BF16)
    x1, h2, aff = _bwd(x, o_f, gqkv, gates, rqkv, lab, modb, rlog, mall_b, lvl_b, sgb, srb,
                       gla_norm_w[0][None, :], ret_norm_w[0][None, :], w_out[0].astype(BF16),
                       norm2_w[0][None, :], wrh, wrl)

    pos4, off4 = _route(aff.reshape(bsz, ne, nb, LANES), cap)
    pos = pos4.reshape(bsz, ne, t)
    boff = off4[:, :, :, 0]
    base = jnp.transpose(boff[:, :, ::bpt], (0, 2, 1))
    nxt = jnp.concatenate([base[:, 1:], jnp.full((bsz, 1, ne), cap, I32)], axis=1)
    cnt = nxt - base
    basev = jnp.broadcast_to(base[:, :, :, None], (bsz, nt, ne, TILE))
    cntv = jnp.broadcast_to(cnt[:, :, :, None], (bsz, nt, ne, TILE))

    xe = _gather(base, cnt, pos, basev, cntv, h2, cap)
    ye = _experts(xe, w_exp_gate[0].astype(BF16), w_exp_up[0].astype(BF16), w_exp_down[0].astype(BF16), cap)
    return _combine(base, cnt, pos, aff, basev, cntv, x1, modb, final_norm_w[None, :], ye, cap)
```

```python
import functools

import numpy as np
import jax
import jax.numpy as jnp
from jax import lax
from jax.experimental import pallas as pl
from jax.experimental.pallas import tpu as pltpu

F32 = jnp.float32
BF16 = jnp.bfloat16
I32 = jnp.int32

GLA_HEADS = 4
GLA_DK = 64
GLA_DV = 128
GLA_RANK = 16
GLA_TAU = 16.0
RET_HEADS = 4
RET_DK = 128
RET_DV = 128
GRID_W = 64
ROPE_BASE = 10000.0
N_EXPERTS = 16
EC_CAPACITY_FACTOR = 2
N_ADA = 6
EPS = 1e-6

GLA_QK = GLA_HEADS * GLA_DK
GLA_V = GLA_HEADS * GLA_DV
RET_QK = RET_HEADS * RET_DK
RET_V = RET_HEADS * RET_DV
IN_WIDTHS = (GLA_QK, GLA_QK, GLA_V, 2 * GLA_RANK, GLA_V, RET_QK, RET_QK, RET_V, RET_V)

LANES = 128
TILE = 256
GLA_CHUNK = 64
GLA_LEVELS = 6
WIN = 48
ALIGN = 16
WROWS = WIN + ALIGN
GZ_PAD = LANES
VMEM_LIMIT = 56 * 1024 * 1024

_NT = (((1,), (1,)), ((), ()))
_TN = (((0,), (0,)), ((), ()))


def _dot(a, b):
    return jnp.dot(a, b, preferred_element_type=F32)


def _dg(a, b, dims):
    return lax.dot_general(a, b, dims, preferred_element_type=F32)


def _split(a):
    hi = a.astype(BF16)
    lo = (a - hi.astype(F32)).astype(BF16)
    return hi, lo


def _logsig(x):
    return jnp.minimum(x, 0.0) - jnp.log(1.0 + jnp.exp(-jnp.abs(x)))


def _silu(x):
    return x / (1.0 + jnp.exp(-x))


def _rms(x, w):
    return x * lax.rsqrt(jnp.mean(x * x, axis=-1, keepdims=True) + EPS) * w


def _chunk_cumsum_matrix(c, reverse):
    i = np.arange(c)[:, None]
    t = np.arange(c)[None, :]
    same = (i // GLA_CHUNK) == (t // GLA_CHUNK)
    return (same & ((t >= i) if reverse else (t <= i))).astype(np.float32)


def _chunk_indicator(c):
    return (np.arange(c)[:, None] // GLA_CHUNK == np.arange(LANES)[None, :]).astype(np.float32)


def _level_index(reverse):
    i = np.arange(GLA_CHUNK)[:, None]
    j = np.arange(GLA_CHUNK)[None, :]
    x = i ^ j
    lvl = np.where(x > 0, np.floor(np.log2(np.maximum(x, 1))), -1).astype(np.int32)
    bad = (j < i) if reverse else (j > i)
    return np.tile(np.where(bad, 99, lvl).astype(np.int32), (1, GLA_HEADS))


def _head_block_mask():
    r = np.arange(GLA_QK)[:, None] // GLA_DK
    l = np.arange(GLA_V)[None, :] // GLA_DV
    return (r == l).astype(np.float32)


def _project(xn, wall_ref, ghi_ref, glo_ref, gb_ref):
    proj = _dot(xn.astype(BF16), wall_ref[...])
    o = 0
    out = []
    for w in (GLA_QK, GLA_QK, GLA_V, GLA_V, RET_QK, RET_QK, RET_V, RET_V, GZ_PAD):
        out.append(proj[:, o:o + w])
        o += w
    gq, gk, gv, gg, rq, rk, rv, rg, gz = out
    z_hi, z_lo = _split(gz)
    pre = _dot(z_hi, ghi_ref[...]) + _dot(z_lo, ghi_ref[...]) + _dot(z_hi, glo_ref[...]) + gb_ref[...]
    log_a = _logsig(pre) * (1.0 / GLA_TAU)
    return gq * (GLA_DK ** -0.5), gk, gv, gg, rq, rk * (RET_DK ** -0.5), rv, rg, log_a


def _rope(a, cos, sin):
    outs = []
    for h in range(RET_HEADS):
        ah = a[:, h * RET_DK:(h + 1) * RET_DK]
        outs.append(ah * cos + pltpu.roll(ah, RET_DK // 2, 1) * sin)
    return jnp.concatenate(outs, axis=1)


def _stack_heads(a):
    head = lax.broadcasted_iota(I32, a.shape, 1) >> 6
    zero = jnp.zeros_like(a)
    return jnp.concatenate([jnp.where(head == h, a, zero) for h in range(GLA_HEADS)], axis=0)


def _gate_sums(g):
    hi, lo = _split(g)
    return jnp.concatenate([hi, lo], axis=1)


def _level_log_decay(level, g, b, b_ref, row0, reverse):
    n = GLA_CHUNK
    row = lax.broadcasted_iota(I32, (n, GLA_QK), 0)
    upper = ((row >> level) & 1) == 1
    if level == 0:
        return jnp.where(upper, 0.0, g) if reverse else jnp.where(upper, g, 0.0)
    if level == 1:
        nxt = pltpu.roll(g, n - 1, 0)
        prv = pltpu.roll(g, 1, 0)
        r = row & 3
        if reverse:
            return jnp.where(r == 0, g + nxt, jnp.where(r == 1, g, jnp.where(r == 2, 0.0, prv)))
        return jnp.where(r == 0, nxt, jnp.where(r == 1, 0.0, jnp.where(r == 2, g, g + prv)))
    m = 1 << level
    anchors = [jnp.broadcast_to(b_ref[pl.ds(row0 + blk + (m if reverse else m - 1), 1), :], (2 * m, GLA_QK))
               for blk in range(0, n, 2 * m)]
    d = b - (jnp.concatenate(anchors, axis=0) if len(anchors) > 1 else anchors[0])
    return jnp.where(upper, -d, d) if reverse else jnp.where(upper, d, -d)


def _gla_tile(q, k, vb, g, cum_ref, ind_ref, lvl_ref, bdm_ref, s_ref, b_ref, reverse, want_out=True):
    c = k.shape[0]
    g2 = _gate_sums(g)
    r = _dot(cum_ref[...], g2)
    b_ref[...] = r[:, :GLA_QK] + r[:, GLA_QK:]
    cs = _dg(g2, ind_ref[...], _TN)
    tot = cs[:GLA_QK] + cs[GLA_QK:]
    bdm = bdm_ref[...]
    keep = bdm > 0
    lvl = lvl_ref[...] if want_out else None
    nchunk = c // GLA_CHUNK
    outs = [None] * nchunk
    for ci in (reversed(range(nchunk)) if reverse else range(nchunk)):
        row0 = ci * GLA_CHUNK
        rows = slice(row0, row0 + GLA_CHUNK)
        kc = k[rows]
        gc = g[rows]
        vc = vb[rows]
        bc = b_ref[pl.ds(row0, GLA_CHUNK), :]
        s_bd = s_ref[...]
        if want_out:
            qc = q[rows]
            scores = jnp.zeros((GLA_CHUNK, GLA_HEADS * GLA_CHUNK), F32)
            for level in range(GLA_LEVELS):
                e = jnp.exp(_level_log_decay(level, gc, bc, b_ref, row0, reverse))
                p = _dg((qc * e).astype(BF16), _stack_heads((kc * e).astype(BF16)), _NT)
                scores = jnp.where(lvl == level, p, scores)
            p = _dg(qc.astype(BF16), _stack_heads(kc.astype(BF16)), _NT)
            scores = jnp.where(lvl == -1, p, scores)
            v_bd = jnp.concatenate([vc] * GLA_HEADS, axis=0) * bdm
            outs[ci] = (_dot(scores.astype(BF16), v_bd)
                        + _dot((qc * jnp.exp(bc)).astype(BF16), s_bd.astype(BF16)))
        b_end = b_ref[pl.ds(row0 if reverse else row0 + GLA_CHUNK - 1, 1), :]
        kv = _dg((kc * jnp.exp(b_end - bc)).astype(BF16), vc, _TN)
        e_col = jnp.exp(jnp.broadcast_to(tot[:, ci:ci + 1], (GLA_QK, GLA_V)))
        s_ref[...] = e_col * s_bd + jnp.where(keep, kv, 0.0)
    return jnp.concatenate(outs, axis=0) if want_out else None


def _ret_decays(rlog_ref, c, reverse):
    lg = _logsig(rlog_ref[0])
    ii = lax.broadcasted_iota(I32, (c, c), 0)
    jj = lax.broadcasted_iota(I32, (c, c), 1)
    rel = ((jj - ii) if reverse else (ii - jj)).astype(F32)
    pos = lax.broadcasted_iota(I32, (c, RET_DK), 0).astype(F32)
    dmats, qd, kd, cd = [], [], [], []
    for h in range(RET_HEADS):
        lh = lg[h:h + 1, :]
        dmats.append(jnp.where(rel >= 0, jnp.exp(lh * jnp.maximum(rel, 0.0)), 0.0))
        l1 = lh[:, :RET_DK]
        qd.append(jnp.exp(l1 * ((c - pos) if reverse else (pos + 1.0))))
        kd.append(jnp.exp(l1 * (pos if reverse else (c - 1.0 - pos))))
        cd.append(jnp.exp(l1 * float(c)))
    return dmats, jnp.concatenate(qd, axis=1), jnp.concatenate(kd, axis=1), jnp.concatenate(cd, axis=1)


def _ret_tile(q, k, vb, dmat_ref, qdec, kdec, cdec, s_ref):
    outs = []
    for h in range(RET_HEADS):
        sl = slice(h * RET_DK, (h + 1) * RET_DK)
        vh = vb[:, h * RET_DV:(h + 1) * RET_DV]
        qh = q[:, sl]
        kh = k[:, sl]
        sc = _dg(qh.astype(BF16), kh.astype(BF16), _NT) * dmat_ref[h]
        s = s_ref[h]
        outs.append(_dot(sc.astype(BF16), vh) + _dot((qh * qdec[:, sl]).astype(BF16), s.astype(BF16)))
        s_ref[h] = cdec[:, sl] * s + _dg((kh * kdec[:, sl]).astype(BF16), vh, _TN)
    return jnp.concatenate(outs, axis=1)


def _ada_kernel(c_ref, w_ref, b_ref, o_ref):
    s_hi, s_lo = _split(_silu(c_ref[...]))
    w_hi, w_lo = _split(w_ref[...])
    o_ref[...] = _dot(s_hi, w_hi) + _dot(s_lo, w_hi) + _dot(s_hi, w_lo) + b_ref[...]


def _ada(cs, w, b):
    rows, d = cs.shape
    n = w.shape[1]
    tn = 1536
    return pl.pallas_call(
        _ada_kernel,
        out_shape=jax.ShapeDtypeStruct((rows, n), F32),
        grid=(n // tn,),
        in_specs=[pl.BlockSpec((rows, d), lambda i: (0, 0)),
                  pl.BlockSpec((d, tn), lambda i: (0, i)),
                  pl.BlockSpec((1, tn), lambda i: (0, i))],
        out_specs=pl.BlockSpec((rows, tn), lambda i: (0, i)),
        compiler_params=pltpu.CompilerParams(dimension_semantics=("arbitrary",), vmem_limit_bytes=VMEM_LIMIT),
        name="ada",
    )(cs, w, b)


def _ctx_kernel(ctx_ref, mod_ref, n1w_ref, wall_ref, ghi_ref, glo_ref, gb_ref, rlog_ref, cumf_ref, cumb_ref,
                ind_ref, bdm_ref, sgf_ref, sgb_ref, srf_ref, srb_ref, b_scr, *, c):
    mod = mod_ref[0]
    hc = _rms(ctx_ref[0], n1w_ref[...]) * (1.0 + mod[1:2]) + mod[0:1]
    _, gk, gv, _, _, rk, rv, _, log_a = _project(hc, wall_ref, ghi_ref, glo_ref, gb_ref)
    gvb = gv.astype(BF16)
    rvb = rv.astype(BF16)
    for d, (cum_ref, out_g, out_r) in enumerate(((cumf_ref, sgf_ref, srf_ref), (cumb_ref, sgb_ref, srb_ref))):
        out_g[0] = jnp.zeros((GLA_QK, GLA_V), F32)
        _gla_tile(None, gk, gvb, log_a[:, d * GLA_QK:(d + 1) * GLA_QK], cum_ref, ind_ref, None, bdm_ref,
                  out_g.at[0], b_scr, reverse=bool(d), want_out=False)
        _, _, kdec, _ = _ret_decays(rlog_ref.at[d:d + 1], c, reverse=bool(d))
        for h in range(RET_HEADS):
            sl = slice(h * RET_DK, (h + 1) * RET_DK)
            out_r[0, h] = _dg((rk[:, sl] * kdec[:, sl]).astype(BF16), rvb[:, h * RET_DV:(h + 1) * RET_DV], _TN)


def _ctx_states(ctx, modc, n1w, wall, ghi, glo, gb, rlog, cum_f, cum_b, ind, bdm):
    bsz, c, d = ctx.shape
    const = lambda a: pl.BlockSpec(a.shape, lambda b: (0,) * a.ndim)
    consts = (modc, n1w, wall, ghi, glo, gb, rlog, cum_f, cum_b, ind, bdm)
    return pl.pallas_call(
        functools.partial(_ctx_kernel, c=c),
        out_shape=(jax.ShapeDtypeStruct((bsz, GLA_QK, GLA_V), F32),
                   jax.ShapeDtypeStruct((bsz, GLA_QK, GLA_V), F32),
                   jax.ShapeDtypeStruct((bsz, RET_HEADS, RET_DK, RET_DV), F32),
                   jax.ShapeDtypeStruct((bsz, RET_HEADS, RET_DK, RET_DV), F32)),
        grid=(bsz,),
        in_specs=[pl.BlockSpec((1, c, d), lambda b: (b, 0, 0))] + [const(a) for a in consts],
        out_specs=(pl.BlockSpec((1, GLA_QK, GLA_V), lambda b: (b, 0, 0)),
                   pl.BlockSpec((1, GLA_QK, GLA_V), lambda b: (b, 0, 0)),
                   pl.BlockSpec((1, RET_HEADS, RET_DK, RET_DV), lambda b: (b, 0, 0, 0)),
                   pl.BlockSpec((1, RET_HEADS, RET_DK, RET_DV), lambda b: (b, 0, 0, 0))),
        scratch_shapes=[pltpu.VMEM((c, GLA_QK), F32)],
        compiler_params=pltpu.CompilerParams(dimension_semantics=("arbitrary",), vmem_limit_bytes=VMEM_LIMIT),
        name="ctx_states",
    )(ctx, *consts)


def _fwd_kernel(x_ref, mod_ref, n1w_ref, wall_ref, ghi_ref, glo_ref, gb_ref, cos_ref, sin_ref, rlog_ref,
                cum_ref, ind_ref, lvl_ref, bdm_ref, sg0_ref, sr0_ref,
                of_ref, gqkv_ref, gates_ref, rqkv_ref, lab_ref,
                sg_scr, sr_scr, dmat_scr, qdec_scr, kdec_scr, cdec_scr, b_scr, *, c):
    @pl.when(pl.program_id(1) == 0)
    def _init():
        sg_scr[...] = sg0_ref[0]
        sr_scr[...] = sr0_ref[0]
        dmats, qd, kd, cd = _ret_decays(rlog_ref, c, reverse=False)
        for h in range(RET_HEADS):
            dmat_scr[h] = dmats[h]
        qdec_scr[...] = qd
        kdec_scr[...] = kd
        cdec_scr[...] = cd

    mod = mod_ref[0]
    h = _rms(x_ref[0], n1w_ref[...]) * (1.0 + mod[1:2]) + mod[0:1]
    gq, gk, gv, gg, rq, rk, rv, rg, log_a = _project(h, wall_ref, ghi_ref, glo_ref, gb_ref)
    cos = cos_ref[...]
    sin = sin_ref[...]
    rq = _rope(rq, cos, sin)
    rk = _rope(rk, cos, sin)
    gvb = gv.astype(BF16)
    rvb = rv.astype(BF16)
    gqkv_ref[0] = jnp.concatenate([gq.astype(BF16), gk.astype(BF16), gvb], axis=1)
    gates_ref[0] = jnp.concatenate([gg, rg], axis=1).astype(BF16)
    rqkv_ref[0] = jnp.concatenate([rq.astype(BF16), rk.astype(BF16), rvb], axis=1)
    lab_ref[0] = log_a[:, GLA_QK:]
    o_g = _gla_tile(gq, gk, gvb, log_a[:, :GLA_QK], cum_ref, ind_ref, lvl_ref, bdm_ref, sg_scr, b_scr, False)
    o_r = _ret_tile(rq, rk, rvb, dmat_scr, qdec_scr[...], kdec_scr[...], cdec_scr[...], sr_scr)
    of_ref[0] = jnp.concatenate([o_g, o_r], axis=1)


def _fwd(x, modb, n1w, wall, ghi, glo, gb, cosf, sins, rlog, cum_f, ind, lvl_f, bdm, sgf, srf):
    bsz, t, d = x.shape
    c = TILE
    nt = t // c
    const = lambda shape: pl.BlockSpec(shape, lambda b, j: (0,) * len(shape))
    tile = lambda w: pl.BlockSpec((1, c, w), lambda b, j: (b, j, 0))
    mixw = GLA_V + RET_V
    return pl.pallas_call(
        functools.partial(_fwd_kernel, c=c),
        out_shape=(jax.ShapeDtypeStruct((bsz, t, mixw), F32),
                   jax.ShapeDtypeStruct((bsz, t, 2 * GLA_QK + GLA_V), BF16),
                   jax.ShapeDtypeStruct((bsz, t, GLA_V + RET_V), BF16),
                   jax.ShapeDtypeStruct((bsz, t, 2 * RET_QK + RET_V), BF16),
                   jax.ShapeDtypeStruct((bsz, t, GLA_QK), F32)),
        grid=(bsz, nt),
        in_specs=[tile(d),
                  pl.BlockSpec((1,) + modb.shape[1:], lambda b, j: (b, 0, 0)),
                  const(n1w.shape), const(wall.shape), const(ghi.shape), const(glo.shape), const(gb.shape),
                  pl.BlockSpec((c, RET_DK), lambda b, j: (j, 0)),
                  pl.BlockSpec((c, RET_DK), lambda b, j: (j, 0)),
                  pl.BlockSpec((1,) + rlog.shape[1:], lambda b, j: (0, 0, 0)),
                  const(cum_f.shape), const(ind.shape), const(lvl_f.shape), const(bdm.shape),
                  pl.BlockSpec((1, GLA_QK, GLA_V), lambda b, j: (b, 0, 0)),
                  pl.BlockSpec((1, RET_HEADS, RET_DK, RET_DV), lambda b, j: (b, 0, 0, 0))],
        out_specs=(tile(mixw), tile(2 * GLA_QK + GLA_V), tile(GLA_V + RET_V), tile(2 * RET_QK + RET_V),
                   tile(GLA_QK)),
        scratch_shapes=[pltpu.VMEM((GLA_QK, GLA_V), F32),
                        pltpu.VMEM((RET_HEADS, RET_DK, RET_DV), F32),
                        pltpu.VMEM((RET_HEADS, c, c), F32),
                        pltpu.VMEM((c, RET_QK), F32),
                        pltpu.VMEM((c, RET_QK), F32),
                        pltpu.VMEM((1, RET_QK), F32),
                        pltpu.VMEM((c, GLA_QK), F32)],
        compiler_params=pltpu.CompilerParams(dimension_semantics=("arbitrary", "arbitrary"),
                                             vmem_limit_bytes=VMEM_LIMIT),
        name="mixer_fwd",
    )(x, modb, n1w, wall, ghi, glo, gb, cosf, sins, rlog, cum_f, ind, lvl_f, bdm, sgf, srf)


def _bwd_kernel(x_ref, of_ref, gqkv_ref, gates_ref, rqkv_ref, lab_ref, mod_ref, rlog_ref,
                cum_ref, ind_ref, lvl_ref, bdm_ref,
                sg0_ref, sr0_ref, gnw_ref, rnw_ref, wout_ref, n2w_ref, wrh_ref, wrl_ref,
                x1_ref, h2_ref, aff_ref,
                sg_scr, sr_scr, dmat_scr, qdec_scr, kdec_scr, cdec_scr, b_scr, *, c):
    @pl.when(pl.program_id(1) == 0)
    def _init():
        sg_scr[...] = sg0_ref[0]
        sr_scr[...] = sr0_ref[0]
        dmats, qd, kd, cd = _ret_decays(rlog_ref, c, reverse=True)
        for h in range(RET_HEADS):
            dmat_scr[h] = dmats[h]
        qdec_scr[...] = qd
        kdec_scr[...] = kd
        cdec_scr[...] = cd

    gqkv = gqkv_ref[0]
    rqkv = rqkv_ref[0]
    gq = gqkv[:, :GLA_QK].astype(F32)
    gk = gqkv[:, GLA_QK:2 * GLA_QK].astype(F32)
    gvb = gqkv[:, 2 * GLA_QK:]
    rq = rqkv[:, :RET_QK].astype(F32)
    rk = rqkv[:, RET_QK:2 * RET_QK].astype(F32)
    rvb = rqkv[:, 2 * RET_QK:]
    o_f = of_ref[0]
    o_g = o_f[:, :GLA_V] + _gla_tile(gq, gk, gvb, lab_ref[0], cum_ref, ind_ref, lvl_ref, bdm_ref, sg_scr, b_scr,
                                     True)
    o_r = o_f[:, GLA_V:] + _ret_tile(rq, rk, rvb, dmat_scr, qdec_scr[...], kdec_scr[...], cdec_scr[...], sr_scr)

    gates = gates_ref[0].astype(F32)
    gnw = gnw_ref[...]
    rnw = rnw_ref[...]
    parts = []
    for h in range(GLA_HEADS):
        sl = slice(h * GLA_DV, (h + 1) * GLA_DV)
        oh = o_g[:, sl]
        parts.append(oh * lax.rsqrt(jnp.mean(oh * oh, axis=-1, keepdims=True) + EPS) * gnw[:, sl])
    for h in range(RET_HEADS):
        sl = slice(h * RET_DV, (h + 1) * RET_DV)
        oh = o_r[:, sl]
        mu = jnp.mean(oh, axis=-1, keepdims=True)
        dv = oh - mu
        parts.append(dv * lax.rsqrt(jnp.mean(dv * dv, axis=-1, keepdims=True) + EPS) * rnw[:, sl])
    mix = jnp.concatenate(parts, axis=1) * _silu(gates)
    mod = mod_ref[0]
    x1 = x_ref[0] + mod[2:3] * _dot(mix.astype(BF16), wout_ref[...])
    x1_ref[0] = x1
    h2 = _rms(x1, n2w_ref[...]) * (1.0 + mod[4:5]) + mod[3:4]
    h_hi, h_lo = _split(h2)
    h2_ref[0] = h_hi
    wrh = wrh_ref[...]
    logit = _dg(wrh, h_hi, _NT) + _dg(wrh, h_lo, _NT) + _dg(wrl_ref[...], h_hi, _NT)
    ex = jnp.exp(logit - jnp.max(logit, axis=0, keepdims=True))
    aff_ref[0] = ex / jnp.sum(ex, axis=0, keepdims=True)


def _bwd(x, o_f, gqkv, gates, rqkv, lab, modb, rlog, cum_b, ind, lvl_b, bdm, sgb, srb, gnw, rnw, wout, n2w,
         wrh, wrl):
    bsz, t, d = x.shape
    c = TILE
    nt = t // c
    ne = wrh.shape[0]
    const = lambda shape: pl.BlockSpec(shape, lambda b, j: (0,) * len(shape))
    tile = lambda w: pl.BlockSpec((1, c, w), lambda b, j: (b, nt - 1 - j, 0))
    return pl.pallas_call(
        functools.partial(_bwd_kernel, c=c),
        out_shape=(jax.ShapeDtypeStruct((bsz, t, d), F32),
                   jax.ShapeDtypeStruct((bsz, t, d), BF16),
                   jax.ShapeDtypeStruct((bsz, ne, t), F32)),
        grid=(bsz, nt),
        in_specs=[tile(d), tile(o_f.shape[2]), tile(gqkv.shape[2]), tile(gates.shape[2]), tile(rqkv.shape[2]),
                  tile(lab.shape[2]),
                  pl.BlockSpec((1,) + modb.shape[1:], lambda b, j: (b, 0, 0)),
                  pl.BlockSpec((1,) + rlog.shape[1:], lambda b, j: (1, 0, 0)),
                  const(cum_b.shape), const(ind.shape), const(lvl_b.shape), const(bdm.shape),
                  pl.BlockSpec((1, GLA_QK, GLA_V), lambda b, j: (b, 0, 0)),
                  pl.BlockSpec((1, RET_HEADS, RET_DK, RET_DV), lambda b, j: (b, 0, 0, 0)),
                  const(gnw.shape), const(rnw.shape), const(wout.shape), const(n2w.shape),
                  const(wrh.shape), const(wrl.shape)],
        out_specs=(tile(d), tile(d), pl.BlockSpec((1, ne, c), lambda b, j: (b, 0, nt - 1 - j))),
        scratch_shapes=[pltpu.VMEM((GLA_QK, GLA_V), F32),
                        pltpu.VMEM((RET_HEADS, RET_DK, RET_DV), F32),
                        pltpu.VMEM((RET_HEADS, c, c), F32),
                        pltpu.VMEM((c, RET_QK), F32),
                        pltpu.VMEM((c, RET_QK), F32),
                        pltpu.VMEM((1, RET_QK), F32),
                        pltpu.VMEM((c, GLA_QK), F32)],
        compiler_params=pltpu.CompilerParams(dimension_semantics=("arbitrary", "arbitrary"),
                                             vmem_limit_bytes=VMEM_LIMIT),
        name="mixer_bwd",
    )(x, o_f, gqkv, gates, rqkv, lab, modb, rlog, cum_b, ind, lvl_b, bdm, sgb, srb, gnw, rnw, wout, n2w, wrh, wrl)


def _route_kernel(aff_ref, pos_ref, off_ref, *, cap, nb):
    a = aff_ref[0]
    ne = a.shape[0]
    bits = lax.bitcast_convert_type(a, I32)
    kf = float(cap)

    def count(mask):
        return jnp.sum(jnp.sum(jnp.where(mask, 1.0, 0.0), axis=2, keepdims=True), axis=1, keepdims=True)

    def search(i, cur):
        cand = cur | jnp.left_shift(jnp.int32(1), 30 - i)
        return jnp.where(count(bits >= cand) >= kf, cand, cur)

    kth = lax.fori_loop(0, 31, search, jnp.zeros((ne, 1, 1), I32))
    gt = bits > kth
    eq = bits == kth
    need = kf - count(gt)

    upper = (lax.broadcasted_iota(I32, (LANES, LANES), 0) <= lax.broadcasted_iota(I32, (LANES, LANES), 1))
    upper = jnp.where(upper, 1.0, 0.0).astype(BF16)
    ones = jnp.ones((LANES, LANES), BF16)
    lower = (lax.broadcasted_iota(I32, (ne, nb, nb), 2) < lax.broadcasted_iota(I32, (ne, nb, nb), 1))
    lower = jnp.where(lower, 1.0, 0.0).astype(BF16)

    def excl_prefix(mask):
        m = jnp.where(mask, 1.0, 0.0)
        mb = m.astype(BF16).reshape(ne * nb, LANES)
        inc = _dot(mb, upper).reshape(ne, nb, LANES)
        tot = _dot(mb, ones).reshape(ne, nb, LANES)
        offs = lax.dot_general(lower, tot.astype(BF16), (((2,), (1,)), ((0,), (0,))), preferred_element_type=F32)
        return inc - m + offs, offs

    eq_rank, _ = excl_prefix(eq)
    sel = gt | (eq & (eq_rank < need))
    rank, offs = excl_prefix(sel)
    pos_ref[0] = jnp.where(sel, rank, -1.0).astype(I32)
    off_ref[0] = offs.astype(I32)


def _route(aff4, cap):
    bsz, ne, nb, _ = aff4.shape
    spec = pl.BlockSpec((1, ne, nb, LANES), lambda b: (b, 0, 0, 0))
    return pl.pallas_call(
        functools.partial(_route_kernel, cap=cap, nb=nb),
        out_shape=(jax.ShapeDtypeStruct(aff4.shape, I32), jax.ShapeDtypeStruct(aff4.shape, I32)),
        grid=(bsz,),
        in_specs=[spec],
        out_specs=(spec, spec),
        compiler_params=pltpu.CompilerParams(dimension_semantics=("arbitrary",), vmem_limit_bytes=VMEM_LIMIT),
        name="route",
    )(aff4)


def _tile_counts(cnt_ref, b, j, ne):
    m = cnt_ref[b, j, 0]
    for e in range(1, ne):
        m = jnp.maximum(m, cnt_ref[b, j, e])
    return m


def _window_select(rel, valid, val, ne):
    c = rel.shape[1]
    w = lax.broadcasted_iota(I32, (ne, WROWS, c), 1)
    relm = jnp.where(valid, rel, -1)
    sel = jnp.where(relm[:, None, :] == w, jnp.broadcast_to(val[:, None, :], (ne, WROWS, c)), 0.0)
    return sel.reshape(ne * WROWS, c)


def _round_slots(basev, cntv, r):
    start = basev + jnp.minimum(r * WIN, cntv)
    num = jnp.clip(cntv - r * WIN, 0, WIN)
    return start, num


def _round_slots_scalar(base, cnt, r):
    return base + jnp.minimum(r * WIN, cnt), jnp.clip(cnt - r * WIN, 0, WIN)


def _align_down(v):
    shift = ALIGN.bit_length() - 1
    return (v >> shift) << shift


def _gather_kernel(base_ref, cnt_ref, pos_ref, basev_ref, cntv_ref, h2_ref, xe_ref,
                   xbuf, xrows, carry, zbuf, sem, zsem, nissued, *, cap, ne):
    b = pl.program_id(0)
    j = pl.program_id(1)
    last_step = (b == pl.num_programs(0) - 1) & (j == pl.num_programs(1) - 1)

    def window_copy(slot, e, row0):
        return pltpu.make_async_copy(xbuf.at[slot, pl.ds(e * WROWS, WROWS)],
                                     xe_ref.at[b, e, pl.ds(row0, WROWS)], sem.at[slot, e])

    def wait_round(g):
        @pl.when(g >= 0)
        def _():
            for e in range(ne):
                window_copy(g % 2, e, 0).wait()

    @pl.when((b == 0) & (j == 0))
    def _start():
        nissued[0] = 0
        zbuf[...] = jnp.zeros(zbuf.shape, BF16)

    @pl.when(j == 0)
    def _start_sample():
        carry[...] = jnp.zeros(carry.shape, F32)
        cps = [pltpu.make_async_copy(zbuf, xe_ref.at[b, e, pl.ds(cap, WROWS)], zsem.at[e]) for e in range(ne)]
        for cp in cps:
            cp.start()
        for cp in cps:
            cp.wait()

    pos = pos_ref[0]
    basev = basev_ref[0, 0]
    cntv = cntv_ref[0, 0]
    h2 = h2_ref[0]
    ones = jnp.ones(pos.shape, F32)
    nrounds = (_tile_counts(cnt_ref, b, j, ne) + (WIN - 1)) // WIN

    def round_body(r, _):
        g = nissued[0]
        slot = g % 2
        start, num = _round_slots(basev, cntv, r)
        valid = (pos >= start) & (pos < start + num)
        onehot = _window_select(pos - _align_down(start), valid, ones, ne).astype(BF16)
        first = []
        for e in range(ne):
            s, n = _round_slots_scalar(base_ref[b, j, e], cnt_ref[b, j, e], r)
            first.append(pl.multiple_of(_align_down(s), ALIGN))
            nxt = pl.multiple_of(_align_down(s + n) - _align_down(s), ALIGN)
            rows = pl.ds(e * WROWS, WROWS)
            xrows[...] = _dot(onehot[e * WROWS:(e + 1) * WROWS], h2)
            xrows[pl.ds(0, ALIGN), :] += carry[pl.ds(e * ALIGN, ALIGN), :]
            xbuf[slot, rows, :] = xrows[...].astype(BF16)
            carry[pl.ds(e * ALIGN, ALIGN), :] = xrows[pl.ds(nxt, ALIGN), :]
        wait_round(g - 1)
        for e in range(ne):
            window_copy(slot, e, first[e]).start()
        nissued[0] = g + 1
        return 0

    lax.fori_loop(0, nrounds, round_body, 0)

    @pl.when(last_step)
    def _drain():
        wait_round(nissued[0] - 1)


def _gather(base, cnt, pos, basev, cntv, h2, cap):
    bsz, t, d = h2.shape
    ne = pos.shape[1]
    c = TILE
    nt = t // c
    grid_spec = pltpu.PrefetchScalarGridSpec(
        num_scalar_prefetch=2,
        grid=(bsz, nt),
        in_specs=[pl.BlockSpec((1, ne, c), lambda b, j, *_: (b, 0, j)),
                  pl.BlockSpec((1, 1, ne, c), lambda b, j, *_: (b, j, 0, 0)),
                  pl.BlockSpec((1, 1, ne, c), lambda b, j, *_: (b, j, 0, 0)),
                  pl.BlockSpec((1, c, d), lambda b, j, *_: (b, j, 0))],
        out_specs=pl.BlockSpec(memory_space=pl.ANY),
        scratch_shapes=[pltpu.VMEM((2, ne * WROWS, d), BF16), pltpu.VMEM((WROWS, d), F32),
                        pltpu.VMEM((ne * ALIGN, d), F32), pltpu.VMEM((WROWS, d), BF16),
                        pltpu.SemaphoreType.DMA((2, ne)), pltpu.SemaphoreType.DMA((ne,)),
                        pltpu.SMEM((1,), I32)],
    )
    return pl.pallas_call(
        functools.partial(_gather_kernel, cap=cap, ne=ne),
        out_shape=jax.ShapeDtypeStruct((bsz, ne, cap + WROWS, d), BF16),
        grid_spec=grid_spec,
        compiler_params=pltpu.CompilerParams(dimension_semantics=("arbitrary", "arbitrary"),
                                             vmem_limit_bytes=VMEM_LIMIT),
        name="moe_gather",
    )(base, cnt, pos, basev, cntv, h2)


def _expert_kernel(xe_ref, wg_ref, wu_ref, wd_ref, ye_ref):
    xb = xe_ref[0, 0]
    a = _dot(xb, wg_ref[0])
    u = _dot(xb, wu_ref[0])
    ye_ref[0, 0] = _dot((_silu(a) * u).astype(BF16), wd_ref[0]).astype(BF16)


def _experts(xe, wg, wu, wd, cap):
    bsz, ne, _, d = xe.shape
    ff = wg.shape[2]
    rows = min(512, cap)
    return pl.pallas_call(
        _expert_kernel,
        out_shape=jax.ShapeDtypeStruct((bsz, ne, cap, d), BF16),
        grid=(ne, bsz, cap // rows),
        in_specs=[pl.BlockSpec((1, 1, rows, d), lambda e, b, r: (b, e, r, 0)),
                  pl.BlockSpec((1, d, ff), lambda e, b, r: (e, 0, 0)),
                  pl.BlockSpec((1, d, ff), lambda e, b, r: (e, 0, 0)),
                  pl.BlockSpec((1, ff, d), lambda e, b, r: (e, 0, 0))],
        out_specs=pl.BlockSpec((1, 1, rows, d), lambda e, b, r: (b, e, r, 0)),
        compiler_params=pltpu.CompilerParams(dimension_semantics=("arbitrary", "arbitrary", "arbitrary"),
                                             vmem_limit_bytes=VMEM_LIMIT),
        name="moe_experts",
    )(xe, wg, wu, wd)


def _combine_kernel(base_ref, cnt_ref, pos_ref, aff_ref, basev_ref, cntv_ref, x1_ref, mod_ref, fnw_ref, ye_ref,
                    out_ref, stage, acc, sem, *, cap, ne):
    b = pl.program_id(0)
    j = pl.program_id(1)
    nt = pl.num_programs(1)
    step = b * nt + j
    pos = pos_ref[0]
    gate = aff_ref[0]
    basev = basev_ref[0, 0]
    cntv = cntv_ref[0, 0]
    last = cap - WROWS

    def fetch(slot, bb, jj, r):
        cps = []
        for e in range(ne):
            s, _n = _round_slots_scalar(base_ref[bb, jj, e], cnt_ref[bb, jj, e], r)
            row0 = pl.multiple_of(jnp.minimum(_align_down(s), last), ALIGN)
            cps.append(pltpu.make_async_copy(ye_ref.at[bb, e, pl.ds(row0, WROWS)],
                                             stage.at[slot, pl.ds(e * WROWS, WROWS)], sem.at[slot, e]))
        return cps

    def weights(r):
        start, num = _round_slots(basev, cntv, r)
        valid = (pos >= start) & (pos < start + num)
        return _split(_window_select(pos - jnp.minimum(_align_down(start), last), valid, gate, ne))

    def expand(w, slot):
        rows = stage[slot]
        return _dg(w[0], rows, _TN) + _dg(w[1], rows, _TN)

    @pl.when(step == 0)
    def _first():
        for cp in fetch(0, b, j, 0):
            cp.start()

    @pl.when(step + 1 < pl.num_programs(0) * nt)
    def _prefetch():
        wrap = j + 1 == nt
        for cp in fetch((step + 1) % 2, jnp.where(wrap, b + 1, b), jnp.where(wrap, 0, j + 1), 0):
            cp.start()

    w0 = weights(0)
    slot = step % 2
    for cp in fetch(slot, b, j, 0):
        cp.wait()
    acc[...] = expand(w0, slot)

    def round_body(r, _):
        cps = fetch(2, b, j, r)
        for cp in cps:
            cp.start()
        w = weights(r)
        for cp in cps:
            cp.wait()
        acc[...] += expand(w, 2)
        return 0

    nrounds = (_tile_counts(cnt_ref, b, j, ne) + (WIN - 1)) // WIN
    lax.fori_loop(1, nrounds, round_body, 0)
    mod = mod_ref[0]
    x2 = x1_ref[0] + mod[5:6] * acc[...]
    out_ref[0] = _rms(x2, fnw_ref[...])


def _combine(base, cnt, pos, aff, basev, cntv, x1, modb, fnw, ye, cap):
    bsz, t, d = x1.shape
    ne = pos.shape[1]
    c = TILE
    nt = t // c
    grid_spec = pltpu.PrefetchScalarGridSpec(
        num_scalar_prefetch=2,
        grid=(bsz, nt),
        in_specs=[pl.BlockSpec((1, ne, c), lambda b, j, *_: (b, 0, j)),
                  pl.BlockSpec((1, ne, c), lambda b, j, *_: (b, 0, j)),
                  pl.BlockSpec((1, 1, ne, c), lambda b, j, *_: (b, j, 0, 0)),
                  pl.BlockSpec((1, 1, ne, c), lambda b, j, *_: (b, j, 0, 0)),
                  pl.BlockSpec((1, c, d), lambda b, j, *_: (b, j, 0)),
                  pl.BlockSpec((1,) + modb.shape[1:], lambda b, j, *_: (b, 0, 0)),
                  pl.BlockSpec(fnw.shape, lambda b, j, *_: (0, 0)),
                  pl.BlockSpec(memory_space=pl.ANY)],
        out_specs=pl.BlockSpec((1, c, d), lambda b, j, *_: (b, j, 0)),
        scratch_shapes=[pltpu.VMEM((3, ne * WROWS, d), BF16), pltpu.VMEM((c, d), F32),
                        pltpu.SemaphoreType.DMA((3, ne))],
    )
    return pl.pallas_call(
        functools.partial(_combine_kernel, cap=cap, ne=ne),
        out_shape=jax.ShapeDtypeStruct((bsz, t, d), F32),
        grid_spec=grid_spec,
        compiler_params=pltpu.CompilerParams(dimension_semantics=("arbitrary", "arbitrary"),
                                             vmem_limit_bytes=VMEM_LIMIT),
        name="moe_combine",
    )(base, cnt, pos, aff, basev, cntv, x1, modb, fnw, ye)


def _rope_tables(t):
    rows = t // GRID_W
    row = jnp.broadcast_to(jnp.arange(rows)[:, None], (rows, GRID_W)).reshape(-1).astype(F32)
    col = jnp.broadcast_to(jnp.arange(GRID_W)[None, :], (rows, GRID_W)).reshape(-1).astype(F32)
    n_freq = RET_DK // 4
    inv = ROPE_BASE ** (-jnp.arange(n_freq, dtype=F32) / n_freq)
    ang = jnp.concatenate([row[:, None] * inv, col[:, None] * inv], axis=-1)
    cos = jnp.cos(ang)
    sin = jnp.sin(ang)
    return jnp.concatenate([cos, cos], axis=1), jnp.concatenate([-sin, sin], axis=1)


def _mixer_weights(w_in, gate_w, gate_b):
    pts = np.cumsum(IN_WIDTHS)[:-1]
    gq, gk, gv, gz, gg, rq, rk, rv, rg = jnp.split(w_in, [int(p) for p in pts], axis=1)
    gz = jnp.pad(gz, ((0, 0), (0, GZ_PAD - 2 * GLA_RANK)))
    wall = jnp.concatenate([gq, gk, gv, gg, rq, rk, rv, rg, gz], axis=1).astype(BF16)
    gmat = jnp.zeros((GZ_PAD, 2 * GLA_QK), F32)
    gmat = gmat.at[:GLA_RANK, :GLA_QK].set(gate_w[0]).at[GLA_RANK:2 * GLA_RANK, GLA_QK:].set(gate_w[1])
    ghi = gmat.astype(BF16)
    glo = (gmat - ghi.astype(F32)).astype(BF16)
    return wall, ghi, glo, gate_b.reshape(1, 2 * GLA_QK)


def kernel(x, c, ctx, c_ctx, w_ada, b_ada, norm1_w, w_in, gla_gate_w, gla_gate_b, ret_decay_logit, gla_norm_w,
           ret_norm_w, w_out, norm2_w, w_router, w_exp_gate, w_exp_up, w_exp_down, final_norm_w):
    bsz, t, d = x.shape
    depth = w_ada.shape[0]
    assert depth == 1 and t % TILE == 0 and ctx.shape[1] == TILE
    ne = w_router.shape[2]
    cap = EC_CAPACITY_FACTOR * t // ne
    assert cap >= WROWS and cap % 8 == 0 and cap % min(512, cap) == 0
    nt = t // TILE
    nb = t // LANES
    bpt = TILE // LANES

    cs = jnp.concatenate([c, c_ctx[None, :], jnp.zeros((8 - bsz - 1, d), F32)], axis=0)
    mod = _ada(cs, w_ada[0], b_ada[0][None, :])
    mod = jnp.pad(mod.reshape(8, N_ADA, d), ((0, 0), (0, 8 - N_ADA), (0, 0)))
    modb = mod[:bsz]
    modc = mod[bsz:bsz + 1]

    wall, ghi, glo, gb = _mixer_weights(w_in[0], gla_gate_w[0], gla_gate_b[0])
    n1w = norm1_w[0][None, :]
    rlog = jnp.broadcast_to(ret_decay_logit[0][:, :, None], (2, RET_HEADS, TILE)).astype(F32)
    cum_f = jnp.asarray(_chunk_cumsum_matrix(TILE, False), BF16)
    cum_b = jnp.asarray(_chunk_cumsum_matrix(TILE, True), BF16)
    ind = jnp.asarray(_chunk_indicator(TILE), BF16)
    lvl_f = jnp.asarray(_level_index(False))
    lvl_b = jnp.asarray(_level_index(True))
    bdm = jnp.asarray(_head_block_mask(), BF16)
    cosf, sins = _rope_tables(t)

    sgf, sgb, srf, srb = _ctx_states(ctx, modc, n1w, wall, ghi, glo, gb, rlog, cum_f, cum_b, ind, bdm)
    o_f, gqkv, gates, rqkv, lab = _fwd(x, modb, n1w, wall, ghi, glo, gb, cosf, sins, rlog, cum_f, ind, lvl_f, bdm,
                                       sgf, srf)

    wr = w_router[0].T
    wrh = wr.astype(BF16)
    wrl = (wr - wrh.astype(F32)).astype(BF16)
    x1, h2, aff = _bwd(x, o_f, gqkv, gates, rqkv, lab, modb, rlog, cum_b, ind, lvl_b, bdm, sgb, srb,
                       gla_norm_w[0][None, :], ret_norm_w[0][None, :], w_out[0].astype(BF16),
                       norm2_w[0][None, :], wrh, wrl)

    pos4, off4 = _route(aff.reshape(bsz, ne, nb, LANES), cap)
    pos = pos4.reshape(bsz, ne, t)
    boff = off4[:, :, :, 0]
    base = jnp.transpose(boff[:, :, ::bpt], (0, 2, 1))
    nxt = jnp.concatenate([base[:, 1:], jnp.full((bsz, 1, ne), cap, I32)], axis=1)
    cnt = nxt - base
    basev = jnp.broadcast_to(base[:, :, :, None], (bsz, nt, ne, TILE))
    cntv = jnp.broadcast_to(cnt[:, :, :, None], (bsz, nt, ne, TILE))

    xe = _gather(base, cnt, pos, basev, cntv, h2, cap)
    ye = _experts(xe, w_exp_gate[0].astype(BF16), w_exp_up[0].astype(BF16), w_exp_down[0].astype(BF16), cap)
    return _combine(base, cnt, pos, aff, basev, cntv, x1, modb, final_norm_w[None, :], ye, cap)
```

```python
import functools

import numpy as np
import jax
import jax.numpy as jnp
from jax import lax
from jax.experimental import pallas as pl
from jax.experimental.pallas import tpu as pltpu

F32 = jnp.float32
BF16 = jnp.bfloat16
I32 = jnp.int32

GLA_HEADS = 4
GLA_DK = 64
GLA_DV = 128
GLA_RANK = 16
GLA_TAU = 16.0
RET_HEADS = 4
RET_DK = 128
RET_DV = 128
GRID_W = 64
ROPE_BASE = 10000.0
N_EXPERTS = 16
EC_CAPACITY_FACTOR = 2
N_ADA = 6
EPS = 1e-6

GLA_QK = GLA_HEADS * GLA_DK
GLA_V = GLA_HEADS * GLA_DV
RET_QK = RET_HEADS * RET_DK
RET_V = RET_HEADS * RET_DV
IN_WIDTHS = (GLA_QK, GLA_QK, GLA_V, 2 * GLA_RANK, GLA_V, RET_QK, RET_QK, RET_V, RET_V)

LANES = 128
MXU_N = 256
TILE = 256
GLA_CHUNK = 64
GLA_LEVELS = 6
WIN = 48
ALIGN = 16
WROWS = WIN + ALIGN
GZ_PAD = LANES
VMEM_LIMIT = 56 * 1024 * 1024

_NT = (((1,), (1,)), ((), ()))
_TN = (((0,), (0,)), ((), ()))


def _dot(a, b):
    return jnp.dot(a, b, preferred_element_type=F32)


def _dg(a, b, dims):
    return lax.dot_general(a, b, dims, preferred_element_type=F32)


def _split(a):
    hi = a.astype(BF16)
    lo = (a - hi.astype(F32)).astype(BF16)
    return hi, lo


def _logsig(x):
    return jnp.minimum(x, 0.0) - jnp.log(1.0 + jnp.exp(-jnp.abs(x)))


def _silu(x):
    return x / (1.0 + jnp.exp(-x))


def _rms(x, w):
    return x * lax.rsqrt(jnp.mean(x * x, axis=-1, keepdims=True) + EPS) * w


def _chunk_cumsum_matrix(c, reverse):
    i = np.arange(c)[:, None]
    t = np.arange(c)[None, :]
    same = (i // GLA_CHUNK) == (t // GLA_CHUNK)
    return (same & ((t >= i) if reverse else (t <= i))).astype(np.float32)


def _chunk_indicator(c):
    return (np.arange(c)[:, None] // GLA_CHUNK == np.arange(LANES)[None, :]).astype(np.float32)


def _level_index(reverse):
    i = np.arange(GLA_CHUNK)[:, None]
    j = np.arange(GLA_CHUNK)[None, :]
    x = i ^ j
    lvl = np.where(x > 0, np.floor(np.log2(np.maximum(x, 1))), -1).astype(np.int32)
    bad = (j < i) if reverse else (j > i)
    return np.tile(np.where(bad, 99, lvl).astype(np.int32), (1, GLA_HEADS))


def _head_block_mask():
    r = np.arange(GLA_QK)[:, None] // GLA_DK
    l = np.arange(GLA_V)[None, :] // GLA_DV
    return (r == l).astype(np.float32)


def _project(xn, wall_ref, ghi_ref, glo_ref, gb_ref):
    proj = _dot(xn.astype(BF16), wall_ref[...])
    o = 0
    out = []
    for w in (GLA_QK, GLA_QK, GLA_V, GLA_V, RET_QK, RET_QK, RET_V, RET_V, GZ_PAD):
        out.append(proj[:, o:o + w])
        o += w
    gq, gk, gv, gg, rq, rk, rv, rg, gz = out
    z_hi, z_lo = _split(gz)
    pre = _dot(z_hi, ghi_ref[...]) + _dot(z_lo, ghi_ref[...]) + _dot(z_hi, glo_ref[...]) + gb_ref[...]
    log_a = _logsig(pre) * (1.0 / GLA_TAU)
    return gq * (GLA_DK ** -0.5), gk, gv, gg, rq, rk * (RET_DK ** -0.5), rv, rg, log_a


def _rope(a, cos, sin):
    outs = []
    for h in range(RET_HEADS):
        ah = a[:, h * RET_DK:(h + 1) * RET_DK]
        outs.append(ah * cos + pltpu.roll(ah, RET_DK // 2, 1) * sin)
    return jnp.concatenate(outs, axis=1)


def _stack_heads(a):
    head = lax.broadcasted_iota(I32, a.shape, 1) >> 6
    zero = jnp.zeros_like(a)
    return jnp.concatenate([jnp.where(head == h, a, zero) for h in range(GLA_HEADS)], axis=0)


def _gate_sums(g):
    hi, lo = _split(g)
    return jnp.concatenate([hi, lo], axis=1)


def _level_log_decay(level, g, b, b_ref, row0, reverse):
    n = GLA_CHUNK
    row = lax.broadcasted_iota(I32, (n, GLA_QK), 0)
    upper = ((row >> level) & 1) == 1
    if level == 0:
        return jnp.where(upper, 0.0, g) if reverse else jnp.where(upper, g, 0.0)
    if level == 1:
        nxt = pltpu.roll(g, n - 1, 0)
        prv = pltpu.roll(g, 1, 0)
        r = row & 3
        if reverse:
            return jnp.where(r == 0, g + nxt, jnp.where(r == 1, g, jnp.where(r == 2, 0.0, prv)))
        return jnp.where(r == 0, nxt, jnp.where(r == 1, 0.0, jnp.where(r == 2, g, g + prv)))
    m = 1 << level
    anchors = [jnp.broadcast_to(b_ref[pl.ds(row0 + blk + (m if reverse else m - 1), 1), :], (2 * m, GLA_QK))
               for blk in range(0, n, 2 * m)]
    d = b - (jnp.concatenate(anchors, axis=0) if len(anchors) > 1 else anchors[0])
    return jnp.where(upper, -d, d) if reverse else jnp.where(upper, d, -d)


def _interleave(*stages):
    order = sorted((span * (k + 0.5) / n, i) for i, (_, n, span) in enumerate(stages) for k in range(n))
    for _, i in order:
        next(stages[i][0])
    for gen, _, _ in stages:
        for _ in gen:
            raise AssertionError("stage has more pieces than declared")


def _gla_steps(qkv_ref, g_ref, cum_ref, ind_ref, lvl_ref, bdm_ref, s_ref, b_ref, reverse, emit):
    c = g_ref.shape[0]
    g2 = _gate_sums(g_ref[...])
    r = _dot(cum_ref[...], g2)
    b_ref[...] = r[:, :GLA_QK] + r[:, GLA_QK:]
    cs = _dg(g2, ind_ref[...], _TN)
    tot = cs[:GLA_QK] + cs[GLA_QK:]
    yield
    nchunk = c // GLA_CHUNK
    for ci in (reversed(range(nchunk)) if reverse else range(nchunk)):
        row0 = ci * GLA_CHUNK
        rows = pl.ds(row0, GLA_CHUNK)
        kc = qkv_ref[rows, GLA_QK:2 * GLA_QK].astype(F32)
        vc = qkv_ref[rows, 2 * GLA_QK:]
        gc = g_ref[rows, :]
        bc = b_ref[rows, :]
        bdm = bdm_ref[...]
        s_bd = s_ref[...]
        if emit is not None:
            qc = qkv_ref[rows, :GLA_QK].astype(F32)
            lvl = lvl_ref[...]
            scores = jnp.zeros((GLA_CHUNK, GLA_HEADS * GLA_CHUNK), F32)
            for level in range(GLA_LEVELS):
                e = jnp.exp(_level_log_decay(level, gc, bc, b_ref, row0, reverse))
                p = _dg((qc * e).astype(BF16), _stack_heads((kc * e).astype(BF16)), _NT)
                scores = jnp.where(lvl == level, p, scores)
            p = _dg(qc.astype(BF16), _stack_heads(kc.astype(BF16)), _NT)
            scores = jnp.where(lvl == -1, p, scores)
            v_bd = jnp.concatenate([vc] * GLA_HEADS, axis=0) * bdm
            emit(row0, _dot(scores.astype(BF16), v_bd) + _dot((qc * jnp.exp(bc)).astype(BF16), s_bd.astype(BF16)))
        b_end = b_ref[pl.ds(row0 if reverse else row0 + GLA_CHUNK - 1, 1), :]
        kv = _dg((kc * jnp.exp(b_end - bc)).astype(BF16), vc, _TN)
        e_col = jnp.exp(jnp.broadcast_to(tot[:, ci:ci + 1], (GLA_QK, GLA_V)))
        s_ref[...] = e_col * s_bd + jnp.where(bdm > 0, kv, 0.0)
        yield


def _ret_decays(rlog_ref, c, reverse):
    lg = _logsig(rlog_ref[0])
    ii = lax.broadcasted_iota(I32, (c, c), 0)
    jj = lax.broadcasted_iota(I32, (c, c), 1)
    rel = ((jj - ii) if reverse else (ii - jj)).astype(F32)
    pos = lax.broadcasted_iota(I32, (c, RET_DK), 0).astype(F32)
    dmats, qd, kd, cd = [], [], [], []
    for h in range(RET_HEADS):
        lh = lg[h:h + 1, :]
        dmats.append(jnp.where(rel >= 0, jnp.exp(lh * jnp.maximum(rel, 0.0)), 0.0))
        l1 = lh[:, :RET_DK]
        qd.append(jnp.exp(l1 * ((c - pos) if reverse else (pos + 1.0))))
        kd.append(jnp.exp(l1 * (pos if reverse else (c - 1.0 - pos))))
        cd.append(jnp.exp(l1 * float(c)))
    return dmats, jnp.concatenate(qd, axis=1), jnp.concatenate(kd, axis=1), jnp.concatenate(cd, axis=1)


def _ret_steps(qkv_ref, dmat_ref, qdec_ref, kdec_ref, cdec_ref, s_ref, emit):
    for h in range(RET_HEADS):
        sl = slice(h * RET_DK, (h + 1) * RET_DK)
        qb = qkv_ref[:, h * RET_DK:(h + 1) * RET_DK]
        kb = qkv_ref[:, RET_QK + h * RET_DK:RET_QK + (h + 1) * RET_DK]
        vh = qkv_ref[:, 2 * RET_QK + h * RET_DV:2 * RET_QK + (h + 1) * RET_DV]
        sc = _dg(qb, kb, _NT) * dmat_ref[h]
        s = s_ref[h]
        emit(h, _dot(sc.astype(BF16), vh) + _dot((qb.astype(F32) * qdec_ref[:, sl]).astype(BF16), s.astype(BF16)))
        s_ref[h] = cdec_ref[:, sl] * s + _dg((kb.astype(F32) * kdec_ref[:, sl]).astype(BF16), vh, _TN)
        yield


def _proj_steps(x_ref, mod_ref, n1w_ref, wall_ref, ghi_ref, glo_ref, gb_ref, cos_ref, sin_ref,
                gqkv_ref, gates_ref, rqkv_ref, lab_ref, nxt_g, nxt_r, nxt_l):
    mod = mod_ref[0]
    hb = (_rms(x_ref[0], n1w_ref[...]) * (1.0 + mod[1:2]) + mod[0:1]).astype(BF16)
    yield

    def cols(o, w):
        return _dot(hb, wall_ref[:, o:o + w])

    def put(val, o, out_ref, stage_ref):
        val = val.astype(BF16)
        out_ref[0, :, o:o + val.shape[1]] = val
        if stage_ref is not None:
            stage_ref[:, o:o + val.shape[1]] = val

    put(cols(0, GLA_QK) * (GLA_DK ** -0.5), 0, gqkv_ref, nxt_g)
    put(cols(GLA_QK, GLA_QK), GLA_QK, gqkv_ref, nxt_g)
    yield
    put(cols(2 * GLA_QK, GLA_V), 2 * GLA_QK, gqkv_ref, nxt_g)
    yield
    o = 2 * GLA_QK + GLA_V
    put(cols(o, GLA_V), 0, gates_ref, None)
    yield
    o += GLA_V
    cos = cos_ref[...]
    sin = sin_ref[...]
    put(_rope(cols(o, RET_QK), cos, sin), 0, rqkv_ref, nxt_r)
    yield
    o += RET_QK
    put(_rope(cols(o, RET_QK) * (RET_DK ** -0.5), cos, sin), RET_QK, rqkv_ref, nxt_r)
    yield
    o += RET_QK
    put(cols(o, RET_V), 2 * RET_QK, rqkv_ref, nxt_r)
    yield
    o += RET_V
    put(cols(o, RET_V), GLA_V, gates_ref, None)
    yield
    o += RET_V
    z_hi, z_lo = _split(cols(o, GZ_PAD))
    pre = _dot(z_hi, ghi_ref[...]) + _dot(z_lo, ghi_ref[...]) + _dot(z_hi, glo_ref[...]) + gb_ref[...]
    log_a = _logsig(pre) * (1.0 / GLA_TAU)
    nxt_l[...] = log_a[:, :GLA_QK]
    lab_ref[0] = log_a[:, GLA_QK:]
    yield


def _ada_kernel(c_ref, w_ref, b_ref, o_ref):
    s_hi, s_lo = _split(_silu(c_ref[...]))
    w_hi, w_lo = _split(w_ref[...])
    o_ref[...] = _dot(s_hi, w_hi) + _dot(s_lo, w_hi) + _dot(s_hi, w_lo) + b_ref[...]


def _ada(cs, w, b):
    rows, d = cs.shape
    n = w.shape[1]
    tn = 1536
    return pl.pallas_call(
        _ada_kernel,
        out_shape=jax.ShapeDtypeStruct((rows, n), F32),
        grid=(n // tn,),
        in_specs=[pl.BlockSpec((rows, d), lambda i: (0, 0)),
                  pl.BlockSpec((d, tn), lambda i: (0, i)),
                  pl.BlockSpec((1, tn), lambda i: (0, i))],
        out_specs=pl.BlockSpec((rows, tn), lambda i: (0, i)),
        compiler_params=pltpu.CompilerParams(dimension_semantics=("arbitrary",), vmem_limit_bytes=VMEM_LIMIT),
        name="ada",
    )(cs, w, b)


def _ctx_kernel(ctx_ref, mod_ref, n1w_ref, wall_ref, ghi_ref, glo_ref, gb_ref, rlog_ref, cumf_ref, cumb_ref,
                ind_ref, bdm_ref, sgf_ref, sgb_ref, srf_ref, srb_ref, b_scr, kv_scr, g_scr, *, c):
    mod = mod_ref[0]
    hc = _rms(ctx_ref[0], n1w_ref[...]) * (1.0 + mod[1:2]) + mod[0:1]
    _, gk, gv, _, _, rk, rv, _, log_a = _project(hc, wall_ref, ghi_ref, glo_ref, gb_ref)
    kv_scr[:, GLA_QK:] = jnp.concatenate([gk, gv], axis=1).astype(BF16)
    rvb = rv.astype(BF16)
    for d, (cum_ref, out_g, out_r) in enumerate(((cumf_ref, sgf_ref, srf_ref), (cumb_ref, sgb_ref, srb_ref))):
        out_g[0] = jnp.zeros((GLA_QK, GLA_V), F32)
        g_scr[...] = log_a[:, d * GLA_QK:(d + 1) * GLA_QK]
        _interleave((_gla_steps(kv_scr, g_scr, cum_ref, ind_ref, None, bdm_ref, out_g.at[0], b_scr, bool(d), None),
                     1 + c // GLA_CHUNK, 1.0))
        _, _, kdec, _ = _ret_decays(rlog_ref.at[d:d + 1], c, reverse=bool(d))
        for h in range(RET_HEADS):
            sl = slice(h * RET_DK, (h + 1) * RET_DK)
            out_r[0, h] = _dg((rk[:, sl] * kdec[:, sl]).astype(BF16), rvb[:, h * RET_DV:(h + 1) * RET_DV], _TN)


def _ctx_states(ctx, modc, n1w, wall, ghi, glo, gb, rlog, cum_f, cum_b, ind, bdm):
    bsz, c, d = ctx.shape
    const = lambda a: pl.BlockSpec(a.shape, lambda b: (0,) * a.ndim)
    consts = (modc, n1w, wall, ghi, glo, gb, rlog, cum_f, cum_b, ind, bdm)
    return pl.pallas_call(
        functools.partial(_ctx_kernel, c=c),
        out_shape=(jax.ShapeDtypeStruct((bsz, GLA_QK, GLA_V), F32),
                   jax.ShapeDtypeStruct((bsz, GLA_QK, GLA_V), F32),
                   jax.ShapeDtypeStruct((bsz, RET_HEADS, RET_DK, RET_DV), F32),
                   jax.ShapeDtypeStruct((bsz, RET_HEADS, RET_DK, RET_DV), F32)),
        grid=(bsz,),
        in_specs=[pl.BlockSpec((1, c, d), lambda b: (b, 0, 0))] + [const(a) for a in consts],
        out_specs=(pl.BlockSpec((1, GLA_QK, GLA_V), lambda b: (b, 0, 0)),
                   pl.BlockSpec((1, GLA_QK, GLA_V), lambda b: (b, 0, 0)),
                   pl.BlockSpec((1, RET_HEADS, RET_DK, RET_DV), lambda b: (b, 0, 0, 0)),
                   pl.BlockSpec((1, RET_HEADS, RET_DK, RET_DV), lambda b: (b, 0, 0, 0))),
        scratch_shapes=[pltpu.VMEM((c, GLA_QK), F32), pltpu.VMEM((c, 2 * GLA_QK + GLA_V), BF16),
                        pltpu.VMEM((c, GLA_QK), F32)],
        compiler_params=pltpu.CompilerParams(dimension_semantics=("arbitrary",), vmem_limit_bytes=VMEM_LIMIT),
        name="ctx_states",
    )(ctx, *consts)


def _fwd_kernel(x_ref, mod_ref, n1w_ref, wall_ref, ghi_ref, glo_ref, gb_ref, cos_ref, sin_ref, rlog_ref,
                cum_ref, ind_ref, lvl_ref, bdm_ref, sg0_ref, sr0_ref,
                of_ref, gqkv_ref, gates_ref, rqkv_ref, lab_ref,
                sg_scr, sr_scr, dmat_scr, qdec_scr, kdec_scr, cdec_scr, b_scr,
                cur_g, cur_r, cur_l, nxt_g, nxt_r, nxt_l, *, c):
    j = pl.program_id(1)

    @pl.when(j == 0)
    def _first():
        sg_scr[...] = jnp.zeros(sg_scr.shape, F32)
        sr_scr[...] = jnp.zeros(sr_scr.shape, F32)
        nxt_g[...] = jnp.zeros(nxt_g.shape, BF16)
        nxt_r[...] = jnp.zeros(nxt_r.shape, BF16)
        nxt_l[...] = jnp.zeros(nxt_l.shape, F32)
        dmats, qd, kd, cd = _ret_decays(rlog_ref, c, reverse=False)
        for h in range(RET_HEADS):
            dmat_scr[h] = dmats[h]
        qdec_scr[...] = qd
        kdec_scr[...] = kd
        cdec_scr[...] = cd

    @pl.when(j == 1)
    def _seed():
        sg_scr[...] = sg0_ref[0]
        sr_scr[...] = sr0_ref[0]

    cur_g[...] = nxt_g[...]
    cur_r[...] = nxt_r[...]
    cur_l[...] = nxt_l[...]

    def emit_gla(row0, out):
        of_ref[0, pl.ds(row0, GLA_CHUNK), 0:GLA_V] = out

    def emit_ret(h, out):
        of_ref[0, :, GLA_V + h * RET_DV:GLA_V + (h + 1) * RET_DV] = out

    gla = _gla_steps(cur_g, cur_l, cum_ref, ind_ref, lvl_ref, bdm_ref, sg_scr, b_scr, False, emit_gla)
    ret = _ret_steps(cur_r, dmat_scr, qdec_scr, kdec_scr, cdec_scr, sr_scr, emit_ret)
    proj = _proj_steps(x_ref, mod_ref, n1w_ref, wall_ref, ghi_ref, glo_ref, gb_ref, cos_ref, sin_ref,
                       gqkv_ref, gates_ref, rqkv_ref, lab_ref, nxt_g, nxt_r, nxt_l)
    _interleave((gla, 1 + c // GLA_CHUNK, 1.0), (ret, RET_HEADS, 1.0), (proj, 9, 1.0))


def _fwd(x, modb, n1w, wall, ghi, glo, gb, cosf, sins, rlog, cum_f, ind, lvl_f, bdm, sgf, srf):
    bsz, t, d = x.shape
    c = TILE
    nt = t // c
    const = lambda shape: pl.BlockSpec(shape, lambda b, j: (0,) * len(shape))
    proj_tile = lambda w: pl.BlockSpec((1, c, w), lambda b, j: (b, jnp.minimum(j, nt - 1), 0))
    scan_tile = lambda w: pl.BlockSpec((1, c, w), lambda b, j: (b, jnp.maximum(j - 1, 0), 0))
    rope_tile = pl.BlockSpec((c, RET_DK), lambda b, j: (jnp.minimum(j, nt - 1), 0))
    mixw = GLA_V + RET_V
    gw, rw = 2 * GLA_QK + GLA_V, 2 * RET_QK + RET_V
    return pl.pallas_call(
        functools.partial(_fwd_kernel, c=c),
        out_shape=(jax.ShapeDtypeStruct((bsz, t, mixw), F32),
                   jax.ShapeDtypeStruct((bsz, t, gw), BF16),
                   jax.ShapeDtypeStruct((bsz, t, GLA_V + RET_V), BF16),
                   jax.ShapeDtypeStruct((bsz, t, rw), BF16),
                   jax.ShapeDtypeStruct((bsz, t, GLA_QK), F32)),
        grid=(bsz, nt + 1),
        in_specs=[proj_tile(d),
                  pl.BlockSpec((1,) + modb.shape[1:], lambda b, j: (b, 0, 0)),
                  const(n1w.shape), const(wall.shape), const(ghi.shape), const(glo.shape), const(gb.shape),
                  rope_tile, rope_tile,
                  pl.BlockSpec((1,) + rlog.shape[1:], lambda b, j: (0, 0, 0)),
                  const(cum_f.shape), const(ind.shape), const(lvl_f.shape), const(bdm.shape),
                  pl.BlockSpec((1, GLA_QK, GLA_V), lambda b, j: (b, 0, 0)),
                  pl.BlockSpec((1, RET_HEADS, RET_DK, RET_DV), lambda b, j: (b, 0, 0, 0))],
        out_specs=(scan_tile(mixw), proj_tile(gw), proj_tile(GLA_V + RET_V), proj_tile(rw), proj_tile(GLA_QK)),
        scratch_shapes=[pltpu.VMEM((GLA_QK, GLA_V), F32),
                        pltpu.VMEM((RET_HEADS, RET_DK, RET_DV), F32),
                        pltpu.VMEM((RET_HEADS, c, c), F32),
                        pltpu.VMEM((c, RET_QK), F32),
                        pltpu.VMEM((c, RET_QK), F32),
                        pltpu.VMEM((1, RET_QK), F32),
                        pltpu.VMEM((c, GLA_QK), F32),
                        pltpu.VMEM((c, gw), BF16), pltpu.VMEM((c, rw), BF16), pltpu.VMEM((c, GLA_QK), F32),
                        pltpu.VMEM((c, gw), BF16), pltpu.VMEM((c, rw), BF16), pltpu.VMEM((c, GLA_QK), F32)],
        compiler_params=pltpu.CompilerParams(dimension_semantics=("arbitrary", "arbitrary"),
                                             vmem_limit_bytes=VMEM_LIMIT),
        name="mixer_fwd",
    )(x, modb, n1w, wall, ghi, glo, gb, cosf, sins, rlog, cum_f, ind, lvl_f, bdm, sgf, srf)


def _bwd_kernel(x_ref, of_ref, gqkv_ref, gates_ref, rqkv_ref, lab_ref, mod_ref, rlog_ref,
                cum_ref, ind_ref, lvl_ref, bdm_ref,
                sg0_ref, sr0_ref, gnw_ref, rnw_ref, wout_ref, n2w_ref, wrh_ref, wrl_ref,
                x1_ref, h2_ref, aff_ref,
                sg_scr, sr_scr, dmat_scr, qdec_scr, kdec_scr, cdec_scr, b_scr, cur_m, nxt_m, mixb, *, c):
    j = pl.program_id(1)

    @pl.when(j == 0)
    def _first():
        sg_scr[...] = sg0_ref[0]
        sr_scr[...] = sr0_ref[0]
        nxt_m[...] = jnp.zeros(nxt_m.shape, F32)
        dmats, qd, kd, cd = _ret_decays(rlog_ref, c, reverse=True)
        for h in range(RET_HEADS):
            dmat_scr[h] = dmats[h]
        qdec_scr[...] = qd
        kdec_scr[...] = kd
        cdec_scr[...] = cd

    cur_m[...] = nxt_m[...]

    def emit_gla(row0, out):
        rows = pl.ds(row0, GLA_CHUNK)
        nxt_m[rows, 0:GLA_V] = of_ref[0, rows, 0:GLA_V] + out

    def emit_ret(h, out):
        cols = slice(GLA_V + h * RET_DV, GLA_V + (h + 1) * RET_DV)
        nxt_m[:, cols] = of_ref[0, :, cols] + out

    gla = _gla_steps(gqkv_ref.at[0], lab_ref.at[0], cum_ref, ind_ref, lvl_ref, bdm_ref, sg_scr, b_scr, True,
                     emit_gla)
    ret = _ret_steps(rqkv_ref.at[0], dmat_scr, qdec_scr, kdec_scr, cdec_scr, sr_scr, emit_ret)

    def epilogue():
        for h in range(GLA_HEADS + RET_HEADS):
            sl = slice(h * GLA_DV, (h + 1) * GLA_DV)
            oh = cur_m[:, sl]
            if h < GLA_HEADS:
                y = oh * lax.rsqrt(jnp.mean(oh * oh, axis=-1, keepdims=True) + EPS) * gnw_ref[:, sl]
            else:
                dv = oh - jnp.mean(oh, axis=-1, keepdims=True)
                y = (dv * lax.rsqrt(jnp.mean(dv * dv, axis=-1, keepdims=True) + EPS)
                     * rnw_ref[:, h * RET_DV - GLA_V:(h + 1) * RET_DV - GLA_V])
            mixb[:, sl] = (y * _silu(gates_ref[0, :, sl].astype(F32))).astype(BF16)
            yield
        mod = mod_ref[0]
        d = x_ref.shape[2]
        step = d // 4
        for p in range(4):
            cs = slice(p * step, (p + 1) * step)
            x1_ref[0, :, cs] = x_ref[0, :, cs] + mod[2:3, cs] * _dot(mixb[...], wout_ref[:, cs])
            yield
        h2 = _rms(x1_ref[0], n2w_ref[...]) * (1.0 + mod[4:5]) + mod[3:4]
        h_hi, h_lo = _split(h2)
        h2_ref[0] = h_hi
        yield
        wrh = wrh_ref[...]
        logit = _dg(wrh, h_hi, _NT) + _dg(wrh, h_lo, _NT) + _dg(wrl_ref[...], h_hi, _NT)
        ex = jnp.exp(logit - jnp.max(logit, axis=0, keepdims=True))
        aff_ref[0] = ex / jnp.sum(ex, axis=0, keepdims=True)
        yield

    _interleave((gla, 1 + c // GLA_CHUNK, 1.0), (ret, RET_HEADS, 1.0),
                (epilogue(), GLA_HEADS + RET_HEADS + 6, 0.8))


def _bwd(x, o_f, gqkv, gates, rqkv, lab, modb, rlog, cum_b, ind, lvl_b, bdm, sgb, srb, gnw, rnw, wout, n2w,
         wrh, wrl):
    bsz, t, d = x.shape
    c = TILE
    nt = t // c
    ne = wrh.shape[0]
    const = lambda shape: pl.BlockSpec(shape, lambda b, j: (0,) * len(shape))
    scan_at = lambda j: nt - 1 - jnp.minimum(j, nt - 1)
    mix_at = lambda j: nt - 1 - jnp.maximum(j - 1, 0)
    scan_tile = lambda w: pl.BlockSpec((1, c, w), lambda b, j: (b, scan_at(j), 0))
    tile = lambda w: pl.BlockSpec((1, c, w), lambda b, j: (b, mix_at(j), 0))
    return pl.pallas_call(
        functools.partial(_bwd_kernel, c=c),
        out_shape=(jax.ShapeDtypeStruct((bsz, t, d), F32),
                   jax.ShapeDtypeStruct((bsz, t, d), BF16),
                   jax.ShapeDtypeStruct((bsz, ne, t), F32)),
        grid=(bsz, nt + 1),
        in_specs=[tile(d), scan_tile(o_f.shape[2]), scan_tile(gqkv.shape[2]), tile(gates.shape[2]),
                  scan_tile(rqkv.shape[2]), scan_tile(lab.shape[2]),
                  pl.BlockSpec((1,) + modb.shape[1:], lambda b, j: (b, 0, 0)),
                  pl.BlockSpec((1,) + rlog.shape[1:], lambda b, j: (1, 0, 0)),
                  const(cum_b.shape), const(ind.shape), const(lvl_b.shape), const(bdm.shape),
                  pl.BlockSpec((1, GLA_QK, GLA_V), lambda b, j: (b, 0, 0)),
                  pl.BlockSpec((1, RET_HEADS, RET_DK, RET_DV), lambda b, j: (b, 0, 0, 0)),
                  const(gnw.shape), const(rnw.shape), const(wout.shape), const(n2w.shape),
                  const(wrh.shape), const(wrl.shape)],
        out_specs=(tile(d), tile(d), pl.BlockSpec((1, ne, c), lambda b, j: (b, 0, mix_at(j)))),
        scratch_shapes=[pltpu.VMEM((GLA_QK, GLA_V), F32),
                        pltpu.VMEM((RET_HEADS, RET_DK, RET_DV), F32),
                        pltpu.VMEM((RET_HEADS, c, c), F32),
                        pltpu.VMEM((c, RET_QK), F32),
                        pltpu.VMEM((c, RET_QK), F32),
                        pltpu.VMEM((1, RET_QK), F32),
                        pltpu.VMEM((c, GLA_QK), F32),
                        pltpu.VMEM((c, GLA_V + RET_V), F32),
                        pltpu.VMEM((c, GLA_V + RET_V), F32),
                        pltpu.VMEM((c, GLA_V + RET_V), BF16)],
        compiler_params=pltpu.CompilerParams(dimension_semantics=("arbitrary", "arbitrary"),
                                             vmem_limit_bytes=VMEM_LIMIT),
        name="mixer_bwd",
    )(x, o_f, gqkv, gates, rqkv, lab, modb, rlog, cum_b, ind, lvl_b, bdm, sgb, srb, gnw, rnw, wout, n2w, wrh, wrl)


def _route_kernel(aff_ref, pos_ref, off_ref, *, cap, nb):
    a = aff_ref[0]
    ne = a.shape[0]
    bits = lax.bitcast_convert_type(a, I32)
    kf = float(cap)

    def count(mask):
        return jnp.sum(jnp.sum(jnp.where(mask, 1.0, 0.0), axis=2, keepdims=True), axis=1, keepdims=True)

    def search(i, cur):
        cand = cur | jnp.left_shift(jnp.int32(1), 30 - i)
        return jnp.where(count(bits >= cand) >= kf, cand, cur)

    kth = lax.fori_loop(0, 31, search, jnp.zeros((ne, 1, 1), I32))
    gt = bits > kth
    eq = bits == kth
    need = kf - count(gt)

    upper = (lax.broadcasted_iota(I32, (LANES, LANES), 0) <= lax.broadcasted_iota(I32, (LANES, LANES), 1))
    upper = jnp.where(upper, 1.0, 0.0).astype(BF16)
    ones = jnp.ones((LANES, LANES), BF16)
    lower = (lax.broadcasted_iota(I32, (ne, nb, nb), 2) < lax.broadcasted_iota(I32, (ne, nb, nb), 1))
    lower = jnp.where(lower, 1.0, 0.0).astype(BF16)

    def excl_prefix(mask):
        m = jnp.where(mask, 1.0, 0.0)
        mb = m.astype(BF16).reshape(ne * nb, LANES)
        inc = _dot(mb, upper).reshape(ne, nb, LANES)
        tot = _dot(mb, ones).reshape(ne, nb, LANES)
        offs = lax.dot_general(lower, tot.astype(BF16), (((2,), (1,)), ((0,), (0,))), preferred_element_type=F32)
        return inc - m + offs, offs

    eq_rank, _ = excl_prefix(eq)
    sel = gt | (eq & (eq_rank < need))
    rank, offs = excl_prefix(sel)
    pos_ref[0] = jnp.where(sel, rank, -1.0).astype(I32)
    off_ref[0] = offs.astype(I32)


def _route(aff4, cap):
    bsz, ne, nb, _ = aff4.shape
    spec = pl.BlockSpec((1, ne, nb, LANES), lambda b: (b, 0, 0, 0))
    return pl.pallas_call(
        functools.partial(_route_kernel, cap=cap, nb=nb),
        out_shape=(jax.ShapeDtypeStruct(aff4.shape, I32), jax.ShapeDtypeStruct(aff4.shape, I32)),
        grid=(bsz,),
        in_specs=[spec],
        out_specs=(spec, spec),
        compiler_params=pltpu.CompilerParams(dimension_semantics=("arbitrary",), vmem_limit_bytes=VMEM_LIMIT),
        name="route",
    )(aff4)


def _tile_counts(cnt_ref, b, j, ne):
    m = cnt_ref[b, j, 0]
    for e in range(1, ne):
        m = jnp.maximum(m, cnt_ref[b, j, e])
    return m


def _window_select(rel, valid, val, ne):
    c = rel.shape[1]
    w = lax.broadcasted_iota(I32, (ne, WROWS, c), 1)
    relm = jnp.where(valid, rel, -1)
    sel = jnp.where(relm[:, None, :] == w, jnp.broadcast_to(val[:, None, :], (ne, WROWS, c)), 0.0)
    return sel.reshape(ne * WROWS, c)


def _round_slots(basev, cntv, r):
    start = basev + jnp.minimum(r * WIN, cntv)
    num = jnp.clip(cntv - r * WIN, 0, WIN)
    return start, num


def _round_slots_scalar(base, cnt, r):
    return base + jnp.minimum(r * WIN, cnt), jnp.clip(cnt - r * WIN, 0, WIN)


def _align_down(v):
    shift = ALIGN.bit_length() - 1
    return (v >> shift) << shift


def _gather_kernel(base_ref, cnt_ref, pos_ref, basev_ref, cntv_ref, h2_ref, xe_ref,
                   xbuf, carry, zbuf, sem, zsem, nissued, *, cap, ne):
    b = pl.program_id(0)
    j = pl.program_id(1)
    last_step = (b == pl.num_programs(0) - 1) & (j == pl.num_programs(1) - 1)

    def window_copy(slot, e, row0):
        return pltpu.make_async_copy(xbuf.at[slot, pl.ds(e * WROWS, WROWS)],
                                     xe_ref.at[b, e, pl.ds(row0, WROWS)], sem.at[slot, e])

    def wait_round(g):
        @pl.when(g >= 0)
        def _():
            for e in range(ne):
                window_copy(g % 2, e, 0).wait()

    @pl.when((b == 0) & (j == 0))
    def _start():
        nissued[0] = 0
        zbuf[...] = jnp.zeros(zbuf.shape, BF16)

    @pl.when(j == 0)
    def _start_sample():
        carry[...] = jnp.zeros(carry.shape, BF16)
        cps = [pltpu.make_async_copy(zbuf, xe_ref.at[b, e, pl.ds(cap, WROWS)], zsem.at[e]) for e in range(ne)]
        for cp in cps:
            cp.start()
        for cp in cps:
            cp.wait()

    pos = pos_ref[0]
    basev = basev_ref[0, 0]
    cntv = cntv_ref[0, 0]
    h2 = h2_ref[0]
    ones = jnp.ones(pos.shape, F32)
    nrounds = (_tile_counts(cnt_ref, b, j, ne) + (WIN - 1)) // WIN

    def round_body(r, _):
        g = nissued[0]
        slot = g % 2
        start, num = _round_slots(basev, cntv, r)
        valid = (pos >= start) & (pos < start + num)
        onehot = _window_select(pos - _align_down(start), valid, ones, ne).astype(BF16)
        for col0 in range(0, h2.shape[1], MXU_N):
            xbuf[slot, :, col0:col0 + MXU_N] = _dot(onehot, h2[:, col0:col0 + MXU_N]).astype(BF16)
        first = []
        for e in range(ne):
            s, n = _round_slots_scalar(base_ref[b, j, e], cnt_ref[b, j, e], r)
            first.append(pl.multiple_of(_align_down(s), ALIGN))
            nxt = pl.multiple_of(_align_down(s + n) - _align_down(s), ALIGN)
            row0 = e * WROWS
            xbuf[slot, pl.ds(row0, ALIGN), :] += carry[pl.ds(e * ALIGN, ALIGN), :]
            carry[pl.ds(e * ALIGN, ALIGN), :] = xbuf[slot, pl.ds(pl.multiple_of(row0 + nxt, ALIGN), ALIGN), :]
        wait_round(g - 1)
        for e in range(ne):
            window_copy(slot, e, first[e]).start()
        nissued[0] = g + 1
        return 0

    lax.fori_loop(0, nrounds, round_body, 0)

    @pl.when(last_step)
    def _drain():
        wait_round(nissued[0] - 1)


def _gather(base, cnt, pos, basev, cntv, h2, cap):
    bsz, t, d = h2.shape
    ne = pos.shape[1]
    c = TILE
    nt = t // c
    grid_spec = pltpu.PrefetchScalarGridSpec(
        num_scalar_prefetch=2,
        grid=(bsz, nt),
        in_specs=[pl.BlockSpec((1, ne, c), lambda b, j, *_: (b, 0, j)),
                  pl.BlockSpec((1, 1, ne, c), lambda b, j, *_: (b, j, 0, 0)),
                  pl.BlockSpec((1, 1, ne, c), lambda b, j, *_: (b, j, 0, 0)),
                  pl.BlockSpec((1, c, d), lambda b, j, *_: (b, j, 0))],
        out_specs=pl.BlockSpec(memory_space=pl.ANY),
        scratch_shapes=[pltpu.VMEM((2, ne * WROWS, d), BF16),
                        pltpu.VMEM((ne * ALIGN, d), BF16), pltpu.VMEM((WROWS, d), BF16),
                        pltpu.SemaphoreType.DMA((2, ne)), pltpu.SemaphoreType.DMA((ne,)),
                        pltpu.SMEM((1,), I32)],
    )
    return pl.pallas_call(
        functools.partial(_gather_kernel, cap=cap, ne=ne),
        out_shape=jax.ShapeDtypeStruct((bsz, ne, cap + WROWS, d), BF16),
        grid_spec=grid_spec,
        compiler_params=pltpu.CompilerParams(dimension_semantics=("arbitrary", "arbitrary"),
                                             vmem_limit_bytes=VMEM_LIMIT),
        name="moe_gather",
    )(base, cnt, pos, basev, cntv, h2)


def _expert_kernel(xe_ref, wg_ref, wu_ref, wd_ref, ye_ref):
    xb = xe_ref[0, 0]
    a = _dot(xb, wg_ref[0])
    u = _dot(xb, wu_ref[0])
    ye_ref[0, 0] = _dot((_silu(a) * u).astype(BF16), wd_ref[0]).astype(BF16)


def _experts(xe, wg, wu, wd, cap):
    bsz, ne, _, d = xe.shape
    ff = wg.shape[2]
    rows = min(512, cap)
    return pl.pallas_call(
        _expert_kernel,
        out_shape=jax.ShapeDtypeStruct((bsz, ne, cap, d), BF16),
        grid=(ne, bsz, cap // rows),
        in_specs=[pl.BlockSpec((1, 1, rows, d), lambda e, b, r: (b, e, r, 0)),
                  pl.BlockSpec((1, d, ff), lambda e, b, r: (e, 0, 0)),
                  pl.BlockSpec((1, d, ff), lambda e, b, r: (e, 0, 0)),
                  pl.BlockSpec((1, ff, d), lambda e, b, r: (e, 0, 0))],
        out_specs=pl.BlockSpec((1, 1, rows, d), lambda e, b, r: (b, e, r, 0)),
        compiler_params=pltpu.CompilerParams(dimension_semantics=("arbitrary", "arbitrary", "arbitrary"),
                                             vmem_limit_bytes=VMEM_LIMIT),
        name="moe_experts",
    )(xe, wg, wu, wd)


def _combine_kernel(base_ref, cnt_ref, pos_ref, aff_ref, basev_ref, cntv_ref, x1_ref, mod_ref, fnw_ref, ye_ref,
                    out_ref, stage, acc, sem, *, cap, ne):
    b = pl.program_id(0)
    j = pl.program_id(1)
    nt = pl.num_programs(1)
    step = b * nt + j
    pos = pos_ref[0]
    gate = aff_ref[0]
    basev = basev_ref[0, 0]
    cntv = cntv_ref[0, 0]
    last = cap - WROWS

    def fetch(slot, bb, jj, r):
        cps = []
        for e in range(ne):
            s, _n = _round_slots_scalar(base_ref[bb, jj, e], cnt_ref[bb, jj, e], r)
            row0 = pl.multiple_of(jnp.minimum(_align_down(s), last), ALIGN)
            cps.append(pltpu.make_async_copy(ye_ref.at[bb, e, pl.ds(row0, WROWS)],
                                             stage.at[slot, pl.ds(e * WROWS, WROWS)], sem.at[slot, e]))
        return cps

    def weights(r):
        start, num = _round_slots(basev, cntv, r)
        valid = (pos >= start) & (pos < start + num)
        return _split(_window_select(pos - jnp.minimum(_align_down(start), last), valid, gate, ne))

    def expand(w, slot):
        rows = stage[slot]
        return _dg(w[0], rows, _TN) + _dg(w[1], rows, _TN)

    @pl.when(step == 0)
    def _first():
        for cp in fetch(0, b, j, 0):
            cp.start()

    @pl.when(step + 1 < pl.num_programs(0) * nt)
    def _prefetch():
        wrap = j + 1 == nt
        for cp in fetch((step + 1) % 2, jnp.where(wrap, b + 1, b), jnp.where(wrap, 0, j + 1), 0):
            cp.start()

    w0 = weights(0)
    slot = step % 2
    for cp in fetch(slot, b, j, 0):
        cp.wait()
    acc[...] = expand(w0, slot)

    def round_body(r, _):
        cps = fetch(2, b, j, r)
        for cp in cps:
            cp.start()
        w = weights(r)
        for cp in cps:
            cp.wait()
        acc[...] += expand(w, 2)
        return 0

    nrounds = (_tile_counts(cnt_ref, b, j, ne) + (WIN - 1)) // WIN
    lax.fori_loop(1, nrounds, round_body, 0)
    mod = mod_ref[0]
    x2 = x1_ref[0] + mod[5:6] * acc[...]
    out_ref[0] = _rms(x2, fnw_ref[...])


def _combine(base, cnt, pos, aff, basev, cntv, x1, modb, fnw, ye, cap):
    bsz, t, d = x1.shape
    ne = pos.shape[1]
    c = TILE
    nt = t // c
    grid_spec = pltpu.PrefetchScalarGridSpec(
        num_scalar_prefetch=2,
        grid=(bsz, nt),
        in_specs=[pl.BlockSpec((1, ne, c), lambda b, j, *_: (b, 0, j)),
                  pl.BlockSpec((1, ne, c), lambda b, j, *_: (b, 0, j)),
                  pl.BlockSpec((1, 1, ne, c), lambda b, j, *_: (b, j, 0, 0)),
                  pl.BlockSpec((1, 1, ne, c), lambda b, j, *_: (b, j, 0, 0)),
                  pl.BlockSpec((1, c, d), lambda b, j, *_: (b, j, 0)),
                  pl.BlockSpec((1,) + modb.shape[1:], lambda b, j, *_: (b, 0, 0)),
                  pl.BlockSpec(fnw.shape, lambda b, j, *_: (0, 0)),
                  pl.BlockSpec(memory_space=pl.ANY)],
        out_specs=pl.BlockSpec((1, c, d), lambda b, j, *_: (b, j, 0)),
        scratch_shapes=[pltpu.VMEM((3, ne * WROWS, d), BF16), pltpu.VMEM((c, d), F32),
                        pltpu.SemaphoreType.DMA((3, ne))],
    )
    return pl.pallas_call(
        functools.partial(_combine_kernel, cap=cap, ne=ne),
        out_shape=jax.ShapeDtypeStruct((bsz, t, d), F32),
        grid_spec=grid_spec,
        compiler_params=pltpu.CompilerParams(dimension_semantics=("arbitrary", "arbitrary"),
                                             vmem_limit_bytes=VMEM_LIMIT),
        name="moe_combine",
    )(base, cnt, pos, aff, basev, cntv, x1, modb, fnw, ye)


def _rope_tables(t):
    rows = t // GRID_W
    row = jnp.broadcast_to(jnp.arange(rows)[:, None], (rows, GRID_W)).reshape(-1).astype(F32)
    col = jnp.broadcast_to(jnp.arange(GRID_W)[None, :], (rows, GRID_W)).reshape(-1).astype(F32)
    n_freq = RET_DK // 4
    inv = ROPE_BASE ** (-jnp.arange(n_freq, dtype=F32) / n_freq)
    ang = jnp.concatenate([row[:, None] * inv, col[:, None] * inv], axis=-1)
    cos = jnp.cos(ang)
    sin = jnp.sin(ang)
    return jnp.concatenate([cos, cos], axis=1), jnp.concatenate([-sin, sin], axis=1)


def _mixer_weights(w_in, gate_w, gate_b):
    pts = np.cumsum(IN_WIDTHS)[:-1]
    gq, gk, gv, gz, gg, rq, rk, rv, rg = jnp.split(w_in, [int(p) for p in pts], axis=1)
    gz = jnp.pad(gz, ((0, 0), (0, GZ_PAD - 2 * GLA_RANK)))
    wall = jnp.concatenate([gq, gk, gv, gg, rq, rk, rv, rg, gz], axis=1).astype(BF16)
    gmat = jnp.zeros((GZ_PAD, 2 * GLA_QK), F32)
    gmat = gmat.at[:GLA_RANK, :GLA_QK].set(gate_w[0]).at[GLA_RANK:2 * GLA_RANK, GLA_QK:].set(gate_w[1])
    ghi = gmat.astype(BF16)
    glo = (gmat - ghi.astype(F32)).astype(BF16)
    return wall, ghi, glo, gate_b.reshape(1, 2 * GLA_QK)


def kernel(x, c, ctx, c_ctx, w_ada, b_ada, norm1_w, w_in, gla_gate_w, gla_gate_b, ret_decay_logit, gla_norm_w,
           ret_norm_w, w_out, norm2_w, w_router, w_exp_gate, w_exp_up, w_exp_down, final_norm_w):
    bsz, t, d = x.shape
    depth = w_ada.shape[0]
    assert depth == 1 and t % TILE == 0 and ctx.shape[1] == TILE
    ne = w_router.shape[2]
    cap = EC_CAPACITY_FACTOR * t // ne
    assert cap >= WROWS and cap % 8 == 0 and cap % min(512, cap) == 0
    nt = t // TILE
    nb = t // LANES
    bpt = TILE // LANES

    cs = jnp.concatenate([c, c_ctx[None, :], jnp.zeros((8 - bsz - 1, d), F32)], axis=0)
    mod = _ada(cs, w_ada[0], b_ada[0][None, :])
    mod = jnp.pad(mod.reshape(8, N_ADA, d), ((0, 0), (0, 8 - N_ADA), (0, 0)))
    modb = mod[:bsz]
    modc = mod[bsz:bsz + 1]

    wall, ghi, glo, gb = _mixer_weights(w_in[0], gla_gate_w[0], gla_gate_b[0])
    n1w = norm1_w[0][None, :]
    rlog = jnp.broadcast_to(ret_decay_logit[0][:, :, None], (2, RET_HEADS, TILE)).astype(F32)
    cum_f = jnp.asarray(_chunk_cumsum_matrix(TILE, False), BF16)
    cum_b = jnp.asarray(_chunk_cumsum_matrix(TILE, True), BF16)
    ind = jnp.asarray(_chunk_indicator(TILE), BF16)
    lvl_f = jnp.asarray(_level_index(False))
    lvl_b = jnp.asarray(_level_index(True))
    bdm = jnp.asarray(_head_block_mask(), BF16)
    cosf, sins = _rope_tables(t)

    sgf, sgb, srf, srb = _ctx_states(ctx, modc, n1w, wall, ghi, glo, gb, rlog, cum_f, cum_b, ind, bdm)
    o_f, gqkv, gates, rqkv, lab = _fwd(x, modb, n1w, wall, ghi, glo, gb, cosf, sins, rlog, cum_f, ind, lvl_f, bdm,
                                       sgf, srf)

    wr = w_router[0].T
    wrh = wr.astype(BF16)
    wrl = (wr - wrh.astype(F32)).astype(BF16)
    x1, h2, aff = _bwd(x, o_f, gqkv, gates, rqkv, lab, modb, rlog, cum_b, ind, lvl_b, bdm, sgb, srb,
                       gla_norm_w[0][None, :], ret_norm_w[0][None, :], w_out[0].astype(BF16),
                       norm2_w[0][None, :], wrh, wrl)

    pos4, off4 = _route(aff.reshape(bsz, ne, nb, LANES), cap)
    pos = pos4.reshape(bsz, ne, t)
    boff = off4[:, :, :, 0]
    base = jnp.transpose(boff[:, :, ::bpt], (0, 2, 1))
    nxt = jnp.concatenate([base[:, 1:], jnp.full((bsz, 1, ne), cap, I32)], axis=1)
    cnt = nxt - base
    basev = jnp.broadcast_to(base[:, :, :, None], (bsz, nt, ne, TILE))
    cntv = jnp.broadcast_to(cnt[:, :, :, None], (bsz, nt, ne, TILE))

    xe = _gather(base, cnt, pos, basev, cntv, h2, cap)
    ye = _experts(xe, w_exp_gate[0].astype(BF16), w_exp_up[0].astype(BF16), w_exp_down[0].astype(BF16), cap)
    return _combine(base, cnt, pos, aff, basev, cntv, x1, modb, final_norm_w[None, :], ye, cap)
```

```python
import functools

import numpy as np
import jax
import jax.numpy as jnp
from jax import lax
from jax.experimental import pallas as pl
from jax.experimental.pallas import tpu as pltpu

F32 = jnp.float32
BF16 = jnp.bfloat16
I32 = jnp.int32

GLA_HEADS = 4
GLA_DK = 64
GLA_DV = 128
GLA_RANK = 16
GLA_TAU = 16.0
RET_HEADS = 4
RET_DK = 128
RET_DV = 128
GRID_W = 64
ROPE_BASE = 10000.0
N_EXPERTS = 16
EC_CAPACITY_FACTOR = 2
N_ADA = 6
EPS = 1e-6

GLA_QK = GLA_HEADS * GLA_DK
GLA_V = GLA_HEADS * GLA_DV
RET_QK = RET_HEADS * RET_DK
RET_V = RET_HEADS * RET_DV
IN_WIDTHS = (GLA_QK, GLA_QK, GLA_V, 2 * GLA_RANK, GLA_V, RET_QK, RET_QK, RET_V, RET_V)

LANES = 128
MXU_N = 256
TILE = 256
GLA_CHUNK = 64
GLA_LEVELS = 6
WIN = 48
ALIGN = 16
WROWS = WIN + ALIGN
GZ_PAD = LANES
EXPERT_ROWS = 512
LN2 = float(np.log(2.0))
MIN_EXP = -149
EXP_STEPS = 8
MANTISSA_STEPS = 56
VMEM_LIMIT = 56 * 1024 * 1024

_NT = (((1,), (1,)), ((), ()))
_TN = (((0,), (0,)), ((), ()))


def _dot(a, b):
    return jnp.dot(a, b, preferred_element_type=F32)


def _dg(a, b, dims):
    return lax.dot_general(a, b, dims, preferred_element_type=F32)


def _split(a):
    hi = a.astype(BF16)
    lo = (a - hi.astype(F32)).astype(BF16)
    return hi, lo


def _logsig(x):
    return jnp.minimum(x, 0.0) - jnp.log(1.0 + jnp.exp(-jnp.abs(x)))


def _silu(x):
    return x / (1.0 + jnp.exp(-x))


def _rms(x, w):
    return x * lax.rsqrt(jnp.mean(x * x, axis=-1, keepdims=True) + EPS) * w


def _chunk_cumsum_matrix(c, reverse):
    i = np.arange(c)[:, None]
    t = np.arange(c)[None, :]
    same = (i // GLA_CHUNK) == (t // GLA_CHUNK)
    return (same & ((t >= i) if reverse else (t <= i))).astype(np.float32)


def _chunk_indicator(c):
    return (np.arange(c)[:, None] // GLA_CHUNK == np.arange(LANES)[None, :]).astype(np.float32)


def _level_index(reverse):
    i = np.arange(GLA_CHUNK)[:, None]
    j = np.arange(GLA_CHUNK)[None, :]
    x = i ^ j
    lvl = np.where(x > 0, np.floor(np.log2(np.maximum(x, 1))), -1).astype(np.int32)
    bad = (j < i) if reverse else (j > i)
    return np.tile(np.where(bad, 99, lvl).astype(np.int32), (1, GLA_HEADS))


def _head_block_mask():
    r = np.arange(GLA_QK)[:, None] // GLA_DK
    l = np.arange(GLA_V)[None, :] // GLA_DV
    return (r == l).astype(np.float32)


def _project(xn, wall_ref, ghi_ref, glo_ref, gb_ref):
    proj = _dot(xn.astype(BF16), wall_ref[...])
    o = 0
    out = []
    for w in (GLA_QK, GLA_QK, GLA_V, GLA_V, RET_QK, RET_QK, RET_V, RET_V, GZ_PAD):
        out.append(proj[:, o:o + w])
        o += w
    gq, gk, gv, gg, rq, rk, rv, rg, gz = out
    z_hi, z_lo = _split(gz)
    pre = _dot(z_hi, ghi_ref[...]) + _dot(z_lo, ghi_ref[...]) + _dot(z_hi, glo_ref[...]) + gb_ref[...]
    log_a = _logsig(pre) * (1.0 / GLA_TAU)
    return gq * (GLA_DK ** -0.5), gk, gv, gg, rq, rk * (RET_DK ** -0.5), rv, rg, log_a


def _rope(a, cos, sin):
    outs = []
    for h in range(RET_HEADS):
        ah = a[:, h * RET_DK:(h + 1) * RET_DK]
        outs.append(ah * cos + pltpu.roll(ah, RET_DK // 2, 1) * sin)
    return jnp.concatenate(outs, axis=1)


def _stack_heads(a):
    head = lax.broadcasted_iota(I32, a.shape, 1) >> 6
    zero = jnp.zeros_like(a)
    return jnp.concatenate([jnp.where(head == h, a, zero) for h in range(GLA_HEADS)], axis=0)


def _gate_sums(g):
    hi, lo = _split(g)
    return jnp.concatenate([hi, lo], axis=1)


def _level_log_decay(level, g, b, b_ref, row0, reverse):
    n = GLA_CHUNK
    row = lax.broadcasted_iota(I32, (n, GLA_QK), 0)
    upper = ((row >> level) & 1) == 1
    if level == 0:
        return jnp.where(upper, 0.0, g) if reverse else jnp.where(upper, g, 0.0)
    if level == 1:
        nxt = pltpu.roll(g, n - 1, 0)
        prv = pltpu.roll(g, 1, 0)
        r = row & 3
        if reverse:
            return jnp.where(r == 0, g + nxt, jnp.where(r == 1, g, jnp.where(r == 2, 0.0, prv)))
        return jnp.where(r == 0, nxt, jnp.where(r == 1, 0.0, jnp.where(r == 2, g, g + prv)))
    m = 1 << level
    anchors = [jnp.broadcast_to(b_ref[pl.ds(row0 + blk + (m if reverse else m - 1), 1), :], (2 * m, GLA_QK))
               for blk in range(0, n, 2 * m)]
    d = b - (jnp.concatenate(anchors, axis=0) if len(anchors) > 1 else anchors[0])
    return jnp.where(upper, -d, d) if reverse else jnp.where(upper, d, -d)


def _interleave(*stages):
    order = sorted((span * (k + 0.5) / n, i) for i, (_, n, span) in enumerate(stages) for k in range(n))
    for _, i in order:
        next(stages[i][0])
    for gen, _, _ in stages:
        for _ in gen:
            raise AssertionError("stage has more pieces than declared")


def _gla_steps(qkv_ref, g_ref, cum_ref, ind_ref, lvl_ref, bdm_ref, s_ref, b_ref, reverse, emit):
    c = g_ref.shape[0]
    g2 = _gate_sums(g_ref[...])
    r = _dot(cum_ref[...], g2)
    b_ref[...] = r[:, :GLA_QK] + r[:, GLA_QK:]
    cs = _dg(g2, ind_ref[...], _TN)
    tot = cs[:GLA_QK] + cs[GLA_QK:]
    yield
    nchunk = c // GLA_CHUNK
    for ci in (reversed(range(nchunk)) if reverse else range(nchunk)):
        row0 = ci * GLA_CHUNK
        rows = pl.ds(row0, GLA_CHUNK)
        kc = qkv_ref[rows, GLA_QK:2 * GLA_QK].astype(F32)
        vc = qkv_ref[rows, 2 * GLA_QK:]
        gc = g_ref[rows, :]
        bc = b_ref[rows, :]
        bdm = bdm_ref[...]
        s_bd = s_ref[...]
        if emit is not None:
            qc = qkv_ref[rows, :GLA_QK].astype(F32)
            lvl = lvl_ref[...]
            scores = jnp.zeros((GLA_CHUNK, GLA_HEADS * GLA_CHUNK), F32)
            for level in range(GLA_LEVELS):
                e = jnp.exp(_level_log_decay(level, gc, bc, b_ref, row0, reverse))
                p = _dg((qc * e).astype(BF16), _stack_heads((kc * e).astype(BF16)), _NT)
                scores = jnp.where(lvl == level, p, scores)
            p = _dg(qc.astype(BF16), _stack_heads(kc.astype(BF16)), _NT)
            scores = jnp.where(lvl == -1, p, scores)
            v_bd = jnp.concatenate([vc] * GLA_HEADS, axis=0) * bdm
            emit(row0, _dot(scores.astype(BF16), v_bd) + _dot((qc * jnp.exp(bc)).astype(BF16), s_bd.astype(BF16)))
        b_end = b_ref[pl.ds(row0 if reverse else row0 + GLA_CHUNK - 1, 1), :]
        kv = _dg((kc * jnp.exp(b_end - bc)).astype(BF16), vc, _TN)
        e_col = jnp.exp(jnp.broadcast_to(tot[:, ci:ci + 1], (GLA_QK, GLA_V)))
        s_ref[...] = e_col * s_bd + jnp.where(bdm > 0, kv, 0.0)
        yield


def _ret_decays(rlog_ref, c, reverse):
    lg = _logsig(rlog_ref[0])
    ii = lax.broadcasted_iota(I32, (c, c), 0)
    jj = lax.broadcasted_iota(I32, (c, c), 1)
    rel = ((jj - ii) if reverse else (ii - jj)).astype(F32)
    pos = lax.broadcasted_iota(I32, (c, RET_DK), 0).astype(F32)
    dmats, qd, kd, cd = [], [], [], []
    for h in range(RET_HEADS):
        lh = lg[h:h + 1, :]
        dmats.append(jnp.where(rel >= 0, jnp.exp(lh * jnp.maximum(rel, 0.0)), 0.0))
        l1 = lh[:, :RET_DK]
        qd.append(jnp.exp(l1 * ((c - pos) if reverse else (pos + 1.0))))
        kd.append(jnp.exp(l1 * (pos if reverse else (c - 1.0 - pos))))
        cd.append(jnp.exp(l1 * float(c)))
    return dmats, jnp.concatenate(qd, axis=1), jnp.concatenate(kd, axis=1), jnp.concatenate(cd, axis=1)


def _ret_steps(qkv_ref, dmat_ref, qdec_ref, kdec_ref, cdec_ref, s_ref, emit):
    for h in range(RET_HEADS):
        sl = slice(h * RET_DK, (h + 1) * RET_DK)
        qb = qkv_ref[:, h * RET_DK:(h + 1) * RET_DK]
        kb = qkv_ref[:, RET_QK + h * RET_DK:RET_QK + (h + 1) * RET_DK]
        vh = qkv_ref[:, 2 * RET_QK + h * RET_DV:2 * RET_QK + (h + 1) * RET_DV]
        sc = _dg(qb, kb, _NT) * dmat_ref[h]
        s = s_ref[h]
        emit(h, _dot(sc.astype(BF16), vh) + _dot((qb.astype(F32) * qdec_ref[:, sl]).astype(BF16), s.astype(BF16)))
        s_ref[h] = cdec_ref[:, sl] * s + _dg((kb.astype(F32) * kdec_ref[:, sl]).astype(BF16), vh, _TN)
        yield


def _proj_steps(x_ref, mod_ref, n1w_ref, wall_ref, ghi_ref, glo_ref, gb_ref, cos_ref, sin_ref,
                gqkv_ref, gates_ref, rqkv_ref, lab_ref, nxt_g, nxt_r, nxt_l):
    mod = mod_ref[0]
    hb = (_rms(x_ref[0], n1w_ref[...]) * (1.0 + mod[1:2]) + mod[0:1]).astype(BF16)
    yield

    def cols(o, w):
        return _dot(hb, wall_ref[:, o:o + w])

    def put(val, o, out_ref, stage_ref):
        val = val.astype(BF16)
        out_ref[0, :, o:o + val.shape[1]] = val
        if stage_ref is not None:
            stage_ref[:, o:o + val.shape[1]] = val

    put(cols(0, GLA_QK) * (GLA_DK ** -0.5), 0, gqkv_ref, nxt_g)
    put(cols(GLA_QK, GLA_QK), GLA_QK, gqkv_ref, nxt_g)
    yield
    put(cols(2 * GLA_QK, GLA_V), 2 * GLA_QK, gqkv_ref, nxt_g)
    yield
    o = 2 * GLA_QK + GLA_V
    put(cols(o, GLA_V), 0, gates_ref, None)
    yield
    o += GLA_V
    cos = cos_ref[...]
    sin = sin_ref[...]
    put(_rope(cols(o, RET_QK), cos, sin), 0, rqkv_ref, nxt_r)
    yield
    o += RET_QK
    put(_rope(cols(o, RET_QK) * (RET_DK ** -0.5), cos, sin), RET_QK, rqkv_ref, nxt_r)
    yield
    o += RET_QK
    put(cols(o, RET_V), 2 * RET_QK, rqkv_ref, nxt_r)
    yield
    o += RET_V
    put(cols(o, RET_V), GLA_V, gates_ref, None)
    yield
    o += RET_V
    z_hi, z_lo = _split(cols(o, GZ_PAD))
    pre = _dot(z_hi, ghi_ref[...]) + _dot(z_lo, ghi_ref[...]) + _dot(z_hi, glo_ref[...]) + gb_ref[...]
    log_a = _logsig(pre) * (1.0 / GLA_TAU)
    nxt_l[...] = log_a[:, :GLA_QK]
    lab_ref[0] = log_a[:, GLA_QK:]
    yield


def _ada_kernel(c_ref, w_ref, b_ref, o_ref):
    s_hi, s_lo = _split(_silu(c_ref[...]))
    w_hi, w_lo = _split(w_ref[...])
    o_ref[...] = _dot(s_hi, w_hi) + _dot(s_lo, w_hi) + _dot(s_hi, w_lo) + b_ref[...]


def _ada(cs, w, b):
    rows, d = cs.shape
    n = w.shape[1]
    tn = 1536
    return pl.pallas_call(
        _ada_kernel,
        out_shape=jax.ShapeDtypeStruct((rows, n), F32),
        grid=(n // tn,),
        in_specs=[pl.BlockSpec((rows, d), lambda i: (0, 0)),
                  pl.BlockSpec((d, tn), lambda i: (0, i)),
                  pl.BlockSpec((1, tn), lambda i: (0, i))],
        out_specs=pl.BlockSpec((rows, tn), lambda i: (0, i)),
        compiler_params=pltpu.CompilerParams(dimension_semantics=("arbitrary",), vmem_limit_bytes=VMEM_LIMIT),
        name="ada",
    )(cs, w, b)


def _ctx_kernel(ctx_ref, mod_ref, n1w_ref, wall_ref, ghi_ref, glo_ref, gb_ref, rlog_ref, cumf_ref, cumb_ref,
                ind_ref, bdm_ref, sgf_ref, sgb_ref, srf_ref, srb_ref, b_scr, kv_scr, g_scr, *, c):
    mod = mod_ref[0]
    hc = _rms(ctx_ref[0], n1w_ref[...]) * (1.0 + mod[1:2]) + mod[0:1]
    _, gk, gv, _, _, rk, rv, _, log_a = _project(hc, wall_ref, ghi_ref, glo_ref, gb_ref)
    kv_scr[:, GLA_QK:] = jnp.concatenate([gk, gv], axis=1).astype(BF16)
    rvb = rv.astype(BF16)
    for d, (cum_ref, out_g, out_r) in enumerate(((cumf_ref, sgf_ref, srf_ref), (cumb_ref, sgb_ref, srb_ref))):
        out_g[0] = jnp.zeros((GLA_QK, GLA_V), F32)
        g_scr[...] = log_a[:, d * GLA_QK:(d + 1) * GLA_QK]
        _interleave((_gla_steps(kv_scr, g_scr, cum_ref, ind_ref, None, bdm_ref, out_g.at[0], b_scr, bool(d), None),
                     1 + c // GLA_CHUNK, 1.0))
        _, _, kdec, _ = _ret_decays(rlog_ref.at[d:d + 1], c, reverse=bool(d))
        for h in range(RET_HEADS):
            sl = slice(h * RET_DK, (h + 1) * RET_DK)
            out_r[0, h] = _dg((rk[:, sl] * kdec[:, sl]).astype(BF16), rvb[:, h * RET_DV:(h + 1) * RET_DV], _TN)


def _ctx_states(ctx, modc, n1w, wall, ghi, glo, gb, rlog, cum_f, cum_b, ind, bdm):
    bsz, c, d = ctx.shape
    const = lambda a: pl.BlockSpec(a.shape, lambda b: (0,) * a.ndim)
    consts = (modc, n1w, wall, ghi, glo, gb, rlog, cum_f, cum_b, ind, bdm)
    return pl.pallas_call(
        functools.partial(_ctx_kernel, c=c),
        out_shape=(jax.ShapeDtypeStruct((bsz, GLA_QK, GLA_V), F32),
                   jax.ShapeDtypeStruct((bsz, GLA_QK, GLA_V), F32),
                   jax.ShapeDtypeStruct((bsz, RET_HEADS, RET_DK, RET_DV), F32),
                   jax.ShapeDtypeStruct((bsz, RET_HEADS, RET_DK, RET_DV), F32)),
        grid=(bsz,),
        in_specs=[pl.BlockSpec((1, c, d), lambda b: (b, 0, 0))] + [const(a) for a in consts],
        out_specs=(pl.BlockSpec((1, GLA_QK, GLA_V), lambda b: (b, 0, 0)),
                   pl.BlockSpec((1, GLA_QK, GLA_V), lambda b: (b, 0, 0)),
                   pl.BlockSpec((1, RET_HEADS, RET_DK, RET_DV), lambda b: (b, 0, 0, 0)),
                   pl.BlockSpec((1, RET_HEADS, RET_DK, RET_DV), lambda b: (b, 0, 0, 0))),
        scratch_shapes=[pltpu.VMEM((c, GLA_QK), F32), pltpu.VMEM((c, 2 * GLA_QK + GLA_V), BF16),
                        pltpu.VMEM((c, GLA_QK), F32)],
        compiler_params=pltpu.CompilerParams(dimension_semantics=("arbitrary",), vmem_limit_bytes=VMEM_LIMIT),
        name="ctx_states",
    )(ctx, *consts)


def _fwd_kernel(x_ref, mod_ref, n1w_ref, wall_ref, ghi_ref, glo_ref, gb_ref, cos_ref, sin_ref, rlog_ref,
                cum_ref, ind_ref, lvl_ref, bdm_ref, sg0_ref, sr0_ref,
                of_ref, gqkv_ref, gates_ref, rqkv_ref, lab_ref,
                sg_scr, sr_scr, dmat_scr, qdec_scr, kdec_scr, cdec_scr, b_scr,
                cur_g, cur_r, cur_l, nxt_g, nxt_r, nxt_l, *, c):
    j = pl.program_id(1)

    @pl.when(j == 0)
    def _first():
        sg_scr[...] = jnp.zeros(sg_scr.shape, F32)
        sr_scr[...] = jnp.zeros(sr_scr.shape, F32)
        nxt_g[...] = jnp.zeros(nxt_g.shape, BF16)
        nxt_r[...] = jnp.zeros(nxt_r.shape, BF16)
        nxt_l[...] = jnp.zeros(nxt_l.shape, F32)
        dmats, qd, kd, cd = _ret_decays(rlog_ref, c, reverse=False)
        for h in range(RET_HEADS):
            dmat_scr[h] = dmats[h]
        qdec_scr[...] = qd
        kdec_scr[...] = kd
        cdec_scr[...] = cd

    @pl.when(j == 1)
    def _seed():
        sg_scr[...] = sg0_ref[0]
        sr_scr[...] = sr0_ref[0]

    cur_g[...] = nxt_g[...]
    cur_r[...] = nxt_r[...]
    cur_l[...] = nxt_l[...]

    def emit_gla(row0, out):
        of_ref[0, pl.ds(row0, GLA_CHUNK), 0:GLA_V] = out

    def emit_ret(h, out):
        of_ref[0, :, GLA_V + h * RET_DV:GLA_V + (h + 1) * RET_DV] = out

    gla = _gla_steps(cur_g, cur_l, cum_ref, ind_ref, lvl_ref, bdm_ref, sg_scr, b_scr, False, emit_gla)
    ret = _ret_steps(cur_r, dmat_scr, qdec_scr, kdec_scr, cdec_scr, sr_scr, emit_ret)
    proj = _proj_steps(x_ref, mod_ref, n1w_ref, wall_ref, ghi_ref, glo_ref, gb_ref, cos_ref, sin_ref,
                       gqkv_ref, gates_ref, rqkv_ref, lab_ref, nxt_g, nxt_r, nxt_l)
    _interleave((gla, 1 + c // GLA_CHUNK, 1.0), (ret, RET_HEADS, 1.0), (proj, 9, 1.0))


def _fwd(x, modb, n1w, wall, ghi, glo, gb, cosf, sins, rlog, cum_f, ind, lvl_f, bdm, sgf, srf):
    bsz, t, d = x.shape
    c = TILE
    nt = t // c
    const = lambda shape: pl.BlockSpec(shape, lambda b, j: (0,) * len(shape))
    proj_tile = lambda w: pl.BlockSpec((1, c, w), lambda b, j: (b, jnp.minimum(j, nt - 1), 0))
    scan_tile = lambda w: pl.BlockSpec((1, c, w), lambda b, j: (b, jnp.maximum(j - 1, 0), 0))
    rope_tile = pl.BlockSpec((c, RET_DK), lambda b, j: (jnp.minimum(j, nt - 1), 0))
    mixw = GLA_V + RET_V
    gw, rw = 2 * GLA_QK + GLA_V, 2 * RET_QK + RET_V
    return pl.pallas_call(
        functools.partial(_fwd_kernel, c=c),
        out_shape=(jax.ShapeDtypeStruct((bsz, t, mixw), F32),
                   jax.ShapeDtypeStruct((bsz, t, gw), BF16),
                   jax.ShapeDtypeStruct((bsz, t, GLA_V + RET_V), BF16),
                   jax.ShapeDtypeStruct((bsz, t, rw), BF16),
                   jax.ShapeDtypeStruct((bsz, t, GLA_QK), F32)),
        grid=(bsz, nt + 1),
        in_specs=[proj_tile(d),
                  pl.BlockSpec((1,) + modb.shape[1:], lambda b, j: (b, 0, 0)),
                  const(n1w.shape), const(wall.shape), const(ghi.shape), const(glo.shape), const(gb.shape),
                  rope_tile, rope_tile,
                  pl.BlockSpec((1,) + rlog.shape[1:], lambda b, j: (0, 0, 0)),
                  const(cum_f.shape), const(ind.shape), const(lvl_f.shape), const(bdm.shape),
                  pl.BlockSpec((1, GLA_QK, GLA_V), lambda b, j: (b, 0, 0)),
                  pl.BlockSpec((1, RET_HEADS, RET_DK, RET_DV), lambda b, j: (b, 0, 0, 0))],
        out_specs=(scan_tile(mixw), proj_tile(gw), proj_tile(GLA_V + RET_V), proj_tile(rw), proj_tile(GLA_QK)),
        scratch_shapes=[pltpu.VMEM((GLA_QK, GLA_V), F32),
                        pltpu.VMEM((RET_HEADS, RET_DK, RET_DV), F32),
                        pltpu.VMEM((RET_HEADS, c, c), F32),
                        pltpu.VMEM((c, RET_QK), F32),
                        pltpu.VMEM((c, RET_QK), F32),
                        pltpu.VMEM((1, RET_QK), F32),
                        pltpu.VMEM((c, GLA_QK), F32),
                        pltpu.VMEM((c, gw), BF16), pltpu.VMEM((c, rw), BF16), pltpu.VMEM((c, GLA_QK), F32),
                        pltpu.VMEM((c, gw), BF16), pltpu.VMEM((c, rw), BF16), pltpu.VMEM((c, GLA_QK), F32)],
        compiler_params=pltpu.CompilerParams(dimension_semantics=("arbitrary", "arbitrary"),
                                             vmem_limit_bytes=VMEM_LIMIT),
        name="mixer_fwd",
    )(x, modb, n1w, wall, ghi, glo, gb, cosf, sins, rlog, cum_f, ind, lvl_f, bdm, sgf, srf)


def _bwd_kernel(x_ref, of_ref, gqkv_ref, gates_ref, rqkv_ref, lab_ref, mod_ref, rlog_ref,
                cum_ref, ind_ref, lvl_ref, bdm_ref,
                sg0_ref, sr0_ref, gnw_ref, rnw_ref, wout_ref, n2w_ref, wrh_ref, wrl_ref,
                x1_ref, h2_ref, aff_ref,
                sg_scr, sr_scr, dmat_scr, qdec_scr, kdec_scr, cdec_scr, b_scr, cur_m, nxt_m, mixb, *, c):
    j = pl.program_id(1)

    @pl.when(j == 0)
    def _first():
        sg_scr[...] = sg0_ref[0]
        sr_scr[...] = sr0_ref[0]
        nxt_m[...] = jnp.zeros(nxt_m.shape, F32)
        dmats, qd, kd, cd = _ret_decays(rlog_ref, c, reverse=True)
        for h in range(RET_HEADS):
            dmat_scr[h] = dmats[h]
        qdec_scr[...] = qd
        kdec_scr[...] = kd
        cdec_scr[...] = cd

    cur_m[...] = nxt_m[...]

    def emit_gla(row0, out):
        rows = pl.ds(row0, GLA_CHUNK)
        nxt_m[rows, 0:GLA_V] = of_ref[0, rows, 0:GLA_V] + out

    def emit_ret(h, out):
        cols = slice(GLA_V + h * RET_DV, GLA_V + (h + 1) * RET_DV)
        nxt_m[:, cols] = of_ref[0, :, cols] + out

    gla = _gla_steps(gqkv_ref.at[0], lab_ref.at[0], cum_ref, ind_ref, lvl_ref, bdm_ref, sg_scr, b_scr, True,
                     emit_gla)
    ret = _ret_steps(rqkv_ref.at[0], dmat_scr, qdec_scr, kdec_scr, cdec_scr, sr_scr, emit_ret)

    def epilogue():
        for h in range(GLA_HEADS + RET_HEADS):
            sl = slice(h * GLA_DV, (h + 1) * GLA_DV)
            oh = cur_m[:, sl]
            if h < GLA_HEADS:
                y = oh * lax.rsqrt(jnp.mean(oh * oh, axis=-1, keepdims=True) + EPS) * gnw_ref[:, sl]
            else:
                dv = oh - jnp.mean(oh, axis=-1, keepdims=True)
                y = (dv * lax.rsqrt(jnp.mean(dv * dv, axis=-1, keepdims=True) + EPS)
                     * rnw_ref[:, h * RET_DV - GLA_V:(h + 1) * RET_DV - GLA_V])
            mixb[:, sl] = (y * _silu(gates_ref[0, :, sl].astype(F32))).astype(BF16)
            yield
        mod = mod_ref[0]
        d = x_ref.shape[2]
        step = d // 4
        for p in range(4):
            cs = slice(p * step, (p + 1) * step)
            x1_ref[0, :, cs] = x_ref[0, :, cs] + mod[2:3, cs] * _dot(mixb[...], wout_ref[:, cs])
            yield
        h2 = _rms(x1_ref[0], n2w_ref[...]) * (1.0 + mod[4:5]) + mod[3:4]
        h_hi, h_lo = _split(h2)
        h2_ref[0] = h_hi
        yield
        wrh = wrh_ref[...]
        logit = _dg(wrh, h_hi, _NT) + _dg(wrh, h_lo, _NT) + _dg(wrl_ref[...], h_hi, _NT)
        ex = jnp.exp(logit - jnp.max(logit, axis=0, keepdims=True))
        aff_ref[0] = ex / jnp.sum(ex, axis=0, keepdims=True)
        yield

    _interleave((gla, 1 + c // GLA_CHUNK, 1.0), (ret, RET_HEADS, 1.0),
                (epilogue(), GLA_HEADS + RET_HEADS + 6, 0.8))


def _bwd(x, o_f, gqkv, gates, rqkv, lab, modb, rlog, cum_b, ind, lvl_b, bdm, sgb, srb, gnw, rnw, wout, n2w,
         wrh, wrl):
    bsz, t, d = x.shape
    c = TILE
    nt = t // c
    ne = wrh.shape[0]
    const = lambda shape: pl.BlockSpec(shape, lambda b, j: (0,) * len(shape))
    scan_at = lambda j: nt - 1 - jnp.minimum(j, nt - 1)
    mix_at = lambda j: nt - 1 - jnp.maximum(j - 1, 0)
    scan_tile = lambda w: pl.BlockSpec((1, c, w), lambda b, j: (b, scan_at(j), 0))
    tile = lambda w: pl.BlockSpec((1, c, w), lambda b, j: (b, mix_at(j), 0))
    return pl.pallas_call(
        functools.partial(_bwd_kernel, c=c),
        out_shape=(jax.ShapeDtypeStruct((bsz, t, d), F32),
                   jax.ShapeDtypeStruct((bsz, t, d), BF16),
                   jax.ShapeDtypeStruct((bsz, ne, t), F32)),
        grid=(bsz, nt + 1),
        in_specs=[tile(d), scan_tile(o_f.shape[2]), scan_tile(gqkv.shape[2]), tile(gates.shape[2]),
                  scan_tile(rqkv.shape[2]), scan_tile(lab.shape[2]),
                  pl.BlockSpec((1,) + modb.shape[1:], lambda b, j: (b, 0, 0)),
                  pl.BlockSpec((1,) + rlog.shape[1:], lambda b, j: (1, 0, 0)),
                  const(cum_b.shape), const(ind.shape), const(lvl_b.shape), const(bdm.shape),
                  pl.BlockSpec((1, GLA_QK, GLA_V), lambda b, j: (b, 0, 0)),
                  pl.BlockSpec((1, RET_HEADS, RET_DK, RET_DV), lambda b, j: (b, 0, 0, 0)),
                  const(gnw.shape), const(rnw.shape), const(wout.shape), const(n2w.shape),
                  const(wrh.shape), const(wrl.shape)],
        out_specs=(tile(d), tile(d), pl.BlockSpec((1, ne, c), lambda b, j: (b, 0, mix_at(j)))),
        scratch_shapes=[pltpu.VMEM((GLA_QK, GLA_V), F32),
                        pltpu.VMEM((RET_HEADS, RET_DK, RET_DV), F32),
                        pltpu.VMEM((RET_HEADS, c, c), F32),
                        pltpu.VMEM((c, RET_QK), F32),
                        pltpu.VMEM((c, RET_QK), F32),
                        pltpu.VMEM((1, RET_QK), F32),
                        pltpu.VMEM((c, GLA_QK), F32),
                        pltpu.VMEM((c, GLA_V + RET_V), F32),
                        pltpu.VMEM((c, GLA_V + RET_V), F32),
                        pltpu.VMEM((c, GLA_V + RET_V), BF16)],
        compiler_params=pltpu.CompilerParams(dimension_semantics=("arbitrary", "arbitrary"),
                                             vmem_limit_bytes=VMEM_LIMIT),
        name="mixer_bwd",
    )(x, o_f, gqkv, gates, rqkv, lab, modb, rlog, cum_b, ind, lvl_b, bdm, sgb, srb, gnw, rnw, wout, n2w, wrh, wrl)


def _route_kernel(aff_ref, pos_ref, off_ref, *, cap, nb):
    a = aff_ref[0]
    ne = a.shape[0]
    kf = float(cap)

    def count(mask):
        return jnp.sum(jnp.sum(jnp.where(mask, 1.0, 0.0), axis=2, keepdims=True), axis=1, keepdims=True)

    def bisect(lo, hi, mid, thr):
        ok = count(a >= thr(mid)) >= kf
        return jnp.where(ok, mid, lo), jnp.where(ok, hi, mid)

    pow2 = lambda e: jnp.exp(e * LN2)
    lo_e = jnp.full((ne, 1, 1), float(MIN_EXP - 1), F32)
    hi_e = jnp.full((ne, 1, 1), 1.0, F32)
    lo_e, hi_e = lax.fori_loop(0, EXP_STEPS, lambda i, c: bisect(c[0], c[1], jnp.floor((c[0] + c[1]) * 0.5), pow2),
                               (lo_e, hi_e))
    lo, hi = lax.fori_loop(0, MANTISSA_STEPS,
                           lambda i, c: bisect(c[0], c[1], c[0] + (c[1] - c[0]) * 0.5, lambda v: v),
                           (pow2(lo_e), pow2(hi_e)))
    kth = jnp.min(jnp.min(jnp.where(a >= lo, a, jnp.inf), axis=2, keepdims=True), axis=1, keepdims=True)
    gt = a > kth
    eq = a == kth
    need = kf - count(gt)

    upper = (lax.broadcasted_iota(I32, (LANES, LANES), 0) <= lax.broadcasted_iota(I32, (LANES, LANES), 1))
    upper = jnp.where(upper, 1.0, 0.0).astype(BF16)
    ones = jnp.ones((LANES, LANES), BF16)
    lower = (lax.broadcasted_iota(I32, (ne, nb, nb), 2) < lax.broadcasted_iota(I32, (ne, nb, nb), 1))
    lower = jnp.where(lower, 1.0, 0.0).astype(BF16)

    def excl_prefix(mask):
        m = jnp.where(mask, 1.0, 0.0)
        mb = m.astype(BF16).reshape(ne * nb, LANES)
        inc = _dot(mb, upper).reshape(ne, nb, LANES)
        tot = _dot(mb, ones).reshape(ne, nb, LANES)
        offs = lax.dot_general(lower, tot.astype(BF16), (((2,), (1,)), ((0,), (0,))), preferred_element_type=F32)
        return inc - m + offs, offs

    eq_rank, _ = excl_prefix(eq)
    sel = gt | (eq & (eq_rank < need))
    rank, offs = excl_prefix(sel)
    pos_ref[0] = jnp.where(sel, rank, -1.0).astype(I32)
    off_ref[0] = offs.astype(I32)


def _route(aff4, cap):
    bsz, ne, nb, _ = aff4.shape
    spec = pl.BlockSpec((1, ne, nb, LANES), lambda b: (b, 0, 0, 0))
    return pl.pallas_call(
        functools.partial(_route_kernel, cap=cap, nb=nb),
        out_shape=(jax.ShapeDtypeStruct(aff4.shape, I32), jax.ShapeDtypeStruct(aff4.shape, I32)),
        grid=(bsz,),
        in_specs=[spec],
        out_specs=(spec, spec),
        compiler_params=pltpu.CompilerParams(dimension_semantics=("arbitrary",), vmem_limit_bytes=VMEM_LIMIT),
        name="route",
    )(aff4)


def _tile_counts(cnt_ref, b, j, ne):
    m = cnt_ref[b, j, 0]
    for e in range(1, ne):
        m = jnp.maximum(m, cnt_ref[b, j, e])
    return m


def _window_select(rel, valid, val, ne):
    c = rel.shape[1]
    w = lax.broadcasted_iota(I32, (ne, WROWS, c), 1)
    relm = jnp.where(valid, rel, -1)
    sel = jnp.where(relm[:, None, :] == w, jnp.broadcast_to(val[:, None, :], (ne, WROWS, c)), 0.0)
    return sel.reshape(ne * WROWS, c)


def _round_slots(basev, cntv, r):
    start = basev + jnp.minimum(r * WIN, cntv)
    num = jnp.clip(cntv - r * WIN, 0, WIN)
    return start, num


def _round_slots_scalar(base, cnt, r):
    return base + jnp.minimum(r * WIN, cnt), jnp.clip(cnt - r * WIN, 0, WIN)


def _align_down(v):
    shift = ALIGN.bit_length() - 1
    return (v >> shift) << shift


def _gather_kernel(base_ref, cnt_ref, pos_ref, aff_ref, basev_ref, cntv_ref, h2_ref, xe_ref,
                   xbuf, carry, zbuf, sem, zsem, nissued, *, cap, ne):
    b = pl.program_id(0)
    j = pl.program_id(1)
    last_step = (b == pl.num_programs(0) - 1) & (j == pl.num_programs(1) - 1)

    def window_copy(slot, e, row0):
        return pltpu.make_async_copy(xbuf.at[slot, pl.ds(e * WROWS, WROWS)],
                                     xe_ref.at[b, e, pl.ds(row0, WROWS)], sem.at[slot, e])

    def wait_round(g):
        @pl.when(g >= 0)
        def _():
            for e in range(ne):
                window_copy(g % 2, e, 0).wait()

    @pl.when((b == 0) & (j == 0))
    def _start():
        nissued[0] = 0
        zbuf[...] = jnp.zeros(zbuf.shape, BF16)

    @pl.when(j == 0)
    def _start_sample():
        carry[...] = jnp.zeros(carry.shape, BF16)
        cps = [pltpu.make_async_copy(zbuf, xe_ref.at[b, e, pl.ds(cap, WROWS)], zsem.at[e]) for e in range(ne)]
        for cp in cps:
            cp.start()
        for cp in cps:
            cp.wait()

    pos = pos_ref[0]
    basev = basev_ref[0, 0]
    cntv = cntv_ref[0, 0]
    h2 = h2_ref[0]
    ones = jnp.ones(pos.shape, F32)
    nrounds = (_tile_counts(cnt_ref, b, j, ne) + (WIN - 1)) // WIN

    def round_body(r, _):
        g = nissued[0]
        slot = g % 2
        start, num = _round_slots(basev, cntv, r)
        valid = (pos >= start) & (pos < start + num)
        rel = pos - _align_down(start)
        onehot = _window_select(rel, valid, ones, ne).astype(BF16)
        d = h2.shape[1]
        for col0 in range(0, d, MXU_N):
            xbuf[slot, :, col0:col0 + MXU_N] = _dot(onehot, h2[:, col0:col0 + MXU_N]).astype(BF16)
        gcol = jnp.sum(_window_select(rel, valid, aff_ref[0], ne), axis=1, keepdims=True)
        gcol = jnp.broadcast_to(gcol, (ne * WROWS, LANES))
        g_hi = gcol.astype(BF16).astype(F32)
        first_half = lax.broadcasted_iota(I32, (ne * WROWS, LANES), 1) < LANES // 2
        xbuf[slot, :, d:] = jnp.where(first_half, g_hi, gcol - g_hi).astype(BF16)
        first = []
        for e in range(ne):
            s, n = _round_slots_scalar(base_ref[b, j, e], cnt_ref[b, j, e], r)
            first.append(pl.multiple_of(_align_down(s), ALIGN))
            nxt = pl.multiple_of(_align_down(s + n) - _align_down(s), ALIGN)
            row0 = e * WROWS
            xbuf[slot, pl.ds(row0, ALIGN), :] += carry[pl.ds(e * ALIGN, ALIGN), :]
            carry[pl.ds(e * ALIGN, ALIGN), :] = xbuf[slot, pl.ds(pl.multiple_of(row0 + nxt, ALIGN), ALIGN), :]
        wait_round(g - 1)
        for e in range(ne):
            window_copy(slot, e, first[e]).start()
        nissued[0] = g + 1
        return 0

    lax.fori_loop(0, nrounds, round_body, 0)

    @pl.when(last_step)
    def _drain():
        wait_round(nissued[0] - 1)


def _gather(base, cnt, pos, aff, basev, cntv, h2, cap):
    bsz, t, d = h2.shape
    ne = pos.shape[1]
    c = TILE
    nt = t // c
    width = d + LANES
    grid_spec = pltpu.PrefetchScalarGridSpec(
        num_scalar_prefetch=2,
        grid=(bsz, nt),
        in_specs=[pl.BlockSpec((1, ne, c), lambda b, j, *_: (b, 0, j)),
                  pl.BlockSpec((1, ne, c), lambda b, j, *_: (b, 0, j)),
                  pl.BlockSpec((1, 1, ne, c), lambda b, j, *_: (b, j, 0, 0)),
                  pl.BlockSpec((1, 1, ne, c), lambda b, j, *_: (b, j, 0, 0)),
                  pl.BlockSpec((1, c, d), lambda b, j, *_: (b, j, 0))],
        out_specs=pl.BlockSpec(memory_space=pl.ANY),
        scratch_shapes=[pltpu.VMEM((2, ne * WROWS, width), BF16),
                        pltpu.VMEM((ne * ALIGN, width), BF16), pltpu.VMEM((WROWS, width), BF16),
                        pltpu.SemaphoreType.DMA((2, ne)), pltpu.SemaphoreType.DMA((ne,)),
                        pltpu.SMEM((1,), I32)],
    )
    return pl.pallas_call(
        functools.partial(_gather_kernel, cap=cap, ne=ne),
        out_shape=jax.ShapeDtypeStruct((bsz, ne, cap + WROWS, width), BF16),
        grid_spec=grid_spec,
        compiler_params=pltpu.CompilerParams(dimension_semantics=("arbitrary", "arbitrary"),
                                             vmem_limit_bytes=VMEM_LIMIT),
        name="moe_gather",
    )(base, cnt, pos, aff, basev, cntv, h2)


def _expert_kernel(xe_ref, wg_hbm, wu_hbm, wd_hbm, ye_ref,
                   wga, wua, wda, wgb, wub, wdb, stg, stu, std, sem, *, d, ne, steps):
    e = pl.program_id(0)
    k = pl.program_id(1) * pl.num_programs(2) + pl.program_id(2)
    rows_in = d // steps
    rows_out = wd_hbm.shape[1] // steps

    def chunk_copies(slot, ee, kk):
        r_in = pl.ds(pl.multiple_of(kk * rows_in, ALIGN), rows_in)
        r_out = pl.ds(pl.multiple_of(kk * rows_out, ALIGN), rows_out)
        return [pltpu.make_async_copy(wg_hbm.at[ee, r_in], stg.at[slot], sem.at[slot, 0]),
                pltpu.make_async_copy(wu_hbm.at[ee, r_in], stu.at[slot], sem.at[slot, 1]),
                pltpu.make_async_copy(wd_hbm.at[ee, r_out], std.at[slot], sem.at[slot, 2])]

    def cast_chunk(slot, kk, dst):
        r_in = pl.ds(pl.multiple_of(kk * rows_in, ALIGN), rows_in)
        r_out = pl.ds(pl.multiple_of(kk * rows_out, ALIGN), rows_out)
        dst[0][r_in, :] = stg[slot].astype(BF16)
        dst[1][r_in, :] = stu[slot].astype(BF16)
        dst[2][r_out, :] = std[slot].astype(BF16)

    @pl.when((e == 0) & (k == 0))
    def _first_expert():
        for kk in range(steps):
            cps = chunk_copies(kk % 2, 0, kk)
            for cp in cps:
                cp.start()
            for cp in cps:
                cp.wait()
            cast_chunk(kk % 2, kk, (wga, wua, wda))
        if ne > 1:
            for cp in chunk_copies(0, 1, 0):
                cp.start()

    def step(cur, nxt):
        slot = k % 2

        @pl.when(e + 1 < ne)
        def _next_weights():
            for cp in chunk_copies(slot, e + 1, k):
                cp.wait()
            cast_chunk(slot, k, nxt)

        last_chunk = k + 1 == steps

        @pl.when(jnp.where(last_chunk, e + 2 < ne, e + 1 < ne))
        def _start_next_chunk():
            for cp in chunk_copies(1 - slot, jnp.where(last_chunk, e + 2, e + 1), jnp.where(last_chunk, 0, k + 1)):
                cp.start()

        xin = xe_ref[0, 0]
        xb = xin[:, :d]
        gate = xin[:, d:d + 1].astype(F32) + xin[:, d + LANES // 2:d + LANES // 2 + 1].astype(F32)
        a = _dot(xb, cur[0][...])
        u = _dot(xb, cur[1][...])
        y = _dot((_silu(a) * u).astype(BF16), cur[2][...])
        ye_ref[0, 0] = (y * gate).astype(BF16)

    @pl.when(e % 2 == 0)
    def _even():
        step((wga, wua, wda), (wgb, wub, wdb))

    @pl.when(e % 2 == 1)
    def _odd():
        step((wgb, wub, wdb), (wga, wua, wda))


def _experts(xe, wg, wu, wd, cap):
    bsz, ne, _, width = xe.shape
    d = width - LANES
    ff = wg.shape[2]
    rows = min(EXPERT_ROWS, cap)
    steps = bsz * (cap // rows)
    assert steps % 2 == 0 and d % (steps * ALIGN) == 0 and ff % (steps * ALIGN) == 0
    hbm = pl.BlockSpec(memory_space=pl.ANY)
    return pl.pallas_call(
        functools.partial(_expert_kernel, d=d, ne=ne, steps=steps),
        out_shape=jax.ShapeDtypeStruct((bsz, ne, cap, d), BF16),
        grid=(ne, bsz, cap // rows),
        in_specs=[pl.BlockSpec((1, 1, rows, width), lambda e, b, r: (b, e, r, 0)), hbm, hbm, hbm],
        out_specs=pl.BlockSpec((1, 1, rows, d), lambda e, b, r: (b, e, r, 0)),
        scratch_shapes=[pltpu.VMEM((d, ff), BF16), pltpu.VMEM((d, ff), BF16), pltpu.VMEM((ff, d), BF16),
                        pltpu.VMEM((d, ff), BF16), pltpu.VMEM((d, ff), BF16), pltpu.VMEM((ff, d), BF16),
                        pltpu.VMEM((2, d // steps, ff), F32), pltpu.VMEM((2, d // steps, ff), F32),
                        pltpu.VMEM((2, ff // steps, d), F32), pltpu.SemaphoreType.DMA((2, 3))],
        compiler_params=pltpu.CompilerParams(dimension_semantics=("arbitrary", "arbitrary", "arbitrary"),
                                             vmem_limit_bytes=VMEM_LIMIT),
        name="moe_experts",
    )(xe, wg, wu, wd)


def _combine_kernel(base_ref, cnt_ref, pos_ref, basev_ref, cntv_ref, x1_ref, mod_ref, fnw_ref, ye_ref,
                    out_ref, stage, acc, sem, *, cap, ne):
    b = pl.program_id(0)
    j = pl.program_id(1)
    nt = pl.num_programs(1)
    step = b * nt + j
    pos = pos_ref[0]
    ones = jnp.ones(pos.shape, F32)
    basev = basev_ref[0, 0]
    cntv = cntv_ref[0, 0]
    last = cap - WROWS

    def fetch(slot, bb, jj, r):
        cps = []
        for e in range(ne):
            s, _n = _round_slots_scalar(base_ref[bb, jj, e], cnt_ref[bb, jj, e], r)
            row0 = pl.multiple_of(jnp.minimum(_align_down(s), last), ALIGN)
            cps.append(pltpu.make_async_copy(ye_ref.at[bb, e, pl.ds(row0, WROWS)],
                                             stage.at[slot, pl.ds(e * WROWS, WROWS)], sem.at[slot, e]))
        return cps

    def weights(r):
        start, num = _round_slots(basev, cntv, r)
        valid = (pos >= start) & (pos < start + num)
        return _window_select(pos - jnp.minimum(_align_down(start), last), valid, ones, ne).astype(BF16)

    def expand(w, slot):
        return _dg(w, stage[slot], _TN)

    @pl.when(step == 0)
    def _first():
        for cp in fetch(0, b, j, 0):
            cp.start()

    @pl.when(step + 1 < pl.num_programs(0) * nt)
    def _prefetch():
        wrap = j + 1 == nt
        for cp in fetch((step + 1) % 2, jnp.where(wrap, b + 1, b), jnp.where(wrap, 0, j + 1), 0):
            cp.start()

    w0 = weights(0)
    slot = step % 2
    for cp in fetch(slot, b, j, 0):
        cp.wait()
    acc[...] = expand(w0, slot)

    def round_body(r, _):
        cps = fetch(2, b, j, r)
        for cp in cps:
            cp.start()
        w = weights(r)
        for cp in cps:
            cp.wait()
        acc[...] += expand(w, 2)
        return 0

    nrounds = (_tile_counts(cnt_ref, b, j, ne) + (WIN - 1)) // WIN
    lax.fori_loop(1, nrounds, round_body, 0)
    mod = mod_ref[0]
    x2 = x1_ref[0] + mod[5:6] * acc[...]
    out_ref[0] = _rms(x2, fnw_ref[...])


def _combine(base, cnt, pos, basev, cntv, x1, modb, fnw, ye, cap):
    bsz, t, d = x1.shape
    ne = pos.shape[1]
    c = TILE
    nt = t // c
    grid_spec = pltpu.PrefetchScalarGridSpec(
        num_scalar_prefetch=2,
        grid=(bsz, nt),
        in_specs=[pl.BlockSpec((1, ne, c), lambda b, j, *_: (b, 0, j)),
                  pl.BlockSpec((1, 1, ne, c), lambda b, j, *_: (b, j, 0, 0)),
                  pl.BlockSpec((1, 1, ne, c), lambda b, j, *_: (b, j, 0, 0)),
                  pl.BlockSpec((1, c, d), lambda b, j, *_: (b, j, 0)),
                  pl.BlockSpec((1,) + modb.shape[1:], lambda b, j, *_: (b, 0, 0)),
                  pl.BlockSpec(fnw.shape, lambda b, j, *_: (0, 0)),
                  pl.BlockSpec(memory_space=pl.ANY)],
        out_specs=pl.BlockSpec((1, c, d), lambda b, j, *_: (b, j, 0)),
        scratch_shapes=[pltpu.VMEM((3, ne * WROWS, d), BF16), pltpu.VMEM((c, d), F32),
                        pltpu.SemaphoreType.DMA((3, ne))],
    )
    return pl.pallas_call(
        functools.partial(_combine_kernel, cap=cap, ne=ne),
        out_shape=jax.ShapeDtypeStruct((bsz, t, d), F32),
        grid_spec=grid_spec,
        compiler_params=pltpu.CompilerParams(dimension_semantics=("arbitrary", "arbitrary"),
                                             vmem_limit_bytes=VMEM_LIMIT),
        name="moe_combine",
    )(base, cnt, pos, basev, cntv, x1, modb, fnw, ye)


def _rope_tables(t):
    rows = t // GRID_W
    row = jnp.broadcast_to(jnp.arange(rows)[:, None], (rows, GRID_W)).reshape(-1).astype(F32)
    col = jnp.broadcast_to(jnp.arange(GRID_W)[None, :], (rows, GRID_W)).reshape(-1).astype(F32)
    n_freq = RET_DK // 4
    inv = ROPE_BASE ** (-jnp.arange(n_freq, dtype=F32) / n_freq)
    ang = jnp.concatenate([row[:, None] * inv, col[:, None] * inv], axis=-1)
    cos = jnp.cos(ang)
    sin = jnp.sin(ang)
    return jnp.concatenate([cos, cos], axis=1), jnp.concatenate([-sin, sin], axis=1)


def _mixer_weights(w_in, gate_w, gate_b):
    pts = np.cumsum(IN_WIDTHS)[:-1]
    gq, gk, gv, gz, gg, rq, rk, rv, rg = jnp.split(w_in, [int(p) for p in pts], axis=1)
    gz = jnp.pad(gz, ((0, 0), (0, GZ_PAD - 2 * GLA_RANK)))
    wall = jnp.concatenate([gq, gk, gv, gg, rq, rk, rv, rg, gz], axis=1).astype(BF16)
    gmat = jnp.zeros((GZ_PAD, 2 * GLA_QK), F32)
    gmat = gmat.at[:GLA_RANK, :GLA_QK].set(gate_w[0]).at[GLA_RANK:2 * GLA_RANK, GLA_QK:].set(gate_w[1])
    ghi = gmat.astype(BF16)
    glo = (gmat - ghi.astype(F32)).astype(BF16)
    return wall, ghi, glo, gate_b.reshape(1, 2 * GLA_QK)


def kernel(x, c, ctx, c_ctx, w_ada, b_ada, norm1_w, w_in, gla_gate_w, gla_gate_b, ret_decay_logit, gla_norm_w,
           ret_norm_w, w_out, norm2_w, w_router, w_exp_gate, w_exp_up, w_exp_down, final_norm_w):
    bsz, t, d = x.shape
    depth = w_ada.shape[0]
    assert depth == 1 and t % TILE == 0 and ctx.shape[1] == TILE
    ne = w_router.shape[2]
    cap = EC_CAPACITY_FACTOR * t // ne
    assert cap >= WROWS and cap % ALIGN == 0 and cap % min(EXPERT_ROWS, cap) == 0
    nt = t // TILE
    nb = t // LANES
    bpt = TILE // LANES

    cs = jnp.concatenate([c, c_ctx[None, :], jnp.zeros((8 - bsz - 1, d), F32)], axis=0)
    mod = _ada(cs, w_ada[0], b_ada[0][None, :])
    mod = jnp.pad(mod.reshape(8, N_ADA, d), ((0, 0), (0, 8 - N_ADA), (0, 0)))
    modb = mod[:bsz]
    modc = mod[bsz:bsz + 1]

    wall, ghi, glo, gb = _mixer_weights(w_in[0], gla_gate_w[0], gla_gate_b[0])
    n1w = norm1_w[0][None, :]
    rlog = jnp.broadcast_to(ret_decay_logit[0][:, :, None], (2, RET_HEADS, TILE)).astype(F32)
    cum_f = jnp.asarray(_chunk_cumsum_matrix(TILE, False), BF16)
    cum_b = jnp.asarray(_chunk_cumsum_matrix(TILE, True), BF16)
    ind = jnp.asarray(_chunk_indicator(TILE), BF16)
    lvl_f = jnp.asarray(_level_index(False))
    lvl_b = jnp.asarray(_level_index(True))
    bdm = jnp.asarray(_head_block_mask(), BF16)
    cosf, sins = _rope_tables(t)

    sgf, sgb, srf, srb = _ctx_states(ctx, modc, n1w, wall, ghi, glo, gb, rlog, cum_f, cum_b, ind, bdm)
    o_f, gqkv, gates, rqkv, lab = _fwd(x, modb, n1w, wall, ghi, glo, gb, cosf, sins, rlog, cum_f, ind, lvl_f, bdm,
                                       sgf, srf)

    wr = w_router[0].T
    wrh = wr.astype(BF16)
    wrl = (wr - wrh.astype(F32)).astype(BF16)
    x1, h2, aff = _bwd(x, o_f, gqkv, gates, rqkv, lab, modb, rlog, cum_b, ind, lvl_b, bdm, sgb, srb,
                       gla_norm_w[0][None, :], ret_norm_w[0][None, :], w_out[0].astype(BF16),
                       norm2_w[0][None, :], wrh, wrl)

    pos4, off4 = _route(aff.reshape(bsz, ne, nb, LANES), cap)
    pos = pos4.reshape(bsz, ne, t)
    boff = off4[:, :, :, 0]
    base = jnp.transpose(boff[:, :, ::bpt], (0, 2, 1))
    nxt = jnp.concatenate([base[:, 1:], jnp.full((bsz, 1, ne), cap, I32)], axis=1)
    cnt = nxt - base
    basev = jnp.broadcast_to(base[:, :, :, None], (bsz, nt, ne, TILE))
    cntv = jnp.broadcast_to(cnt[:, :, :, None], (bsz, nt, ne, TILE))

    xe = _gather(base, cnt, pos, aff, basev, cntv, h2, cap)
    ye = _experts(xe, w_exp_gate[0], w_exp_up[0], w_exp_down[0], cap)
    return _combine(base, cnt, pos, basev, cntv, x1, modb, final_norm_w[None, :], ye, cap)
```

```python
import functools

import numpy as np
import jax
import jax.numpy as jnp
from jax import lax
from jax.experimental import pallas as pl
from jax.experimental.pallas import tpu as pltpu

F32 = jnp.float32
BF16 = jnp.bfloat16
I32 = jnp.int32

GLA_HEADS = 4
GLA_DK = 64
GLA_DV = 128
GLA_RANK = 16
GLA_TAU = 16.0
RET_HEADS = 4
RET_DK = 128
RET_DV = 128
GRID_W = 64
ROPE_BASE = 10000.0
N_EXPERTS = 16
EC_CAPACITY_FACTOR = 2
N_ADA = 6
EPS = 1e-6

GLA_QK = GLA_HEADS * GLA_DK
GLA_V = GLA_HEADS * GLA_DV
RET_QK = RET_HEADS * RET_DK
RET_V = RET_HEADS * RET_DV
IN_WIDTHS = (GLA_QK, GLA_QK, GLA_V, 2 * GLA_RANK, GLA_V, RET_QK, RET_QK, RET_V, RET_V)

LANES = 128
MXU_N = 256
TILE = 256
GLA_CHUNK = 64
GLA_LEVELS = 6
WIN = 48
ALIGN = 16
WROWS = WIN + ALIGN
GZ_PAD = LANES
EXPERT_ROWS = 512
LN2 = float(np.log(2.0))
MIN_EXP = -149
EXP_STEPS = 8
MANTISSA_STEPS = 56
VMEM_LIMIT = 56 * 1024 * 1024

_NT = (((1,), (1,)), ((), ()))
_TN = (((0,), (0,)), ((), ()))


def _dot(a, b):
    return jnp.dot(a, b, preferred_element_type=F32)


def _dg(a, b, dims):
    return lax.dot_general(a, b, dims, preferred_element_type=F32)


def _split(a):
    hi = a.astype(BF16)
    lo = (a - hi.astype(F32)).astype(BF16)
    return hi, lo


def _logsig(x):
    return jnp.minimum(x, 0.0) - jnp.log(1.0 + jnp.exp(-jnp.abs(x)))


def _silu(x):
    return x / (1.0 + jnp.exp(-x))


def _rms(x, w):
    return x * lax.rsqrt(jnp.mean(x * x, axis=-1, keepdims=True) + EPS) * w


def _chunk_cumsum_matrix(c, reverse):
    i = np.arange(c)[:, None]
    t = np.arange(c)[None, :]
    same = (i // GLA_CHUNK) == (t // GLA_CHUNK)
    return (same & ((t >= i) if reverse else (t <= i))).astype(np.float32)


def _chunk_indicator(c):
    return (np.arange(c)[:, None] // GLA_CHUNK == np.arange(LANES)[None, :]).astype(np.float32)


def _level_index(reverse):
    i = np.arange(GLA_CHUNK)[:, None]
    j = np.arange(GLA_CHUNK)[None, :]
    x = i ^ j
    lvl = np.where(x > 0, np.floor(np.log2(np.maximum(x, 1))), -1).astype(np.int32)
    bad = (j < i) if reverse else (j > i)
    return np.tile(np.where(bad, 99, lvl).astype(np.int32), (1, GLA_HEADS))


def _head_block_mask():
    r = np.arange(GLA_QK)[:, None] // GLA_DK
    l = np.arange(GLA_V)[None, :] // GLA_DV
    return (r == l).astype(np.float32)


def _project(xn, wall_ref, ghi_ref, glo_ref, gb_ref):
    proj = _dot(xn.astype(BF16), wall_ref[...])
    o = 0
    out = []
    for w in (GLA_QK, GLA_QK, GLA_V, GLA_V, RET_QK, RET_QK, RET_V, RET_V, GZ_PAD):
        out.append(proj[:, o:o + w])
        o += w
    gq, gk, gv, gg, rq, rk, rv, rg, gz = out
    z_hi, z_lo = _split(gz)
    pre = _dot(z_hi, ghi_ref[...]) + _dot(z_lo, ghi_ref[...]) + _dot(z_hi, glo_ref[...]) + gb_ref[...]
    log_a = _logsig(pre) * (1.0 / GLA_TAU)
    return gq * (GLA_DK ** -0.5), gk, gv, gg, rq, rk * (RET_DK ** -0.5), rv, rg, log_a


def _rope(a, cos, sin):
    outs = []
    for h in range(RET_HEADS):
        ah = a[:, h * RET_DK:(h + 1) * RET_DK]
        outs.append(ah * cos + pltpu.roll(ah, RET_DK // 2, 1) * sin)
    return jnp.concatenate(outs, axis=1)


def _stack_heads(a):
    head = lax.broadcasted_iota(I32, a.shape, 1) >> 6
    zero = jnp.zeros_like(a)
    return jnp.concatenate([jnp.where(head == h, a, zero) for h in range(GLA_HEADS)], axis=0)


def _gate_sums(g):
    hi, lo = _split(g)
    return jnp.concatenate([hi, lo], axis=1)


def _level_log_decay(level, g, b, b_ref, row0, reverse):
    n = GLA_CHUNK
    row = lax.broadcasted_iota(I32, (n, GLA_QK), 0)
    upper = ((row >> level) & 1) == 1
    if level == 0:
        return jnp.where(upper, 0.0, g) if reverse else jnp.where(upper, g, 0.0)
    if level == 1:
        nxt = pltpu.roll(g, n - 1, 0)
        prv = pltpu.roll(g, 1, 0)
        r = row & 3
        if reverse:
            return jnp.where(r == 0, g + nxt, jnp.where(r == 1, g, jnp.where(r == 2, 0.0, prv)))
        return jnp.where(r == 0, nxt, jnp.where(r == 1, 0.0, jnp.where(r == 2, g, g + prv)))
    m = 1 << level
    anchors = [jnp.broadcast_to(b_ref[pl.ds(row0 + blk + (m if reverse else m - 1), 1), :], (2 * m, GLA_QK))
               for blk in range(0, n, 2 * m)]
    d = b - (jnp.concatenate(anchors, axis=0) if len(anchors) > 1 else anchors[0])
    return jnp.where(upper, -d, d) if reverse else jnp.where(upper, d, -d)


def _interleave(*stages):
    order = sorted((span * (k + 0.5) / n, i) for i, (_, n, span) in enumerate(stages) for k in range(n))
    for _, i in order:
        next(stages[i][0])
    for gen, _, _ in stages:
        for _ in gen:
            raise AssertionError("stage has more pieces than declared")


def _gla_steps(qkv_ref, g_ref, cum_ref, ind_ref, lvl_ref, bdm_ref, s_ref, b_ref, reverse, emit):
    c = g_ref.shape[0]
    g2 = _gate_sums(g_ref[...])
    r = _dot(cum_ref[...], g2)
    b_ref[...] = r[:, :GLA_QK] + r[:, GLA_QK:]
    cs = _dg(g2, ind_ref[...], _TN)
    tot = cs[:GLA_QK] + cs[GLA_QK:]
    yield
    nchunk = c // GLA_CHUNK
    for ci in (reversed(range(nchunk)) if reverse else range(nchunk)):
        row0 = ci * GLA_CHUNK
        rows = pl.ds(row0, GLA_CHUNK)
        kc = qkv_ref[rows, GLA_QK:2 * GLA_QK].astype(F32)
        vc = qkv_ref[rows, 2 * GLA_QK:]
        gc = g_ref[rows, :]
        bc = b_ref[rows, :]
        bdm = bdm_ref[...]
        s_bd = s_ref[...]
        if emit is not None:
            qc = qkv_ref[rows, :GLA_QK].astype(F32)
            lvl = lvl_ref[...]
            scores = jnp.zeros((GLA_CHUNK, GLA_HEADS * GLA_CHUNK), F32)
            for level in range(GLA_LEVELS):
                e = jnp.exp(_level_log_decay(level, gc, bc, b_ref, row0, reverse))
                p = _dg((qc * e).astype(BF16), _stack_heads((kc * e).astype(BF16)), _NT)
                scores = jnp.where(lvl == level, p, scores)
            p = _dg(qc.astype(BF16), _stack_heads(kc.astype(BF16)), _NT)
            scores = jnp.where(lvl == -1, p, scores)
            v_bd = jnp.concatenate([vc] * GLA_HEADS, axis=0) * bdm
            emit(row0, _dot(scores.astype(BF16), v_bd) + _dot((qc * jnp.exp(bc)).astype(BF16), s_bd.astype(BF16)))
        b_end = b_ref[pl.ds(row0 if reverse else row0 + GLA_CHUNK - 1, 1), :]
        kv = _dg((kc * jnp.exp(b_end - bc)).astype(BF16), vc, _TN)
        e_col = jnp.exp(jnp.broadcast_to(tot[:, ci:ci + 1], (GLA_QK, GLA_V)))
        s_ref[...] = e_col * s_bd + jnp.where(bdm > 0, kv, 0.0)
        yield


def _ret_decays(rlog_ref, c, reverse):
    lg = _logsig(rlog_ref[0])
    ii = lax.broadcasted_iota(I32, (c, c), 0)
    jj = lax.broadcasted_iota(I32, (c, c), 1)
    rel = ((jj - ii) if reverse else (ii - jj)).astype(F32)
    pos = lax.broadcasted_iota(I32, (c, RET_DK), 0).astype(F32)
    dmats, qd, kd, cd = [], [], [], []
    for h in range(RET_HEADS):
        lh = lg[h:h + 1, :]
        dmats.append(jnp.where(rel >= 0, jnp.exp(lh * jnp.maximum(rel, 0.0)), 0.0))
        l1 = lh[:, :RET_DK]
        qd.append(jnp.exp(l1 * ((c - pos) if reverse else (pos + 1.0))))
        kd.append(jnp.exp(l1 * (pos if reverse else (c - 1.0 - pos))))
        cd.append(jnp.exp(l1 * float(c)))
    return dmats, jnp.concatenate(qd, axis=1), jnp.concatenate(kd, axis=1), jnp.concatenate(cd, axis=1)


def _ret_steps(qkv_ref, dmat_ref, qdec_ref, kdec_ref, cdec_ref, s_ref, emit):
    for h in range(RET_HEADS):
        sl = slice(h * RET_DK, (h + 1) * RET_DK)
        qb = qkv_ref[:, h * RET_DK:(h + 1) * RET_DK]
        kb = qkv_ref[:, RET_QK + h * RET_DK:RET_QK + (h + 1) * RET_DK]
        vh = qkv_ref[:, 2 * RET_QK + h * RET_DV:2 * RET_QK + (h + 1) * RET_DV]
        sc = _dg(qb, kb, _NT) * dmat_ref[h]
        s = s_ref[h]
        emit(h, _dot(sc.astype(BF16), vh) + _dot((qb.astype(F32) * qdec_ref[:, sl]).astype(BF16), s.astype(BF16)))
        s_ref[h] = cdec_ref[:, sl] * s + _dg((kb.astype(F32) * kdec_ref[:, sl]).astype(BF16), vh, _TN)
        yield


def _proj_steps(x_ref, mod_ref, n1w_ref, wall_ref, ghi_ref, glo_ref, gb_ref, rope_col_ref, rope_row_ref,
                gqkv_ref, gates_ref, rqkv_ref, lab_ref, nxt_g, nxt_r, nxt_l):
    mod = mod_ref[0]
    hb = (_rms(x_ref[0], n1w_ref[...]) * (1.0 + mod[1:2]) + mod[0:1]).astype(BF16)
    yield

    def cols(o, w):
        return _dot(hb, wall_ref[:, o:o + w])

    def put(val, o, out_ref, stage_ref):
        val = val.astype(BF16)
        out_ref[0, :, o:o + val.shape[1]] = val
        if stage_ref is not None:
            stage_ref[:, o:o + val.shape[1]] = val

    put(cols(0, GLA_QK) * (GLA_DK ** -0.5), 0, gqkv_ref, nxt_g)
    put(cols(GLA_QK, GLA_QK), GLA_QK, gqkv_ref, nxt_g)
    yield
    put(cols(2 * GLA_QK, GLA_V), 2 * GLA_QK, gqkv_ref, nxt_g)
    yield
    o = 2 * GLA_QK + GLA_V
    put(cols(o, GLA_V), 0, gates_ref, None)
    yield
    o += GLA_V
    rows_of = lambda i: jnp.concatenate(
        [jnp.broadcast_to(rope_row_ref[0, i, q:q + 1, :], (GRID_W, RET_DK)) for q in range(TILE // GRID_W)], axis=0)
    cos = rows_of(0) + rope_col_ref[0]
    sin = rows_of(1) + rope_col_ref[1]
    put(_rope(cols(o, RET_QK), cos, sin), 0, rqkv_ref, nxt_r)
    yield
    o += RET_QK
    put(_rope(cols(o, RET_QK) * (RET_DK ** -0.5), cos, sin), RET_QK, rqkv_ref, nxt_r)
    yield
    o += RET_QK
    put(cols(o, RET_V), 2 * RET_QK, rqkv_ref, nxt_r)
    yield
    o += RET_V
    put(cols(o, RET_V), GLA_V, gates_ref, None)
    yield
    o += RET_V
    z_hi, z_lo = _split(cols(o, GZ_PAD))
    pre = _dot(z_hi, ghi_ref[...]) + _dot(z_lo, ghi_ref[...]) + _dot(z_hi, glo_ref[...]) + gb_ref[...]
    log_a = _logsig(pre) * (1.0 / GLA_TAU)
    nxt_l[...] = log_a[:, :GLA_QK]
    lab_ref[0] = log_a[:, GLA_QK:]
    yield


def _ada_kernel(c_ref, w_ref, b_ref, o_ref):
    s_hi, s_lo = _split(_silu(c_ref[...]))
    w_hi, w_lo = _split(w_ref[...])
    o_ref[...] = _dot(s_hi, w_hi) + _dot(s_lo, w_hi) + _dot(s_hi, w_lo) + b_ref[...]


def _ada(cs, w, b):
    rows, d = cs.shape
    n = w.shape[1]
    tn = 1536
    return pl.pallas_call(
        _ada_kernel,
        out_shape=jax.ShapeDtypeStruct((rows, n), F32),
        grid=(n // tn,),
        in_specs=[pl.BlockSpec((rows, d), lambda i: (0, 0)),
                  pl.BlockSpec((d, tn), lambda i: (0, i)),
                  pl.BlockSpec((1, tn), lambda i: (0, i))],
        out_specs=pl.BlockSpec((rows, tn), lambda i: (0, i)),
        compiler_params=pltpu.CompilerParams(dimension_semantics=("arbitrary",), vmem_limit_bytes=VMEM_LIMIT),
        name="ada",
    )(cs, w, b)


def _ctx_kernel(ctx_ref, mod_ref, n1w_ref, wall_ref, ghi_ref, glo_ref, gb_ref, rlog_ref, cumf_ref, cumb_ref,
                ind_ref, bdm_ref, sgf_ref, sgb_ref, srf_ref, srb_ref, b_scr, kv_scr, g_scr, *, c):
    mod = mod_ref[0]
    hc = _rms(ctx_ref[0], n1w_ref[...]) * (1.0 + mod[1:2]) + mod[0:1]
    _, gk, gv, _, _, rk, rv, _, log_a = _project(hc, wall_ref, ghi_ref, glo_ref, gb_ref)
    kv_scr[:, GLA_QK:] = jnp.concatenate([gk, gv], axis=1).astype(BF16)
    rvb = rv.astype(BF16)
    for d, (cum_ref, out_g, out_r) in enumerate(((cumf_ref, sgf_ref, srf_ref), (cumb_ref, sgb_ref, srb_ref))):
        out_g[0] = jnp.zeros((GLA_QK, GLA_V), F32)
        g_scr[...] = log_a[:, d * GLA_QK:(d + 1) * GLA_QK]
        _interleave((_gla_steps(kv_scr, g_scr, cum_ref, ind_ref, None, bdm_ref, out_g.at[0], b_scr, bool(d), None),
                     1 + c // GLA_CHUNK, 1.0))
        _, _, kdec, _ = _ret_decays(rlog_ref.at[d:d + 1], c, reverse=bool(d))
        for h in range(RET_HEADS):
            sl = slice(h * RET_DK, (h + 1) * RET_DK)
            out_r[0, h] = _dg((rk[:, sl] * kdec[:, sl]).astype(BF16), rvb[:, h * RET_DV:(h + 1) * RET_DV], _TN)


def _ctx_states(ctx, modc, n1w, wall, ghi, glo, gb, rlog, cum_f, cum_b, ind, bdm):
    bsz, c, d = ctx.shape
    const = lambda a: pl.BlockSpec(a.shape, lambda b: (0,) * a.ndim)
    consts = (modc, n1w, wall, ghi, glo, gb, rlog, cum_f, cum_b, ind, bdm)
    return pl.pallas_call(
        functools.partial(_ctx_kernel, c=c),
        out_shape=(jax.ShapeDtypeStruct((bsz, GLA_QK, GLA_V), F32),
                   jax.ShapeDtypeStruct((bsz, GLA_QK, GLA_V), F32),
                   jax.ShapeDtypeStruct((bsz, RET_HEADS, RET_DK, RET_DV), F32),
                   jax.ShapeDtypeStruct((bsz, RET_HEADS, RET_DK, RET_DV), F32)),
        grid=(bsz,),
        in_specs=[pl.BlockSpec((1, c, d), lambda b: (b, 0, 0))] + [const(a) for a in consts],
        out_specs=(pl.BlockSpec((1, GLA_QK, GLA_V), lambda b: (b, 0, 0)),
                   pl.BlockSpec((1, GLA_QK, GLA_V), lambda b: (b, 0, 0)),
                   pl.BlockSpec((1, RET_HEADS, RET_DK, RET_DV), lambda b: (b, 0, 0, 0)),
                   pl.BlockSpec((1, RET_HEADS, RET_DK, RET_DV), lambda b: (b, 0, 0, 0))),
        scratch_shapes=[pltpu.VMEM((c, GLA_QK), F32), pltpu.VMEM((c, 2 * GLA_QK + GLA_V), BF16),
                        pltpu.VMEM((c, GLA_QK), F32)],
        compiler_params=pltpu.CompilerParams(dimension_semantics=("arbitrary",), vmem_limit_bytes=VMEM_LIMIT),
        name="ctx_states",
    )(ctx, *consts)


def _fwd_kernel(x_ref, mod_ref, n1w_ref, wall_ref, ghi_ref, glo_ref, gb_ref, rope_col_ref, rope_row_ref, rlog_ref,
                cum_ref, ind_ref, lvl_ref, bdm_ref, sg0_ref, sr0_ref,
                of_ref, gqkv_ref, gates_ref, rqkv_ref, lab_ref,
                sg_scr, sr_scr, dmat_scr, qdec_scr, kdec_scr, cdec_scr, b_scr,
                cur_g, cur_r, cur_l, nxt_g, nxt_r, nxt_l, *, c):
    j = pl.program_id(1)

    @pl.when(j == 0)
    def _first():
        sg_scr[...] = jnp.zeros(sg_scr.shape, F32)
        sr_scr[...] = jnp.zeros(sr_scr.shape, F32)
        nxt_g[...] = jnp.zeros(nxt_g.shape, BF16)
        nxt_r[...] = jnp.zeros(nxt_r.shape, BF16)
        nxt_l[...] = jnp.zeros(nxt_l.shape, F32)
        dmats, qd, kd, cd = _ret_decays(rlog_ref, c, reverse=False)
        for h in range(RET_HEADS):
            dmat_scr[h] = dmats[h]
        qdec_scr[...] = qd
        kdec_scr[...] = kd
        cdec_scr[...] = cd

    @pl.when(j == 1)
    def _seed():
        sg_scr[...] = sg0_ref[0]
        sr_scr[...] = sr0_ref[0]

    cur_g[...] = nxt_g[...]
    cur_r[...] = nxt_r[...]
    cur_l[...] = nxt_l[...]

    def emit_gla(row0, out):
        of_ref[0, pl.ds(row0, GLA_CHUNK), 0:GLA_V] = out

    def emit_ret(h, out):
        of_ref[0, :, GLA_V + h * RET_DV:GLA_V + (h + 1) * RET_DV] = out

    gla = _gla_steps(cur_g, cur_l, cum_ref, ind_ref, lvl_ref, bdm_ref, sg_scr, b_scr, False, emit_gla)
    ret = _ret_steps(cur_r, dmat_scr, qdec_scr, kdec_scr, cdec_scr, sr_scr, emit_ret)
    proj = _proj_steps(x_ref, mod_ref, n1w_ref, wall_ref, ghi_ref, glo_ref, gb_ref, rope_col_ref, rope_row_ref,
                       gqkv_ref, gates_ref, rqkv_ref, lab_ref, nxt_g, nxt_r, nxt_l)
    _interleave((gla, 1 + c // GLA_CHUNK, 1.0), (ret, RET_HEADS, 1.0), (proj, 9, 1.0))


def _fwd(x, modb, n1w, wall, ghi, glo, gb, rope_col, rope_row, rlog, cum_f, ind, lvl_f, bdm, sgf, srf):
    bsz, t, d = x.shape
    c = TILE
    nt = t // c
    const = lambda shape: pl.BlockSpec(shape, lambda b, j: (0,) * len(shape))
    proj_tile = lambda w: pl.BlockSpec((1, c, w), lambda b, j: (b, jnp.minimum(j, nt - 1), 0))
    scan_tile = lambda w: pl.BlockSpec((1, c, w), lambda b, j: (b, jnp.maximum(j - 1, 0), 0))
    rope_tile = pl.BlockSpec((1,) + rope_row.shape[1:], lambda b, j: (jnp.minimum(j, nt - 1), 0, 0, 0))
    mixw = GLA_V + RET_V
    gw, rw = 2 * GLA_QK + GLA_V, 2 * RET_QK + RET_V
    return pl.pallas_call(
        functools.partial(_fwd_kernel, c=c),
        out_shape=(jax.ShapeDtypeStruct((bsz, t, mixw), F32),
                   jax.ShapeDtypeStruct((bsz, t, gw), BF16),
                   jax.ShapeDtypeStruct((bsz, t, GLA_V + RET_V), BF16),
                   jax.ShapeDtypeStruct((bsz, t, rw), BF16),
                   jax.ShapeDtypeStruct((bsz, t, GLA_QK), F32)),
        grid=(bsz, nt + 1),
        in_specs=[proj_tile(d),
                  pl.BlockSpec((1,) + modb.shape[1:], lambda b, j: (b, 0, 0)),
                  const(n1w.shape), const(wall.shape), const(ghi.shape), const(glo.shape), const(gb.shape),
                  const(rope_col.shape), rope_tile,
                  pl.BlockSpec((1,) + rlog.shape[1:], lambda b, j: (0, 0, 0)),
                  const(cum_f.shape), const(ind.shape), const(lvl_f.shape), const(bdm.shape),
                  pl.BlockSpec((1, GLA_QK, GLA_V), lambda b, j: (b, 0, 0)),
                  pl.BlockSpec((1, RET_HEADS, RET_DK, RET_DV), lambda b, j: (b, 0, 0, 0))],
        out_specs=(scan_tile(mixw), proj_tile(gw), proj_tile(GLA_V + RET_V), proj_tile(rw), proj_tile(GLA_QK)),
        scratch_shapes=[pltpu.VMEM((GLA_QK, GLA_V), F32),
                        pltpu.VMEM((RET_HEADS, RET_DK, RET_DV), F32),
                        pltpu.VMEM((RET_HEADS, c, c), F32),
                        pltpu.VMEM((c, RET_QK), F32),
                        pltpu.VMEM((c, RET_QK), F32),
                        pltpu.VMEM((1, RET_QK), F32),
                        pltpu.VMEM((c, GLA_QK), F32),
                        pltpu.VMEM((c, gw), BF16), pltpu.VMEM((c, rw), BF16), pltpu.VMEM((c, GLA_QK), F32),
                        pltpu.VMEM((c, gw), BF16), pltpu.VMEM((c, rw), BF16), pltpu.VMEM((c, GLA_QK), F32)],
        compiler_params=pltpu.CompilerParams(dimension_semantics=("arbitrary", "arbitrary"),
                                             vmem_limit_bytes=VMEM_LIMIT),
        name="mixer_fwd",
    )(x, modb, n1w, wall, ghi, glo, gb, rope_col, rope_row, rlog, cum_f, ind, lvl_f, bdm, sgf, srf)


def _bwd_kernel(x_ref, of_ref, gqkv_ref, gates_ref, rqkv_ref, lab_ref, mod_ref, rlog_ref,
                cum_ref, ind_ref, lvl_ref, bdm_ref,
                sg0_ref, sr0_ref, gnw_ref, rnw_ref, wout_ref, n2w_ref, wrh_ref, wrl_ref,
                x1_ref, h2_ref, aff_ref,
                sg_scr, sr_scr, dmat_scr, qdec_scr, kdec_scr, cdec_scr, b_scr, cur_m, nxt_m, mixb, *, c):
    j = pl.program_id(1)

    @pl.when(j == 0)
    def _first():
        sg_scr[...] = sg0_ref[0]
        sr_scr[...] = sr0_ref[0]
        nxt_m[...] = jnp.zeros(nxt_m.shape, F32)
        dmats, qd, kd, cd = _ret_decays(rlog_ref, c, reverse=True)
        for h in range(RET_HEADS):
            dmat_scr[h] = dmats[h]
        qdec_scr[...] = qd
        kdec_scr[...] = kd
        cdec_scr[...] = cd

    cur_m[...] = nxt_m[...]

    def emit_gla(row0, out):
        rows = pl.ds(row0, GLA_CHUNK)
        nxt_m[rows, 0:GLA_V] = of_ref[0, rows, 0:GLA_V] + out

    def emit_ret(h, out):
        cols = slice(GLA_V + h * RET_DV, GLA_V + (h + 1) * RET_DV)
        nxt_m[:, cols] = of_ref[0, :, cols] + out

    gla = _gla_steps(gqkv_ref.at[0], lab_ref.at[0], cum_ref, ind_ref, lvl_ref, bdm_ref, sg_scr, b_scr, True,
                     emit_gla)
    ret = _ret_steps(rqkv_ref.at[0], dmat_scr, qdec_scr, kdec_scr, cdec_scr, sr_scr, emit_ret)

    def epilogue():
        for h in range(GLA_HEADS + RET_HEADS):
            sl = slice(h * GLA_DV, (h + 1) * GLA_DV)
            oh = cur_m[:, sl]
            if h < GLA_HEADS:
                y = oh * lax.rsqrt(jnp.mean(oh * oh, axis=-1, keepdims=True) + EPS) * gnw_ref[:, sl]
            else:
                dv = oh - jnp.mean(oh, axis=-1, keepdims=True)
                y = (dv * lax.rsqrt(jnp.mean(dv * dv, axis=-1, keepdims=True) + EPS)
                     * rnw_ref[:, h * RET_DV - GLA_V:(h + 1) * RET_DV - GLA_V])
            mixb[:, sl] = (y * _silu(gates_ref[0, :, sl].astype(F32))).astype(BF16)
            yield
        mod = mod_ref[0]
        d = x_ref.shape[2]
        step = d // 4
        for p in range(4):
            cs = slice(p * step, (p + 1) * step)
            x1_ref[0, :, cs] = x_ref[0, :, cs] + mod[2:3, cs] * _dot(mixb[...], wout_ref[:, cs])
            yield
        h2 = _rms(x1_ref[0], n2w_ref[...]) * (1.0 + mod[4:5]) + mod[3:4]
        h_hi, h_lo = _split(h2)
        h2_ref[0] = h_hi
        yield
        wrh = wrh_ref[...]
        logit = _dg(wrh, h_hi, _NT) + _dg(wrh, h_lo, _NT) + _dg(wrl_ref[...], h_hi, _NT)
        ex = jnp.exp(logit - jnp.max(logit, axis=0, keepdims=True))
        aff_ref[0] = ex / jnp.sum(ex, axis=0, keepdims=True)
        yield

    _interleave((gla, 1 + c // GLA_CHUNK, 1.0), (ret, RET_HEADS, 1.0),
                (epilogue(), GLA_HEADS + RET_HEADS + 6, 0.8))


def _bwd(x, o_f, gqkv, gates, rqkv, lab, modb, rlog, cum_b, ind, lvl_b, bdm, sgb, srb, gnw, rnw, wout, n2w,
         wrh, wrl):
    bsz, t, d = x.shape
    c = TILE
    nt = t // c
    ne = wrh.shape[0]
    const = lambda shape: pl.BlockSpec(shape, lambda b, j: (0,) * len(shape))
    scan_at = lambda j: nt - 1 - jnp.minimum(j, nt - 1)
    mix_at = lambda j: nt - 1 - jnp.maximum(j - 1, 0)
    scan_tile = lambda w: pl.BlockSpec((1, c, w), lambda b, j: (b, scan_at(j), 0))
    tile = lambda w: pl.BlockSpec((1, c, w), lambda b, j: (b, mix_at(j), 0))
    return pl.pallas_call(
        functools.partial(_bwd_kernel, c=c),
        out_shape=(jax.ShapeDtypeStruct((bsz, t, d), F32),
                   jax.ShapeDtypeStruct((bsz, t, d), BF16),
                   jax.ShapeDtypeStruct((bsz, ne, t), F32)),
        grid=(bsz, nt + 1),
        in_specs=[tile(d), scan_tile(o_f.shape[2]), scan_tile(gqkv.shape[2]), tile(gates.shape[2]),
                  scan_tile(rqkv.shape[2]), scan_tile(lab.shape[2]),
                  pl.BlockSpec((1,) + modb.shape[1:], lambda b, j: (b, 0, 0)),
                  pl.BlockSpec((1,) + rlog.shape[1:], lambda b, j: (1, 0, 0)),
                  const(cum_b.shape), const(ind.shape), const(lvl_b.shape), const(bdm.shape),
                  pl.BlockSpec((1, GLA_QK, GLA_V), lambda b, j: (b, 0, 0)),
                  pl.BlockSpec((1, RET_HEADS, RET_DK, RET_DV), lambda b, j: (b, 0, 0, 0)),
                  const(gnw.shape), const(rnw.shape), const(wout.shape), const(n2w.shape),
                  const(wrh.shape), const(wrl.shape)],
        out_specs=(tile(d), tile(d), pl.BlockSpec((1, ne, c), lambda b, j: (b, 0, mix_at(j)))),
        scratch_shapes=[pltpu.VMEM((GLA_QK, GLA_V), F32),
                        pltpu.VMEM((RET_HEADS, RET_DK, RET_DV), F32),
                        pltpu.VMEM((RET_HEADS, c, c), F32),
                        pltpu.VMEM((c, RET_QK), F32),
                        pltpu.VMEM((c, RET_QK), F32),
                        pltpu.VMEM((1, RET_QK), F32),
                        pltpu.VMEM((c, GLA_QK), F32),
                        pltpu.VMEM((c, GLA_V + RET_V), F32),
                        pltpu.VMEM((c, GLA_V + RET_V), F32),
                        pltpu.VMEM((c, GLA_V + RET_V), BF16)],
        compiler_params=pltpu.CompilerParams(dimension_semantics=("arbitrary", "arbitrary"),
                                             vmem_limit_bytes=VMEM_LIMIT),
        name="mixer_bwd",
    )(x, o_f, gqkv, gates, rqkv, lab, modb, rlog, cum_b, ind, lvl_b, bdm, sgb, srb, gnw, rnw, wout, n2w, wrh, wrl)


def _route_kernel(aff_ref, pos_ref, off_ref, *, cap, nb):
    a = aff_ref[0]
    ne = a.shape[0]
    kf = float(cap)

    def count(mask):
        return jnp.sum(jnp.sum(jnp.where(mask, 1.0, 0.0), axis=1, keepdims=True), axis=2, keepdims=True)

    def bisect(lo, hi, mid, thr):
        ok = count(a >= thr(mid)) >= kf
        return jnp.where(ok, mid, lo), jnp.where(ok, hi, mid)

    pow2 = lambda e: jnp.exp(e * LN2)
    lo_e = jnp.full((ne, 1, 1), float(MIN_EXP - 1), F32)
    hi_e = jnp.full((ne, 1, 1), 1.0, F32)
    lo_e, hi_e = lax.fori_loop(0, EXP_STEPS, lambda i, c: bisect(c[0], c[1], jnp.floor((c[0] + c[1]) * 0.5), pow2),
                               (lo_e, hi_e))
    lo, hi = lax.fori_loop(0, MANTISSA_STEPS,
                           lambda i, c: bisect(c[0], c[1], c[0] + (c[1] - c[0]) * 0.5, lambda v: v),
                           (pow2(lo_e), pow2(hi_e)))
    kth = jnp.min(jnp.min(jnp.where(a >= lo, a, jnp.inf), axis=1, keepdims=True), axis=2, keepdims=True)
    gt = a > kth
    eq = a == kth
    need = kf - count(gt)

    upper = (lax.broadcasted_iota(I32, (LANES, LANES), 0) <= lax.broadcasted_iota(I32, (LANES, LANES), 1))
    upper = jnp.where(upper, 1.0, 0.0).astype(BF16)
    ones = jnp.ones((LANES, LANES), BF16)
    lower = (lax.broadcasted_iota(I32, (ne, nb, nb), 2) < lax.broadcasted_iota(I32, (ne, nb, nb), 1))
    lower = jnp.where(lower, 1.0, 0.0).astype(BF16)

    def excl_prefix(mask):
        m = jnp.where(mask, 1.0, 0.0)
        mb = m.astype(BF16).reshape(ne * nb, LANES)
        inc = _dot(mb, upper).reshape(ne, nb, LANES)
        tot = _dot(mb, ones).reshape(ne, nb, LANES)
        offs = lax.dot_general(lower, tot.astype(BF16), (((2,), (1,)), ((0,), (0,))), preferred_element_type=F32)
        return inc - m + offs, offs

    eq_rank, _ = excl_prefix(eq)
    sel = gt | (eq & (eq_rank < need))
    rank, offs = excl_prefix(sel)
    pos_ref[0] = jnp.where(sel, rank, -1.0).astype(I32)
    off_ref[0] = offs.astype(I32)


def _route(aff4, cap):
    bsz, ne, nb, _ = aff4.shape
    spec = pl.BlockSpec((1, ne, nb, LANES), lambda b: (b, 0, 0, 0))
    return pl.pallas_call(
        functools.partial(_route_kernel, cap=cap, nb=nb),
        out_shape=(jax.ShapeDtypeStruct(aff4.shape, I32), jax.ShapeDtypeStruct(aff4.shape, I32)),
        grid=(bsz,),
        in_specs=[spec],
        out_specs=(spec, spec),
        compiler_params=pltpu.CompilerParams(dimension_semantics=("arbitrary",), vmem_limit_bytes=VMEM_LIMIT),
        name="route",
    )(aff4)


def _tile_counts(cnt_ref, b, j, ne):
    m = cnt_ref[b, j, 0]
    for e in range(1, ne):
        m = jnp.maximum(m, cnt_ref[b, j, e])
    return m


def _window_select(rel, valid, val, ne):
    c = rel.shape[1]
    w = lax.broadcasted_iota(I32, (ne, WROWS, c), 1)
    relm = jnp.where(valid, rel, -1)
    sel = jnp.where(relm[:, None, :] == w, jnp.broadcast_to(val[:, None, :], (ne, WROWS, c)), 0.0)
    return sel.reshape(ne * WROWS, c)


def _round_slots(basev, cntv, r):
    start = basev + jnp.minimum(r * WIN, cntv)
    num = jnp.clip(cntv - r * WIN, 0, WIN)
    return start, num


def _round_slots_scalar(base, cnt, r):
    return base + jnp.minimum(r * WIN, cnt), jnp.clip(cnt - r * WIN, 0, WIN)


def _align_down(v):
    shift = ALIGN.bit_length() - 1
    return (v >> shift) << shift


def _gather_kernel(base_ref, cnt_ref, pos_ref, aff_ref, basev_ref, cntv_ref, h2_ref, xe_ref,
                   xbuf, carry, zbuf, sem, zsem, nissued, *, cap, ne):
    b = pl.program_id(0)
    j = pl.program_id(1)
    last_step = (b == pl.num_programs(0) - 1) & (j == pl.num_programs(1) - 1)

    def window_copy(slot, e, row0):
        return pltpu.make_async_copy(xbuf.at[slot, pl.ds(e * WROWS, WROWS)],
                                     xe_ref.at[b, e, pl.ds(row0, WROWS)], sem.at[slot, e])

    def wait_round(g):
        @pl.when(g >= 0)
        def _():
            for e in range(ne):
                window_copy(g % 2, e, 0).wait()

    @pl.when((b == 0) & (j == 0))
    def _start():
        nissued[0] = 0
        zbuf[...] = jnp.zeros(zbuf.shape, BF16)

    @pl.when(j == 0)
    def _start_sample():
        carry[...] = jnp.zeros(carry.shape, BF16)
        cps = [pltpu.make_async_copy(zbuf, xe_ref.at[b, e, pl.ds(cap, WROWS)], zsem.at[e]) for e in range(ne)]
        for cp in cps:
            cp.start()
        for cp in cps:
            cp.wait()

    pos = pos_ref[0]
    basev = basev_ref[0, 0]
    cntv = cntv_ref[0, 0]
    h2 = h2_ref[0]
    ones = jnp.ones(pos.shape, F32)
    nrounds = (_tile_counts(cnt_ref, b, j, ne) + (WIN - 1)) // WIN

    def round_body(r, _):
        g = nissued[0]
        slot = g % 2
        start, num = _round_slots(basev, cntv, r)
        valid = (pos >= start) & (pos < start + num)
        rel = pos - _align_down(start)
        onehot = _window_select(rel, valid, ones, ne).astype(BF16)
        d = h2.shape[1]
        for col0 in range(0, d, MXU_N):
            xbuf[slot, :, col0:col0 + MXU_N] = _dot(onehot, h2[:, col0:col0 + MXU_N]).astype(BF16)
        gcol = jnp.sum(_window_select(rel, valid, aff_ref[0], ne), axis=1, keepdims=True)
        gcol = jnp.broadcast_to(gcol, (ne * WROWS, LANES))
        g_hi = gcol.astype(BF16).astype(F32)
        first_half = lax.broadcasted_iota(I32, (ne * WROWS, LANES), 1) < LANES // 2
        xbuf[slot, :, d:] = jnp.where(first_half, g_hi, gcol - g_hi).astype(BF16)
        first = []
        for e in range(ne):
            s, n = _round_slots_scalar(base_ref[b, j, e], cnt_ref[b, j, e], r)
            first.append(pl.multiple_of(_align_down(s), ALIGN))
            nxt = pl.multiple_of(_align_down(s + n) - _align_down(s), ALIGN)
            row0 = e * WROWS
            xbuf[slot, pl.ds(row0, ALIGN), :] += carry[pl.ds(e * ALIGN, ALIGN), :]
            carry[pl.ds(e * ALIGN, ALIGN), :] = xbuf[slot, pl.ds(pl.multiple_of(row0 + nxt, ALIGN), ALIGN), :]
        wait_round(g - 1)
        for e in range(ne):
            window_copy(slot, e, first[e]).start()
        nissued[0] = g + 1
        return 0

    lax.fori_loop(0, nrounds, round_body, 0)

    @pl.when(last_step)
    def _drain():
        wait_round(nissued[0] - 1)


def _gather(base, cnt, pos, aff, basev, cntv, h2, cap):
    bsz, t, d = h2.shape
    ne = pos.shape[1]
    c = TILE
    nt = t // c
    width = d + LANES
    grid_spec = pltpu.PrefetchScalarGridSpec(
        num_scalar_prefetch=2,
        grid=(bsz, nt),
        in_specs=[pl.BlockSpec((1, ne, c), lambda b, j, *_: (b, 0, j)),
                  pl.BlockSpec((1, ne, c), lambda b, j, *_: (b, 0, j)),
                  pl.BlockSpec((1, 1, ne, c), lambda b, j, *_: (b, j, 0, 0)),
                  pl.BlockSpec((1, 1, ne, c), lambda b, j, *_: (b, j, 0, 0)),
                  pl.BlockSpec((1, c, d), lambda b, j, *_: (b, j, 0))],
        out_specs=pl.BlockSpec(memory_space=pl.ANY),
        scratch_shapes=[pltpu.VMEM((2, ne * WROWS, width), BF16),
                        pltpu.VMEM((ne * ALIGN, width), BF16), pltpu.VMEM((WROWS, width), BF16),
                        pltpu.SemaphoreType.DMA((2, ne)), pltpu.SemaphoreType.DMA((ne,)),
                        pltpu.SMEM((1,), I32)],
    )
    return pl.pallas_call(
        functools.partial(_gather_kernel, cap=cap, ne=ne),
        out_shape=jax.ShapeDtypeStruct((bsz, ne, cap + WROWS, width), BF16),
        grid_spec=grid_spec,
        compiler_params=pltpu.CompilerParams(dimension_semantics=("arbitrary", "arbitrary"),
                                             vmem_limit_bytes=VMEM_LIMIT),
        name="moe_gather",
    )(base, cnt, pos, aff, basev, cntv, h2)


def _expert_kernel(xe_ref, wg_hbm, wu_hbm, wd_hbm, ye_ref,
                   wga, wua, wda, wgb, wub, wdb, stg, stu, std, sem, *, d, ne, steps):
    e = pl.program_id(0)
    k = pl.program_id(1) * pl.num_programs(2) + pl.program_id(2)
    rows_in = d // steps
    rows_out = wd_hbm.shape[1] // steps

    def chunk_copies(slot, ee, kk):
        r_in = pl.ds(pl.multiple_of(kk * rows_in, ALIGN), rows_in)
        r_out = pl.ds(pl.multiple_of(kk * rows_out, ALIGN), rows_out)
        return [pltpu.make_async_copy(wg_hbm.at[ee, r_in], stg.at[slot], sem.at[slot, 0]),
                pltpu.make_async_copy(wu_hbm.at[ee, r_in], stu.at[slot], sem.at[slot, 1]),
                pltpu.make_async_copy(wd_hbm.at[ee, r_out], std.at[slot], sem.at[slot, 2])]

    def cast_chunk(slot, kk, dst):
        r_in = pl.ds(pl.multiple_of(kk * rows_in, ALIGN), rows_in)
        r_out = pl.ds(pl.multiple_of(kk * rows_out, ALIGN), rows_out)
        dst[0][r_in, :] = stg[slot].astype(BF16)
        dst[1][r_in, :] = stu[slot].astype(BF16)
        dst[2][r_out, :] = std[slot].astype(BF16)

    @pl.when((e == 0) & (k == 0))
    def _first_expert():
        for kk in range(steps):
            cps = chunk_copies(kk % 2, 0, kk)
            for cp in cps:
                cp.start()
            for cp in cps:
                cp.wait()
            cast_chunk(kk % 2, kk, (wga, wua, wda))
        if ne > 1:
            for cp in chunk_copies(0, 1, 0):
                cp.start()

    def step(cur, nxt):
        slot = k % 2
        last_chunk = k + 1 == steps

        @pl.when(jnp.where(last_chunk, e + 2 < ne, e + 1 < ne))
        def _start_next_chunk():
            for cp in chunk_copies(1 - slot, jnp.where(last_chunk, e + 2, e + 1), jnp.where(last_chunk, 0, k + 1)):
                cp.start()

        @pl.when(e + 1 < ne)
        def _next_weights():
            for cp in chunk_copies(slot, e + 1, k):
                cp.wait()
            cast_chunk(slot, k, nxt)

        xin = xe_ref[0, 0]
        xb = xin[:, :d]
        gate = xin[:, d:d + 1].astype(F32) + xin[:, d + LANES // 2:d + LANES // 2 + 1].astype(F32)
        a = _dot(xb, cur[0][...])
        u = _dot(xb, cur[1][...])
        y = _dot((_silu(a) * u).astype(BF16), cur[2][...])
        ye_ref[0, 0] = (y * gate).astype(BF16)

    @pl.when(e % 2 == 0)
    def _even():
        step((wga, wua, wda), (wgb, wub, wdb))

    @pl.when(e % 2 == 1)
    def _odd():
        step((wgb, wub, wdb), (wga, wua, wda))


def _experts(xe, wg, wu, wd, cap):
    bsz, ne, _, width = xe.shape
    d = width - LANES
    ff = wg.shape[2]
    rows = min(EXPERT_ROWS, cap)
    steps = bsz * (cap // rows)
    assert steps % 2 == 0 and d % (steps * ALIGN) == 0 and ff % (steps * ALIGN) == 0
    hbm = pl.BlockSpec(memory_space=pl.ANY)
    return pl.pallas_call(
        functools.partial(_expert_kernel, d=d, ne=ne, steps=steps),
        out_shape=jax.ShapeDtypeStruct((bsz, ne, cap, d), BF16),
        grid=(ne, bsz, cap // rows),
        in_specs=[pl.BlockSpec((1, 1, rows, width), lambda e, b, r: (b, e, r, 0)), hbm, hbm, hbm],
        out_specs=pl.BlockSpec((1, 1, rows, d), lambda e, b, r: (b, e, r, 0)),
        scratch_shapes=[pltpu.VMEM((d, ff), BF16), pltpu.VMEM((d, ff), BF16), pltpu.VMEM((ff, d), BF16),
                        pltpu.VMEM((d, ff), BF16), pltpu.VMEM((d, ff), BF16), pltpu.VMEM((ff, d), BF16),
                        pltpu.VMEM((2, d // steps, ff), F32), pltpu.VMEM((2, d // steps, ff), F32),
                        pltpu.VMEM((2, ff // steps, d), F32), pltpu.SemaphoreType.DMA((2, 3))],
        compiler_params=pltpu.CompilerParams(dimension_semantics=("arbitrary", "arbitrary", "arbitrary"),
                                             vmem_limit_bytes=VMEM_LIMIT),
        name="moe_experts",
    )(xe, wg, wu, wd)


def _combine_kernel(base_ref, cnt_ref, pos_ref, basev_ref, cntv_ref, x1_ref, mod_ref, fnw_ref, ye_ref,
                    out_ref, stage, acc, sem, *, cap, ne):
    b = pl.program_id(0)
    j = pl.program_id(1)
    nt = pl.num_programs(1)
    step = b * nt + j
    pos = pos_ref[0]
    ones = jnp.ones(pos.shape, F32)
    basev = basev_ref[0, 0]
    cntv = cntv_ref[0, 0]
    last = cap - WROWS

    def fetch(slot, bb, jj, r):
        cps = []
        for e in range(ne):
            s, _n = _round_slots_scalar(base_ref[bb, jj, e], cnt_ref[bb, jj, e], r)
            row0 = pl.multiple_of(jnp.minimum(_align_down(s), last), ALIGN)
            cps.append(pltpu.make_async_copy(ye_ref.at[bb, e, pl.ds(row0, WROWS)],
                                             stage.at[slot, pl.ds(e * WROWS, WROWS)], sem.at[slot, e]))
        return cps

    def weights(r):
        start, num = _round_slots(basev, cntv, r)
        valid = (pos >= start) & (pos < start + num)
        return _window_select(pos - jnp.minimum(_align_down(start), last), valid, ones, ne).astype(BF16)

    def expand(w, slot):
        return _dg(w, stage[slot], _TN)

    @pl.when(step == 0)
    def _first():
        for cp in fetch(0, b, j, 0):
            cp.start()

    @pl.when(step + 1 < pl.num_programs(0) * nt)
    def _prefetch():
        wrap = j + 1 == nt
        for cp in fetch((step + 1) % 2, jnp.where(wrap, b + 1, b), jnp.where(wrap, 0, j + 1), 0):
            cp.start()

    w0 = weights(0)
    slot = step % 2
    for cp in fetch(slot, b, j, 0):
        cp.wait()
    acc[...] = expand(w0, slot)

    def round_body(r, _):
        cps = fetch(2, b, j, r)
        for cp in cps:
            cp.start()
        w = weights(r)
        for cp in cps:
            cp.wait()
        acc[...] += expand(w, 2)
        return 0

    nrounds = (_tile_counts(cnt_ref, b, j, ne) + (WIN - 1)) // WIN
    lax.fori_loop(1, nrounds, round_body, 0)
    mod = mod_ref[0]
    x2 = x1_ref[0] + mod[5:6] * acc[...]
    out_ref[0] = _rms(x2, fnw_ref[...])


def _combine(base, cnt, pos, basev, cntv, x1, modb, fnw, ye, cap):
    bsz, t, d = x1.shape
    ne = pos.shape[1]
    c = TILE
    nt = t // c
    grid_spec = pltpu.PrefetchScalarGridSpec(
        num_scalar_prefetch=2,
        grid=(bsz, nt),
        in_specs=[pl.BlockSpec((1, ne, c), lambda b, j, *_: (b, 0, j)),
                  pl.BlockSpec((1, 1, ne, c), lambda b, j, *_: (b, j, 0, 0)),
                  pl.BlockSpec((1, 1, ne, c), lambda b, j, *_: (b, j, 0, 0)),
                  pl.BlockSpec((1, c, d), lambda b, j, *_: (b, j, 0)),
                  pl.BlockSpec((1,) + modb.shape[1:], lambda b, j, *_: (b, 0, 0)),
                  pl.BlockSpec(fnw.shape, lambda b, j, *_: (0, 0)),
                  pl.BlockSpec(memory_space=pl.ANY)],
        out_specs=pl.BlockSpec((1, c, d), lambda b, j, *_: (b, j, 0)),
        scratch_shapes=[pltpu.VMEM((3, ne * WROWS, d), BF16), pltpu.VMEM((c, d), F32),
                        pltpu.SemaphoreType.DMA((3, ne))],
    )
    return pl.pallas_call(
        functools.partial(_combine_kernel, cap=cap, ne=ne),
        out_shape=jax.ShapeDtypeStruct((bsz, t, d), F32),
        grid_spec=grid_spec,
        compiler_params=pltpu.CompilerParams(dimension_semantics=("arbitrary", "arbitrary"),
                                             vmem_limit_bytes=VMEM_LIMIT),
        name="moe_combine",
    )(base, cnt, pos, basev, cntv, x1, modb, fnw, ye)


def _rope_tables(t):
    n_freq = RET_DK // 4
    inv = ROPE_BASE ** (-np.arange(n_freq, dtype=np.float64) / n_freq)
    zeros = lambda n: np.zeros((n, n_freq))

    def lanes(row_part, col_part):
        cos = np.concatenate([np.cos(row_part), np.cos(col_part)] * 2, axis=1)
        sin = np.concatenate([-np.sin(row_part), -np.sin(col_part), np.sin(row_part), np.sin(col_part)], axis=1)
        return cos, sin

    col = (np.arange(TILE) % GRID_W)[:, None] * inv
    cos_c, sin_c = lanes(zeros(TILE), col)
    cos_c[:, :n_freq] = 0.0
    cos_c[:, 2 * n_freq:3 * n_freq] = 0.0
    rows_per_tile = TILE // GRID_W
    row = np.arange(t // GRID_W)[:, None] * inv
    cos_r, sin_r = lanes(row, zeros(t // GRID_W))
    cos_r[:, n_freq:2 * n_freq] = 0.0
    cos_r[:, 3 * n_freq:] = 0.0
    row_tab = np.zeros((t // TILE, 2, 8, RET_DK))
    row_tab[:, 0, :rows_per_tile] = cos_r.reshape(t // TILE, rows_per_tile, RET_DK)
    row_tab[:, 1, :rows_per_tile] = sin_r.reshape(t // TILE, rows_per_tile, RET_DK)
    return jnp.asarray(np.stack([cos_c, sin_c]), F32), jnp.asarray(row_tab, F32)


def _mixer_weights(w_in, gate_w, gate_b):
    pts = np.cumsum(IN_WIDTHS)[:-1]
    gq, gk, gv, gz, gg, rq, rk, rv, rg = jnp.split(w_in, [int(p) for p in pts], axis=1)
    gz = jnp.pad(gz, ((0, 0), (0, GZ_PAD - 2 * GLA_RANK)))
    wall = jnp.concatenate([gq, gk, gv, gg, rq, rk, rv, rg, gz], axis=1).astype(BF16)
    gmat = jnp.zeros((GZ_PAD, 2 * GLA_QK), F32)
    gmat = gmat.at[:GLA_RANK, :GLA_QK].set(gate_w[0]).at[GLA_RANK:2 * GLA_RANK, GLA_QK:].set(gate_w[1])
    ghi = gmat.astype(BF16)
    glo = (gmat - ghi.astype(F32)).astype(BF16)
    return wall, ghi, glo, gate_b.reshape(1, 2 * GLA_QK)


def kernel(x, c, ctx, c_ctx, w_ada, b_ada, norm1_w, w_in, gla_gate_w, gla_gate_b, ret_decay_logit, gla_norm_w,
           ret_norm_w, w_out, norm2_w, w_router, w_exp_gate, w_exp_up, w_exp_down, final_norm_w):
    bsz, t, d = x.shape
    depth = w_ada.shape[0]
    assert depth == 1 and t % TILE == 0 and ctx.shape[1] == TILE
    ne = w_router.shape[2]
    cap = EC_CAPACITY_FACTOR * t // ne
    assert cap >= WROWS and cap % ALIGN == 0 and cap % min(EXPERT_ROWS, cap) == 0
    nt = t // TILE
    nb = t // LANES
    bpt = TILE // LANES

    cs = jnp.concatenate([c, c_ctx[None, :], jnp.zeros((8 - bsz - 1, d), F32)], axis=0)
    mod = _ada(cs, w_ada[0], b_ada[0][None, :])
    mod = jnp.pad(mod.reshape(8, N_ADA, d), ((0, 0), (0, 8 - N_ADA), (0, 0)))
    modb = mod[:bsz]
    modc = mod[bsz:bsz + 1]

    wall, ghi, glo, gb = _mixer_weights(w_in[0], gla_gate_w[0], gla_gate_b[0])
    n1w = norm1_w[0][None, :]
    rlog = jnp.broadcast_to(ret_decay_logit[0][:, :, None], (2, RET_HEADS, TILE)).astype(F32)
    cum_f = jnp.asarray(_chunk_cumsum_matrix(TILE, False), BF16)
    cum_b = jnp.asarray(_chunk_cumsum_matrix(TILE, True), BF16)
    ind = jnp.asarray(_chunk_indicator(TILE), BF16)
    lvl_f = jnp.asarray(_level_index(False))
    lvl_b = jnp.asarray(_level_index(True))
    bdm = jnp.asarray(_head_block_mask(), BF16)
    rope_col, rope_row = _rope_tables(t)

    sgf, sgb, srf, srb = _ctx_states(ctx, modc, n1w, wall, ghi, glo, gb, rlog, cum_f, cum_b, ind, bdm)
    o_f, gqkv, gates, rqkv, lab = _fwd(x, modb, n1w, wall, ghi, glo, gb, rope_col, rope_row, rlog, cum_f, ind, lvl_f, bdm,
                                       sgf, srf)

    wr = w_router[0].T
    wrh = wr.astype(BF16)
    wrl = (wr - wrh.astype(F32)).astype(BF16)
    x1, h2, aff = _bwd(x, o_f, gqkv, gates, rqkv, lab, modb, rlog, cum_b, ind, lvl_b, bdm, sgb, srb,
                       gla_norm_w[0][None, :], ret_norm_w[0][None, :], w_out[0].astype(BF16),
                       norm2_w[0][None, :], wrh, wrl)

    pos4, off4 = _route(aff.reshape(bsz, ne, nb, LANES), cap)
    pos = pos4.reshape(bsz, ne, t)
    boff = off4[:, :, :, 0]
    base = jnp.transpose(boff[:, :, ::bpt], (0, 2, 1))
    nxt = jnp.concatenate([base[:, 1:], jnp.full((bsz, 1, ne), cap, I32)], axis=1)
    cnt = nxt - base
    basev = jnp.broadcast_to(base[:, :, :, None], (bsz, nt, ne, TILE))
    cntv = jnp.broadcast_to(cnt[:, :, :, None], (bsz, nt, ne, TILE))

    xe = _gather(base, cnt, pos, aff, basev, cntv, h2, cap)
    ye = _experts(xe, w_exp_gate[0], w_exp_up[0], w_exp_down[0], cap)
    return _combine(base, cnt, pos, basev, cntv, x1, modb, final_norm_w[None, :], ye, cap)
```

```python
import functools

import numpy as np
import jax
import jax.numpy as jnp
from jax import lax
from jax.experimental import pallas as pl
from jax.experimental.pallas import tpu as pltpu

F32 = jnp.float32
BF16 = jnp.bfloat16
I32 = jnp.int32

GLA_HEADS = 4
GLA_DK = 64
GLA_DV = 128
GLA_RANK = 16
GLA_TAU = 16.0
RET_HEADS = 4
RET_DK = 128
RET_DV = 128
GRID_W = 64
ROPE_BASE = 10000.0
N_EXPERTS = 16
EC_CAPACITY_FACTOR = 2
N_ADA = 6
EPS = 1e-6

GLA_QK = GLA_HEADS * GLA_DK
GLA_V = GLA_HEADS * GLA_DV
RET_QK = RET_HEADS * RET_DK
RET_V = RET_HEADS * RET_DV
IN_WIDTHS = (GLA_QK, GLA_QK, GLA_V, 2 * GLA_RANK, GLA_V, RET_QK, RET_QK, RET_V, RET_V)

LANES = 128
MXU_N = 256
TILE = 256
GLA_CHUNK = 64
GLA_LEVELS = 6
WIN = 48
ALIGN = 16
WROWS = WIN + ALIGN
GZ_PAD = LANES
EXPERT_ROWS = 512
LN2 = float(np.log(2.0))
MIN_EXP = -149
EXP_STEPS = 8
MANTISSA_STEPS = 56
VMEM_LIMIT = 56 * 1024 * 1024

_NT = (((1,), (1,)), ((), ()))
_TN = (((0,), (0,)), ((), ()))


def _dot(a, b):
    return jnp.dot(a, b, preferred_element_type=F32)


def _dg(a, b, dims):
    return lax.dot_general(a, b, dims, preferred_element_type=F32)


def _split(a):
    hi = a.astype(BF16)
    lo = (a - hi.astype(F32)).astype(BF16)
    return hi, lo


def _logsig(x):
    return jnp.minimum(x, 0.0) - jnp.log(1.0 + jnp.exp(-jnp.abs(x)))


def _silu(x):
    return x / (1.0 + jnp.exp(-x))


def _rms(x, w):
    return x * lax.rsqrt(jnp.mean(x * x, axis=-1, keepdims=True) + EPS) * w


def _chunk_cumsum_matrix(c, reverse):
    i = np.arange(c)[:, None]
    t = np.arange(c)[None, :]
    same = (i // GLA_CHUNK) == (t // GLA_CHUNK)
    return (same & ((t >= i) if reverse else (t <= i))).astype(np.float32)


def _chunk_indicator(c):
    return (np.arange(c)[:, None] // GLA_CHUNK == np.arange(LANES)[None, :]).astype(np.float32)


def _level_index(reverse):
    i = np.arange(GLA_CHUNK)[:, None]
    j = np.arange(GLA_CHUNK)[None, :]
    x = i ^ j
    lvl = np.where(x > 0, np.floor(np.log2(np.maximum(x, 1))), -1).astype(np.int32)
    bad = (j < i) if reverse else (j > i)
    return np.tile(np.where(bad, 99, lvl).astype(np.int32), (1, GLA_HEADS))


def _head_block_mask():
    r = np.arange(GLA_QK)[:, None] // GLA_DK
    l = np.arange(GLA_V)[None, :] // GLA_DV
    return (r == l).astype(np.float32)


def _log_gates(gz, gpk_ref, gb_ref):
    z_hi = gz.astype(BF16).astype(F32)
    group = lax.broadcasted_iota(I32, gz.shape, 1) >> ((2 * GLA_RANK).bit_length() - 1)
    packed = jnp.where(group == 1, gz - z_hi, z_hi).astype(BF16)
    return _logsig(_dot(packed, gpk_ref[...]) + gb_ref[...]) * (1.0 / GLA_TAU)


def _project(xn, wall_ref, gpk_ref, gb_ref):
    proj = _dot(xn.astype(BF16), wall_ref[...])
    o = 0
    out = []
    for w in (GLA_QK, GLA_QK, GLA_V, GLA_V, RET_QK, RET_QK, RET_V, RET_V, GZ_PAD):
        out.append(proj[:, o:o + w])
        o += w
    gq, gk, gv, gg, rq, rk, rv, rg, gz = out
    log_a = _log_gates(gz, gpk_ref, gb_ref)
    return gq * (GLA_DK ** -0.5), gk, gv, gg, rq, rk * (RET_DK ** -0.5), rv, rg, log_a


def _rope(a, cos, sin):
    outs = []
    for h in range(RET_HEADS):
        ah = a[:, h * RET_DK:(h + 1) * RET_DK]
        outs.append(ah * cos + pltpu.roll(ah, RET_DK // 2, 1) * sin)
    return jnp.concatenate(outs, axis=1)


def _stack_heads(a):
    head = lax.broadcasted_iota(I32, a.shape, 1) >> 6
    zero = jnp.zeros_like(a)
    return jnp.concatenate([jnp.where(head == h, a, zero) for h in range(GLA_HEADS)], axis=0)


def _gate_sums(g):
    hi, lo = _split(g)
    return jnp.concatenate([hi, lo], axis=1)


def _level_log_decay(level, g, b, b_ref, row0, reverse):
    n = GLA_CHUNK
    row = lax.broadcasted_iota(I32, (n, GLA_QK), 0)
    upper = ((row >> level) & 1) == 1
    if level == 0:
        return jnp.where(upper, 0.0, g) if reverse else jnp.where(upper, g, 0.0)
    if level == 1:
        nxt = pltpu.roll(g, n - 1, 0)
        prv = pltpu.roll(g, 1, 0)
        r = row & 3
        if reverse:
            return jnp.where(r == 0, g + nxt, jnp.where(r == 1, g, jnp.where(r == 2, 0.0, prv)))
        return jnp.where(r == 0, nxt, jnp.where(r == 1, 0.0, jnp.where(r == 2, g, g + prv)))
    m = 1 << level
    anchors = [jnp.broadcast_to(b_ref[pl.ds(row0 + blk + (m if reverse else m - 1), 1), :], (2 * m, GLA_QK))
               for blk in range(0, n, 2 * m)]
    d = b - (jnp.concatenate(anchors, axis=0) if len(anchors) > 1 else anchors[0])
    return jnp.where(upper, -d, d) if reverse else jnp.where(upper, d, -d)


def _interleave(*stages):
    order = sorted((span * (k + 0.5) / n, i) for i, (_, n, span) in enumerate(stages) for k in range(n))
    for _, i in order:
        next(stages[i][0])
    for gen, _, _ in stages:
        for _ in gen:
            raise AssertionError("stage has more pieces than declared")


def _gla_steps(qkv_ref, g_ref, cum_ref, ind_ref, lvl_ref, bdm_ref, s_ref, b_ref, reverse, emit):
    c = g_ref.shape[0]
    g2 = _gate_sums(g_ref[...])
    r = _dot(cum_ref[...], g2)
    b_ref[...] = r[:, :GLA_QK] + r[:, GLA_QK:]
    cs = _dg(g2, ind_ref[...], _TN)
    tot = cs[:GLA_QK] + cs[GLA_QK:]
    yield
    nchunk = c // GLA_CHUNK
    for ci in (reversed(range(nchunk)) if reverse else range(nchunk)):
        row0 = ci * GLA_CHUNK
        rows = pl.ds(row0, GLA_CHUNK)
        kc = qkv_ref[rows, GLA_QK:2 * GLA_QK].astype(F32)
        vc = qkv_ref[rows, 2 * GLA_QK:]
        gc = g_ref[rows, :]
        bc = b_ref[rows, :]
        bdm = bdm_ref[...]
        s_bd = s_ref[...]
        if emit is not None:
            qc = qkv_ref[rows, :GLA_QK].astype(F32)
            lvl = lvl_ref[...]
            scores = jnp.zeros((GLA_CHUNK, GLA_HEADS * GLA_CHUNK), F32)
            for level in range(GLA_LEVELS):
                e = jnp.exp(_level_log_decay(level, gc, bc, b_ref, row0, reverse))
                p = _dg((qc * e).astype(BF16), _stack_heads((kc * e).astype(BF16)), _NT)
                scores = jnp.where(lvl == level, p, scores)
            p = _dg(qc.astype(BF16), _stack_heads(kc.astype(BF16)), _NT)
            scores = jnp.where(lvl == -1, p, scores)
            v_bd = jnp.concatenate([vc] * GLA_HEADS, axis=0) * bdm
            emit(row0, _dot(scores.astype(BF16), v_bd) + _dot((qc * jnp.exp(bc)).astype(BF16), s_bd.astype(BF16)))
        b_end = b_ref[pl.ds(row0 if reverse else row0 + GLA_CHUNK - 1, 1), :]
        kv = _dg((kc * jnp.exp(b_end - bc)).astype(BF16), vc, _TN)
        e_col = jnp.exp(jnp.broadcast_to(tot[:, ci:ci + 1], (GLA_QK, GLA_V)))
        s_ref[...] = e_col * s_bd + jnp.where(bdm > 0, kv, 0.0)
        yield


def _ret_decays(rlog_ref, c, reverse):
    lg = _logsig(rlog_ref[0])
    ii = lax.broadcasted_iota(I32, (c, c), 0)
    jj = lax.broadcasted_iota(I32, (c, c), 1)
    rel = ((jj - ii) if reverse else (ii - jj)).astype(F32)
    pos = lax.broadcasted_iota(I32, (c, RET_DK), 0).astype(F32)
    dmats, qd, kd, cd = [], [], [], []
    for h in range(RET_HEADS):
        lh = lg[h:h + 1, :]
        dmats.append(jnp.where(rel >= 0, jnp.exp(lh * jnp.maximum(rel, 0.0)), 0.0))
        l1 = lh[:, :RET_DK]
        qd.append(jnp.exp(l1 * ((c - pos) if reverse else (pos + 1.0))))
        kd.append(jnp.exp(l1 * (pos if reverse else (c - 1.0 - pos))))
        cd.append(jnp.exp(l1 * float(c)))
    return dmats, jnp.concatenate(qd, axis=1), jnp.concatenate(kd, axis=1), jnp.concatenate(cd, axis=1)


def _ret_steps(qkv_ref, dmat_ref, qdec_ref, kdec_ref, cdec_ref, s_ref, emit):
    for h in range(RET_HEADS):
        sl = slice(h * RET_DK, (h + 1) * RET_DK)
        qb = qkv_ref[:, h * RET_DK:(h + 1) * RET_DK]
        kb = qkv_ref[:, RET_QK + h * RET_DK:RET_QK + (h + 1) * RET_DK]
        vh = qkv_ref[:, 2 * RET_QK + h * RET_DV:2 * RET_QK + (h + 1) * RET_DV]
        sc = _dg(qb, kb, _NT) * dmat_ref[h]
        s = s_ref[h]
        emit(h, _dot(sc.astype(BF16), vh) + _dot((qb.astype(F32) * qdec_ref[:, sl]).astype(BF16), s.astype(BF16)))
        s_ref[h] = cdec_ref[:, sl] * s + _dg((kb.astype(F32) * kdec_ref[:, sl]).astype(BF16), vh, _TN)
        yield


def _proj_steps(x_ref, mod_ref, n1w_ref, wall_ref, gpk_ref, gb_ref, rope_col_ref, rope_row_ref,
                gqkv_ref, gates_ref, rqkv_ref, lab_ref, nxt_g, nxt_r, nxt_l):
    mod = mod_ref[0]
    hb = (_rms(x_ref[0], n1w_ref[...]) * (1.0 + mod[1:2]) + mod[0:1]).astype(BF16)
    yield

    def cols(o, w):
        return _dot(hb, wall_ref[:, o:o + w])

    def put(val, o, out_ref, stage_ref):
        val = val.astype(BF16)
        out_ref[0, :, o:o + val.shape[1]] = val
        if stage_ref is not None:
            stage_ref[:, o:o + val.shape[1]] = val

    put(cols(0, GLA_QK) * (GLA_DK ** -0.5), 0, gqkv_ref, nxt_g)
    put(cols(GLA_QK, GLA_QK), GLA_QK, gqkv_ref, nxt_g)
    yield
    put(cols(2 * GLA_QK, GLA_V), 2 * GLA_QK, gqkv_ref, nxt_g)
    yield
    o = 2 * GLA_QK + GLA_V
    put(cols(o, GLA_V), 0, gates_ref, None)
    yield
    o += GLA_V
    rows_of = lambda i: jnp.concatenate(
        [jnp.broadcast_to(rope_row_ref[0, i, q:q + 1, :], (GRID_W, RET_DK)) for q in range(TILE // GRID_W)], axis=0)
    cos = rows_of(0) + rope_col_ref[0]
    sin = rows_of(1) + rope_col_ref[1]
    put(_rope(cols(o, RET_QK), cos, sin), 0, rqkv_ref, nxt_r)
    yield
    o += RET_QK
    put(_rope(cols(o, RET_QK) * (RET_DK ** -0.5), cos, sin), RET_QK, rqkv_ref, nxt_r)
    yield
    o += RET_QK
    put(cols(o, RET_V), 2 * RET_QK, rqkv_ref, nxt_r)
    yield
    o += RET_V
    put(cols(o, RET_V), GLA_V, gates_ref, None)
    yield
    o += RET_V
    log_a = _log_gates(cols(o, GZ_PAD), gpk_ref, gb_ref)
    nxt_l[...] = log_a[:, :GLA_QK]
    lab_ref[0] = log_a[:, GLA_QK:]
    yield


def _ada_kernel(c_ref, w_ref, b_ref, o_ref):
    s_hi, s_lo = _split(_silu(c_ref[...]))
    w_hi, w_lo = _split(w_ref[...])
    o_ref[...] = _dot(s_hi, w_hi) + _dot(s_lo, w_hi) + _dot(s_hi, w_lo) + b_ref[...]


def _ada(cs, w, b):
    rows, d = cs.shape
    n = w.shape[1]
    tn = 1536
    return pl.pallas_call(
        _ada_kernel,
        out_shape=jax.ShapeDtypeStruct((rows, n), F32),
        grid=(n // tn,),
        in_specs=[pl.BlockSpec((rows, d), lambda i: (0, 0)),
                  pl.BlockSpec((d, tn), lambda i: (0, i)),
                  pl.BlockSpec((1, tn), lambda i: (0, i))],
        out_specs=pl.BlockSpec((rows, tn), lambda i: (0, i)),
        compiler_params=pltpu.CompilerParams(dimension_semantics=("arbitrary",), vmem_limit_bytes=VMEM_LIMIT),
        name="ada",
    )(cs, w, b)


def _ctx_kernel(ctx_ref, mod_ref, n1w_ref, wall_ref, gpk_ref, gb_ref, rlog_ref, cumf_ref, cumb_ref,
                ind_ref, bdm_ref, sgf_ref, sgb_ref, srf_ref, srb_ref, b_scr, kv_scr, g_scr, *, c):
    mod = mod_ref[0]
    hc = _rms(ctx_ref[0], n1w_ref[...]) * (1.0 + mod[1:2]) + mod[0:1]
    _, gk, gv, _, _, rk, rv, _, log_a = _project(hc, wall_ref, gpk_ref, gb_ref)
    kv_scr[:, GLA_QK:] = jnp.concatenate([gk, gv], axis=1).astype(BF16)
    rvb = rv.astype(BF16)
    for d, (cum_ref, out_g, out_r) in enumerate(((cumf_ref, sgf_ref, srf_ref), (cumb_ref, sgb_ref, srb_ref))):
        out_g[0] = jnp.zeros((GLA_QK, GLA_V), F32)
        g_scr[...] = log_a[:, d * GLA_QK:(d + 1) * GLA_QK]
        _interleave((_gla_steps(kv_scr, g_scr, cum_ref, ind_ref, None, bdm_ref, out_g.at[0], b_scr, bool(d), None),
                     1 + c // GLA_CHUNK, 1.0))
        _, _, kdec, _ = _ret_decays(rlog_ref.at[d:d + 1], c, reverse=bool(d))
        for h in range(RET_HEADS):
            sl = slice(h * RET_DK, (h + 1) * RET_DK)
            out_r[0, h] = _dg((rk[:, sl] * kdec[:, sl]).astype(BF16), rvb[:, h * RET_DV:(h + 1) * RET_DV], _TN)


def _ctx_states(ctx, modc, n1w, wall, gpk, gb, rlog, cum_f, cum_b, ind, bdm):
    bsz, c, d = ctx.shape
    const = lambda a: pl.BlockSpec(a.shape, lambda b: (0,) * a.ndim)
    consts = (modc, n1w, wall, gpk, gb, rlog, cum_f, cum_b, ind, bdm)
    return pl.pallas_call(
        functools.partial(_ctx_kernel, c=c),
        out_shape=(jax.ShapeDtypeStruct((bsz, GLA_QK, GLA_V), F32),
                   jax.ShapeDtypeStruct((bsz, GLA_QK, GLA_V), F32),
                   jax.ShapeDtypeStruct((bsz, RET_HEADS, RET_DK, RET_DV), F32),
                   jax.ShapeDtypeStruct((bsz, RET_HEADS, RET_DK, RET_DV), F32)),
        grid=(bsz,),
        in_specs=[pl.BlockSpec((1, c, d), lambda b: (b, 0, 0))] + [const(a) for a in consts],
        out_specs=(pl.BlockSpec((1, GLA_QK, GLA_V), lambda b: (b, 0, 0)),
                   pl.BlockSpec((1, GLA_QK, GLA_V), lambda b: (b, 0, 0)),
                   pl.BlockSpec((1, RET_HEADS, RET_DK, RET_DV), lambda b: (b, 0, 0, 0)),
                   pl.BlockSpec((1, RET_HEADS, RET_DK, RET_DV), lambda b: (b, 0, 0, 0))),
        scratch_shapes=[pltpu.VMEM((c, GLA_QK), F32), pltpu.VMEM((c, 2 * GLA_QK + GLA_V), BF16),
                        pltpu.VMEM((c, GLA_QK), F32)],
        compiler_params=pltpu.CompilerParams(dimension_semantics=("arbitrary",), vmem_limit_bytes=VMEM_LIMIT),
        name="ctx_states",
    )(ctx, *consts)


def _fwd_kernel(x_ref, mod_ref, n1w_ref, wall_ref, gpk_ref, gb_ref, rope_col_ref, rope_row_ref, rlog_ref,
                cum_ref, ind_ref, lvl_ref, bdm_ref, sg0_ref, sr0_ref,
                of_ref, gqkv_ref, gates_ref, rqkv_ref, lab_ref,
                sg_scr, sr_scr, dmat_scr, qdec_scr, kdec_scr, cdec_scr, b_scr,
                cur_g, cur_r, cur_l, nxt_g, nxt_r, nxt_l, *, c):
    j = pl.program_id(1)

    @pl.when(j == 0)
    def _first():
        sg_scr[...] = jnp.zeros(sg_scr.shape, F32)
        sr_scr[...] = jnp.zeros(sr_scr.shape, F32)
        nxt_g[...] = jnp.zeros(nxt_g.shape, BF16)
        nxt_r[...] = jnp.zeros(nxt_r.shape, BF16)
        nxt_l[...] = jnp.zeros(nxt_l.shape, F32)
        dmats, qd, kd, cd = _ret_decays(rlog_ref, c, reverse=False)
        for h in range(RET_HEADS):
            dmat_scr[h] = dmats[h]
        qdec_scr[...] = qd
        kdec_scr[...] = kd
        cdec_scr[...] = cd

    @pl.when(j == 1)
    def _seed():
        sg_scr[...] = sg0_ref[0]
        sr_scr[...] = sr0_ref[0]

    cur_g[...] = nxt_g[...]
    cur_r[...] = nxt_r[...]
    cur_l[...] = nxt_l[...]

    def emit_gla(row0, out):
        of_ref[0, pl.ds(row0, GLA_CHUNK), 0:GLA_V] = out

    def emit_ret(h, out):
        of_ref[0, :, GLA_V + h * RET_DV:GLA_V + (h + 1) * RET_DV] = out

    gla = _gla_steps(cur_g, cur_l, cum_ref, ind_ref, lvl_ref, bdm_ref, sg_scr, b_scr, False, emit_gla)
    ret = _ret_steps(cur_r, dmat_scr, qdec_scr, kdec_scr, cdec_scr, sr_scr, emit_ret)
    proj = _proj_steps(x_ref, mod_ref, n1w_ref, wall_ref, gpk_ref, gb_ref, rope_col_ref, rope_row_ref,
                       gqkv_ref, gates_ref, rqkv_ref, lab_ref, nxt_g, nxt_r, nxt_l)
    _interleave((gla, 1 + c // GLA_CHUNK, 1.0), (ret, RET_HEADS, 1.0), (proj, 9, 1.0))


def _fwd(x, modb, n1w, wall, gpk, gb, rope_col, rope_row, rlog, cum_f, ind, lvl_f, bdm, sgf, srf):
    bsz, t, d = x.shape
    c = TILE
    nt = t // c
    const = lambda shape: pl.BlockSpec(shape, lambda b, j: (0,) * len(shape))
    proj_tile = lambda w: pl.BlockSpec((1, c, w), lambda b, j: (b, jnp.minimum(j, nt - 1), 0))
    scan_tile = lambda w: pl.BlockSpec((1, c, w), lambda b, j: (b, jnp.maximum(j - 1, 0), 0))
    rope_tile = pl.BlockSpec((1,) + rope_row.shape[1:], lambda b, j: (jnp.minimum(j, nt - 1), 0, 0, 0))
    mixw = GLA_V + RET_V
    gw, rw = 2 * GLA_QK + GLA_V, 2 * RET_QK + RET_V
    return pl.pallas_call(
        functools.partial(_fwd_kernel, c=c),
        out_shape=(jax.ShapeDtypeStruct((bsz, t, mixw), F32),
                   jax.ShapeDtypeStruct((bsz, t, gw), BF16),
                   jax.ShapeDtypeStruct((bsz, t, GLA_V + RET_V), BF16),
                   jax.ShapeDtypeStruct((bsz, t, rw), BF16),
                   jax.ShapeDtypeStruct((bsz, t, GLA_QK), F32)),
        grid=(bsz, nt + 1),
        in_specs=[proj_tile(d),
                  pl.BlockSpec((1,) + modb.shape[1:], lambda b, j: (b, 0, 0)),
                  const(n1w.shape), const(wall.shape), const(gpk.shape), const(gb.shape),
                  const(rope_col.shape), rope_tile,
                  pl.BlockSpec((1,) + rlog.shape[1:], lambda b, j: (0, 0, 0)),
                  const(cum_f.shape), const(ind.shape), const(lvl_f.shape), const(bdm.shape),
                  pl.BlockSpec((1, GLA_QK, GLA_V), lambda b, j: (b, 0, 0)),
                  pl.BlockSpec((1, RET_HEADS, RET_DK, RET_DV), lambda b, j: (b, 0, 0, 0))],
        out_specs=(scan_tile(mixw), proj_tile(gw), proj_tile(GLA_V + RET_V), proj_tile(rw), proj_tile(GLA_QK)),
        scratch_shapes=[pltpu.VMEM((GLA_QK, GLA_V), F32),
                        pltpu.VMEM((RET_HEADS, RET_DK, RET_DV), F32),
                        pltpu.VMEM((RET_HEADS, c, c), F32),
                        pltpu.VMEM((c, RET_QK), F32),
                        pltpu.VMEM((c, RET_QK), F32),
                        pltpu.VMEM((1, RET_QK), F32),
                        pltpu.VMEM((c, GLA_QK), F32),
                        pltpu.VMEM((c, gw), BF16), pltpu.VMEM((c, rw), BF16), pltpu.VMEM((c, GLA_QK), F32),
                        pltpu.VMEM((c, gw), BF16), pltpu.VMEM((c, rw), BF16), pltpu.VMEM((c, GLA_QK), F32)],
        compiler_params=pltpu.CompilerParams(dimension_semantics=("arbitrary", "arbitrary"),
                                             vmem_limit_bytes=VMEM_LIMIT),
        name="mixer_fwd",
    )(x, modb, n1w, wall, gpk, gb, rope_col, rope_row, rlog, cum_f, ind, lvl_f, bdm, sgf, srf)


def _bwd_kernel(x_ref, of_ref, gqkv_ref, gates_ref, rqkv_ref, lab_ref, mod_ref, rlog_ref,
                cum_ref, ind_ref, lvl_ref, bdm_ref,
                sg0_ref, sr0_ref, gnw_ref, rnw_ref, wout_ref, n2w_ref, wrh_ref, wrl_ref,
                x1_ref, h2_ref, aff_ref,
                sg_scr, sr_scr, dmat_scr, qdec_scr, kdec_scr, cdec_scr, b_scr, cur_m, nxt_m, mixb, *, c):
    j = pl.program_id(1)

    @pl.when(j == 0)
    def _first():
        sg_scr[...] = sg0_ref[0]
        sr_scr[...] = sr0_ref[0]
        nxt_m[...] = jnp.zeros(nxt_m.shape, F32)
        dmats, qd, kd, cd = _ret_decays(rlog_ref, c, reverse=True)
        for h in range(RET_HEADS):
            dmat_scr[h] = dmats[h]
        qdec_scr[...] = qd
        kdec_scr[...] = kd
        cdec_scr[...] = cd

    cur_m[...] = nxt_m[...]

    def emit_gla(row0, out):
        rows = pl.ds(row0, GLA_CHUNK)
        nxt_m[rows, 0:GLA_V] = of_ref[0, rows, 0:GLA_V] + out

    def emit_ret(h, out):
        cols = slice(GLA_V + h * RET_DV, GLA_V + (h + 1) * RET_DV)
        nxt_m[:, cols] = of_ref[0, :, cols] + out

    gla = _gla_steps(gqkv_ref.at[0], lab_ref.at[0], cum_ref, ind_ref, lvl_ref, bdm_ref, sg_scr, b_scr, True,
                     emit_gla)
    ret = _ret_steps(rqkv_ref.at[0], dmat_scr, qdec_scr, kdec_scr, cdec_scr, sr_scr, emit_ret)

    def epilogue():
        for h in range(GLA_HEADS + RET_HEADS):
            sl = slice(h * GLA_DV, (h + 1) * GLA_DV)
            oh = cur_m[:, sl]
            if h < GLA_HEADS:
                y = oh * lax.rsqrt(jnp.mean(oh * oh, axis=-1, keepdims=True) + EPS) * gnw_ref[:, sl]
            else:
                dv = oh - jnp.mean(oh, axis=-1, keepdims=True)
                y = (dv * lax.rsqrt(jnp.mean(dv * dv, axis=-1, keepdims=True) + EPS)
                     * rnw_ref[:, h * RET_DV - GLA_V:(h + 1) * RET_DV - GLA_V])
            mixb[:, sl] = (y * _silu(gates_ref[0, :, sl].astype(F32))).astype(BF16)
            yield
        mod = mod_ref[0]
        d = x_ref.shape[2]
        step = d // 4
        for p in range(4):
            cs = slice(p * step, (p + 1) * step)
            x1_ref[0, :, cs] = x_ref[0, :, cs] + mod[2:3, cs] * _dot(mixb[...], wout_ref[:, cs])
            yield
        h2 = _rms(x1_ref[0], n2w_ref[...]) * (1.0 + mod[4:5]) + mod[3:4]
        h_hi, h_lo = _split(h2)
        h2_ref[0] = h_hi
        yield
        wrh = wrh_ref[...]
        logit = _dg(wrh, h_hi, _NT) + _dg(wrh, h_lo, _NT) + _dg(wrl_ref[...], h_hi, _NT)
        ex = jnp.exp(logit - jnp.max(logit, axis=0, keepdims=True))
        aff_ref[0] = ex / jnp.sum(ex, axis=0, keepdims=True)
        yield

    _interleave((gla, 1 + c // GLA_CHUNK, 1.0), (ret, RET_HEADS, 1.0),
                (epilogue(), GLA_HEADS + RET_HEADS + 6, 0.8))


def _bwd(x, o_f, gqkv, gates, rqkv, lab, modb, rlog, cum_b, ind, lvl_b, bdm, sgb, srb, gnw, rnw, wout, n2w,
         wrh, wrl):
    bsz, t, d = x.shape
    c = TILE
    nt = t // c
    ne = wrh.shape[0]
    const = lambda shape: pl.BlockSpec(shape, lambda b, j: (0,) * len(shape))
    scan_at = lambda j: nt - 1 - jnp.minimum(j, nt - 1)
    mix_at = lambda j: nt - 1 - jnp.maximum(j - 1, 0)
    scan_tile = lambda w: pl.BlockSpec((1, c, w), lambda b, j: (b, scan_at(j), 0))
    tile = lambda w: pl.BlockSpec((1, c, w), lambda b, j: (b, mix_at(j), 0))
    return pl.pallas_call(
        functools.partial(_bwd_kernel, c=c),
        out_shape=(jax.ShapeDtypeStruct((bsz, t, d), F32),
                   jax.ShapeDtypeStruct((bsz, t, d), BF16),
                   jax.ShapeDtypeStruct((bsz, ne, t), F32)),
        grid=(bsz, nt + 1),
        in_specs=[tile(d), scan_tile(o_f.shape[2]), scan_tile(gqkv.shape[2]), tile(gates.shape[2]),
                  scan_tile(rqkv.shape[2]), scan_tile(lab.shape[2]),
                  pl.BlockSpec((1,) + modb.shape[1:], lambda b, j: (b, 0, 0)),
                  pl.BlockSpec((1,) + rlog.shape[1:], lambda b, j: (1, 0, 0)),
                  const(cum_b.shape), const(ind.shape), const(lvl_b.shape), const(bdm.shape),
                  pl.BlockSpec((1, GLA_QK, GLA_V), lambda b, j: (b, 0, 0)),
                  pl.BlockSpec((1, RET_HEADS, RET_DK, RET_DV), lambda b, j: (b, 0, 0, 0)),
                  const(gnw.shape), const(rnw.shape), const(wout.shape), const(n2w.shape),
                  const(wrh.shape), const(wrl.shape)],
        out_specs=(tile(d), tile(d), pl.BlockSpec((1, ne, c), lambda b, j: (b, 0, mix_at(j)))),
        scratch_shapes=[pltpu.VMEM((GLA_QK, GLA_V), F32),
                        pltpu.VMEM((RET_HEADS, RET_DK, RET_DV), F32),
                        pltpu.VMEM((RET_HEADS, c, c), F32),
                        pltpu.VMEM((c, RET_QK), F32),
                        pltpu.VMEM((c, RET_QK), F32),
                        pltpu.VMEM((1, RET_QK), F32),
                        pltpu.VMEM((c, GLA_QK), F32),
                        pltpu.VMEM((c, GLA_V + RET_V), F32),
                        pltpu.VMEM((c, GLA_V + RET_V), F32),
                        pltpu.VMEM((c, GLA_V + RET_V), BF16)],
        compiler_params=pltpu.CompilerParams(dimension_semantics=("arbitrary", "arbitrary"),
                                             vmem_limit_bytes=VMEM_LIMIT),
        name="mixer_bwd",
    )(x, o_f, gqkv, gates, rqkv, lab, modb, rlog, cum_b, ind, lvl_b, bdm, sgb, srb, gnw, rnw, wout, n2w, wrh, wrl)


def _route_kernel(aff_ref, pos_ref, off_ref, *, cap, nb):
    a = aff_ref[0]
    ne = a.shape[0]
    kf = float(cap)

    def count(mask):
        return jnp.sum(jnp.sum(jnp.where(mask, 1.0, 0.0), axis=1, keepdims=True), axis=2, keepdims=True)

    def bisect(lo, hi, mid, thr):
        ok = count(a >= thr(mid)) >= kf
        return jnp.where(ok, mid, lo), jnp.where(ok, hi, mid)

    pow2 = lambda e: jnp.exp(e * LN2)
    lo_e = jnp.full((ne, 1, 1), float(MIN_EXP - 1), F32)
    hi_e = jnp.full((ne, 1, 1), 1.0, F32)
    lo_e, hi_e = lax.fori_loop(0, EXP_STEPS, lambda i, c: bisect(c[0], c[1], jnp.floor((c[0] + c[1]) * 0.5), pow2),
                               (lo_e, hi_e))
    lo, hi = lax.fori_loop(0, MANTISSA_STEPS,
                           lambda i, c: bisect(c[0], c[1], c[0] + (c[1] - c[0]) * 0.5, lambda v: v),
                           (pow2(lo_e), pow2(hi_e)))
    kth = jnp.min(jnp.min(jnp.where(a >= lo, a, jnp.inf), axis=1, keepdims=True), axis=2, keepdims=True)
    gt = a > kth
    eq = a == kth
    need = kf - count(gt)

    upper = (lax.broadcasted_iota(I32, (LANES, LANES), 0) <= lax.broadcasted_iota(I32, (LANES, LANES), 1))
    upper = jnp.where(upper, 1.0, 0.0).astype(BF16)
    ones = jnp.ones((LANES, LANES), BF16)
    lower = (lax.broadcasted_iota(I32, (ne, nb, nb), 2) < lax.broadcasted_iota(I32, (ne, nb, nb), 1))
    lower = jnp.where(lower, 1.0, 0.0).astype(BF16)

    def excl_prefix(mask):
        m = jnp.where(mask, 1.0, 0.0)
        mb = m.astype(BF16).reshape(ne * nb, LANES)
        inc = _dot(mb, upper).reshape(ne, nb, LANES)
        tot = _dot(mb, ones).reshape(ne, nb, LANES)
        offs = lax.dot_general(lower, tot.astype(BF16), (((2,), (1,)), ((0,), (0,))), preferred_element_type=F32)
        return inc - m + offs, offs

    eq_rank, _ = excl_prefix(eq)
    sel = gt | (eq & (eq_rank < need))
    rank, offs = excl_prefix(sel)
    pos_ref[0] = jnp.where(sel, rank, -1.0).astype(I32)
    off_ref[0] = offs.astype(I32)


def _route(aff4, cap):
    bsz, ne, nb, _ = aff4.shape
    spec = pl.BlockSpec((1, ne, nb, LANES), lambda b: (b, 0, 0, 0))
    return pl.pallas_call(
        functools.partial(_route_kernel, cap=cap, nb=nb),
        out_shape=(jax.ShapeDtypeStruct(aff4.shape, I32), jax.ShapeDtypeStruct(aff4.shape, I32)),
        grid=(bsz,),
        in_specs=[spec],
        out_specs=(spec, spec),
        compiler_params=pltpu.CompilerParams(dimension_semantics=("arbitrary",), vmem_limit_bytes=VMEM_LIMIT),
        name="route",
    )(aff4)


def _tile_counts(cnt_ref, b, j, ne):
    m = cnt_ref[b, j, 0]
    for e in range(1, ne):
        m = jnp.maximum(m, cnt_ref[b, j, e])
    return m


def _window_select(rel, valid, val, ne):
    c = rel.shape[1]
    w = lax.broadcasted_iota(I32, (ne, WROWS, c), 1)
    relm = jnp.where(valid, rel, -1)
    sel = jnp.where(relm[:, None, :] == w, jnp.broadcast_to(val[:, None, :], (ne, WROWS, c)), 0.0)
    return sel.reshape(ne * WROWS, c)


def _round_slots(basev, cntv, r):
    start = basev + jnp.minimum(r * WIN, cntv)
    num = jnp.clip(cntv - r * WIN, 0, WIN)
    return start, num


def _round_slots_scalar(base, cnt, r):
    return base + jnp.minimum(r * WIN, cnt), jnp.clip(cnt - r * WIN, 0, WIN)


def _align_down(v):
    shift = ALIGN.bit_length() - 1
    return (v >> shift) << shift


def _gather_kernel(base_ref, cnt_ref, pos_ref, aff_ref, basev_ref, cntv_ref, h2_ref, xe_ref,
                   xbuf, carry, zbuf, sem, zsem, nissued, *, cap, ne):
    b = pl.program_id(0)
    j = pl.program_id(1)
    last_step = (b == pl.num_programs(0) - 1) & (j == pl.num_programs(1) - 1)

    def window_copy(slot, e, row0):
        return pltpu.make_async_copy(xbuf.at[slot, pl.ds(e * WROWS, WROWS)],
                                     xe_ref.at[b, e, pl.ds(row0, WROWS)], sem.at[slot, e])

    def wait_round(g):
        @pl.when(g >= 0)
        def _():
            for e in range(ne):
                window_copy(g % 2, e, 0).wait()

    @pl.when((b == 0) & (j == 0))
    def _start():
        nissued[0] = 0
        zbuf[...] = jnp.zeros(zbuf.shape, BF16)

    @pl.when(j == 0)
    def _start_sample():
        carry[...] = jnp.zeros(carry.shape, BF16)
        cps = [pltpu.make_async_copy(zbuf, xe_ref.at[b, e, pl.ds(cap, WROWS)], zsem.at[e]) for e in range(ne)]
        for cp in cps:
            cp.start()
        for cp in cps:
            cp.wait()

    pos = pos_ref[0]
    basev = basev_ref[0, 0]
    cntv = cntv_ref[0, 0]
    h2 = h2_ref[0]
    ones = jnp.ones(pos.shape, F32)
    nrounds = (_tile_counts(cnt_ref, b, j, ne) + (WIN - 1)) // WIN

    def round_body(r, _):
        g = nissued[0]
        slot = g % 2
        start, num = _round_slots(basev, cntv, r)
        valid = (pos >= start) & (pos < start + num)
        rel = pos - _align_down(start)
        onehot = _window_select(rel, valid, ones, ne).astype(BF16)
        d = h2.shape[1]
        for col0 in range(0, d, MXU_N):
            xbuf[slot, :, col0:col0 + MXU_N] = _dot(onehot, h2[:, col0:col0 + MXU_N]).astype(BF16)
        gcol = jnp.sum(_window_select(rel, valid, aff_ref[0], ne), axis=1, keepdims=True)
        gcol = jnp.broadcast_to(gcol, (ne * WROWS, LANES))
        g_hi = gcol.astype(BF16).astype(F32)
        first_half = lax.broadcasted_iota(I32, (ne * WROWS, LANES), 1) < LANES // 2
        xbuf[slot, :, d:] = jnp.where(first_half, g_hi, gcol - g_hi).astype(BF16)
        first = []
        for e in range(ne):
            s, n = _round_slots_scalar(base_ref[b, j, e], cnt_ref[b, j, e], r)
            first.append(pl.multiple_of(_align_down(s), ALIGN))
            nxt = pl.multiple_of(_align_down(s + n) - _align_down(s), ALIGN)
            row0 = e * WROWS
            xbuf[slot, pl.ds(row0, ALIGN), :] += carry[pl.ds(e * ALIGN, ALIGN), :]
            carry[pl.ds(e * ALIGN, ALIGN), :] = xbuf[slot, pl.ds(pl.multiple_of(row0 + nxt, ALIGN), ALIGN), :]
        wait_round(g - 1)
        for e in range(ne):
            window_copy(slot, e, first[e]).start()
        nissued[0] = g + 1
        return 0

    lax.fori_loop(0, nrounds, round_body, 0)

    @pl.when(last_step)
    def _drain():
        wait_round(nissued[0] - 1)


def _gather(base, cnt, pos, aff, basev, cntv, h2, cap):
    bsz, t, d = h2.shape
    ne = pos.shape[1]
    c = TILE
    nt = t // c
    width = d + LANES
    grid_spec = pltpu.PrefetchScalarGridSpec(
        num_scalar_prefetch=2,
        grid=(bsz, nt),
        in_specs=[pl.BlockSpec((1, ne, c), lambda b, j, *_: (b, 0, j)),
                  pl.BlockSpec((1, ne, c), lambda b, j, *_: (b, 0, j)),
                  pl.BlockSpec((1, 1, ne, c), lambda b, j, *_: (b, j, 0, 0)),
                  pl.BlockSpec((1, 1, ne, c), lambda b, j, *_: (b, j, 0, 0)),
                  pl.BlockSpec((1, c, d), lambda b, j, *_: (b, j, 0))],
        out_specs=pl.BlockSpec(memory_space=pl.ANY),
        scratch_shapes=[pltpu.VMEM((2, ne * WROWS, width), BF16),
                        pltpu.VMEM((ne * ALIGN, width), BF16), pltpu.VMEM((WROWS, width), BF16),
                        pltpu.SemaphoreType.DMA((2, ne)), pltpu.SemaphoreType.DMA((ne,)),
                        pltpu.SMEM((1,), I32)],
    )
    return pl.pallas_call(
        functools.partial(_gather_kernel, cap=cap, ne=ne),
        out_shape=jax.ShapeDtypeStruct((bsz, ne, cap + WROWS, width), BF16),
        grid_spec=grid_spec,
        compiler_params=pltpu.CompilerParams(dimension_semantics=("arbitrary", "arbitrary"),
                                             vmem_limit_bytes=VMEM_LIMIT),
        name="moe_gather",
    )(base, cnt, pos, aff, basev, cntv, h2)


def _expert_kernel(xe_ref, wg_hbm, wu_hbm, wd_hbm, ye_ref,
                   wgua, wda, wgub, wdb, stg, stu, std, sem, *, d, ne, steps):
    e = pl.program_id(0)
    k = pl.program_id(1) * pl.num_programs(2) + pl.program_id(2)
    rows_in = d // steps
    rows_out = wd_hbm.shape[1] // steps

    def chunk_copies(slot, ee, kk):
        r_in = pl.ds(pl.multiple_of(kk * rows_in, ALIGN), rows_in)
        r_out = pl.ds(pl.multiple_of(kk * rows_out, ALIGN), rows_out)
        return [pltpu.make_async_copy(wg_hbm.at[ee, r_in], stg.at[slot], sem.at[slot, 0]),
                pltpu.make_async_copy(wu_hbm.at[ee, r_in], stu.at[slot], sem.at[slot, 1]),
                pltpu.make_async_copy(wd_hbm.at[ee, r_out], std.at[slot], sem.at[slot, 2])]

    def cast_chunk(slot, kk, dst):
        r_in = pl.ds(pl.multiple_of(kk * rows_in, ALIGN), rows_in)
        r_out = pl.ds(pl.multiple_of(kk * rows_out, ALIGN), rows_out)
        ff = stg.shape[2]
        dst[0][r_in, 0:ff] = stg[slot].astype(BF16)
        dst[0][r_in, ff:2 * ff] = stu[slot].astype(BF16)
        dst[1][r_out, :] = std[slot].astype(BF16)

    @pl.when((e == 0) & (k == 0))
    def _first_expert():
        for kk in range(steps):
            cps = chunk_copies(kk % 2, 0, kk)
            for cp in cps:
                cp.start()
            for cp in cps:
                cp.wait()
            cast_chunk(kk % 2, kk, (wgua, wda))
        if ne > 1:
            for cp in chunk_copies(0, 1, 0):
                cp.start()

    def step(cur, nxt):
        slot = k % 2
        last_chunk = k + 1 == steps

        @pl.when(jnp.where(last_chunk, e + 2 < ne, e + 1 < ne))
        def _start_next_chunk():
            for cp in chunk_copies(1 - slot, jnp.where(last_chunk, e + 2, e + 1), jnp.where(last_chunk, 0, k + 1)):
                cp.start()

        @pl.when(e + 1 < ne)
        def _next_weights():
            for cp in chunk_copies(slot, e + 1, k):
                cp.wait()
            cast_chunk(slot, k, nxt)

        xin = xe_ref[0, 0]
        xb = xin[:, :d]
        gate = xin[:, d:d + 1].astype(F32) + xin[:, d + LANES // 2:d + LANES // 2 + 1].astype(F32)
        ff = cur[1].shape[0]
        au = _dot(xb, cur[0][...])
        y = _dot((_silu(au[:, :ff]) * au[:, ff:]).astype(BF16), cur[1][...])
        ye_ref[0, 0] = (y * gate).astype(BF16)

    @pl.when(e % 2 == 0)
    def _even():
        step((wgua, wda), (wgub, wdb))

    @pl.when(e % 2 == 1)
    def _odd():
        step((wgub, wdb), (wgua, wda))


def _experts(xe, wg, wu, wd, cap):
    bsz, ne, _, width = xe.shape
    d = width - LANES
    ff = wg.shape[2]
    rows = min(EXPERT_ROWS, cap)
    steps = bsz * (cap // rows)
    assert steps % 2 == 0 and d % (steps * ALIGN) == 0 and ff % (steps * ALIGN) == 0
    hbm = pl.BlockSpec(memory_space=pl.ANY)
    return pl.pallas_call(
        functools.partial(_expert_kernel, d=d, ne=ne, steps=steps),
        out_shape=jax.ShapeDtypeStruct((bsz, ne, cap, d), BF16),
        grid=(ne, bsz, cap // rows),
        in_specs=[pl.BlockSpec((1, 1, rows, width), lambda e, b, r: (b, e, r, 0)), hbm, hbm, hbm],
        out_specs=pl.BlockSpec((1, 1, rows, d), lambda e, b, r: (b, e, r, 0)),
        scratch_shapes=[pltpu.VMEM((d, 2 * ff), BF16), pltpu.VMEM((ff, d), BF16),
                        pltpu.VMEM((d, 2 * ff), BF16), pltpu.VMEM((ff, d), BF16),
                        pltpu.VMEM((2, d // steps, ff), F32), pltpu.VMEM((2, d // steps, ff), F32),
                        pltpu.VMEM((2, ff // steps, d), F32), pltpu.SemaphoreType.DMA((2, 3))],
        compiler_params=pltpu.CompilerParams(dimension_semantics=("arbitrary", "arbitrary", "arbitrary"),
                                             vmem_limit_bytes=VMEM_LIMIT),
        name="moe_experts",
    )(xe, wg, wu, wd)


def _combine_kernel(base_ref, cnt_ref, pos_ref, basev_ref, cntv_ref, x1_ref, mod_ref, fnw_ref, ye_ref,
                    out_ref, stage, acc, sem, *, cap, ne):
    b = pl.program_id(0)
    j = pl.program_id(1)
    nt = pl.num_programs(1)
    step = b * nt + j
    pos = pos_ref[0]
    ones = jnp.ones(pos.shape, F32)
    basev = basev_ref[0, 0]
    cntv = cntv_ref[0, 0]
    last = cap - WROWS

    def fetch(slot, bb, jj, r):
        cps = []
        for e in range(ne):
            s, _n = _round_slots_scalar(base_ref[bb, jj, e], cnt_ref[bb, jj, e], r)
            row0 = pl.multiple_of(jnp.minimum(_align_down(s), last), ALIGN)
            cps.append(pltpu.make_async_copy(ye_ref.at[bb, e, pl.ds(row0, WROWS)],
                                             stage.at[slot, pl.ds(e * WROWS, WROWS)], sem.at[slot, e]))
        return cps

    def weights(r):
        start, num = _round_slots(basev, cntv, r)
        valid = (pos >= start) & (pos < start + num)
        return _window_select(pos - jnp.minimum(_align_down(start), last), valid, ones, ne).astype(BF16)

    def expand(w, slot):
        return _dg(w, stage[slot], _TN)

    @pl.when(step == 0)
    def _first():
        for cp in fetch(0, b, j, 0):
            cp.start()

    @pl.when(step + 1 < pl.num_programs(0) * nt)
    def _prefetch():
        wrap = j + 1 == nt
        for cp in fetch((step + 1) % 2, jnp.where(wrap, b + 1, b), jnp.where(wrap, 0, j + 1), 0):
            cp.start()

    w0 = weights(0)
    slot = step % 2
    for cp in fetch(slot, b, j, 0):
        cp.wait()
    acc[...] = expand(w0, slot)

    def round_body(r, _):
        cps = fetch(2, b, j, r)
        for cp in cps:
            cp.start()
        w = weights(r)
        for cp in cps:
            cp.wait()
        acc[...] += expand(w, 2)
        return 0

    nrounds = (_tile_counts(cnt_ref, b, j, ne) + (WIN - 1)) // WIN
    lax.fori_loop(1, nrounds, round_body, 0)
    mod = mod_ref[0]
    x2 = x1_ref[0] + mod[5:6] * acc[...]
    out_ref[0] = _rms(x2, fnw_ref[...])


def _combine(base, cnt, pos, basev, cntv, x1, modb, fnw, ye, cap):
    bsz, t, d = x1.shape
    ne = pos.shape[1]
    c = TILE
    nt = t // c
    grid_spec = pltpu.PrefetchScalarGridSpec(
        num_scalar_prefetch=2,
        grid=(bsz, nt),
        in_specs=[pl.BlockSpec((1, ne, c), lambda b, j, *_: (b, 0, j)),
                  pl.BlockSpec((1, 1, ne, c), lambda b, j, *_: (b, j, 0, 0)),
                  pl.BlockSpec((1, 1, ne, c), lambda b, j, *_: (b, j, 0, 0)),
                  pl.BlockSpec((1, c, d), lambda b, j, *_: (b, j, 0)),
                  pl.BlockSpec((1,) + modb.shape[1:], lambda b, j, *_: (b, 0, 0)),
                  pl.BlockSpec(fnw.shape, lambda b, j, *_: (0, 0)),
                  pl.BlockSpec(memory_space=pl.ANY)],
        out_specs=pl.BlockSpec((1, c, d), lambda b, j, *_: (b, j, 0)),
        scratch_shapes=[pltpu.VMEM((3, ne * WROWS, d), BF16), pltpu.VMEM((c, d), F32),
                        pltpu.SemaphoreType.DMA((3, ne))],
    )
    return pl.pallas_call(
        functools.partial(_combine_kernel, cap=cap, ne=ne),
        out_shape=jax.ShapeDtypeStruct((bsz, t, d), F32),
        grid_spec=grid_spec,
        compiler_params=pltpu.CompilerParams(dimension_semantics=("arbitrary", "arbitrary"),
                                             vmem_limit_bytes=VMEM_LIMIT),
        name="moe_combine",
    )(base, cnt, pos, basev, cntv, x1, modb, fnw, ye)


def _rope_tables(t):
    n_freq = RET_DK // 4
    inv = ROPE_BASE ** (-np.arange(n_freq, dtype=np.float64) / n_freq)
    zeros = lambda n: np.zeros((n, n_freq))

    def lanes(row_part, col_part):
        cos = np.concatenate([np.cos(row_part), np.cos(col_part)] * 2, axis=1)
        sin = np.concatenate([-np.sin(row_part), -np.sin(col_part), np.sin(row_part), np.sin(col_part)], axis=1)
        return cos, sin

    col = (np.arange(TILE) % GRID_W)[:, None] * inv
    cos_c, sin_c = lanes(zeros(TILE), col)
    cos_c[:, :n_freq] = 0.0
    cos_c[:, 2 * n_freq:3 * n_freq] = 0.0
    rows_per_tile = TILE // GRID_W
    row = np.arange(t // GRID_W)[:, None] * inv
    cos_r, sin_r = lanes(row, zeros(t // GRID_W))
    cos_r[:, n_freq:2 * n_freq] = 0.0
    cos_r[:, 3 * n_freq:] = 0.0
    row_tab = np.zeros((t // TILE, 2, 8, RET_DK))
    row_tab[:, 0, :rows_per_tile] = cos_r.reshape(t // TILE, rows_per_tile, RET_DK)
    row_tab[:, 1, :rows_per_tile] = sin_r.reshape(t // TILE, rows_per_tile, RET_DK)
    return jnp.asarray(np.stack([cos_c, sin_c]), F32), jnp.asarray(row_tab, F32)


def _mixer_weights(w_in, gate_w, gate_b):
    pts = np.cumsum(IN_WIDTHS)[:-1]
    gq, gk, gv, gz, gg, rq, rk, rv, rg = jnp.split(w_in, [int(p) for p in pts], axis=1)
    zw = 2 * GLA_RANK
    gz = jnp.concatenate([gz, gz, gz, jnp.zeros((gz.shape[0], GZ_PAD - 3 * zw), F32)], axis=1)
    wall = jnp.concatenate([gq, gk, gv, gg, rq, rk, rv, rg, gz], axis=1).astype(BF16)
    gmat = jnp.zeros((zw, 2 * GLA_QK), F32)
    gmat = gmat.at[:GLA_RANK, :GLA_QK].set(gate_w[0]).at[GLA_RANK:, GLA_QK:].set(gate_w[1])
    ghi = gmat.astype(BF16)
    glo = (gmat - ghi.astype(F32)).astype(BF16)
    gpk = jnp.concatenate([ghi, ghi, glo, jnp.zeros((GZ_PAD - 3 * zw, 2 * GLA_QK), BF16)], axis=0)
    return wall, gpk, gate_b.reshape(1, 2 * GLA_QK)


def kernel(x, c, ctx, c_ctx, w_ada, b_ada, norm1_w, w_in, gla_gate_w, gla_gate_b, ret_decay_logit, gla_norm_w,
           ret_norm_w, w_out, norm2_w, w_router, w_exp_gate, w_exp_up, w_exp_down, final_norm_w):
    bsz, t, d = x.shape
    depth = w_ada.shape[0]
    assert depth == 1 and t % TILE == 0 and ctx.shape[1] == TILE
    ne = w_router.shape[2]
    cap = EC_CAPACITY_FACTOR * t // ne
    assert cap >= WROWS and cap % ALIGN == 0 and cap % min(EXPERT_ROWS, cap) == 0
    nt = t // TILE
    nb = t // LANES
    bpt = TILE // LANES

    cs = jnp.concatenate([c, c_ctx[None, :], jnp.zeros((8 - bsz - 1, d), F32)], axis=0)
    mod = _ada(cs, w_ada[0], b_ada[0][None, :])
    mod = jnp.pad(mod.reshape(8, N_ADA, d), ((0, 0), (0, 8 - N_ADA), (0, 0)))
    modb = mod[:bsz]
    modc = mod[bsz:bsz + 1]

    wall, gpk, gb = _mixer_weights(w_in[0], gla_gate_w[0], gla_gate_b[0])
    n1w = norm1_w[0][None, :]
    rlog = jnp.broadcast_to(ret_decay_logit[0][:, :, None], (2, RET_HEADS, TILE)).astype(F32)
    cum_f = jnp.asarray(_chunk_cumsum_matrix(TILE, False), BF16)
    cum_b = jnp.asarray(_chunk_cumsum_matrix(TILE, True), BF16)
    ind = jnp.asarray(_chunk_indicator(TILE), BF16)
    lvl_f = jnp.asarray(_level_index(False))
    lvl_b = jnp.asarray(_level_index(True))
    bdm = jnp.asarray(_head_block_mask(), BF16)
    rope_col, rope_row = _rope_tables(t)

    sgf, sgb, srf, srb = _ctx_states(ctx, modc, n1w, wall, gpk, gb, rlog, cum_f, cum_b, ind, bdm)
    o_f, gqkv, gates, rqkv, lab = _fwd(x, modb, n1w, wall, gpk, gb, rope_col, rope_row, rlog, cum_f, ind, lvl_f, bdm,
                                       sgf, srf)

    wr = w_router[0].T
    wrh = wr.astype(BF16)
    wrl = (wr - wrh.astype(F32)).astype(BF16)
    x1, h2, aff = _bwd(x, o_f, gqkv, gates, rqkv, lab, modb, rlog, cum_b, ind, lvl_b, bdm, sgb, srb,
                       gla_norm_w[0][None, :], ret_norm_w[0][None, :], w_out[0].astype(BF16),
                       norm2_w[0][None, :], wrh, wrl)

    pos4, off4 = _route(aff.reshape(bsz, ne, nb, LANES), cap)
    pos = pos4.reshape(bsz, ne, t)
    boff = off4[:, :, :, 0]
    base = jnp.transpose(boff[:, :, ::bpt], (0, 2, 1))
    nxt = jnp.concatenate([base[:, 1:], jnp.full((bsz, 1, ne), cap, I32)], axis=1)
    cnt = nxt - base
    basev = jnp.broadcast_to(base[:, :, :, None], (bsz, nt, ne, TILE))
    cntv = jnp.broadcast_to(cnt[:, :, :, None], (bsz, nt, ne, TILE))

    xe = _gather(base, cnt, pos, aff, basev, cntv, h2, cap)
    ye = _experts(xe, w_exp_gate[0], w_exp_up[0], w_exp_down[0], cap)
    return _combine(base, cnt, pos, basev, cntv, x1, modb, final_norm_w[None, :], ye, cap)
```

```python
import functools

import numpy as np
import jax
import jax.numpy as jnp
from jax import lax
from jax.experimental import pallas as pl
from jax.experimental.pallas import tpu as pltpu

F32 = jnp.float32
BF16 = jnp.bfloat16
I32 = jnp.int32

GLA_HEADS = 4
GLA_DK = 64
GLA_DV = 128
GLA_RANK = 16
GLA_TAU = 16.0
RET_HEADS = 4
RET_DK = 128
RET_DV = 128
GRID_W = 64
ROPE_BASE = 10000.0
N_EXPERTS = 16
EC_CAPACITY_FACTOR = 2
N_ADA = 6
EPS = 1e-6

GLA_QK = GLA_HEADS * GLA_DK
GLA_V = GLA_HEADS * GLA_DV
RET_QK = RET_HEADS * RET_DK
RET_V = RET_HEADS * RET_DV
IN_WIDTHS = (GLA_QK, GLA_QK, GLA_V, 2 * GLA_RANK, GLA_V, RET_QK, RET_QK, RET_V, RET_V)

LANES = 128
MXU_N = 256
TILE = 256
GLA_CHUNK = 64
GLA_LEVELS = 6
WIN = 48
ALIGN = 16
WROWS = WIN + ALIGN
GZ_PAD = LANES
EXPERT_ROWS = 512
LN2 = float(np.log(2.0))
MIN_EXP = -149
EXP_STEPS = 8
MANTISSA_STEPS = 56
VMEM_LIMIT = 56 * 1024 * 1024

_NT = (((1,), (1,)), ((), ()))
_TN = (((0,), (0,)), ((), ()))


def _dot(a, b):
    return jnp.dot(a, b, preferred_element_type=F32)


def _dg(a, b, dims):
    return lax.dot_general(a, b, dims, preferred_element_type=F32)


def _split(a):
    hi = a.astype(BF16)
    lo = (a - hi.astype(F32)).astype(BF16)
    return hi, lo


def _logsig(x):
    return jnp.minimum(x, 0.0) - jnp.log(1.0 + jnp.exp(-jnp.abs(x)))


def _silu(x):
    return x / (1.0 + jnp.exp(-x))


def _rms(x, w):
    return x * lax.rsqrt(jnp.mean(x * x, axis=-1, keepdims=True) + EPS) * w


def _chunk_cumsum_matrix(c, reverse):
    i = np.arange(c)[:, None]
    t = np.arange(c)[None, :]
    same = (i // GLA_CHUNK) == (t // GLA_CHUNK)
    return (same & ((t >= i) if reverse else (t <= i))).astype(np.float32)


def _chunk_indicator(c):
    return (np.arange(c)[:, None] // GLA_CHUNK == np.arange(LANES)[None, :]).astype(np.float32)


def _level_index(reverse):
    i = np.arange(GLA_CHUNK)[:, None]
    j = np.arange(GLA_CHUNK)[None, :]
    x = i ^ j
    lvl = np.where(x > 0, np.floor(np.log2(np.maximum(x, 1))), -1).astype(np.int32)
    bad = (j < i) if reverse else (j > i)
    return np.tile(np.where(bad, 99, lvl).astype(np.int32), (1, GLA_HEADS))


def _head_block_mask():
    r = np.arange(GLA_QK)[:, None] // GLA_DK
    l = np.arange(GLA_V)[None, :] // GLA_DV
    return (r == l).astype(np.float32)


def _log_gates(gz, gpk_ref, gb_ref):
    z_hi = gz.astype(BF16).astype(F32)
    group = lax.broadcasted_iota(I32, gz.shape, 1) >> ((2 * GLA_RANK).bit_length() - 1)
    packed = jnp.where(group == 1, gz - z_hi, z_hi).astype(BF16)
    return _logsig(_dot(packed, gpk_ref[...]) + gb_ref[...]) * (1.0 / GLA_TAU)


def _project(xn, wall_ref, gpk_ref, gb_ref):
    proj = _dot(xn.astype(BF16), wall_ref[...])
    o = 0
    out = []
    for w in (GLA_QK, GLA_QK, GLA_V, GLA_V, RET_QK, RET_QK, RET_V, RET_V, GZ_PAD):
        out.append(proj[:, o:o + w])
        o += w
    gq, gk, gv, gg, rq, rk, rv, rg, gz = out
    log_a = _log_gates(gz, gpk_ref, gb_ref)
    return gq * (GLA_DK ** -0.5), gk, gv, gg, rq, rk * (RET_DK ** -0.5), rv, rg, log_a


def _rope(a, cos, sin):
    outs = []
    for h in range(RET_HEADS):
        ah = a[:, h * RET_DK:(h + 1) * RET_DK]
        outs.append(ah * cos + pltpu.roll(ah, RET_DK // 2, 1) * sin)
    return jnp.concatenate(outs, axis=1)


def _stack_heads(a):
    head = lax.broadcasted_iota(I32, a.shape, 1) >> 6
    zero = jnp.zeros_like(a)
    return jnp.concatenate([jnp.where(head == h, a, zero) for h in range(GLA_HEADS)], axis=0)


def _gate_sums(g):
    hi, lo = _split(g)
    return jnp.concatenate([hi, lo], axis=1)


def _level_log_decay(level, g, b, b_ref, row0, reverse, row):
    n = GLA_CHUNK
    upper = ((row >> level) & 1) == 1
    if level == 0:
        return jnp.where(upper, 0.0, g) if reverse else jnp.where(upper, g, 0.0)
    if level == 1:
        nxt = pltpu.roll(g, n - 1, 0)
        prv = pltpu.roll(g, 1, 0)
        r = row & 3
        if reverse:
            return jnp.where(r == 0, g + nxt, jnp.where(r == 1, g, jnp.where(r == 2, 0.0, prv)))
        return jnp.where(r == 0, nxt, jnp.where(r == 1, 0.0, jnp.where(r == 2, g, g + prv)))
    m = 1 << level
    anchors = [jnp.broadcast_to(b_ref[pl.ds(row0 + blk + (m if reverse else m - 1), 1), :], (2 * m, GLA_QK))
               for blk in range(0, n, 2 * m)]
    d = b - (jnp.concatenate(anchors, axis=0) if len(anchors) > 1 else anchors[0])
    return jnp.where(upper, -d, d) if reverse else jnp.where(upper, d, -d)


def _interleave(*stages):
    order = sorted((span * (k + 0.5) / n, i) for i, (_, n, span) in enumerate(stages) for k in range(n))
    for _, i in order:
        next(stages[i][0])
    for gen, _, _ in stages:
        for _ in gen:
            raise AssertionError("stage has more pieces than declared")


def _gla_steps(qkv_ref, g_ref, cum_ref, ind_ref, lvl_ref, bdm_ref, s_ref, b_ref, reverse, emit):
    c = g_ref.shape[0]
    g2 = _gate_sums(g_ref[...])
    r = _dot(cum_ref[...], g2)
    b_ref[...] = r[:, :GLA_QK] + r[:, GLA_QK:]
    cs = _dg(g2, ind_ref[...], _TN)
    tot = cs[:GLA_QK] + cs[GLA_QK:]
    yield
    nchunk = c // GLA_CHUNK
    row = lax.broadcasted_iota(I32, (GLA_CHUNK, GLA_QK), 0)
    lvl = lvl_ref[...] if emit is not None else None
    for ci in (reversed(range(nchunk)) if reverse else range(nchunk)):
        row0 = ci * GLA_CHUNK
        rows = pl.ds(row0, GLA_CHUNK)
        kc = qkv_ref[rows, GLA_QK:2 * GLA_QK].astype(F32)
        vc = qkv_ref[rows, 2 * GLA_QK:]
        gc = g_ref[rows, :]
        bc = b_ref[rows, :]
        bdm = bdm_ref[...]
        s = s_ref[...]
        if emit is not None:
            qc = qkv_ref[rows, :GLA_QK].astype(F32)
            scores = jnp.zeros((GLA_CHUNK, GLA_HEADS * GLA_CHUNK), F32)
            for level in range(GLA_LEVELS):
                e = jnp.exp(_level_log_decay(level, gc, bc, b_ref, row0, reverse, row))
                p = _dg((qc * e).astype(BF16), _stack_heads((kc * e).astype(BF16)), _NT)
                scores = jnp.where(lvl == level, p, scores)
            p = _dg(qc.astype(BF16), _stack_heads(kc.astype(BF16)), _NT)
            scores = jnp.where(lvl == -1, p, scores)
            v_bd = jnp.concatenate([vc] * GLA_HEADS, axis=0) * bdm
            s_bd = jnp.concatenate([s.astype(BF16)] * GLA_HEADS, axis=1) * bdm
            emit(row0, _dot(scores.astype(BF16), v_bd) + _dot((qc * jnp.exp(bc)).astype(BF16), s_bd))
        b_end = b_ref[pl.ds(row0 if reverse else row0 + GLA_CHUNK - 1, 1), :]
        kv = _dg((kc * jnp.exp(b_end - bc)).astype(BF16), vc, _TN)
        own = jnp.concatenate([kv[h * GLA_DK:(h + 1) * GLA_DK, h * GLA_DV:(h + 1) * GLA_DV]
                               for h in range(GLA_HEADS)], axis=0)
        s_ref[...] = jnp.exp(jnp.broadcast_to(tot[:, ci:ci + 1], (GLA_QK, GLA_DV))) * s + own
        yield


def _ret_decays(rlog_ref, c, reverse):
    lg = _logsig(rlog_ref[0])
    ii = lax.broadcasted_iota(I32, (c, c), 0)
    jj = lax.broadcasted_iota(I32, (c, c), 1)
    rel = ((jj - ii) if reverse else (ii - jj)).astype(F32)
    pos = lax.broadcasted_iota(I32, (c, RET_DK), 0).astype(F32)
    dmats, qd, kd, cd = [], [], [], []
    for h in range(RET_HEADS):
        lh = lg[h:h + 1, :]
        dmats.append(jnp.where(rel >= 0, jnp.exp(lh * jnp.maximum(rel, 0.0)), 0.0))
        l1 = lh[:, :RET_DK]
        qd.append(jnp.exp(l1 * ((c - pos) if reverse else (pos + 1.0))))
        kd.append(jnp.exp(l1 * (pos if reverse else (c - 1.0 - pos))))
        cd.append(jnp.exp(l1 * float(c)))
    return dmats, jnp.concatenate(qd, axis=1), jnp.concatenate(kd, axis=1), jnp.concatenate(cd, axis=1)


def _ret_steps(qkv_ref, dmat_ref, qdec_ref, kdec_ref, cdec_ref, s_ref, emit):
    for h in range(RET_HEADS):
        sl = slice(h * RET_DK, (h + 1) * RET_DK)
        qb = qkv_ref[:, h * RET_DK:(h + 1) * RET_DK]
        kb = qkv_ref[:, RET_QK + h * RET_DK:RET_QK + (h + 1) * RET_DK]
        vh = qkv_ref[:, 2 * RET_QK + h * RET_DV:2 * RET_QK + (h + 1) * RET_DV]
        sc = _dg(qb, kb, _NT) * dmat_ref[h]
        s = s_ref[h]
        emit(h, _dot(sc.astype(BF16), vh) + _dot((qb.astype(F32) * qdec_ref[:, sl]).astype(BF16), s.astype(BF16)))
        s_ref[h] = cdec_ref[:, sl] * s + _dg((kb.astype(F32) * kdec_ref[:, sl]).astype(BF16), vh, _TN)
        yield


def _proj_steps(x_ref, mod_ref, n1w_ref, wall_ref, gpk_ref, gb_ref, rope_col_ref, rope_row_ref,
                gqkv_ref, gates_ref, rqkv_ref, lab_ref, nxt_g, nxt_r, nxt_l):
    bsz, c, _ = x_ref.shape
    normed = []
    for b in range(bsz):
        mod = mod_ref[b]
        normed.append((_rms(x_ref[b], n1w_ref[...]) * (1.0 + mod[1:2]) + mod[0:1]).astype(BF16))
    hb = jnp.concatenate(normed, axis=0)
    yield

    def cols(o, w):
        return _dot(hb, wall_ref[:, o:o + w])

    def put(val, o, out_ref, stage_ref, post=None):
        for b in range(bsz):
            part = val[b * c:(b + 1) * c]
            part = (part if post is None else post(part)).astype(BF16)
            out_ref[b, :, o:o + part.shape[1]] = part
            if stage_ref is not None:
                stage_ref[b, :, o:o + part.shape[1]] = part

    put(cols(0, GLA_QK) * (GLA_DK ** -0.5), 0, gqkv_ref, nxt_g)
    put(cols(GLA_QK, GLA_QK), GLA_QK, gqkv_ref, nxt_g)
    yield
    put(cols(2 * GLA_QK, GLA_V), 2 * GLA_QK, gqkv_ref, nxt_g)
    yield
    o = 2 * GLA_QK + GLA_V
    put(cols(o, GLA_V), 0, gates_ref, None)
    yield
    o += GLA_V
    rows_of = lambda i: jnp.concatenate(
        [jnp.broadcast_to(rope_row_ref[0, i, q:q + 1, :], (GRID_W, RET_DK)) for q in range(TILE // GRID_W)], axis=0)
    cos = rows_of(0) + rope_col_ref[0]
    sin = rows_of(1) + rope_col_ref[1]
    rope = lambda a: _rope(a, cos, sin)
    put(cols(o, RET_QK), 0, rqkv_ref, nxt_r, rope)
    yield
    o += RET_QK
    put(cols(o, RET_QK) * (RET_DK ** -0.5), RET_QK, rqkv_ref, nxt_r, rope)
    yield
    o += RET_QK
    put(cols(o, RET_V), 2 * RET_QK, rqkv_ref, nxt_r)
    yield
    o += RET_V
    put(cols(o, RET_V), GLA_V, gates_ref, None)
    yield
    o += RET_V
    log_a = _log_gates(cols(o, GZ_PAD), gpk_ref, gb_ref)
    for b in range(bsz):
        nxt_l[b] = log_a[b * c:(b + 1) * c, :GLA_QK]
        lab_ref[b] = log_a[b * c:(b + 1) * c, GLA_QK:]
    yield


def _ada_kernel(c_ref, w_ref, b_ref, o_ref):
    s_hi, s_lo = _split(_silu(c_ref[...]))
    w_hi, w_lo = _split(w_ref[...])
    o_ref[...] = _dot(s_hi, w_hi) + _dot(s_lo, w_hi) + _dot(s_hi, w_lo) + b_ref[...]


def _ada(cs, w, b):
    rows, d = cs.shape
    n = w.shape[1]
    tn = 1536
    return pl.pallas_call(
        _ada_kernel,
        out_shape=jax.ShapeDtypeStruct((rows, n), F32),
        grid=(n // tn,),
        in_specs=[pl.BlockSpec((rows, d), lambda i: (0, 0)),
                  pl.BlockSpec((d, tn), lambda i: (0, i)),
                  pl.BlockSpec((1, tn), lambda i: (0, i))],
        out_specs=pl.BlockSpec((rows, tn), lambda i: (0, i)),
        compiler_params=pltpu.CompilerParams(dimension_semantics=("arbitrary",), vmem_limit_bytes=VMEM_LIMIT),
        name="ada",
    )(cs, w, b)


def _ctx_kernel(ctx_ref, mod_ref, n1w_ref, wall_ref, gpk_ref, gb_ref, rlog_ref, cumf_ref, cumb_ref,
                ind_ref, bdm_ref, sgf_ref, sgb_ref, srf_ref, srb_ref, b_scr, kv_scr, g_scr, *, c):
    mod = mod_ref[0]
    hc = _rms(ctx_ref[0], n1w_ref[...]) * (1.0 + mod[1:2]) + mod[0:1]
    _, gk, gv, _, _, rk, rv, _, log_a = _project(hc, wall_ref, gpk_ref, gb_ref)
    kv_scr[:, GLA_QK:] = jnp.concatenate([gk, gv], axis=1).astype(BF16)
    rvb = rv.astype(BF16)
    for d, (cum_ref, out_g, out_r) in enumerate(((cumf_ref, sgf_ref, srf_ref), (cumb_ref, sgb_ref, srb_ref))):
        out_g[0] = jnp.zeros((GLA_QK, GLA_DV), F32)
        g_scr[...] = log_a[:, d * GLA_QK:(d + 1) * GLA_QK]
        _interleave((_gla_steps(kv_scr, g_scr, cum_ref, ind_ref, None, bdm_ref, out_g.at[0], b_scr, bool(d), None),
                     1 + c // GLA_CHUNK, 1.0))
        _, _, kdec, _ = _ret_decays(rlog_ref.at[d:d + 1], c, reverse=bool(d))
        for h in range(RET_HEADS):
            sl = slice(h * RET_DK, (h + 1) * RET_DK)
            out_r[0, h] = _dg((rk[:, sl] * kdec[:, sl]).astype(BF16), rvb[:, h * RET_DV:(h + 1) * RET_DV], _TN)


def _ctx_states(ctx, modc, n1w, wall, gpk, gb, rlog, cum_f, cum_b, ind, bdm):
    bsz, c, d = ctx.shape
    const = lambda a: pl.BlockSpec(a.shape, lambda b: (0,) * a.ndim)
    consts = (modc, n1w, wall, gpk, gb, rlog, cum_f, cum_b, ind, bdm)
    return pl.pallas_call(
        functools.partial(_ctx_kernel, c=c),
        out_shape=(jax.ShapeDtypeStruct((bsz, GLA_QK, GLA_DV), F32),
                   jax.ShapeDtypeStruct((bsz, GLA_QK, GLA_DV), F32),
                   jax.ShapeDtypeStruct((bsz, RET_HEADS, RET_DK, RET_DV), F32),
                   jax.ShapeDtypeStruct((bsz, RET_HEADS, RET_DK, RET_DV), F32)),
        grid=(bsz,),
        in_specs=[pl.BlockSpec((1, c, d), lambda b: (b, 0, 0))] + [const(a) for a in consts],
        out_specs=(pl.BlockSpec((1, GLA_QK, GLA_DV),lambda b: (b, 0, 0)),
                   pl.BlockSpec((1, GLA_QK, GLA_DV),lambda b: (b, 0, 0)),
                   pl.BlockSpec((1, RET_HEADS, RET_DK, RET_DV), lambda b: (b, 0, 0, 0)),
                   pl.BlockSpec((1, RET_HEADS, RET_DK, RET_DV), lambda b: (b, 0, 0, 0))),
        scratch_shapes=[pltpu.VMEM((c, GLA_QK), F32), pltpu.VMEM((c, 2 * GLA_QK + GLA_V), BF16),
                        pltpu.VMEM((c, GLA_QK), F32)],
        compiler_params=pltpu.CompilerParams(dimension_semantics=("arbitrary",), vmem_limit_bytes=VMEM_LIMIT),
        name="ctx_states",
    )(ctx, *consts)


def _fwd_kernel(x_ref, mod_ref, n1w_ref, wall_ref, gpk_ref, gb_ref, rope_col_ref, rope_row_ref, rlog_ref,
                cum_ref, ind_ref, lvl_ref, bdm_ref, sg0_ref, sr0_ref,
                of_ref, gqkv_ref, gates_ref, rqkv_ref, lab_ref,
                sg_scr, sr_scr, dmat_scr, qdec_scr, kdec_scr, cdec_scr, b_scr,
                cur_g, cur_r, cur_l, nxt_g, nxt_r, nxt_l, *, c):
    j = pl.program_id(0)
    bsz = x_ref.shape[0]

    @pl.when(j == 0)
    def _first():
        sg_scr[...] = jnp.zeros(sg_scr.shape, F32)
        sr_scr[...] = jnp.zeros(sr_scr.shape, F32)
        nxt_g[...] = jnp.zeros(nxt_g.shape, BF16)
        nxt_r[...] = jnp.zeros(nxt_r.shape, BF16)
        nxt_l[...] = jnp.zeros(nxt_l.shape, F32)
        dmats, qd, kd, cd = _ret_decays(rlog_ref, c, reverse=False)
        for h in range(RET_HEADS):
            dmat_scr[h] = dmats[h]
        qdec_scr[...] = qd
        kdec_scr[...] = kd
        cdec_scr[...] = cd

    @pl.when(j == 1)
    def _seed():
        sg_scr[...] = sg0_ref[...]
        sr_scr[...] = sr0_ref[...]

    cur_g[...] = nxt_g[...]
    cur_r[...] = nxt_r[...]
    cur_l[...] = nxt_l[...]

    def emitters(b):
        def emit_gla(row0, out):
            of_ref[b, pl.ds(row0, GLA_CHUNK), 0:GLA_V] = out

        def emit_ret(h, out):
            of_ref[b, :, GLA_V + h * RET_DV:GLA_V + (h + 1) * RET_DV] = out
        return emit_gla, emit_ret

    stages = []
    for b in range(bsz):
        emit_gla, emit_ret = emitters(b)
        stages.append((_gla_steps(cur_g.at[b], cur_l.at[b], cum_ref, ind_ref, lvl_ref, bdm_ref, sg_scr.at[b],
                                  b_scr.at[b], False, emit_gla), 1 + c // GLA_CHUNK, 1.0))
        stages.append((_ret_steps(cur_r.at[b], dmat_scr, qdec_scr, kdec_scr, cdec_scr, sr_scr.at[b], emit_ret),
                       RET_HEADS, 1.0))
    proj = _proj_steps(x_ref, mod_ref, n1w_ref, wall_ref, gpk_ref, gb_ref, rope_col_ref, rope_row_ref,
                       gqkv_ref, gates_ref, rqkv_ref, lab_ref, nxt_g, nxt_r, nxt_l)
    _interleave(*stages, (proj, 9, 1.0))


def _fwd(x, modb, n1w, wall, gpk, gb, rope_col, rope_row, rlog, cum_f, ind, lvl_f, bdm, sgf, srf):
    bsz, t, d = x.shape
    c = TILE
    nt = t // c
    const = lambda shape: pl.BlockSpec(shape, lambda j: (0,) * len(shape))
    proj_tile = lambda w: pl.BlockSpec((bsz, c, w), lambda j: (0, jnp.minimum(j, nt - 1), 0))
    scan_tile = lambda w: pl.BlockSpec((bsz, c, w), lambda j: (0, jnp.maximum(j - 1, 0), 0))
    rope_tile = pl.BlockSpec((1,) + rope_row.shape[1:], lambda j: (jnp.minimum(j, nt - 1), 0, 0, 0))
    mixw = GLA_V + RET_V
    gw, rw = 2 * GLA_QK + GLA_V, 2 * RET_QK + RET_V
    staging = [pltpu.VMEM((bsz, c, gw), BF16), pltpu.VMEM((bsz, c, rw), BF16), pltpu.VMEM((bsz, c, GLA_QK), F32)]
    return pl.pallas_call(
        functools.partial(_fwd_kernel, c=c),
        out_shape=(jax.ShapeDtypeStruct((bsz, t, mixw), F32),
                   jax.ShapeDtypeStruct((bsz, t, gw), BF16),
                   jax.ShapeDtypeStruct((bsz, t, GLA_V + RET_V), BF16),
                   jax.ShapeDtypeStruct((bsz, t, rw), BF16),
                   jax.ShapeDtypeStruct((bsz, t, GLA_QK), F32)),
        grid=(nt + 1,),
        in_specs=[proj_tile(d),
                  const(modb.shape),
                  const(n1w.shape), const(wall.shape), const(gpk.shape), const(gb.shape),
                  const(rope_col.shape), rope_tile,
                  pl.BlockSpec((1,) + rlog.shape[1:], lambda j: (0, 0, 0)),
                  const(cum_f.shape), const(ind.shape), const(lvl_f.shape), const(bdm.shape),
                  const(sgf.shape), const(srf.shape)],
        out_specs=(scan_tile(mixw), proj_tile(gw), proj_tile(GLA_V + RET_V), proj_tile(rw), proj_tile(GLA_QK)),
        scratch_shapes=[pltpu.VMEM((bsz, GLA_QK, GLA_DV), F32),
                        pltpu.VMEM((bsz, RET_HEADS, RET_DK, RET_DV), F32),
                        pltpu.VMEM((RET_HEADS, c, c), F32),
                        pltpu.VMEM((c, RET_QK), F32),
                        pltpu.VMEM((c, RET_QK), F32),
                        pltpu.VMEM((1, RET_QK), F32),
                        pltpu.VMEM((bsz, c, GLA_QK), F32)] + staging + staging,
        compiler_params=pltpu.CompilerParams(dimension_semantics=("arbitrary",),
                                             vmem_limit_bytes=VMEM_LIMIT),
        name="mixer_fwd",
    )(x, modb, n1w, wall, gpk, gb, rope_col, rope_row, rlog, cum_f, ind, lvl_f, bdm, sgf, srf)


def _bwd_kernel(x_ref, of_ref, gqkv_ref, gates_ref, rqkv_ref, lab_ref, mod_ref, rlog_ref,
                cum_ref, ind_ref, lvl_ref, bdm_ref,
                sg0_ref, sr0_ref, gnw_ref, rnw_ref, wout_ref, n2w_ref, wrh_ref, wrl_ref,
                x1_ref, h2_ref, aff_ref,
                sg_scr, sr_scr, dmat_scr, qdec_scr, kdec_scr, cdec_scr, b_scr, cur_m, nxt_m, mixb, *, c):
    j = pl.program_id(1)

    @pl.when(j == 0)
    def _first():
        sg_scr[...] = sg0_ref[0]
        sr_scr[...] = sr0_ref[0]
        nxt_m[...] = jnp.zeros(nxt_m.shape, F32)
        dmats, qd, kd, cd = _ret_decays(rlog_ref, c, reverse=True)
        for h in range(RET_HEADS):
            dmat_scr[h] = dmats[h]
        qdec_scr[...] = qd
        kdec_scr[...] = kd
        cdec_scr[...] = cd

    cur_m[...] = nxt_m[...]

    def emit_gla(row0, out):
        rows = pl.ds(row0, GLA_CHUNK)
        nxt_m[rows, 0:GLA_V] = of_ref[0, rows, 0:GLA_V] + out

    def emit_ret(h, out):
        cols = slice(GLA_V + h * RET_DV, GLA_V + (h + 1) * RET_DV)
        nxt_m[:, cols] = of_ref[0, :, cols] + out

    gla = _gla_steps(gqkv_ref.at[0], lab_ref.at[0], cum_ref, ind_ref, lvl_ref, bdm_ref, sg_scr, b_scr, True,
                     emit_gla)
    ret = _ret_steps(rqkv_ref.at[0], dmat_scr, qdec_scr, kdec_scr, cdec_scr, sr_scr, emit_ret)

    def epilogue():
        for h in range(GLA_HEADS + RET_HEADS):
            sl = slice(h * GLA_DV, (h + 1) * GLA_DV)
            oh = cur_m[:, sl]
            if h < GLA_HEADS:
                y = oh * lax.rsqrt(jnp.mean(oh * oh, axis=-1, keepdims=True) + EPS) * gnw_ref[:, sl]
            else:
                dv = oh - jnp.mean(oh, axis=-1, keepdims=True)
                y = (dv * lax.rsqrt(jnp.mean(dv * dv, axis=-1, keepdims=True) + EPS)
                     * rnw_ref[:, h * RET_DV - GLA_V:(h + 1) * RET_DV - GLA_V])
            mixb[:, sl] = (y * _silu(gates_ref[0, :, sl].astype(F32))).astype(BF16)
            yield
        mod = mod_ref[0]
        d = x_ref.shape[2]
        step = d // 4
        for p in range(4):
            cs = slice(p * step, (p + 1) * step)
            x1_ref[0, :, cs] = x_ref[0, :, cs] + mod[2:3, cs] * _dot(mixb[...], wout_ref[:, cs])
            yield
        h2 = _rms(x1_ref[0], n2w_ref[...]) * (1.0 + mod[4:5]) + mod[3:4]
        h_hi, h_lo = _split(h2)
        h2_ref[0] = h_hi
        yield
        wrh = wrh_ref[...]
        logit = _dg(wrh, h_hi, _NT) + _dg(wrh, h_lo, _NT) + _dg(wrl_ref[...], h_hi, _NT)
        ex = jnp.exp(logit - jnp.max(logit, axis=0, keepdims=True))
        aff_ref[0] = ex / jnp.sum(ex, axis=0, keepdims=True)
        yield

    _interleave((gla, 1 + c // GLA_CHUNK, 1.0), (ret, RET_HEADS, 0.85),
                (epilogue(), GLA_HEADS + RET_HEADS + 6, 0.75))


def _bwd(x, o_f, gqkv, gates, rqkv, lab, modb, rlog, cum_b, ind, lvl_b, bdm, sgb, srb, gnw, rnw, wout, n2w,
         wrh, wrl):
    bsz, t, d = x.shape
    c = TILE
    nt = t // c
    ne = wrh.shape[0]
    const = lambda shape: pl.BlockSpec(shape, lambda b, j: (0,) * len(shape))
    scan_at = lambda j: nt - 1 - jnp.minimum(j, nt - 1)
    mix_at = lambda j: nt - 1 - jnp.maximum(j - 1, 0)
    scan_tile = lambda w: pl.BlockSpec((1, c, w), lambda b, j: (b, scan_at(j), 0))
    tile = lambda w: pl.BlockSpec((1, c, w), lambda b, j: (b, mix_at(j), 0))
    return pl.pallas_call(
        functools.partial(_bwd_kernel, c=c),
        out_shape=(jax.ShapeDtypeStruct((bsz, t, d), F32),
                   jax.ShapeDtypeStruct((bsz, t, d), BF16),
                   jax.ShapeDtypeStruct((bsz, ne, t), F32)),
        grid=(bsz, nt + 1),
        in_specs=[tile(d), scan_tile(o_f.shape[2]), scan_tile(gqkv.shape[2]), tile(gates.shape[2]),
                  scan_tile(rqkv.shape[2]), scan_tile(lab.shape[2]),
                  pl.BlockSpec((1,) + modb.shape[1:], lambda b, j: (b, 0, 0)),
                  pl.BlockSpec((1,) + rlog.shape[1:], lambda b, j: (1, 0, 0)),
                  const(cum_b.shape), const(ind.shape), const(lvl_b.shape), const(bdm.shape),
                  pl.BlockSpec((1, GLA_QK, GLA_DV),lambda b, j: (b, 0, 0)),
                  pl.BlockSpec((1, RET_HEADS, RET_DK, RET_DV), lambda b, j: (b, 0, 0, 0)),
                  const(gnw.shape), const(rnw.shape), const(wout.shape), const(n2w.shape),
                  const(wrh.shape), const(wrl.shape)],
        out_specs=(tile(d), tile(d), pl.BlockSpec((1, ne, c), lambda b, j: (b, 0, mix_at(j)))),
        scratch_shapes=[pltpu.VMEM((GLA_QK, GLA_DV), F32),
                        pltpu.VMEM((RET_HEADS, RET_DK, RET_DV), F32),
                        pltpu.VMEM((RET_HEADS, c, c), F32),
                        pltpu.VMEM((c, RET_QK), F32),
                        pltpu.VMEM((c, RET_QK), F32),
                        pltpu.VMEM((1, RET_QK), F32),
                        pltpu.VMEM((c, GLA_QK), F32),
                        pltpu.VMEM((c, GLA_V + RET_V), F32),
                        pltpu.VMEM((c, GLA_V + RET_V), F32),
                        pltpu.VMEM((c, GLA_V + RET_V), BF16)],
        compiler_params=pltpu.CompilerParams(dimension_semantics=("arbitrary", "arbitrary"),
                                             vmem_limit_bytes=VMEM_LIMIT),
        name="mixer_bwd",
    )(x, o_f, gqkv, gates, rqkv, lab, modb, rlog, cum_b, ind, lvl_b, bdm, sgb, srb, gnw, rnw, wout, n2w, wrh, wrl)


def _route_kernel(aff_ref, pos_ref, off_ref, *, cap, nb):
    a = aff_ref[0]
    ne = a.shape[0]
    kf = float(cap)

    def count(mask):
        return jnp.sum(jnp.sum(jnp.where(mask, 1.0, 0.0), axis=1, keepdims=True), axis=2, keepdims=True)

    def bisect(lo, hi, mid, thr):
        ok = count(a >= thr(mid)) >= kf
        return jnp.where(ok, mid, lo), jnp.where(ok, hi, mid)

    pow2 = lambda e: jnp.exp(e * LN2)
    lo_e = jnp.full((ne, 1, 1), float(MIN_EXP - 1), F32)
    hi_e = jnp.full((ne, 1, 1), 1.0, F32)
    lo_e, hi_e = lax.fori_loop(0, EXP_STEPS, lambda i, c: bisect(c[0], c[1], jnp.floor((c[0] + c[1]) * 0.5), pow2),
                               (lo_e, hi_e))
    lo, hi = lax.fori_loop(0, MANTISSA_STEPS,
                           lambda i, c: bisect(c[0], c[1], c[0] + (c[1] - c[0]) * 0.5, lambda v: v),
                           (pow2(lo_e), pow2(hi_e)))
    kth = jnp.min(jnp.min(jnp.where(a >= lo, a, jnp.inf), axis=1, keepdims=True), axis=2, keepdims=True)
    gt = a > kth
    eq = a == kth
    need = kf - count(gt)

    upper = (lax.broadcasted_iota(I32, (LANES, LANES), 0) <= lax.broadcasted_iota(I32, (LANES, LANES), 1))
    upper = jnp.where(upper, 1.0, 0.0).astype(BF16)
    ones = jnp.ones((LANES, LANES), BF16)
    lower = (lax.broadcasted_iota(I32, (ne, nb, nb), 2) < lax.broadcasted_iota(I32, (ne, nb, nb), 1))
    lower = jnp.where(lower, 1.0, 0.0).astype(BF16)

    def excl_prefix(mask):
        m = jnp.where(mask, 1.0, 0.0)
        mb = m.astype(BF16).reshape(ne * nb, LANES)
        inc = _dot(mb, upper).reshape(ne, nb, LANES)
        tot = _dot(mb, ones).reshape(ne, nb, LANES)
        offs = lax.dot_general(lower, tot.astype(BF16), (((2,), (1,)), ((0,), (0,))), preferred_element_type=F32)
        return inc - m + offs, offs

    eq_rank, _ = excl_prefix(eq)
    sel = gt | (eq & (eq_rank < need))
    rank, offs = excl_prefix(sel)
    pos_ref[0] = jnp.where(sel, rank, -1.0).astype(I32)
    off_ref[0] = offs.astype(I32)


def _route(aff4, cap):
    bsz, ne, nb, _ = aff4.shape
    spec = pl.BlockSpec((1, ne, nb, LANES), lambda b: (b, 0, 0, 0))
    return pl.pallas_call(
        functools.partial(_route_kernel, cap=cap, nb=nb),
        out_shape=(jax.ShapeDtypeStruct(aff4.shape, I32), jax.ShapeDtypeStruct(aff4.shape, I32)),
        grid=(bsz,),
        in_specs=[spec],
        out_specs=(spec, spec),
        compiler_params=pltpu.CompilerParams(dimension_semantics=("arbitrary",), vmem_limit_bytes=VMEM_LIMIT),
        name="route",
    )(aff4)


def _tile_counts(cnt_ref, b, j, ne):
    m = cnt_ref[b, j, 0]
    for e in range(1, ne):
        m = jnp.maximum(m, cnt_ref[b, j, e])
    return m


def _window_select(rel, valid, val, ne):
    c = rel.shape[1]
    w = lax.broadcasted_iota(I32, (ne, WROWS, c), 1)
    relm = jnp.where(valid, rel, -1)
    sel = jnp.where(relm[:, None, :] == w, jnp.broadcast_to(val[:, None, :], (ne, WROWS, c)), 0.0)
    return sel.reshape(ne * WROWS, c)


def _round_slots(basev, cntv, r):
    start = basev + jnp.minimum(r * WIN, cntv)
    num = jnp.clip(cntv - r * WIN, 0, WIN)
    return start, num


def _round_slots_scalar(base, cnt, r):
    return base + jnp.minimum(r * WIN, cnt), jnp.clip(cnt - r * WIN, 0, WIN)


def _align_down(v):
    shift = ALIGN.bit_length() - 1
    return (v >> shift) << shift


def _gather_kernel(base_ref, cnt_ref, pos_ref, aff_ref, basev_ref, cntv_ref, h2_ref, xe_ref,
                   xbuf, carry, zbuf, sem, zsem, nissued, *, cap, ne):
    b = pl.program_id(0)
    j = pl.program_id(1)
    last_step = (b == pl.num_programs(0) - 1) & (j == pl.num_programs(1) - 1)

    def window_copy(slot, e, row0):
        return pltpu.make_async_copy(xbuf.at[slot, pl.ds(e * WROWS, WROWS)],
                                     xe_ref.at[b, e, pl.ds(row0, WROWS)], sem.at[slot, e])

    def wait_round(g):
        @pl.when(g >= 0)
        def _():
            for e in range(ne):
                window_copy(g % 2, e, 0).wait()

    @pl.when((b == 0) & (j == 0))
    def _start():
        nissued[0] = 0
        zbuf[...] = jnp.zeros(zbuf.shape, BF16)

    @pl.when(j == 0)
    def _start_sample():
        carry[...] = jnp.zeros(carry.shape, BF16)
        cps = [pltpu.make_async_copy(zbuf, xe_ref.at[b, e, pl.ds(cap, WROWS)], zsem.at[e]) for e in range(ne)]
        for cp in cps:
            cp.start()
        for cp in cps:
            cp.wait()

    pos = pos_ref[0]
    basev = basev_ref[0, 0]
    cntv = cntv_ref[0, 0]
    h2 = h2_ref[0]
    ones = jnp.ones(pos.shape, F32)
    nrounds = (_tile_counts(cnt_ref, b, j, ne) + (WIN - 1)) // WIN

    def round_body(r, _):
        g = nissued[0]
        slot = g % 2
        start, num = _round_slots(basev, cntv, r)
        valid = (pos >= start) & (pos < start + num)
        rel = pos - _align_down(start)
        onehot = _window_select(rel, valid, ones, ne).astype(BF16)
        d = h2.shape[1]
        for col0 in range(0, d, MXU_N):
            xbuf[slot, :, col0:col0 + MXU_N] = _dot(onehot, h2[:, col0:col0 + MXU_N]).astype(BF16)
        gcol = jnp.sum(_window_select(rel, valid, aff_ref[0], ne), axis=1, keepdims=True)
        gcol = jnp.broadcast_to(gcol, (ne * WROWS, LANES))
        g_hi = gcol.astype(BF16).astype(F32)
        first_half = lax.broadcasted_iota(I32, (ne * WROWS, LANES), 1) < LANES // 2
        xbuf[slot, :, d:] = jnp.where(first_half, g_hi, gcol - g_hi).astype(BF16)
        first = []
        for e in range(ne):
            s, n = _round_slots_scalar(base_ref[b, j, e], cnt_ref[b, j, e], r)
            first.append(pl.multiple_of(_align_down(s), ALIGN))
            nxt = pl.multiple_of(_align_down(s + n) - _align_down(s), ALIGN)
            row0 = e * WROWS
            xbuf[slot, pl.ds(row0, ALIGN), :] += carry[pl.ds(e * ALIGN, ALIGN), :]
            carry[pl.ds(e * ALIGN, ALIGN), :] = xbuf[slot, pl.ds(pl.multiple_of(row0 + nxt, ALIGN), ALIGN), :]
        wait_round(g - 1)
        for e in range(ne):
            window_copy(slot, e, first[e]).start()
        nissued[0] = g + 1
        return 0

    lax.fori_loop(0, nrounds, round_body, 0)

    @pl.when(last_step)
    def _drain():
        wait_round(nissued[0] - 1)


def _gather(base, cnt, pos, aff, basev, cntv, h2, cap):
    bsz, t, d = h2.shape
    ne = pos.shape[1]
    c = TILE
    nt = t // c
    width = d + LANES
    grid_spec = pltpu.PrefetchScalarGridSpec(
        num_scalar_prefetch=2,
        grid=(bsz, nt),
        in_specs=[pl.BlockSpec((1, ne, c), lambda b, j, *_: (b, 0, j)),
                  pl.BlockSpec((1, ne, c), lambda b, j, *_: (b, 0, j)),
                  pl.BlockSpec((1, 1, ne, c), lambda b, j, *_: (b, j, 0, 0)),
                  pl.BlockSpec((1, 1, ne, c), lambda b, j, *_: (b, j, 0, 0)),
                  pl.BlockSpec((1, c, d), lambda b, j, *_: (b, j, 0))],
        out_specs=pl.BlockSpec(memory_space=pl.ANY),
        scratch_shapes=[pltpu.VMEM((2, ne * WROWS, width), BF16),
                        pltpu.VMEM((ne * ALIGN, width), BF16), pltpu.VMEM((WROWS, width), BF16),
                        pltpu.SemaphoreType.DMA((2, ne)), pltpu.SemaphoreType.DMA((ne,)),
                        pltpu.SMEM((1,), I32)],
    )
    return pl.pallas_call(
        functools.partial(_gather_kernel, cap=cap, ne=ne),
        out_shape=jax.ShapeDtypeStruct((bsz, ne, cap + WROWS, width), BF16),
        grid_spec=grid_spec,
        compiler_params=pltpu.CompilerParams(dimension_semantics=("arbitrary", "arbitrary"),
                                             vmem_limit_bytes=VMEM_LIMIT),
        name="moe_gather",
    )(base, cnt, pos, aff, basev, cntv, h2)


def _expert_kernel(xe_ref, wg_hbm, wu_hbm, wd_hbm, ye_ref,
                   wgua, wda, wgub, wdb, stg, stu, std, sem, *, d, ne, steps):
    e = pl.program_id(0)
    k = pl.program_id(1) * pl.num_programs(2) + pl.program_id(2)
    rows_in = d // steps
    rows_out = wd_hbm.shape[1] // steps

    def chunk_copies(slot, ee, kk):
        r_in = pl.ds(pl.multiple_of(kk * rows_in, ALIGN), rows_in)
        r_out = pl.ds(pl.multiple_of(kk * rows_out, ALIGN), rows_out)
        return [pltpu.make_async_copy(wg_hbm.at[ee, r_in], stg.at[slot], sem.at[slot, 0]),
                pltpu.make_async_copy(wu_hbm.at[ee, r_in], stu.at[slot], sem.at[slot, 1]),
                pltpu.make_async_copy(wd_hbm.at[ee, r_out], std.at[slot], sem.at[slot, 2])]

    def cast_chunk(slot, kk, dst):
        r_in = pl.ds(pl.multiple_of(kk * rows_in, ALIGN), rows_in)
        r_out = pl.ds(pl.multiple_of(kk * rows_out, ALIGN), rows_out)
        ff = stg.shape[2]
        dst[0][r_in, 0:ff] = stg[slot].astype(BF16)
        dst[0][r_in, ff:2 * ff] = stu[slot].astype(BF16)
        dst[1][r_out, :] = std[slot].astype(BF16)

    @pl.when((e == 0) & (k == 0))
    def _first_expert():
        for kk in range(steps):
            cps = chunk_copies(kk % 2, 0, kk)
            for cp in cps:
                cp.start()
            for cp in cps:
                cp.wait()
            cast_chunk(kk % 2, kk, (wgua, wda))
        if ne > 1:
            for cp in chunk_copies(0, 1, 0):
                cp.start()

    def step(cur, nxt):
        slot = k % 2
        last_chunk = k + 1 == steps

        @pl.when(jnp.where(last_chunk, e + 2 < ne, e + 1 < ne))
        def _start_next_chunk():
            for cp in chunk_copies(1 - slot, jnp.where(last_chunk, e + 2, e + 1), jnp.where(last_chunk, 0, k + 1)):
                cp.start()

        @pl.when(e + 1 < ne)
        def _next_weights():
            for cp in chunk_copies(slot, e + 1, k):
                cp.wait()
            cast_chunk(slot, k, nxt)

        xin = xe_ref[0, 0]
        xb = xin[:, :d]
        gate = xin[:, d:d + 1].astype(F32) + xin[:, d + LANES // 2:d + LANES // 2 + 1].astype(F32)
        ff = cur[1].shape[0]
        au = _dot(xb, cur[0][...])
        y = _dot((_silu(au[:, :ff]) * au[:, ff:]).astype(BF16), cur[1][...])
        ye_ref[0, 0] = (y * gate).astype(BF16)

    @pl.when(e % 2 == 0)
    def _even():
        step((wgua, wda), (wgub, wdb))

    @pl.when(e % 2 == 1)
    def _odd():
        step((wgub, wdb), (wgua, wda))


def _experts(xe, wg, wu, wd, cap):
    bsz, ne, _, width = xe.shape
    d = width - LANES
    ff = wg.shape[2]
    rows = min(EXPERT_ROWS, cap)
    steps = bsz * (cap // rows)
    assert steps % 2 == 0 and d % (steps * ALIGN) == 0 and ff % (steps * ALIGN) == 0
    hbm = pl.BlockSpec(memory_space=pl.ANY)
    return pl.pallas_call(
        functools.partial(_expert_kernel, d=d, ne=ne, steps=steps),
        out_shape=jax.ShapeDtypeStruct((bsz, ne, cap, d), BF16),
        grid=(ne, bsz, cap // rows),
        in_specs=[pl.BlockSpec((1, 1, rows, width), lambda e, b, r: (b, e, r, 0)), hbm, hbm, hbm],
        out_specs=pl.BlockSpec((1, 1, rows, d), lambda e, b, r: (b, e, r, 0)),
        scratch_shapes=[pltpu.VMEM((d, 2 * ff), BF16), pltpu.VMEM((ff, d), BF16),
                        pltpu.VMEM((d, 2 * ff), BF16), pltpu.VMEM((ff, d), BF16),
                        pltpu.VMEM((2, d // steps, ff), F32), pltpu.VMEM((2, d // steps, ff), F32),
                        pltpu.VMEM((2, ff // steps, d), F32), pltpu.SemaphoreType.DMA((2, 3))],
        compiler_params=pltpu.CompilerParams(dimension_semantics=("arbitrary", "arbitrary", "arbitrary"),
                                             vmem_limit_bytes=VMEM_LIMIT),
        name="moe_experts",
    )(xe, wg, wu, wd)


def _combine_kernel(base_ref, cnt_ref, pos_ref, basev_ref, cntv_ref, x1_ref, mod_ref, fnw_ref, ye_ref,
                    out_ref, stage, acc, sem, *, cap, ne):
    b = pl.program_id(0)
    j = pl.program_id(1)
    nt = pl.num_programs(1)
    step = b * nt + j
    pos = pos_ref[0]
    ones = jnp.ones(pos.shape, F32)
    basev = basev_ref[0, 0]
    cntv = cntv_ref[0, 0]
    last = cap - WROWS

    def fetch(slot, bb, jj, r):
        cps = []
        for e in range(ne):
            s, _n = _round_slots_scalar(base_ref[bb, jj, e], cnt_ref[bb, jj, e], r)
            row0 = pl.multiple_of(jnp.minimum(_align_down(s), last), ALIGN)
            cps.append(pltpu.make_async_copy(ye_ref.at[bb, e, pl.ds(row0, WROWS)],
                                             stage.at[slot, pl.ds(e * WROWS, WROWS)], sem.at[slot, e]))
        return cps

    def weights(r):
        start, num = _round_slots(basev, cntv, r)
        valid = (pos >= start) & (pos < start + num)
        return _window_select(pos - jnp.minimum(_align_down(start), last), valid, ones, ne).astype(BF16)

    def expand(w, slot):
        return _dg(w, stage[slot], _TN)

    @pl.when(step == 0)
    def _first():
        for cp in fetch(0, b, j, 0):
            cp.start()

    @pl.when(step + 1 < pl.num_programs(0) * nt)
    def _prefetch():
        wrap = j + 1 == nt
        for cp in fetch((step + 1) % 2, jnp.where(wrap, b + 1, b), jnp.where(wrap, 0, j + 1), 0):
            cp.start()

    w0 = weights(0)
    slot = step % 2
    for cp in fetch(slot, b, j, 0):
        cp.wait()
    acc[...] = expand(w0, slot)

    def round_body(r, _):
        cps = fetch(2, b, j, r)
        for cp in cps:
            cp.start()
        w = weights(r)
        for cp in cps:
            cp.wait()
        acc[...] += expand(w, 2)
        return 0

    nrounds = (_tile_counts(cnt_ref, b, j, ne) + (WIN - 1)) // WIN
    lax.fori_loop(1, nrounds, round_body, 0)
    mod = mod_ref[0]
    x2 = x1_ref[0] + mod[5:6] * acc[...]
    out_ref[0] = _rms(x2, fnw_ref[...])


def _combine(base, cnt, pos, basev, cntv, x1, modb, fnw, ye, cap):
    bsz, t, d = x1.shape
    ne = pos.shape[1]
    c = TILE
    nt = t // c
    grid_spec = pltpu.PrefetchScalarGridSpec(
        num_scalar_prefetch=2,
        grid=(bsz, nt),
        in_specs=[pl.BlockSpec((1, ne, c), lambda b, j, *_: (b, 0, j)),
                  pl.BlockSpec((1, 1, ne, c), lambda b, j, *_: (b, j, 0, 0)),
                  pl.BlockSpec((1, 1, ne, c), lambda b, j, *_: (b, j, 0, 0)),
                  pl.BlockSpec((1, c, d), lambda b, j, *_: (b, j, 0)),
                  pl.BlockSpec((1,) + modb.shape[1:], lambda b, j, *_: (b, 0, 0)),
                  pl.BlockSpec(fnw.shape, lambda b, j, *_: (0, 0)),
                  pl.BlockSpec(memory_space=pl.ANY)],
        out_specs=pl.BlockSpec((1, c, d), lambda b, j, *_: (b, j, 0)),
        scratch_shapes=[pltpu.VMEM((3, ne * WROWS, d), BF16), pltpu.VMEM((c, d), F32),
                        pltpu.SemaphoreType.DMA((3, ne))],
    )
    return pl.pallas_call(
        functools.partial(_combine_kernel, cap=cap, ne=ne),
        out_shape=jax.ShapeDtypeStruct((bsz, t, d), F32),
        grid_spec=grid_spec,
        compiler_params=pltpu.CompilerParams(dimension_semantics=("arbitrary", "arbitrary"),
                                             vmem_limit_bytes=VMEM_LIMIT),
        name="moe_combine",
    )(base, cnt, pos, basev, cntv, x1, modb, fnw, ye)


def _rope_tables(t):
    n_freq = RET_DK // 4
    inv = ROPE_BASE ** (-np.arange(n_freq, dtype=np.float64) / n_freq)
    zeros = lambda n: np.zeros((n, n_freq))

    def lanes(row_part, col_part):
        cos = np.concatenate([np.cos(row_part), np.cos(col_part)] * 2, axis=1)
        sin = np.concatenate([-np.sin(row_part), -np.sin(col_part), np.sin(row_part), np.sin(col_part)], axis=1)
        return cos, sin

    col = (np.arange(TILE) % GRID_W)[:, None] * inv
    cos_c, sin_c = lanes(zeros(TILE), col)
    cos_c[:, :n_freq] = 0.0
    cos_c[:, 2 * n_freq:3 * n_freq] = 0.0
    rows_per_tile = TILE // GRID_W
    row = np.arange(t // GRID_W)[:, None] * inv
    cos_r, sin_r = lanes(row, zeros(t // GRID_W))
    cos_r[:, n_freq:2 * n_freq] = 0.0
    cos_r[:, 3 * n_freq:] = 0.0
    row_tab = np.zeros((t // TILE, 2, 8, RET_DK))
    row_tab[:, 0, :rows_per_tile] = cos_r.reshape(t // TILE, rows_per_tile, RET_DK)
    row_tab[:, 1, :rows_per_tile] = sin_r.reshape(t // TILE, rows_per_tile, RET_DK)
    return jnp.asarray(np.stack([cos_c, sin_c]), F32), jnp.asarray(row_tab, F32)


def _mixer_weights(w_in, gate_w, gate_b):
    pts = np.cumsum(IN_WIDTHS)[:-1]
    gq, gk, gv, gz, gg, rq, rk, rv, rg = jnp.split(w_in, [int(p) for p in pts], axis=1)
    zw = 2 * GLA_RANK
    gz = jnp.concatenate([gz, gz, gz, jnp.zeros((gz.shape[0], GZ_PAD - 3 * zw), F32)], axis=1)
    wall = jnp.concatenate([gq, gk, gv, gg, rq, rk, rv, rg, gz], axis=1).astype(BF16)
    gmat = jnp.zeros((zw, 2 * GLA_QK), F32)
    gmat = gmat.at[:GLA_RANK, :GLA_QK].set(gate_w[0]).at[GLA_RANK:, GLA_QK:].set(gate_w[1])
    ghi = gmat.astype(BF16)
    glo = (gmat - ghi.astype(F32)).astype(BF16)
    gpk = jnp.concatenate([ghi, ghi, glo, jnp.zeros((GZ_PAD - 3 * zw, 2 * GLA_QK), BF16)], axis=0)
    return wall, gpk, gate_b.reshape(1, 2 * GLA_QK)


def kernel(x, c, ctx, c_ctx, w_ada, b_ada, norm1_w, w_in, gla_gate_w, gla_gate_b, ret_decay_logit, gla_norm_w,
           ret_norm_w, w_out, norm2_w, w_router, w_exp_gate, w_exp_up, w_exp_down, final_norm_w):
    bsz, t, d = x.shape
    depth = w_ada.shape[0]
    assert depth == 1 and t % TILE == 0 and ctx.shape[1] == TILE
    ne = w_router.shape[2]
    cap = EC_CAPACITY_FACTOR * t // ne
    assert cap >= WROWS and cap % ALIGN == 0 and cap % min(EXPERT_ROWS, cap) == 0
    nt = t // TILE
    nb = t // LANES
    bpt = TILE // LANES

    cs = jnp.concatenate([c, c_ctx[None, :], jnp.zeros((8 - bsz - 1, d), F32)], axis=0)
    mod = _ada(cs, w_ada[0], b_ada[0][None, :])
    mod = jnp.pad(mod.reshape(8, N_ADA, d), ((0, 0), (0, 8 - N_ADA), (0, 0)))
    modb = mod[:bsz]
    modc = mod[bsz:bsz + 1]

    wall, gpk, gb = _mixer_weights(w_in[0], gla_gate_w[0], gla_gate_b[0])
    n1w = norm1_w[0][None, :]
    rlog = jnp.broadcast_to(ret_decay_logit[0][:, :, None], (2, RET_HEADS, TILE)).astype(F32)
    cum_f = jnp.asarray(_chunk_cumsum_matrix(TILE, False), BF16)
    cum_b = jnp.asarray(_chunk_cumsum_matrix(TILE, True), BF16)
    ind = jnp.asarray(_chunk_indicator(TILE), BF16)
    lvl_f = jnp.asarray(_level_index(False))
    lvl_b = jnp.asarray(_level_index(True))
    bdm = jnp.asarray(_head_block_mask(), BF16)
    rope_col, rope_row = _rope_tables(t)

    sgf, sgb, srf, srb = _ctx_states(ctx, modc, n1w, wall, gpk, gb, rlog, cum_f, cum_b, ind, bdm)
    o_f, gqkv, gates, rqkv, lab = _fwd(x, modb, n1w, wall, gpk, gb, rope_col, rope_row, rlog, cum_f, ind, lvl_f, bdm,
                                       sgf, srf)

    wr = w_router[0].T
    wrh = wr.astype(BF16)
    wrl = (wr - wrh.astype(F32)).astype(BF16)
    x1, h2, aff = _bwd(x, o_f, gqkv, gates, rqkv, lab, modb, rlog, cum_b, ind, lvl_b, bdm, sgb, srb,
                       gla_norm_w[0][None, :], ret_norm_w[0][None, :], w_out[0].astype(BF16),
                       norm2_w[0][None, :], wrh, wrl)

    pos4, off4 = _route(aff.reshape(bsz, ne, nb, LANES), cap)
    pos = pos4.reshape(bsz, ne, t)
    boff = off4[:, :, :, 0]
    base = jnp.transpose(boff[:, :, ::bpt], (0, 2, 1))
    nxt = jnp.concatenate([base[:, 1:], jnp.full((bsz, 1, ne), cap, I32)], axis=1)
    cnt = nxt - base
    basev = jnp.broadcast_to(base[:, :, :, None], (bsz, nt, ne, TILE))
    cntv = jnp.broadcast_to(cnt[:, :, :, None], (bsz, nt, ne, TILE))

    xe = _gather(base, cnt, pos, aff, basev, cntv, h2, cap)
    ye = _experts(xe, w_exp_gate[0], w_exp_up[0], w_exp_down[0], cap)
    return _combine(base, cnt, pos, basev, cntv, x1, modb, final_norm_w[None, :], ye, cap)
```

```python
import functools

import numpy as np
import jax
import jax.numpy as jnp
from jax import lax
from jax.experimental import pallas as pl
from jax.experimental.pallas import tpu as pltpu

F32 = jnp.float32
BF16 = jnp.bfloat16
I32 = jnp.int32

GLA_HEADS = 4
GLA_DK = 64
GLA_DV = 128
GLA_RANK = 16
GLA_TAU = 16.0
RET_HEADS = 4
RET_DK = 128
RET_DV = 128
GRID_W = 64
ROPE_BASE = 10000.0
N_EXPERTS = 16
EC_CAPACITY_FACTOR = 2
N_ADA = 6
EPS = 1e-6

GLA_QK = GLA_HEADS * GLA_DK
GLA_V = GLA_HEADS * GLA_DV
RET_QK = RET_HEADS * RET_DK
RET_V = RET_HEADS * RET_DV
IN_WIDTHS = (GLA_QK, GLA_QK, GLA_V, 2 * GLA_RANK, GLA_V, RET_QK, RET_QK, RET_V, RET_V)

LANES = 128
MXU_N = 256
TILE = 256
GLA_CHUNK = 64
GLA_LEVELS = 6
WIN = 48
ALIGN = 16
WROWS = WIN + ALIGN
GZ_PAD = LANES
EXPERT_ROWS = 512
LN2 = float(np.log(2.0))
MIN_EXP = -149
EXP_STEPS = 8
MANTISSA_STEPS = 56
VMEM_LIMIT = 56 * 1024 * 1024

_NT = (((1,), (1,)), ((), ()))
_TN = (((0,), (0,)), ((), ()))


def _dot(a, b):
    return jnp.dot(a, b, preferred_element_type=F32)


def _dg(a, b, dims):
    return lax.dot_general(a, b, dims, preferred_element_type=F32)


def _split(a):
    hi = a.astype(BF16)
    lo = (a - hi.astype(F32)).astype(BF16)
    return hi, lo


def _logsig(x):
    return jnp.minimum(x, 0.0) - jnp.log(1.0 + jnp.exp(-jnp.abs(x)))


def _silu(x):
    return x / (1.0 + jnp.exp(-x))


def _rms(x, w):
    return x * lax.rsqrt(jnp.mean(x * x, axis=-1, keepdims=True) + EPS) * w


def _chunk_cumsum_matrix(c, reverse):
    i = np.arange(c)[:, None]
    t = np.arange(c)[None, :]
    same = (i // GLA_CHUNK) == (t // GLA_CHUNK)
    return (same & ((t >= i) if reverse else (t <= i))).astype(np.float32)


def _chunk_indicator(c):
    return (np.arange(c)[:, None] // GLA_CHUNK == np.arange(LANES)[None, :]).astype(np.float32)


def _level_index(reverse):
    i = np.arange(GLA_CHUNK)[:, None]
    j = np.arange(GLA_CHUNK)[None, :]
    x = i ^ j
    lvl = np.where(x > 0, np.floor(np.log2(np.maximum(x, 1))), -1).astype(np.int32)
    bad = (j < i) if reverse else (j > i)
    return np.tile(np.where(bad, 99, lvl).astype(np.int32), (1, GLA_HEADS))


def _head_block_mask():
    r = np.arange(GLA_QK)[:, None] // GLA_DK
    l = np.arange(GLA_V)[None, :] // GLA_DV
    return (r == l).astype(np.float32)


def _log_gates(gz, gpk_ref, gb_ref):
    z_hi = gz.astype(BF16).astype(F32)
    group = lax.broadcasted_iota(I32, gz.shape, 1) >> ((2 * GLA_RANK).bit_length() - 1)
    packed = jnp.where(group == 1, gz - z_hi, z_hi).astype(BF16)
    return _logsig(_dot(packed, gpk_ref[...]) + gb_ref[...]) * (1.0 / GLA_TAU)


def _project(xn, wall_ref, gpk_ref, gb_ref):
    proj = _dot(xn.astype(BF16), wall_ref[...])
    o = 0
    out = []
    for w in (GLA_QK, GLA_QK, GLA_V, GLA_V, RET_QK, RET_QK, RET_V, RET_V, GZ_PAD):
        out.append(proj[:, o:o + w])
        o += w
    gq, gk, gv, gg, rq, rk, rv, rg, gz = out
    log_a = _log_gates(gz, gpk_ref, gb_ref)
    return gq * (GLA_DK ** -0.5), gk, gv, gg, rq, rk * (RET_DK ** -0.5), rv, rg, log_a


def _rope(a, cos, sin):
    outs = []
    for h in range(RET_HEADS):
        ah = a[:, h * RET_DK:(h + 1) * RET_DK]
        outs.append(ah * cos + pltpu.roll(ah, RET_DK // 2, 1) * sin)
    return jnp.concatenate(outs, axis=1)


def _stack_heads(a):
    head = lax.broadcasted_iota(I32, a.shape, 1) >> 6
    zero = jnp.zeros_like(a)
    return jnp.concatenate([jnp.where(head == h, a, zero) for h in range(GLA_HEADS)], axis=0)


def _gate_sums(g):
    hi, lo = _split(g)
    return jnp.concatenate([hi, lo], axis=1)


def _level_log_decay(level, g, b, b_ref, row0, reverse, row):
    n = GLA_CHUNK
    upper = ((row >> level) & 1) == 1
    if level == 0:
        return jnp.where(upper, 0.0, g) if reverse else jnp.where(upper, g, 0.0)
    if level == 1:
        nxt = pltpu.roll(g, n - 1, 0)
        prv = pltpu.roll(g, 1, 0)
        r = row & 3
        if reverse:
            return jnp.where(r == 0, g + nxt, jnp.where(r == 1, g, jnp.where(r == 2, 0.0, prv)))
        return jnp.where(r == 0, nxt, jnp.where(r == 1, 0.0, jnp.where(r == 2, g, g + prv)))
    m = 1 << level
    anchors = [jnp.broadcast_to(b_ref[pl.ds(row0 + blk + (m if reverse else m - 1), 1), :], (2 * m, GLA_QK))
               for blk in range(0, n, 2 * m)]
    d = b - (jnp.concatenate(anchors, axis=0) if len(anchors) > 1 else anchors[0])
    return jnp.where(upper, -d, d) if reverse else jnp.where(upper, d, -d)


def _interleave(*stages):
    order = sorted((span * (k + 0.5) / n, i) for i, (_, n, span) in enumerate(stages) for k in range(n))
    for _, i in order:
        next(stages[i][0])
    for gen, _, _ in stages:
        for _ in gen:
            raise AssertionError("stage has more pieces than declared")


def _gla_steps(qkv_ref, g_ref, cum_ref, ind_ref, lvl_ref, bdm_ref, s_ref, b_ref, reverse, emit):
    c = g_ref.shape[0]
    g2 = _gate_sums(g_ref[...])
    r = _dot(cum_ref[...], g2)
    b_ref[...] = r[:, :GLA_QK] + r[:, GLA_QK:]
    cs = _dg(g2, ind_ref[...], _TN)
    tot = cs[:GLA_QK] + cs[GLA_QK:]
    yield
    nchunk = c // GLA_CHUNK
    row = lax.broadcasted_iota(I32, (GLA_CHUNK, GLA_QK), 0)
    lvl = lvl_ref[...] if emit is not None else None
    for ci in (reversed(range(nchunk)) if reverse else range(nchunk)):
        row0 = ci * GLA_CHUNK
        rows = pl.ds(row0, GLA_CHUNK)
        kc = qkv_ref[rows, GLA_QK:2 * GLA_QK].astype(F32)
        vc = qkv_ref[rows, 2 * GLA_QK:]
        gc = g_ref[rows, :]
        bc = b_ref[rows, :]
        bdm = bdm_ref[...]
        s = s_ref[...]
        if emit is not None:
            qc = qkv_ref[rows, :GLA_QK].astype(F32)
            scores = jnp.zeros((GLA_CHUNK, GLA_HEADS * GLA_CHUNK), F32)
            for level in range(GLA_LEVELS):
                e = jnp.exp(_level_log_decay(level, gc, bc, b_ref, row0, reverse, row))
                p = _dg((qc * e).astype(BF16), _stack_heads((kc * e).astype(BF16)), _NT)
                scores = jnp.where(lvl == level, p, scores)
            p = _dg(qc.astype(BF16), _stack_heads(kc.astype(BF16)), _NT)
            scores = jnp.where(lvl == -1, p, scores)
            v_bd = jnp.concatenate([vc] * GLA_HEADS, axis=0) * bdm
            s_bd = jnp.concatenate([s.astype(BF16)] * GLA_HEADS, axis=1) * bdm
            emit(row0, _dot(scores.astype(BF16), v_bd) + _dot((qc * jnp.exp(bc)).astype(BF16), s_bd))
        b_end = b_ref[pl.ds(row0 if reverse else row0 + GLA_CHUNK - 1, 1), :]
        kv = _dg((kc * jnp.exp(b_end - bc)).astype(BF16), vc, _TN)
        own = jnp.concatenate([kv[h * GLA_DK:(h + 1) * GLA_DK, h * GLA_DV:(h + 1) * GLA_DV]
                               for h in range(GLA_HEADS)], axis=0)
        s_ref[...] = jnp.exp(jnp.broadcast_to(tot[:, ci:ci + 1], (GLA_QK, GLA_DV))) * s + own
        yield


def _ret_decays(rlog_ref, c, reverse):
    lg = _logsig(rlog_ref[0])
    ii = lax.broadcasted_iota(I32, (c, c), 0)
    jj = lax.broadcasted_iota(I32, (c, c), 1)
    rel = ((jj - ii) if reverse else (ii - jj)).astype(F32)
    pos = lax.broadcasted_iota(I32, (c, RET_DK), 0).astype(F32)
    dmats, qd, kd, cd = [], [], [], []
    for h in range(RET_HEADS):
        lh = lg[h:h + 1, :]
        dmats.append(jnp.where(rel >= 0, jnp.exp(lh * jnp.maximum(rel, 0.0)), 0.0))
        l1 = lh[:, :RET_DK]
        qd.append(jnp.exp(l1 * ((c - pos) if reverse else (pos + 1.0))))
        kd.append(jnp.exp(l1 * (pos if reverse else (c - 1.0 - pos))))
        cd.append(jnp.exp(l1 * float(c)))
    return dmats, jnp.concatenate(qd, axis=1), jnp.concatenate(kd, axis=1), jnp.concatenate(cd, axis=1)


def _ret_steps(qkv_ref, dmat_ref, qdec_ref, kdec_ref, cdec_ref, s_ref, emit):
    for h in range(RET_HEADS):
        sl = slice(h * RET_DK, (h + 1) * RET_DK)
        qb = qkv_ref[:, h * RET_DK:(h + 1) * RET_DK]
        kb = qkv_ref[:, RET_QK + h * RET_DK:RET_QK + (h + 1) * RET_DK]
        vh = qkv_ref[:, 2 * RET_QK + h * RET_DV:2 * RET_QK + (h + 1) * RET_DV]
        sc = _dg(qb, kb, _NT) * dmat_ref[h]
        s = s_ref[h]
        emit(h, _dot(sc.astype(BF16), vh) + _dot((qb.astype(F32) * qdec_ref[:, sl]).astype(BF16), s.astype(BF16)))
        s_ref[h] = cdec_ref[:, sl] * s + _dg((kb.astype(F32) * kdec_ref[:, sl]).astype(BF16), vh, _TN)
        yield


def _proj_steps(x_ref, mod_ref, n1w_ref, wall_ref, gpk_ref, gb_ref, rope_col_ref, rope_row_ref,
                gqkv_ref, gates_ref, rqkv_ref, lab_ref, nxt_g, nxt_r, nxt_l):
    bsz, c, _ = x_ref.shape
    normed = []
    for b in range(bsz):
        mod = mod_ref[b]
        normed.append((_rms(x_ref[b], n1w_ref[...]) * (1.0 + mod[1:2]) + mod[0:1]).astype(BF16))
    hb = jnp.concatenate(normed, axis=0)
    yield

    def cols(o, w):
        return _dot(hb, wall_ref[:, o:o + w])

    def put(val, o, out_ref, stage_ref, post=None):
        for b in range(bsz):
            part = val[b * c:(b + 1) * c]
            part = (part if post is None else post(part)).astype(BF16)
            out_ref[b, :, o:o + part.shape[1]] = part
            if stage_ref is not None:
                stage_ref[b, :, o:o + part.shape[1]] = part

    put(cols(0, GLA_QK) * (GLA_DK ** -0.5), 0, gqkv_ref, nxt_g)
    put(cols(GLA_QK, GLA_QK), GLA_QK, gqkv_ref, nxt_g)
    yield
    put(cols(2 * GLA_QK, GLA_V), 2 * GLA_QK, gqkv_ref, nxt_g)
    yield
    o = 2 * GLA_QK + GLA_V
    put(cols(o, GLA_V), 0, gates_ref, None)
    yield
    o += GLA_V
    rows_of = lambda i: jnp.concatenate(
        [jnp.broadcast_to(rope_row_ref[0, i, q:q + 1, :], (GRID_W, RET_DK)) for q in range(TILE // GRID_W)], axis=0)
    cos = rows_of(0) + rope_col_ref[0]
    sin = rows_of(1) + rope_col_ref[1]
    rope = lambda a: _rope(a, cos, sin)
    put(cols(o, RET_QK), 0, rqkv_ref, nxt_r, rope)
    yield
    o += RET_QK
    put(cols(o, RET_QK) * (RET_DK ** -0.5), RET_QK, rqkv_ref, nxt_r, rope)
    yield
    o += RET_QK
    put(cols(o, RET_V), 2 * RET_QK, rqkv_ref, nxt_r)
    yield
    o += RET_V
    put(cols(o, RET_V), GLA_V, gates_ref, None)
    yield
    o += RET_V
    log_a = _log_gates(cols(o, GZ_PAD), gpk_ref, gb_ref)
    for b in range(bsz):
        nxt_l[b] = log_a[b * c:(b + 1) * c, :GLA_QK]
        lab_ref[b] = log_a[b * c:(b + 1) * c, GLA_QK:]
    yield


def _ada_kernel(c_ref, w_ref, b_ref, o_ref):
    s_hi, s_lo = _split(_silu(c_ref[...]))
    w_hi, w_lo = _split(w_ref[...])
    o_ref[...] = _dot(s_hi, w_hi) + _dot(s_lo, w_hi) + _dot(s_hi, w_lo) + b_ref[...]


def _ada(cs, w, b):
    rows, d = cs.shape
    n = w.shape[1]
    tn = 1536
    return pl.pallas_call(
        _ada_kernel,
        out_shape=jax.ShapeDtypeStruct((rows, n), F32),
        grid=(n // tn,),
        in_specs=[pl.BlockSpec((rows, d), lambda i: (0, 0)),
                  pl.BlockSpec((d, tn), lambda i: (0, i)),
                  pl.BlockSpec((1, tn), lambda i: (0, i))],
        out_specs=pl.BlockSpec((rows, tn), lambda i: (0, i)),
        compiler_params=pltpu.CompilerParams(dimension_semantics=("arbitrary",), vmem_limit_bytes=VMEM_LIMIT),
        name="ada",
    )(cs, w, b)


def _ctx_kernel(ctx_ref, mod_ref, n1w_ref, wall_ref, gpk_ref, gb_ref, rlog_ref, cumf_ref, cumb_ref,
                ind_ref, bdm_ref, sgf_ref, sgb_ref, srf_ref, srb_ref, b_scr, kv_scr, g_scr, *, c):
    mod = mod_ref[0]
    hc = _rms(ctx_ref[0], n1w_ref[...]) * (1.0 + mod[1:2]) + mod[0:1]
    _, gk, gv, _, _, rk, rv, _, log_a = _project(hc, wall_ref, gpk_ref, gb_ref)
    kv_scr[:, GLA_QK:] = jnp.concatenate([gk, gv], axis=1).astype(BF16)
    rvb = rv.astype(BF16)
    for d, (cum_ref, out_g, out_r) in enumerate(((cumf_ref, sgf_ref, srf_ref), (cumb_ref, sgb_ref, srb_ref))):
        out_g[0] = jnp.zeros((GLA_QK, GLA_DV), F32)
        g_scr[...] = log_a[:, d * GLA_QK:(d + 1) * GLA_QK]
        _interleave((_gla_steps(kv_scr, g_scr, cum_ref, ind_ref, None, bdm_ref, out_g.at[0], b_scr, bool(d), None),
                     1 + c // GLA_CHUNK, 1.0))
        _, _, kdec, _ = _ret_decays(rlog_ref.at[d:d + 1], c, reverse=bool(d))
        for h in range(RET_HEADS):
            sl = slice(h * RET_DK, (h + 1) * RET_DK)
            out_r[0, h] = _dg((rk[:, sl] * kdec[:, sl]).astype(BF16), rvb[:, h * RET_DV:(h + 1) * RET_DV], _TN)


def _ctx_states(ctx, modc, n1w, wall, gpk, gb, rlog, cum_f, cum_b, ind, bdm):
    bsz, c, d = ctx.shape
    const = lambda a: pl.BlockSpec(a.shape, lambda b: (0,) * a.ndim)
    consts = (modc, n1w, wall, gpk, gb, rlog, cum_f, cum_b, ind, bdm)
    return pl.pallas_call(
        functools.partial(_ctx_kernel, c=c),
        out_shape=(jax.ShapeDtypeStruct((bsz, GLA_QK, GLA_DV), F32),
                   jax.ShapeDtypeStruct((bsz, GLA_QK, GLA_DV), F32),
                   jax.ShapeDtypeStruct((bsz, RET_HEADS, RET_DK, RET_DV), F32),
                   jax.ShapeDtypeStruct((bsz, RET_HEADS, RET_DK, RET_DV), F32)),
        grid=(bsz,),
        in_specs=[pl.BlockSpec((1, c, d), lambda b: (b, 0, 0))] + [const(a) for a in consts],
        out_specs=(pl.BlockSpec((1, GLA_QK, GLA_DV),lambda b: (b, 0, 0)),
                   pl.BlockSpec((1, GLA_QK, GLA_DV),lambda b: (b, 0, 0)),
                   pl.BlockSpec((1, RET_HEADS, RET_DK, RET_DV), lambda b: (b, 0, 0, 0)),
                   pl.BlockSpec((1, RET_HEADS, RET_DK, RET_DV), lambda b: (b, 0, 0, 0))),
        scratch_shapes=[pltpu.VMEM((c, GLA_QK), F32), pltpu.VMEM((c, 2 * GLA_QK + GLA_V), BF16),
                        pltpu.VMEM((c, GLA_QK), F32)],
        compiler_params=pltpu.CompilerParams(dimension_semantics=("arbitrary",), vmem_limit_bytes=VMEM_LIMIT),
        name="ctx_states",
    )(ctx, *consts)


def _fwd_kernel(x_ref, mod_ref, n1w_ref, wall_ref, gpk_ref, gb_ref, rope_col_ref, rope_row_ref, rlog_ref,
                cum_ref, ind_ref, lvl_ref, bdm_ref, sg0_ref, sr0_ref,
                of_ref, gqkv_ref, gates_ref, rqkv_ref, lab_ref,
                sg_scr, sr_scr, dmat_scr, qdec_scr, kdec_scr, cdec_scr, b_scr,
                cur_g, cur_r, cur_l, nxt_g, nxt_r, nxt_l, *, c):
    j = pl.program_id(0)
    bsz = x_ref.shape[0]

    @pl.when(j == 0)
    def _first():
        sg_scr[...] = jnp.zeros(sg_scr.shape, F32)
        sr_scr[...] = jnp.zeros(sr_scr.shape, F32)
        nxt_g[...] = jnp.zeros(nxt_g.shape, BF16)
        nxt_r[...] = jnp.zeros(nxt_r.shape, BF16)
        nxt_l[...] = jnp.zeros(nxt_l.shape, F32)
        dmats, qd, kd, cd = _ret_decays(rlog_ref, c, reverse=False)
        for h in range(RET_HEADS):
            dmat_scr[h] = dmats[h]
        qdec_scr[...] = qd
        kdec_scr[...] = kd
        cdec_scr[...] = cd

    @pl.when(j == 1)
    def _seed():
        sg_scr[...] = sg0_ref[...]
        sr_scr[...] = sr0_ref[...]

    cur_g[...] = nxt_g[...]
    cur_r[...] = nxt_r[...]
    cur_l[...] = nxt_l[...]

    def emitters(b):
        def emit_gla(row0, out):
            of_ref[b, pl.ds(row0, GLA_CHUNK), 0:GLA_V] = out

        def emit_ret(h, out):
            of_ref[b, :, GLA_V + h * RET_DV:GLA_V + (h + 1) * RET_DV] = out
        return emit_gla, emit_ret

    stages = []
    for b in range(bsz):
        emit_gla, emit_ret = emitters(b)
        stages.append((_gla_steps(cur_g.at[b], cur_l.at[b], cum_ref, ind_ref, lvl_ref, bdm_ref, sg_scr.at[b],
                                  b_scr.at[b], False, emit_gla), 1 + c // GLA_CHUNK, 1.0))
        stages.append((_ret_steps(cur_r.at[b], dmat_scr, qdec_scr, kdec_scr, cdec_scr, sr_scr.at[b], emit_ret),
                       RET_HEADS, 1.0))
    proj = _proj_steps(x_ref, mod_ref, n1w_ref, wall_ref, gpk_ref, gb_ref, rope_col_ref, rope_row_ref,
                       gqkv_ref, gates_ref, rqkv_ref, lab_ref, nxt_g, nxt_r, nxt_l)
    _interleave(*stages, (proj, 9, 1.0))


def _fwd(x, modb, n1w, wall, gpk, gb, rope_col, rope_row, rlog, cum_f, ind, lvl_f, bdm, sgf, srf):
    bsz, t, d = x.shape
    c = TILE
    nt = t // c
    const = lambda shape: pl.BlockSpec(shape, lambda j: (0,) * len(shape))
    proj_tile = lambda w: pl.BlockSpec((bsz, c, w), lambda j: (0, jnp.minimum(j, nt - 1), 0))
    scan_tile = lambda w: pl.BlockSpec((bsz, c, w), lambda j: (0, jnp.maximum(j - 1, 0), 0))
    rope_tile = pl.BlockSpec((1,) + rope_row.shape[1:], lambda j: (jnp.minimum(j, nt - 1), 0, 0, 0))
    mixw = GLA_V + RET_V
    gw, rw = 2 * GLA_QK + GLA_V, 2 * RET_QK + RET_V
    staging = [pltpu.VMEM((bsz, c, gw), BF16), pltpu.VMEM((bsz, c, rw), BF16), pltpu.VMEM((bsz, c, GLA_QK), F32)]
    return pl.pallas_call(
        functools.partial(_fwd_kernel, c=c),
        out_shape=(jax.ShapeDtypeStruct((bsz, t, mixw), F32),
                   jax.ShapeDtypeStruct((bsz, t, gw), BF16),
                   jax.ShapeDtypeStruct((bsz, t, GLA_V + RET_V), BF16),
                   jax.ShapeDtypeStruct((bsz, t, rw), BF16),
                   jax.ShapeDtypeStruct((bsz, t, GLA_QK), F32)),
        grid=(nt + 1,),
        in_specs=[proj_tile(d),
                  const(modb.shape),
                  const(n1w.shape), const(wall.shape), const(gpk.shape), const(gb.shape),
                  const(rope_col.shape), rope_tile,
                  pl.BlockSpec((1,) + rlog.shape[1:], lambda j: (0, 0, 0)),
                  const(cum_f.shape), const(ind.shape), const(lvl_f.shape), const(bdm.shape),
                  const(sgf.shape), const(srf.shape)],
        out_specs=(scan_tile(mixw), proj_tile(gw), proj_tile(GLA_V + RET_V), proj_tile(rw), proj_tile(GLA_QK)),
        scratch_shapes=[pltpu.VMEM((bsz, GLA_QK, GLA_DV), F32),
                        pltpu.VMEM((bsz, RET_HEADS, RET_DK, RET_DV), F32),
                        pltpu.VMEM((RET_HEADS, c, c), F32),
                        pltpu.VMEM((c, RET_QK), F32),
                        pltpu.VMEM((c, RET_QK), F32),
                        pltpu.VMEM((1, RET_QK), F32),
                        pltpu.VMEM((bsz, c, GLA_QK), F32)] + staging + staging,
        compiler_params=pltpu.CompilerParams(dimension_semantics=("arbitrary",),
                                             vmem_limit_bytes=VMEM_LIMIT),
        name="mixer_fwd",
    )(x, modb, n1w, wall, gpk, gb, rope_col, rope_row, rlog, cum_f, ind, lvl_f, bdm, sgf, srf)


def _bwd_kernel(x_ref, of_ref, gqkv_ref, gates_ref, rqkv_ref, lab_ref, mod_ref, rlog_ref,
                cum_ref, ind_ref, lvl_ref, bdm_ref,
                sg0_ref, sr0_ref, gnw_ref, rnw_ref, wout_ref, n2w_ref, wrh_ref, wrl_ref,
                x1_ref, h2_ref, aff_ref,
                sg_scr, sr_scr, dmat_scr, qdec_scr, kdec_scr, cdec_scr, b_scr, cur_m, nxt_m, mixb, *, c):
    j = pl.program_id(0)
    bsz = x_ref.shape[0]

    @pl.when(j == 0)
    def _first():
        sg_scr[...] = sg0_ref[...]
        sr_scr[...] = sr0_ref[...]
        nxt_m[...] = jnp.zeros(nxt_m.shape, F32)
        dmats, qd, kd, cd = _ret_decays(rlog_ref, c, reverse=True)
        for h in range(RET_HEADS):
            dmat_scr[h] = dmats[h]
        qdec_scr[...] = qd
        kdec_scr[...] = kd
        cdec_scr[...] = cd

    cur_m[...] = nxt_m[...]

    def emitters(b):
        def emit_gla(row0, out):
            nxt_m[pl.ds(b * c + row0, GLA_CHUNK), 0:GLA_V] = of_ref[b, pl.ds(row0, GLA_CHUNK), 0:GLA_V] + out

        def emit_ret(h, out):
            cols = slice(GLA_V + h * RET_DV, GLA_V + (h + 1) * RET_DV)
            nxt_m[pl.ds(b * c, c), cols] = of_ref[b, :, cols] + out
        return emit_gla, emit_ret

    stages = []
    for b in range(bsz):
        emit_gla, emit_ret = emitters(b)
        stages.append((_gla_steps(gqkv_ref.at[b], lab_ref.at[b], cum_ref, ind_ref, lvl_ref, bdm_ref, sg_scr.at[b],
                                  b_scr.at[b], True, emit_gla), 1 + c // GLA_CHUNK, 1.0))
        stages.append((_ret_steps(rqkv_ref.at[b], dmat_scr, qdec_scr, kdec_scr, cdec_scr, sr_scr.at[b], emit_ret),
                       RET_HEADS, 0.85))

    def epilogue():
        for h in range(GLA_HEADS + RET_HEADS):
            sl = slice(h * GLA_DV, (h + 1) * GLA_DV)
            oh = cur_m[:, sl]
            if h < GLA_HEADS:
                y = oh * lax.rsqrt(jnp.mean(oh * oh, axis=-1, keepdims=True) + EPS) * gnw_ref[:, sl]
            else:
                dv = oh - jnp.mean(oh, axis=-1, keepdims=True)
                y = (dv * lax.rsqrt(jnp.mean(dv * dv, axis=-1, keepdims=True) + EPS)
                     * rnw_ref[:, h * RET_DV - GLA_V:(h + 1) * RET_DV - GLA_V])
            gate = jnp.concatenate([gates_ref[b, :, sl] for b in range(bsz)], axis=0).astype(F32)
            mixb[:, sl] = (y * _silu(gate)).astype(BF16)
            yield
        d = x_ref.shape[2]
        step = d // 4
        for p in range(4):
            cs = slice(p * step, (p + 1) * step)
            out = _dot(mixb[...], wout_ref[:, cs])
            for b in range(bsz):
                x1_ref[b, :, cs] = x_ref[b, :, cs] + mod_ref[b, 2:3, cs] * out[b * c:(b + 1) * c]
            yield
        his, los = [], []
        for b in range(bsz):
            h2 = _rms(x1_ref[b], n2w_ref[...]) * (1.0 + mod_ref[b, 4:5, :]) + mod_ref[b, 3:4, :]
            h_hi, h_lo = _split(h2)
            h2_ref[b] = h_hi
            his.append(h_hi)
            los.append(h_lo)
        yield
        h_hi = jnp.concatenate(his, axis=0)
        h_lo = jnp.concatenate(los, axis=0)
        wrh = wrh_ref[...]
        logit = _dg(wrh, h_hi, _NT) + _dg(wrh, h_lo, _NT) + _dg(wrl_ref[...], h_hi, _NT)
        ex = jnp.exp(logit - jnp.max(logit, axis=0, keepdims=True))
        aff = ex / jnp.sum(ex, axis=0, keepdims=True)
        for b in range(bsz):
            aff_ref[b] = aff[:, b * c:(b + 1) * c]
        yield

    _interleave(*stages, (epilogue(), GLA_HEADS + RET_HEADS + 6, 0.6))


def _bwd(x, o_f, gqkv, gates, rqkv, lab, modb, rlog, cum_b, ind, lvl_b, bdm, sgb, srb, gnw, rnw, wout, n2w,
         wrh, wrl):
    bsz, t, d = x.shape
    c = TILE
    nt = t // c
    ne = wrh.shape[0]
    const = lambda shape: pl.BlockSpec(shape, lambda j: (0,) * len(shape))
    scan_at = lambda j: nt - 1 - jnp.minimum(j, nt - 1)
    mix_at = lambda j: nt - 1 - jnp.maximum(j - 1, 0)
    scan_tile = lambda w: pl.BlockSpec((bsz, c, w), lambda j: (0, scan_at(j), 0))
    tile = lambda w: pl.BlockSpec((bsz, c, w), lambda j: (0, mix_at(j), 0))
    mixw = GLA_V + RET_V
    return pl.pallas_call(
        functools.partial(_bwd_kernel, c=c),
        out_shape=(jax.ShapeDtypeStruct((bsz, t, d), F32),
                   jax.ShapeDtypeStruct((bsz, t, d), BF16),
                   jax.ShapeDtypeStruct((bsz, ne, t), F32)),
        grid=(nt + 1,),
        in_specs=[tile(d), scan_tile(o_f.shape[2]), scan_tile(gqkv.shape[2]), tile(gates.shape[2]),
                  scan_tile(rqkv.shape[2]), scan_tile(lab.shape[2]),
                  const(modb.shape),
                  pl.BlockSpec((1,) + rlog.shape[1:], lambda j: (1, 0, 0)),
                  const(cum_b.shape), const(ind.shape), const(lvl_b.shape), const(bdm.shape),
                  const(sgb.shape), const(srb.shape),
                  const(gnw.shape), const(rnw.shape), const(wout.shape), const(n2w.shape),
                  const(wrh.shape), const(wrl.shape)],
        out_specs=(tile(d), tile(d), pl.BlockSpec((bsz, ne, c), lambda j: (0, 0, mix_at(j)))),
        scratch_shapes=[pltpu.VMEM((bsz, GLA_QK, GLA_DV), F32),
                        pltpu.VMEM((bsz, RET_HEADS, RET_DK, RET_DV), F32),
                        pltpu.VMEM((RET_HEADS, c, c), F32),
                        pltpu.VMEM((c, RET_QK), F32),
                        pltpu.VMEM((c, RET_QK), F32),
                        pltpu.VMEM((1, RET_QK), F32),
                        pltpu.VMEM((bsz, c, GLA_QK), F32),
                        pltpu.VMEM((bsz * c, mixw), F32),
                        pltpu.VMEM((bsz * c, mixw), F32),
                        pltpu.VMEM((bsz * c, mixw), BF16)],
        compiler_params=pltpu.CompilerParams(dimension_semantics=("arbitrary",),
                                             vmem_limit_bytes=VMEM_LIMIT),
        name="mixer_bwd",
    )(x, o_f, gqkv, gates, rqkv, lab, modb, rlog, cum_b, ind, lvl_b, bdm, sgb, srb, gnw, rnw, wout, n2w, wrh, wrl)


def _route_kernel(aff_ref, pos_ref, off_ref, *, cap, nb):
    a = aff_ref[0]
    ne = a.shape[0]
    kf = float(cap)

    def count(mask):
        return jnp.sum(jnp.sum(jnp.where(mask, 1.0, 0.0), axis=1, keepdims=True), axis=2, keepdims=True)

    def bisect(lo, hi, mid, thr):
        ok = count(a >= thr(mid)) >= kf
        return jnp.where(ok, mid, lo), jnp.where(ok, hi, mid)

    pow2 = lambda e: jnp.exp(e * LN2)
    lo_e = jnp.full((ne, 1, 1), float(MIN_EXP - 1), F32)
    hi_e = jnp.full((ne, 1, 1), 1.0, F32)
    lo_e, hi_e = lax.fori_loop(0, EXP_STEPS, lambda i, c: bisect(c[0], c[1], jnp.floor((c[0] + c[1]) * 0.5), pow2),
                               (lo_e, hi_e))
    lo, hi = lax.fori_loop(0, MANTISSA_STEPS,
                           lambda i, c: bisect(c[0], c[1], c[0] + (c[1] - c[0]) * 0.5, lambda v: v),
                           (pow2(lo_e), pow2(hi_e)))
    kth = jnp.min(jnp.min(jnp.where(a >= lo, a, jnp.inf), axis=1, keepdims=True), axis=2, keepdims=True)
    gt = a > kth
    eq = a == kth
    need = kf - count(gt)

    upper = (lax.broadcasted_iota(I32, (LANES, LANES), 0) <= lax.broadcasted_iota(I32, (LANES, LANES), 1))
    upper = jnp.where(upper, 1.0, 0.0).astype(BF16)
    ones = jnp.ones((LANES, LANES), BF16)
    lower = (lax.broadcasted_iota(I32, (ne, nb, nb), 2) < lax.broadcasted_iota(I32, (ne, nb, nb), 1))
    lower = jnp.where(lower, 1.0, 0.0).astype(BF16)

    def excl_prefix(mask):
        m = jnp.where(mask, 1.0, 0.0)
        mb = m.astype(BF16).reshape(ne * nb, LANES)
        inc = _dot(mb, upper).reshape(ne, nb, LANES)
        tot = _dot(mb, ones).reshape(ne, nb, LANES)
        offs = lax.dot_general(lower, tot.astype(BF16), (((2,), (1,)), ((0,), (0,))), preferred_element_type=F32)
        return inc - m + offs, offs

    eq_rank, _ = excl_prefix(eq)
    sel = gt | (eq & (eq_rank < need))
    rank, offs = excl_prefix(sel)
    pos_ref[0] = jnp.where(sel, rank, -1.0).astype(I32)
    off_ref[0] = offs.astype(I32)


def _route(aff4, cap):
    bsz, ne, nb, _ = aff4.shape
    spec = pl.BlockSpec((1, ne, nb, LANES), lambda b: (b, 0, 0, 0))
    return pl.pallas_call(
        functools.partial(_route_kernel, cap=cap, nb=nb),
        out_shape=(jax.ShapeDtypeStruct(aff4.shape, I32), jax.ShapeDtypeStruct(aff4.shape, I32)),
        grid=(bsz,),
        in_specs=[spec],
        out_specs=(spec, spec),
        compiler_params=pltpu.CompilerParams(dimension_semantics=("arbitrary",), vmem_limit_bytes=VMEM_LIMIT),
        name="route",
    )(aff4)


def _tile_counts(cnt_ref, b, j, ne):
    m = cnt_ref[b, j, 0]
    for e in range(1, ne):
        m = jnp.maximum(m, cnt_ref[b, j, e])
    return m


def _window_select(rel, valid, val, ne):
    c = rel.shape[1]
    w = lax.broadcasted_iota(I32, (ne, WROWS, c), 1)
    relm = jnp.where(valid, rel, -1)
    sel = jnp.where(relm[:, None, :] == w, jnp.broadcast_to(val[:, None, :], (ne, WROWS, c)), 0.0)
    return sel.reshape(ne * WROWS, c)


def _round_slots(basev, cntv, r):
    start = basev + jnp.minimum(r * WIN, cntv)
    num = jnp.clip(cntv - r * WIN, 0, WIN)
    return start, num


def _round_slots_scalar(base, cnt, r):
    return base + jnp.minimum(r * WIN, cnt), jnp.clip(cnt - r * WIN, 0, WIN)


def _align_down(v):
    shift = ALIGN.bit_length() - 1
    return (v >> shift) << shift


def _gather_kernel(base_ref, cnt_ref, pos_ref, aff_ref, basev_ref, cntv_ref, h2_ref, xe_ref,
                   xbuf, carry, zbuf, sem, zsem, nissued, *, cap, ne):
    b = pl.program_id(0)
    j = pl.program_id(1)
    last_step = (b == pl.num_programs(0) - 1) & (j == pl.num_programs(1) - 1)

    def window_copy(slot, e, row0):
        return pltpu.make_async_copy(xbuf.at[slot, pl.ds(e * WROWS, WROWS)],
                                     xe_ref.at[b, e, pl.ds(row0, WROWS)], sem.at[slot, e])

    def wait_round(g):
        @pl.when(g >= 0)
        def _():
            for e in range(ne):
                window_copy(g % 2, e, 0).wait()

    @pl.when((b == 0) & (j == 0))
    def _start():
        nissued[0] = 0
        zbuf[...] = jnp.zeros(zbuf.shape, BF16)

    @pl.when(j == 0)
    def _start_sample():
        carry[...] = jnp.zeros(carry.shape, BF16)
        cps = [pltpu.make_async_copy(zbuf, xe_ref.at[b, e, pl.ds(cap, WROWS)], zsem.at[e]) for e in range(ne)]
        for cp in cps:
            cp.start()
        for cp in cps:
            cp.wait()

    pos = pos_ref[0]
    basev = basev_ref[0, 0]
    cntv = cntv_ref[0, 0]
    h2 = h2_ref[0]
    ones = jnp.ones(pos.shape, F32)
    nrounds = (_tile_counts(cnt_ref, b, j, ne) + (WIN - 1)) // WIN

    def round_body(r, _):
        g = nissued[0]
        slot = g % 2
        start, num = _round_slots(basev, cntv, r)
        valid = (pos >= start) & (pos < start + num)
        rel = pos - _align_down(start)
        onehot = _window_select(rel, valid, ones, ne).astype(BF16)
        d = h2.shape[1]
        for col0 in range(0, d, MXU_N):
            xbuf[slot, :, col0:col0 + MXU_N] = _dot(onehot, h2[:, col0:col0 + MXU_N]).astype(BF16)
        gcol = jnp.sum(_window_select(rel, valid, aff_ref[0], ne), axis=1, keepdims=True)
        gcol = jnp.broadcast_to(gcol, (ne * WROWS, LANES))
        g_hi = gcol.astype(BF16).astype(F32)
        first_half = lax.broadcasted_iota(I32, (ne * WROWS, LANES), 1) < LANES // 2
        xbuf[slot, :, d:] = jnp.where(first_half, g_hi, gcol - g_hi).astype(BF16)
        first = []
        for e in range(ne):
            s, n = _round_slots_scalar(base_ref[b, j, e], cnt_ref[b, j, e], r)
            first.append(pl.multiple_of(_align_down(s), ALIGN))
            nxt = pl.multiple_of(_align_down(s + n) - _align_down(s), ALIGN)
            row0 = e * WROWS
            xbuf[slot, pl.ds(row0, ALIGN), :] += carry[pl.ds(e * ALIGN, ALIGN), :]
            carry[pl.ds(e * ALIGN, ALIGN), :] = xbuf[slot, pl.ds(pl.multiple_of(row0 + nxt, ALIGN), ALIGN), :]
        wait_round(g - 1)
        for e in range(ne):
            window_copy(slot, e, first[e]).start()
        nissued[0] = g + 1
        return 0

    lax.fori_loop(0, nrounds, round_body, 0)

    @pl.when(last_step)
    def _drain():
        wait_round(nissued[0] - 1)


def _gather(base, cnt, pos, aff, basev, cntv, h2, cap):
    bsz, t, d = h2.shape
    ne = pos.shape[1]
    c = TILE
    nt = t // c
    width = d + LANES
    grid_spec = pltpu.PrefetchScalarGridSpec(
        num_scalar_prefetch=2,
        grid=(bsz, nt),
        in_specs=[pl.BlockSpec((1, ne, c), lambda b, j, *_: (b, 0, j)),
                  pl.BlockSpec((1, ne, c), lambda b, j, *_: (b, 0, j)),
                  pl.BlockSpec((1, 1, ne, c), lambda b, j, *_: (b, j, 0, 0)),
                  pl.BlockSpec((1, 1, ne, c), lambda b, j, *_: (b, j, 0, 0)),
                  pl.BlockSpec((1, c, d), lambda b, j, *_: (b, j, 0))],
        out_specs=pl.BlockSpec(memory_space=pl.ANY),
        scratch_shapes=[pltpu.VMEM((2, ne * WROWS, width), BF16),
                        pltpu.VMEM((ne * ALIGN, width), BF16), pltpu.VMEM((WROWS, width), BF16),
                        pltpu.SemaphoreType.DMA((2, ne)), pltpu.SemaphoreType.DMA((ne,)),
                        pltpu.SMEM((1,), I32)],
    )
    return pl.pallas_call(
        functools.partial(_gather_kernel, cap=cap, ne=ne),
        out_shape=jax.ShapeDtypeStruct((bsz, ne, cap + WROWS, width), BF16),
        grid_spec=grid_spec,
        compiler_params=pltpu.CompilerParams(dimension_semantics=("arbitrary", "arbitrary"),
                                             vmem_limit_bytes=VMEM_LIMIT),
        name="moe_gather",
    )(base, cnt, pos, aff, basev, cntv, h2)


def _expert_kernel(xe_ref, wg_hbm, wu_hbm, wd_hbm, ye_ref,
                   wgua, wda, wgub, wdb, stg, stu, std, sem, *, d, ne, steps):
    e = pl.program_id(0)
    k = pl.program_id(1) * pl.num_programs(2) + pl.program_id(2)
    rows_in = d // steps
    rows_out = wd_hbm.shape[1] // steps

    def chunk_copies(slot, ee, kk):
        r_in = pl.ds(pl.multiple_of(kk * rows_in, ALIGN), rows_in)
        r_out = pl.ds(pl.multiple_of(kk * rows_out, ALIGN), rows_out)
        return [pltpu.make_async_copy(wg_hbm.at[ee, r_in], stg.at[slot], sem.at[slot, 0]),
                pltpu.make_async_copy(wu_hbm.at[ee, r_in], stu.at[slot], sem.at[slot, 1]),
                pltpu.make_async_copy(wd_hbm.at[ee, r_out], std.at[slot], sem.at[slot, 2])]

    def cast_chunk(slot, kk, dst):
        r_in = pl.ds(pl.multiple_of(kk * rows_in, ALIGN), rows_in)
        r_out = pl.ds(pl.multiple_of(kk * rows_out, ALIGN), rows_out)
        ff = stg.shape[2]
        dst[0][r_in, 0:ff] = stg[slot].astype(BF16)
        dst[0][r_in, ff:2 * ff] = stu[slot].astype(BF16)
        dst[1][r_out, :] = std[slot].astype(BF16)

    @pl.when((e == 0) & (k == 0))
    def _first_expert():
        for kk in range(steps):
            cps = chunk_copies(kk % 2, 0, kk)
            for cp in cps:
                cp.start()
            for cp in cps:
                cp.wait()
            cast_chunk(kk % 2, kk, (wgua, wda))
        if ne > 1:
            for cp in chunk_copies(0, 1, 0):
                cp.start()

    def step(cur, nxt):
        slot = k % 2
        last_chunk = k + 1 == steps

        @pl.when(jnp.where(last_chunk, e + 2 < ne, e + 1 < ne))
        def _start_next_chunk():
            for cp in chunk_copies(1 - slot, jnp.where(last_chunk, e + 2, e + 1), jnp.where(last_chunk, 0, k + 1)):
                cp.start()

        @pl.when(e + 1 < ne)
        def _next_weights():
            for cp in chunk_copies(slot, e + 1, k):
                cp.wait()
            cast_chunk(slot, k, nxt)

        xin = xe_ref[0, 0]
        xb = xin[:, :d]
        gate = xin[:, d:d + 1].astype(F32) + xin[:, d + LANES // 2:d + LANES // 2 + 1].astype(F32)
        ff = cur[1].shape[0]
        au = _dot(xb, cur[0][...])
        y = _dot((_silu(au[:, :ff]) * au[:, ff:]).astype(BF16), cur[1][...])
        ye_ref[0, 0] = (y * gate).astype(BF16)

    @pl.when(e % 2 == 0)
    def _even():
        step((wgua, wda), (wgub, wdb))

    @pl.when(e % 2 == 1)
    def _odd():
        step((wgub, wdb), (wgua, wda))


def _experts(xe, wg, wu, wd, cap):
    bsz, ne, _, width = xe.shape
    d = width - LANES
    ff = wg.shape[2]
    rows = min(EXPERT_ROWS, cap)
    steps = bsz * (cap // rows)
    assert steps % 2 == 0 and d % (steps * ALIGN) == 0 and ff % (steps * ALIGN) == 0
    hbm = pl.BlockSpec(memory_space=pl.ANY)
    return pl.pallas_call(
        functools.partial(_expert_kernel, d=d, ne=ne, steps=steps),
        out_shape=jax.ShapeDtypeStruct((bsz, ne, cap, d), BF16),
        grid=(ne, bsz, cap // rows),
        in_specs=[pl.BlockSpec((1, 1, rows, width), lambda e, b, r: (b, e, r, 0)), hbm, hbm, hbm],
        out_specs=pl.BlockSpec((1, 1, rows, d), lambda e, b, r: (b, e, r, 0)),
        scratch_shapes=[pltpu.VMEM((d, 2 * ff), BF16), pltpu.VMEM((ff, d), BF16),
                        pltpu.VMEM((d, 2 * ff), BF16), pltpu.VMEM((ff, d), BF16),
                        pltpu.VMEM((2, d // steps, ff), F32), pltpu.VMEM((2, d // steps, ff), F32),
                        pltpu.VMEM((2, ff // steps, d), F32), pltpu.SemaphoreType.DMA((2, 3))],
        compiler_params=pltpu.CompilerParams(dimension_semantics=("arbitrary", "arbitrary", "arbitrary"),
                                             vmem_limit_bytes=VMEM_LIMIT),
        name="moe_experts",
    )(xe, wg, wu, wd)


def _combine_kernel(base_ref, cnt_ref, pos_ref, basev_ref, cntv_ref, x1_ref, mod_ref, fnw_ref, ye_ref,
                    out_ref, stage, acc, sem, *, cap, ne):
    b = pl.program_id(0)
    j = pl.program_id(1)
    nt = pl.num_programs(1)
    step = b * nt + j
    pos = pos_ref[0]
    ones = jnp.ones(pos.shape, F32)
    basev = basev_ref[0, 0]
    cntv = cntv_ref[0, 0]
    last = cap - WROWS

    def fetch(slot, bb, jj, r):
        cps = []
        for e in range(ne):
            s, _n = _round_slots_scalar(base_ref[bb, jj, e], cnt_ref[bb, jj, e], r)
            row0 = pl.multiple_of(jnp.minimum(_align_down(s), last), ALIGN)
            cps.append(pltpu.make_async_copy(ye_ref.at[bb, e, pl.ds(row0, WROWS)],
                                             stage.at[slot, pl.ds(e * WROWS, WROWS)], sem.at[slot, e]))
        return cps

    def weights(r):
        start, num = _round_slots(basev, cntv, r)
        valid = (pos >= start) & (pos < start + num)
        return _window_select(pos - jnp.minimum(_align_down(start), last), valid, ones, ne).astype(BF16)

    def expand(w, slot):
        return _dg(w, stage[slot], _TN)

    @pl.when(step == 0)
    def _first():
        for cp in fetch(0, b, j, 0):
            cp.start()

    @pl.when(step + 1 < pl.num_programs(0) * nt)
    def _prefetch():
        wrap = j + 1 == nt
        for cp in fetch((step + 1) % 2, jnp.where(wrap, b + 1, b), jnp.where(wrap, 0, j + 1), 0):
            cp.start()

    w0 = weights(0)
    slot = step % 2
    for cp in fetch(slot, b, j, 0):
        cp.wait()
    acc[...] = expand(w0, slot)

    def round_body(r, _):
        cps = fetch(2, b, j, r)
        for cp in cps:
            cp.start()
        w = weights(r)
        for cp in cps:
            cp.wait()
        acc[...] += expand(w, 2)
        return 0

    nrounds = (_tile_counts(cnt_ref, b, j, ne) + (WIN - 1)) // WIN
    lax.fori_loop(1, nrounds, round_body, 0)
    mod = mod_ref[0]
    x2 = x1_ref[0] + mod[5:6] * acc[...]
    out_ref[0] = _rms(x2, fnw_ref[...])


def _combine(base, cnt, pos, basev, cntv, x1, modb, fnw, ye, cap):
    bsz, t, d = x1.shape
    ne = pos.shape[1]
    c = TILE
    nt = t // c
    grid_spec = pltpu.PrefetchScalarGridSpec(
        num_scalar_prefetch=2,
        grid=(bsz, nt),
        in_specs=[pl.BlockSpec((1, ne, c), lambda b, j, *_: (b, 0, j)),
                  pl.BlockSpec((1, 1, ne, c), lambda b, j, *_: (b, j, 0, 0)),
                  pl.BlockSpec((1, 1, ne, c), lambda b, j, *_: (b, j, 0, 0)),
                  pl.BlockSpec((1, c, d), lambda b, j, *_: (b, j, 0)),
                  pl.BlockSpec((1,) + modb.shape[1:], lambda b, j, *_: (b, 0, 0)),
                  pl.BlockSpec(fnw.shape, lambda b, j, *_: (0, 0)),
                  pl.BlockSpec(memory_space=pl.ANY)],
        out_specs=pl.BlockSpec((1, c, d), lambda b, j, *_: (b, j, 0)),
        scratch_shapes=[pltpu.VMEM((3, ne * WROWS, d), BF16), pltpu.VMEM((c, d), F32),
                        pltpu.SemaphoreType.DMA((3, ne))],
    )
    return pl.pallas_call(
        functools.partial(_combine_kernel, cap=cap, ne=ne),
        out_shape=jax.ShapeDtypeStruct((bsz, t, d), F32),
        grid_spec=grid_spec,
        compiler_params=pltpu.CompilerParams(dimension_semantics=("arbitrary", "arbitrary"),
                                             vmem_limit_bytes=VMEM_LIMIT),
        name="moe_combine",
    )(base, cnt, pos, basev, cntv, x1, modb, fnw, ye)


def _rope_tables(t):
    n_freq = RET_DK // 4
    inv = ROPE_BASE ** (-np.arange(n_freq, dtype=np.float64) / n_freq)
    zeros = lambda n: np.zeros((n, n_freq))

    def lanes(row_part, col_part):
        cos = np.concatenate([np.cos(row_part), np.cos(col_part)] * 2, axis=1)
        sin = np.concatenate([-np.sin(row_part), -np.sin(col_part), np.sin(row_part), np.sin(col_part)], axis=1)
        return cos, sin

    col = (np.arange(TILE) % GRID_W)[:, None] * inv
    cos_c, sin_c = lanes(zeros(TILE), col)
    cos_c[:, :n_freq] = 0.0
    cos_c[:, 2 * n_freq:3 * n_freq] = 0.0
    rows_per_tile = TILE // GRID_W
    row = np.arange(t // GRID_W)[:, None] * inv
    cos_r, sin_r = lanes(row, zeros(t // GRID_W))
    cos_r[:, n_freq:2 * n_freq] = 0.0
    cos_r[:, 3 * n_freq:] = 0.0
    row_tab = np.zeros((t // TILE, 2, 8, RET_DK))
    row_tab[:, 0, :rows_per_tile] = cos_r.reshape(t // TILE, rows_per_tile, RET_DK)
    row_tab[:, 1, :rows_per_tile] = sin_r.reshape(t // TILE, rows_per_tile, RET_DK)
    return jnp.asarray(np.stack([cos_c, sin_c]), F32), jnp.asarray(row_tab, F32)


def _mixer_weights(w_in, gate_w, gate_b):
    pts = np.cumsum(IN_WIDTHS)[:-1]
    gq, gk, gv, gz, gg, rq, rk, rv, rg = jnp.split(w_in, [int(p) for p in pts], axis=1)
    zw = 2 * GLA_RANK
    gz = jnp.concatenate([gz, gz, gz, jnp.zeros((gz.shape[0], GZ_PAD - 3 * zw), F32)], axis=1)
    wall = jnp.concatenate([gq, gk, gv, gg, rq, rk, rv, rg, gz], axis=1).astype(BF16)
    gmat = jnp.zeros((zw, 2 * GLA_QK), F32)
    gmat = gmat.at[:GLA_RANK, :GLA_QK].set(gate_w[0]).at[GLA_RANK:, GLA_QK:].set(gate_w[1])
    ghi = gmat.astype(BF16)
    glo = (gmat - ghi.astype(F32)).astype(BF16)
    gpk = jnp.concatenate([ghi, ghi, glo, jnp.zeros((GZ_PAD - 3 * zw, 2 * GLA_QK), BF16)], axis=0)
    return wall, gpk, gate_b.reshape(1, 2 * GLA_QK)


def kernel(x, c, ctx, c_ctx, w_ada, b_ada, norm1_w, w_in, gla_gate_w, gla_gate_b, ret_decay_logit, gla_norm_w,
           ret_norm_w, w_out, norm2_w, w_router, w_exp_gate, w_exp_up, w_exp_down, final_norm_w):
    bsz, t, d = x.shape
    depth = w_ada.shape[0]
    assert depth == 1 and t % TILE == 0 and ctx.shape[1] == TILE
    ne = w_router.shape[2]
    cap = EC_CAPACITY_FACTOR * t // ne
    assert cap >= WROWS and cap % ALIGN == 0 and cap % min(EXPERT_ROWS, cap) == 0
    nt = t // TILE
    nb = t // LANES
    bpt = TILE // LANES

    cs = jnp.concatenate([c, c_ctx[None, :], jnp.zeros((8 - bsz - 1, d), F32)], axis=0)
    mod = _ada(cs, w_ada[0], b_ada[0][None, :])
    mod = jnp.pad(mod.reshape(8, N_ADA, d), ((0, 0), (0, 8 - N_ADA), (0, 0)))
    modb = mod[:bsz]
    modc = mod[bsz:bsz + 1]

    wall, gpk, gb = _mixer_weights(w_in[0], gla_gate_w[0], gla_gate_b[0])
    n1w = norm1_w[0][None, :]
    rlog = jnp.broadcast_to(ret_decay_logit[0][:, :, None], (2, RET_HEADS, TILE)).astype(F32)
    cum_f = jnp.asarray(_chunk_cumsum_matrix(TILE, False), BF16)
    cum_b = jnp.asarray(_chunk_cumsum_matrix(TILE, True), BF16)
    ind = jnp.asarray(_chunk_indicator(TILE), BF16)
    lvl_f = jnp.asarray(_level_index(False))
    lvl_b = jnp.asarray(_level_index(True))
    bdm = jnp.asarray(_head_block_mask(), BF16)
    rope_col, rope_row = _rope_tables(t)

    sgf, sgb, srf, srb = _ctx_states(ctx, modc, n1w, wall, gpk, gb, rlog, cum_f, cum_b, ind, bdm)
    o_f, gqkv, gates, rqkv, lab = _fwd(x, modb, n1w, wall, gpk, gb, rope_col, rope_row, rlog, cum_f, ind, lvl_f, bdm,
                                       sgf, srf)

    wr = w_router[0].T
    wrh = wr.astype(BF16)
    wrl = (wr - wrh.astype(F32)).astype(BF16)
    x1, h2, aff = _bwd(x, o_f, gqkv, gates, rqkv, lab, modb, rlog, cum_b, ind, lvl_b, bdm, sgb, srb,
                       gla_norm_w[0][None, :], ret_norm_w[0][None, :], w_out[0].astype(BF16),
                       norm2_w[0][None, :], wrh, wrl)

    pos4, off4 = _route(aff.reshape(bsz, ne, nb, LANES), cap)
    pos = pos4.reshape(bsz, ne, t)
    boff = off4[:, :, :, 0]
    base = jnp.transpose(boff[:, :, ::bpt], (0, 2, 1))
    nxt = jnp.concatenate([base[:, 1:], jnp.full((bsz, 1, ne), cap, I32)], axis=1)
    cnt = nxt - base
    basev = jnp.broadcast_to(base[:, :, :, None], (bsz, nt, ne, TILE))
    cntv = jnp.broadcast_to(cnt[:, :, :, None], (bsz, nt, ne, TILE))

    xe = _gather(base, cnt, pos, aff, basev, cntv, h2, cap)
    ye = _experts(xe, w_exp_gate[0], w_exp_up[0], w_exp_down[0], cap)
    return _combine(base, cnt, pos, basev, cntv, x1, modb, final_norm_w[None, :], ye, cap)
```

```python
import functools

import numpy as np
import jax
import jax.numpy as jnp
from jax import lax
from jax.experimental import pallas as pl
from jax.experimental.pallas import tpu as pltpu

F32 = jnp.float32
BF16 = jnp.bfloat16
I32 = jnp.int32

GLA_HEADS = 4
GLA_DK = 64
GLA_DV = 128
GLA_RANK = 16
GLA_TAU = 16.0
RET_HEADS = 4
RET_DK = 128
RET_DV = 128
GRID_W = 64
ROPE_BASE = 10000.0
EC_CAPACITY_FACTOR = 2
N_ADA = 6
EPS = 1e-6

GLA_QK = GLA_HEADS * GLA_DK
GLA_V = GLA_HEADS * GLA_DV
RET_QK = RET_HEADS * RET_DK
RET_V = RET_HEADS * RET_DV
IN_WIDTHS = (GLA_QK, GLA_QK, GLA_V, 2 * GLA_RANK, GLA_V, RET_QK, RET_QK, RET_V, RET_V)

LANES = 128
MXU_N = 256
TILE = 256
GLA_CHUNK = 64
GLA_LEVELS = 6
WIN = 48
ALIGN = 16
WROWS = WIN + ALIGN
GZ_PAD = LANES
EXPERT_ROWS = 512
LN2 = float(np.log(2.0))
MIN_EXP = -149
EXP_STEPS = 8
MANTISSA_STEPS = 56
VMEM_LIMIT = 56 * 1024 * 1024

_NT = (((1,), (1,)), ((), ()))
_TN = (((0,), (0,)), ((), ()))


def _dot(a, b):
    return jnp.dot(a, b, preferred_element_type=F32)


def _dg(a, b, dims):
    return lax.dot_general(a, b, dims, preferred_element_type=F32)


def _split(a):
    hi = a.astype(BF16)
    lo = (a - hi.astype(F32)).astype(BF16)
    return hi, lo


def _logsig(x):
    return jnp.minimum(x, 0.0) - jnp.log(1.0 + jnp.exp(-jnp.abs(x)))


def _silu(x):
    return x / (1.0 + jnp.exp(-x))


def _rms(x, w):
    return x * lax.rsqrt(jnp.mean(x * x, axis=-1, keepdims=True) + EPS) * w


def _chunk_cumsum_matrix(c, reverse):
    i = np.arange(c)[:, None]
    t = np.arange(c)[None, :]
    same = (i // GLA_CHUNK) == (t // GLA_CHUNK)
    return (same & ((t >= i) if reverse else (t <= i))).astype(np.float32)


def _chunk_indicator(c):
    return (np.arange(c)[:, None] // GLA_CHUNK == np.arange(LANES)[None, :]).astype(np.float32)


def _level_index(reverse):
    i = np.arange(GLA_CHUNK)[:, None]
    j = np.arange(GLA_CHUNK)[None, :]
    x = i ^ j
    lvl = np.where(x > 0, np.floor(np.log2(np.maximum(x, 1))), -1).astype(np.int32)
    bad = (j < i) if reverse else (j > i)
    return np.tile(np.where(bad, 99, lvl).astype(np.int32), (1, GLA_HEADS))


def _head_block_mask():
    r = np.arange(GLA_QK)[:, None] // GLA_DK
    l = np.arange(GLA_V)[None, :] // GLA_DV
    return (r == l).astype(np.float32)


def _log_gates(gz, gpk_ref, gb_ref):
    z_hi = gz.astype(BF16).astype(F32)
    group = lax.broadcasted_iota(I32, gz.shape, 1) >> ((2 * GLA_RANK).bit_length() - 1)
    packed = jnp.where(group == 1, gz - z_hi, z_hi).astype(BF16)
    return _logsig(_dot(packed, gpk_ref[...]) + gb_ref[...]) * (1.0 / GLA_TAU)


def _project(xn, wall_ref, gpk_ref, gb_ref):
    proj = _dot(xn.astype(BF16), wall_ref[...])
    o = 0
    out = []
    for w in (GLA_QK, GLA_QK, GLA_V, GLA_V, RET_QK, RET_QK, RET_V, RET_V, GZ_PAD):
        out.append(proj[:, o:o + w])
        o += w
    gq, gk, gv, gg, rq, rk, rv, rg, gz = out
    log_a = _log_gates(gz, gpk_ref, gb_ref)
    return gq * (GLA_DK ** -0.5), gk, gv, gg, rq, rk * (RET_DK ** -0.5), rv, rg, log_a


def _rope(a, cos, sin):
    outs = []
    for h in range(a.shape[1] // RET_DK):
        ah = a[:, h * RET_DK:(h + 1) * RET_DK]
        outs.append(ah * cos + pltpu.roll(ah, RET_DK // 2, 1) * sin)
    return jnp.concatenate(outs, axis=1)


def _stack_heads(a):
    head = lax.broadcasted_iota(I32, a.shape, 1) >> 6
    zero = jnp.zeros_like(a)
    return jnp.concatenate([jnp.where(head == h, a, zero) for h in range(GLA_HEADS)], axis=0)


def _gate_sums(g):
    hi, lo = _split(g)
    return jnp.concatenate([hi, lo], axis=1)


def _level_log_decay(level, g, b, b_ref, row0, reverse, row):
    n = GLA_CHUNK
    upper = ((row >> level) & 1) == 1
    if level == 0:
        return jnp.where(upper, 0.0, g) if reverse else jnp.where(upper, g, 0.0)
    if level == 1:
        nxt = pltpu.roll(g, n - 1, 0)
        prv = pltpu.roll(g, 1, 0)
        r = row & 3
        if reverse:
            return jnp.where(r == 0, g + nxt, jnp.where(r == 1, g, jnp.where(r == 2, 0.0, prv)))
        return jnp.where(r == 0, nxt, jnp.where(r == 1, 0.0, jnp.where(r == 2, g, g + prv)))
    m = 1 << level
    anchors = [jnp.broadcast_to(b_ref[pl.ds(row0 + blk + (m if reverse else m - 1), 1), :], (2 * m, GLA_QK))
               for blk in range(0, n, 2 * m)]
    d = b - (jnp.concatenate(anchors, axis=0) if len(anchors) > 1 else anchors[0])
    return jnp.where(upper, -d, d) if reverse else jnp.where(upper, d, -d)


def _interleave(*stages):
    order = sorted((span * (k + 0.5) / n, i) for i, (_, n, span) in enumerate(stages) for k in range(n))
    for _, i in order:
        next(stages[i][0])
    for gen, _, _ in stages:
        for _ in gen:
            raise AssertionError("stage has more pieces than declared")


def _gla_steps(qkv_ref, g_ref, cum_ref, ind_ref, lvl_ref, bdm_ref, s_ref, b_ref, reverse, emit):
    c = g_ref.shape[0]
    g2 = _gate_sums(g_ref[...])
    r = _dot(cum_ref[...], g2)
    b_ref[...] = r[:, :GLA_QK] + r[:, GLA_QK:]
    cs = _dg(g2, ind_ref[...], _TN)
    tot = cs[:GLA_QK] + cs[GLA_QK:]
    yield
    nchunk = c // GLA_CHUNK
    row = lax.broadcasted_iota(I32, (GLA_CHUNK, GLA_QK), 0)
    lvl = lvl_ref[...] if emit is not None else None
    for ci in (reversed(range(nchunk)) if reverse else range(nchunk)):
        row0 = ci * GLA_CHUNK
        rows = pl.ds(row0, GLA_CHUNK)
        kc = qkv_ref[rows, GLA_QK:2 * GLA_QK].astype(F32)
        vc = qkv_ref[rows, 2 * GLA_QK:]
        gc = g_ref[rows, :]
        bc = b_ref[rows, :]
        bdm = bdm_ref[...]
        s = s_ref[...]
        if emit is not None:
            qc = qkv_ref[rows, :GLA_QK].astype(F32)
            scores = jnp.zeros((GLA_CHUNK, GLA_HEADS * GLA_CHUNK), F32)
            for level in range(GLA_LEVELS):
                e = jnp.exp(_level_log_decay(level, gc, bc, b_ref, row0, reverse, row))
                p = _dg((qc * e).astype(BF16), _stack_heads((kc * e).astype(BF16)), _NT)
                scores = jnp.where(lvl == level, p, scores)
            p = _dg(qc.astype(BF16), _stack_heads(kc.astype(BF16)), _NT)
            scores = jnp.where(lvl == -1, p, scores)
            yield
            v_bd = jnp.concatenate([vc] * GLA_HEADS, axis=0) * bdm
            s_bd = jnp.concatenate([s.astype(BF16)] * GLA_HEADS, axis=1) * bdm
            emit(row0, _dot(scores.astype(BF16), v_bd) + _dot((qc * jnp.exp(bc)).astype(BF16), s_bd))
        b_end = b_ref[pl.ds(row0 if reverse else row0 + GLA_CHUNK - 1, 1), :]
        kv = _dg((kc * jnp.exp(b_end - bc)).astype(BF16), vc, _TN)
        own = jnp.concatenate([kv[h * GLA_DK:(h + 1) * GLA_DK, h * GLA_DV:(h + 1) * GLA_DV]
                               for h in range(GLA_HEADS)], axis=0)
        s_ref[...] = jnp.exp(jnp.broadcast_to(tot[:, ci:ci + 1], (GLA_QK, GLA_DV))) * s + own
        yield


def _ret_decays(rlog_ref, c, reverse):
    lg = _logsig(rlog_ref[0])
    ii = lax.broadcasted_iota(I32, (c, c), 0)
    jj = lax.broadcasted_iota(I32, (c, c), 1)
    rel = ((jj - ii) if reverse else (ii - jj)).astype(F32)
    pos = lax.broadcasted_iota(I32, (c, RET_DK), 0).astype(F32)
    dmats, qd, kd, cd = [], [], [], []
    for h in range(RET_HEADS):
        lh = lg[h:h + 1, :]
        dmats.append(jnp.where(rel >= 0, jnp.exp(lh * jnp.maximum(rel, 0.0)), 0.0))
        l1 = lh[:, :RET_DK]
        qd.append(jnp.exp(l1 * ((c - pos) if reverse else (pos + 1.0))))
        kd.append(jnp.exp(l1 * (pos if reverse else (c - 1.0 - pos))))
        cd.append(jnp.exp(l1 * float(c)))
    return dmats, jnp.concatenate(qd, axis=1), jnp.concatenate(kd, axis=1), jnp.concatenate(cd, axis=1)


def _ret_steps(qkv_ref, dmat_ref, qdec_ref, kdec_ref, cdec_ref, s_ref, emit):
    for h in range(RET_HEADS):
        sl = slice(h * RET_DK, (h + 1) * RET_DK)
        qb = qkv_ref[:, h * RET_DK:(h + 1) * RET_DK]
        kb = qkv_ref[:, RET_QK + h * RET_DK:RET_QK + (h + 1) * RET_DK]
        vh = qkv_ref[:, 2 * RET_QK + h * RET_DV:2 * RET_QK + (h + 1) * RET_DV]
        sc = _dg(qb, kb, _NT) * dmat_ref[h]
        yield
        s = s_ref[h]
        emit(h, _dot(sc.astype(BF16), vh) + _dot((qb.astype(F32) * qdec_ref[:, sl]).astype(BF16), s.astype(BF16)))
        s_ref[h] = cdec_ref[:, sl] * s + _dg((kb.astype(F32) * kdec_ref[:, sl]).astype(BF16), vh, _TN)
        yield


def _proj_steps(x_ref, mod_ref, n1w_ref, wall_ref, gpk_ref, gb_ref, rope_col_ref, rope_row_ref,
                gqkv_ref, gates_ref, rqkv_ref, lab_ref, nxt_g, nxt_r, nxt_l):
    bsz, c, _ = x_ref.shape
    normed = []
    for b in range(bsz):
        mod = mod_ref[b]
        normed.append((_rms(x_ref[b], n1w_ref[...]) * (1.0 + mod[1:2]) + mod[0:1]).astype(BF16))
    hb = jnp.concatenate(normed, axis=0)
    yield

    def cols(o, w):
        return _dot(hb, wall_ref[:, o:o + w])

    def put(val, o, out_ref, stage_ref, post=None):
        for b in range(bsz):
            part = val[b * c:(b + 1) * c]
            part = (part if post is None else post(part)).astype(BF16)
            out_ref[b, :, o:o + part.shape[1]] = part
            if stage_ref is not None:
                stage_ref[b, :, o:o + part.shape[1]] = part

    def group(o, w, dst, out_ref, stage_ref, scale=None, post=None):
        for k in range(0, w, MXU_N):
            val = cols(o + k, MXU_N)
            put(val if scale is None else val * scale, dst + k, out_ref, stage_ref, post)
            yield

    yield from group(0, GLA_QK, 0, gqkv_ref, nxt_g, scale=GLA_DK ** -0.5)
    yield from group(GLA_QK, GLA_QK, GLA_QK, gqkv_ref, nxt_g)
    yield from group(2 * GLA_QK, GLA_V, 2 * GLA_QK, gqkv_ref, nxt_g)
    o = 2 * GLA_QK + GLA_V
    yield from group(o, GLA_V, 0, gates_ref, None)
    o += GLA_V
    rows_of = lambda i: jnp.concatenate(
        [jnp.broadcast_to(rope_row_ref[0, i, q:q + 1, :], (GRID_W, RET_DK)) for q in range(TILE // GRID_W)], axis=0)
    cos = rows_of(0) + rope_col_ref[0]
    sin = rows_of(1) + rope_col_ref[1]
    rope = lambda a: _rope(a, cos, sin)
    yield from group(o, RET_QK, 0, rqkv_ref, nxt_r, post=rope)
    o += RET_QK
    yield from group(o, RET_QK, RET_QK, rqkv_ref, nxt_r, scale=RET_DK ** -0.5, post=rope)
    o += RET_QK
    yield from group(o, RET_V, 2 * RET_QK, rqkv_ref, nxt_r)
    o += RET_V
    yield from group(o, RET_V, GLA_V, gates_ref, None)
    o += RET_V
    log_a = _log_gates(cols(o, GZ_PAD), gpk_ref, gb_ref)
    for b in range(bsz):
        nxt_l[b] = log_a[b * c:(b + 1) * c, :GLA_QK]
        lab_ref[b] = log_a[b * c:(b + 1) * c, GLA_QK:]
    yield


def _ada_kernel(c_ref, w_ref, b_ref, o_ref):
    s_hi, s_lo = _split(_silu(c_ref[...]))
    w_hi, w_lo = _split(w_ref[...])
    o_ref[...] = _dot(s_hi, w_hi) + _dot(s_lo, w_hi) + _dot(s_hi, w_lo) + b_ref[...]


def _ada(cs, w, b):
    rows, d = cs.shape
    n = w.shape[1]
    tn = 1536
    return pl.pallas_call(
        _ada_kernel,
        out_shape=jax.ShapeDtypeStruct((rows, n), F32),
        grid=(n // tn,),
        in_specs=[pl.BlockSpec((rows, d), lambda i: (0, 0)),
                  pl.BlockSpec((d, tn), lambda i: (0, i)),
                  pl.BlockSpec((1, tn), lambda i: (0, i))],
        out_specs=pl.BlockSpec((rows, tn), lambda i: (0, i)),
        compiler_params=pltpu.CompilerParams(dimension_semantics=("arbitrary",), vmem_limit_bytes=VMEM_LIMIT),
        name="ada",
    )(cs, w, b)


def _ctx_kernel(ctx_ref, mod_ref, n1w_ref, wall_ref, gpk_ref, gb_ref, rlog_ref, cumf_ref, cumb_ref,
                ind_ref, bdm_ref, sgf_ref, sgb_ref, srf_ref, srb_ref, b_scr, kv_scr, g_scr, *, c):
    mod = mod_ref[0]
    hc = _rms(ctx_ref[0], n1w_ref[...]) * (1.0 + mod[1:2]) + mod[0:1]
    _, gk, gv, _, _, rk, rv, _, log_a = _project(hc, wall_ref, gpk_ref, gb_ref)
    kv_scr[:, GLA_QK:] = jnp.concatenate([gk, gv], axis=1).astype(BF16)
    rvb = rv.astype(BF16)
    for d, (cum_ref, out_g, out_r) in enumerate(((cumf_ref, sgf_ref, srf_ref), (cumb_ref, sgb_ref, srb_ref))):
        out_g[0] = jnp.zeros((GLA_QK, GLA_DV), F32)
        g_scr[...] = log_a[:, d * GLA_QK:(d + 1) * GLA_QK]
        _interleave((_gla_steps(kv_scr, g_scr, cum_ref, ind_ref, None, bdm_ref, out_g.at[0], b_scr, bool(d), None),
                     1 + c // GLA_CHUNK, 1.0))
        _, _, kdec, _ = _ret_decays(rlog_ref.at[d:d + 1], c, reverse=bool(d))
        for h in range(RET_HEADS):
            sl = slice(h * RET_DK, (h + 1) * RET_DK)
            out_r[0, h] = _dg((rk[:, sl] * kdec[:, sl]).astype(BF16), rvb[:, h * RET_DV:(h + 1) * RET_DV], _TN)


def _ctx_states(ctx, modc, n1w, wall, gpk, gb, rlog, cum_f, cum_b, ind, bdm):
    bsz, c, d = ctx.shape
    const = lambda a: pl.BlockSpec(a.shape, lambda b: (0,) * a.ndim)
    consts = (modc, n1w, wall, gpk, gb, rlog, cum_f, cum_b, ind, bdm)
    return pl.pallas_call(
        functools.partial(_ctx_kernel, c=c),
        out_shape=(jax.ShapeDtypeStruct((bsz, GLA_QK, GLA_DV), F32),
                   jax.ShapeDtypeStruct((bsz, GLA_QK, GLA_DV), F32),
                   jax.ShapeDtypeStruct((bsz, RET_HEADS, RET_DK, RET_DV), F32),
                   jax.ShapeDtypeStruct((bsz, RET_HEADS, RET_DK, RET_DV), F32)),
        grid=(bsz,),
        in_specs=[pl.BlockSpec((1, c, d), lambda b: (b, 0, 0))] + [const(a) for a in consts],
        out_specs=(pl.BlockSpec((1, GLA_QK, GLA_DV),lambda b: (b, 0, 0)),
                   pl.BlockSpec((1, GLA_QK, GLA_DV),lambda b: (b, 0, 0)),
                   pl.BlockSpec((1, RET_HEADS, RET_DK, RET_DV), lambda b: (b, 0, 0, 0)),
                   pl.BlockSpec((1, RET_HEADS, RET_DK, RET_DV), lambda b: (b, 0, 0, 0))),
        scratch_shapes=[pltpu.VMEM((c, GLA_QK), F32), pltpu.VMEM((c, 2 * GLA_QK + GLA_V), BF16),
                        pltpu.VMEM((c, GLA_QK), F32)],
        compiler_params=pltpu.CompilerParams(dimension_semantics=("arbitrary",), vmem_limit_bytes=VMEM_LIMIT),
        name="ctx_states",
    )(ctx, *consts)


def _fwd_kernel(x_ref, mod_ref, n1w_ref, wall_ref, gpk_ref, gb_ref, rope_col_ref, rope_row_ref, rlog_ref,
                cum_ref, ind_ref, lvl_ref, bdm_ref, sg0_ref, sr0_ref,
                of_ref, gqkv_ref, gates_ref, rqkv_ref, lab_ref,
                sg_scr, sr_scr, dmat_scr, qdec_scr, kdec_scr, cdec_scr, b_scr,
                cur_g, cur_r, cur_l, nxt_g, nxt_r, nxt_l, *, c):
    j = pl.program_id(0)
    bsz = x_ref.shape[0]

    @pl.when(j == 0)
    def _first():
        sg_scr[...] = jnp.zeros(sg_scr.shape, F32)
        sr_scr[...] = jnp.zeros(sr_scr.shape, F32)
        nxt_g[...] = jnp.zeros(nxt_g.shape, BF16)
        nxt_r[...] = jnp.zeros(nxt_r.shape, BF16)
        nxt_l[...] = jnp.zeros(nxt_l.shape, F32)
        dmats, qd, kd, cd = _ret_decays(rlog_ref, c, reverse=False)
        for h in range(RET_HEADS):
            dmat_scr[h] = dmats[h]
        qdec_scr[...] = qd
        kdec_scr[...] = kd
        cdec_scr[...] = cd

    @pl.when(j == 1)
    def _seed():
        sg_scr[...] = sg0_ref[...]
        sr_scr[...] = sr0_ref[...]

    cur_g[...] = nxt_g[...]
    cur_r[...] = nxt_r[...]
    cur_l[...] = nxt_l[...]

    def emitters(b):
        def emit_gla(row0, out):
            of_ref[b, pl.ds(row0, GLA_CHUNK), 0:GLA_V] = out

        def emit_ret(h, out):
            of_ref[b, :, GLA_V + h * RET_DV:GLA_V + (h + 1) * RET_DV] = out
        return emit_gla, emit_ret

    stages = []
    for b in range(bsz):
        emit_gla, emit_ret = emitters(b)
        stages.append((_gla_steps(cur_g.at[b], cur_l.at[b], cum_ref, ind_ref, lvl_ref, bdm_ref, sg_scr.at[b],
                                  b_scr.at[b], False, emit_gla), 1 + 2 * (c // GLA_CHUNK), 1.0))
        stages.append((_ret_steps(cur_r.at[b], dmat_scr, qdec_scr, kdec_scr, cdec_scr, sr_scr.at[b], emit_ret),
                       2 * RET_HEADS, 1.0))
    proj = _proj_steps(x_ref, mod_ref, n1w_ref, wall_ref, gpk_ref, gb_ref, rope_col_ref, rope_row_ref,
                       gqkv_ref, gates_ref, rqkv_ref, lab_ref, nxt_g, nxt_r, nxt_l)
    nproj = 2 + (2 * GLA_QK + 2 * GLA_V + 2 * RET_QK + 2 * RET_V) // MXU_N
    _interleave(*stages, (proj, nproj, 1.0))


def _fwd(x, modb, n1w, wall, gpk, gb, rope_col, rope_row, rlog, cum_f, ind, lvl_f, bdm, sgf, srf):
    bsz, t, d = x.shape
    c = TILE
    nt = t // c
    const = lambda shape: pl.BlockSpec(shape, lambda j: (0,) * len(shape))
    proj_tile = lambda w: pl.BlockSpec((bsz, c, w), lambda j: (0, jnp.minimum(j, nt - 1), 0))
    scan_tile = lambda w: pl.BlockSpec((bsz, c, w), lambda j: (0, jnp.maximum(j - 1, 0), 0))
    rope_tile = pl.BlockSpec((1,) + rope_row.shape[1:], lambda j: (jnp.minimum(j, nt - 1), 0, 0, 0))
    mixw = GLA_V + RET_V
    gw, rw = 2 * GLA_QK + GLA_V, 2 * RET_QK + RET_V
    staging = [pltpu.VMEM((bsz, c, gw), BF16), pltpu.VMEM((bsz, c, rw), BF16), pltpu.VMEM((bsz, c, GLA_QK), F32)]
    return pl.pallas_call(
        functools.partial(_fwd_kernel, c=c),
        out_shape=(jax.ShapeDtypeStruct((bsz, t, mixw), F32),
                   jax.ShapeDtypeStruct((bsz, t, gw), BF16),
                   jax.ShapeDtypeStruct((bsz, t, GLA_V + RET_V), BF16),
                   jax.ShapeDtypeStruct((bsz, t, rw), BF16),
                   jax.ShapeDtypeStruct((bsz, t, GLA_QK), F32)),
        grid=(nt + 1,),
        in_specs=[proj_tile(d),
                  const(modb.shape),
                  const(n1w.shape), const(wall.shape), const(gpk.shape), const(gb.shape),
                  const(rope_col.shape), rope_tile,
                  pl.BlockSpec((1,) + rlog.shape[1:], lambda j: (0, 0, 0)),
                  const(cum_f.shape), const(ind.shape), const(lvl_f.shape), const(bdm.shape),
                  const(sgf.shape), const(srf.shape)],
        out_specs=(scan_tile(mixw), proj_tile(gw), proj_tile(GLA_V + RET_V), proj_tile(rw), proj_tile(GLA_QK)),
        scratch_shapes=[pltpu.VMEM((bsz, GLA_QK, GLA_DV), F32),
                        pltpu.VMEM((bsz, RET_HEADS, RET_DK, RET_DV), F32),
                        pltpu.VMEM((RET_HEADS, c, c), F32),
                        pltpu.VMEM((c, RET_QK), F32),
                        pltpu.VMEM((c, RET_QK), F32),
                        pltpu.VMEM((1, RET_QK), F32),
                        pltpu.VMEM((bsz, c, GLA_QK), F32)] + staging + staging,
        compiler_params=pltpu.CompilerParams(dimension_semantics=("arbitrary",),
                                             vmem_limit_bytes=VMEM_LIMIT),
        name="mixer_fwd",
    )(x, modb, n1w, wall, gpk, gb, rope_col, rope_row, rlog, cum_f, ind, lvl_f, bdm, sgf, srf)


def _bwd_kernel(x_ref, of_ref, gqkv_ref, gates_ref, rqkv_ref, lab_ref, mod_ref, rlog_ref,
                cum_ref, ind_ref, lvl_ref, bdm_ref,
                sg0_ref, sr0_ref, gnw_ref, rnw_ref, wout_ref, n2w_ref, wrh_ref, wrl_ref,
                x1_ref, h2_ref, aff_ref,
                sg_scr, sr_scr, dmat_scr, qdec_scr, kdec_scr, cdec_scr, b_scr, cur_m, nxt_m, mixb, *, c):
    j = pl.program_id(0)
    bsz = x_ref.shape[0]

    @pl.when(j == 0)
    def _first():
        sg_scr[...] = sg0_ref[...]
        sr_scr[...] = sr0_ref[...]
        nxt_m[...] = jnp.zeros(nxt_m.shape, F32)
        dmats, qd, kd, cd = _ret_decays(rlog_ref, c, reverse=True)
        for h in range(RET_HEADS):
            dmat_scr[h] = dmats[h]
        qdec_scr[...] = qd
        kdec_scr[...] = kd
        cdec_scr[...] = cd

    cur_m[...] = nxt_m[...]

    def emitters(b):
        def emit_gla(row0, out):
            nxt_m[pl.ds(b * c + row0, GLA_CHUNK), 0:GLA_V] = of_ref[b, pl.ds(row0, GLA_CHUNK), 0:GLA_V] + out

        def emit_ret(h, out):
            cols = slice(GLA_V + h * RET_DV, GLA_V + (h + 1) * RET_DV)
            nxt_m[pl.ds(b * c, c), cols] = of_ref[b, :, cols] + out
        return emit_gla, emit_ret

    stages = []
    for b in range(bsz):
        emit_gla, emit_ret = emitters(b)
        stages.append((_gla_steps(gqkv_ref.at[b], lab_ref.at[b], cum_ref, ind_ref, lvl_ref, bdm_ref, sg_scr.at[b],
                                  b_scr.at[b], True, emit_gla), 1 + 2 * (c // GLA_CHUNK), 1.0))
        stages.append((_ret_steps(rqkv_ref.at[b], dmat_scr, qdec_scr, kdec_scr, cdec_scr, sr_scr.at[b], emit_ret),
                       2 * RET_HEADS, 0.85))

    def epilogue():
        for h in range(GLA_HEADS + RET_HEADS):
            sl = slice(h * GLA_DV, (h + 1) * GLA_DV)
            oh = cur_m[:, sl]
            if h < GLA_HEADS:
                y = oh * lax.rsqrt(jnp.mean(oh * oh, axis=-1, keepdims=True) + EPS) * gnw_ref[:, sl]
            else:
                dv = oh - jnp.mean(oh, axis=-1, keepdims=True)
                y = (dv * lax.rsqrt(jnp.mean(dv * dv, axis=-1, keepdims=True) + EPS)
                     * rnw_ref[:, h * RET_DV - GLA_V:(h + 1) * RET_DV - GLA_V])
            gate = jnp.concatenate([gates_ref[b, :, sl] for b in range(bsz)], axis=0).astype(F32)
            mixb[:, sl] = (y * _silu(gate)).astype(BF16)
            yield
        d = x_ref.shape[2]
        step = d // 4
        for p in range(4):
            cs = slice(p * step, (p + 1) * step)
            out = _dot(mixb[...], wout_ref[:, cs])
            for b in range(bsz):
                x1_ref[b, :, cs] = x_ref[b, :, cs] + mod_ref[b, 2:3, cs] * out[b * c:(b + 1) * c]
            yield
        his, los = [], []
        for b in range(bsz):
            h2 = _rms(x1_ref[b], n2w_ref[...]) * (1.0 + mod_ref[b, 4:5, :]) + mod_ref[b, 3:4, :]
            h_hi, h_lo = _split(h2)
            h2_ref[b] = h_hi
            his.append(h_hi)
            los.append(h_lo)
        yield
        h_hi = jnp.concatenate(his, axis=0)
        h_lo = jnp.concatenate(los, axis=0)
        wrh = wrh_ref[...]
        logit = _dg(wrh, h_hi, _NT) + _dg(wrh, h_lo, _NT) + _dg(wrl_ref[...], h_hi, _NT)
        ex = jnp.exp(logit - jnp.max(logit, axis=0, keepdims=True))
        aff = ex / jnp.sum(ex, axis=0, keepdims=True)
        for b in range(bsz):
            aff_ref[b] = aff[:, b * c:(b + 1) * c]
        yield

    _interleave(*stages, (epilogue(), GLA_HEADS + RET_HEADS + 6, 0.6))


def _bwd(x, o_f, gqkv, gates, rqkv, lab, modb, rlog, cum_b, ind, lvl_b, bdm, sgb, srb, gnw, rnw, wout, n2w,
         wrh, wrl):
    bsz, t, d = x.shape
    c = TILE
    nt = t // c
    ne = wrh.shape[0]
    const = lambda shape: pl.BlockSpec(shape, lambda j: (0,) * len(shape))
    scan_at = lambda j: nt - 1 - jnp.minimum(j, nt - 1)
    mix_at = lambda j: nt - 1 - jnp.maximum(j - 1, 0)
    scan_tile = lambda w: pl.BlockSpec((bsz, c, w), lambda j: (0, scan_at(j), 0))
    tile = lambda w: pl.BlockSpec((bsz, c, w), lambda j: (0, mix_at(j), 0))
    mixw = GLA_V + RET_V
    return pl.pallas_call(
        functools.partial(_bwd_kernel, c=c),
        out_shape=(jax.ShapeDtypeStruct((bsz, t, d), F32),
                   jax.ShapeDtypeStruct((bsz, t, d), BF16),
                   jax.ShapeDtypeStruct((bsz, ne, t), F32)),
        grid=(nt + 1,),
        in_specs=[tile(d), scan_tile(o_f.shape[2]), scan_tile(gqkv.shape[2]), tile(gates.shape[2]),
                  scan_tile(rqkv.shape[2]), scan_tile(lab.shape[2]),
                  const(modb.shape),
                  pl.BlockSpec((1,) + rlog.shape[1:], lambda j: (1, 0, 0)),
                  const(cum_b.shape), const(ind.shape), const(lvl_b.shape), const(bdm.shape),
                  const(sgb.shape), const(srb.shape),
                  const(gnw.shape), const(rnw.shape), const(wout.shape), const(n2w.shape),
                  const(wrh.shape), const(wrl.shape)],
        out_specs=(tile(d), tile(d), pl.BlockSpec((bsz, ne, c), lambda j: (0, 0, mix_at(j)))),
        scratch_shapes=[pltpu.VMEM((bsz, GLA_QK, GLA_DV), F32),
                        pltpu.VMEM((bsz, RET_HEADS, RET_DK, RET_DV), F32),
                        pltpu.VMEM((RET_HEADS, c, c), F32),
                        pltpu.VMEM((c, RET_QK), F32),
                        pltpu.VMEM((c, RET_QK), F32),
                        pltpu.VMEM((1, RET_QK), F32),
                        pltpu.VMEM((bsz, c, GLA_QK), F32),
                        pltpu.VMEM((bsz * c, mixw), F32),
                        pltpu.VMEM((bsz * c, mixw), F32),
                        pltpu.VMEM((bsz * c, mixw), BF16)],
        compiler_params=pltpu.CompilerParams(dimension_semantics=("arbitrary",),
                                             vmem_limit_bytes=VMEM_LIMIT),
        name="mixer_bwd",
    )(x, o_f, gqkv, gates, rqkv, lab, modb, rlog, cum_b, ind, lvl_b, bdm, sgb, srb, gnw, rnw, wout, n2w, wrh, wrl)


def _route_kernel(aff_ref, pos_ref, off_ref, *, cap, nb):
    a = aff_ref[0]
    ne = a.shape[0]
    kf = float(cap)

    def count(mask):
        return jnp.sum(jnp.sum(jnp.where(mask, 1.0, 0.0), axis=1, keepdims=True), axis=2, keepdims=True)

    def bisect(lo, hi, mid, thr):
        ok = count(a >= thr(mid)) >= kf
        return jnp.where(ok, mid, lo), jnp.where(ok, hi, mid)

    pow2 = lambda e: jnp.exp(e * LN2)
    lo_e = jnp.full((ne, 1, 1), float(MIN_EXP - 1), F32)
    hi_e = jnp.full((ne, 1, 1), 1.0, F32)
    lo_e, hi_e = lax.fori_loop(0, EXP_STEPS, lambda i, c: bisect(c[0], c[1], jnp.floor((c[0] + c[1]) * 0.5), pow2),
                               (lo_e, hi_e))
    lo, hi = lax.fori_loop(0, MANTISSA_STEPS,
                           lambda i, c: bisect(c[0], c[1], c[0] + (c[1] - c[0]) * 0.5, lambda v: v),
                           (pow2(lo_e), pow2(hi_e)))
    kth = jnp.min(jnp.min(jnp.where(a >= lo, a, jnp.inf), axis=1, keepdims=True), axis=2, keepdims=True)
    gt = a > kth
    eq = a == kth
    need = kf - count(gt)

    upper = (lax.broadcasted_iota(I32, (LANES, LANES), 0) <= lax.broadcasted_iota(I32, (LANES, LANES), 1))
    upper = jnp.where(upper, 1.0, 0.0).astype(BF16)
    ones = jnp.ones((LANES, LANES), BF16)
    lower = (lax.broadcasted_iota(I32, (ne, nb, nb), 2) < lax.broadcasted_iota(I32, (ne, nb, nb), 1))
    lower = jnp.where(lower, 1.0, 0.0).astype(BF16)

    def excl_prefix(mask):
        m = jnp.where(mask, 1.0, 0.0)
        mb = m.astype(BF16).reshape(ne * nb, LANES)
        inc = _dot(mb, upper).reshape(ne, nb, LANES)
        tot = _dot(mb, ones).reshape(ne, nb, LANES)
        offs = lax.dot_general(lower, tot.astype(BF16), (((2,), (1,)), ((0,), (0,))), preferred_element_type=F32)
        return inc - m + offs, offs

    eq_rank, _ = excl_prefix(eq)
    sel = gt | (eq & (eq_rank < need))
    rank, offs = excl_prefix(sel)
    pos_ref[0] = jnp.where(sel, rank, -1.0).astype(I32)
    off_ref[0] = offs.astype(I32)


def _route(aff4, cap):
    bsz, ne, nb, _ = aff4.shape
    spec = pl.BlockSpec((1, ne, nb, LANES), lambda b: (b, 0, 0, 0))
    return pl.pallas_call(
        functools.partial(_route_kernel, cap=cap, nb=nb),
        out_shape=(jax.ShapeDtypeStruct(aff4.shape, I32), jax.ShapeDtypeStruct(aff4.shape, I32)),
        grid=(bsz,),
        in_specs=[spec],
        out_specs=(spec, spec),
        compiler_params=pltpu.CompilerParams(dimension_semantics=("arbitrary",), vmem_limit_bytes=VMEM_LIMIT),
        name="route",
    )(aff4)


def _tile_counts(cnt_ref, b, j, ne):
    m = cnt_ref[b, j, 0]
    for e in range(1, ne):
        m = jnp.maximum(m, cnt_ref[b, j, e])
    return m


def _window_select(rel, valid, val, ne):
    c = rel.shape[1]
    w = lax.broadcasted_iota(I32, (ne, WROWS, c), 1)
    relm = jnp.where(valid, rel, -1)
    sel = jnp.where(relm[:, None, :] == w, jnp.broadcast_to(val[:, None, :], (ne, WROWS, c)), 0.0)
    return sel.reshape(ne * WROWS, c)


def _round_slots(basev, cntv, r):
    start = basev + jnp.minimum(r * WIN, cntv)
    num = jnp.clip(cntv - r * WIN, 0, WIN)
    return start, num


def _round_slots_scalar(base, cnt, r):
    return base + jnp.minimum(r * WIN, cnt), jnp.clip(cnt - r * WIN, 0, WIN)


def _align_down(v):
    shift = ALIGN.bit_length() - 1
    return (v >> shift) << shift


def _gather_kernel(base_ref, cnt_ref, pos_ref, aff_ref, basev_ref, cntv_ref, h2_ref, xe_ref,
                   xbuf, carry, zbuf, sem, zsem, nissued, *, cap, ne):
    b = pl.program_id(0)
    j = pl.program_id(1)
    last_step = (b == pl.num_programs(0) - 1) & (j == pl.num_programs(1) - 1)

    def window_copy(slot, e, row0):
        return pltpu.make_async_copy(xbuf.at[slot, pl.ds(e * WROWS, WROWS)],
                                     xe_ref.at[b, e, pl.ds(row0, WROWS)], sem.at[slot, e])

    def wait_round(g):
        @pl.when(g >= 0)
        def _():
            for e in range(ne):
                window_copy(g % 2, e, 0).wait()

    @pl.when((b == 0) & (j == 0))
    def _start():
        nissued[0] = 0
        zbuf[...] = jnp.zeros(zbuf.shape, BF16)

    @pl.when(j == 0)
    def _start_sample():
        carry[...] = jnp.zeros(carry.shape, BF16)
        cps = [pltpu.make_async_copy(zbuf, xe_ref.at[b, e, pl.ds(cap, WROWS)], zsem.at[e]) for e in range(ne)]
        for cp in cps:
            cp.start()
        for cp in cps:
            cp.wait()

    pos = pos_ref[0]
    basev = basev_ref[0, 0]
    cntv = cntv_ref[0, 0]
    h2 = h2_ref[0]
    ones = jnp.ones(pos.shape, F32)
    nrounds = (_tile_counts(cnt_ref, b, j, ne) + (WIN - 1)) // WIN

    def round_body(r, _):
        g = nissued[0]
        slot = g % 2
        start, num = _round_slots(basev, cntv, r)
        valid = (pos >= start) & (pos < start + num)
        rel = pos - _align_down(start)
        onehot = _window_select(rel, valid, ones, ne).astype(BF16)
        d = h2.shape[1]
        for col0 in range(0, d, MXU_N):
            xbuf[slot, :, col0:col0 + MXU_N] = _dot(onehot, h2[:, col0:col0 + MXU_N]).astype(BF16)
        gcol = jnp.sum(_window_select(rel, valid, aff_ref[0], ne), axis=1, keepdims=True)
        gcol = jnp.broadcast_to(gcol, (ne * WROWS, LANES))
        g_hi = gcol.astype(BF16).astype(F32)
        first_half = lax.broadcasted_iota(I32, (ne * WROWS, LANES), 1) < LANES // 2
        xbuf[slot, :, d:] = jnp.where(first_half, g_hi, gcol - g_hi).astype(BF16)
        first = []
        for e in range(ne):
            s, n = _round_slots_scalar(base_ref[b, j, e], cnt_ref[b, j, e], r)
            first.append(pl.multiple_of(_align_down(s), ALIGN))
            nxt = pl.multiple_of(_align_down(s + n) - _align_down(s), ALIGN)
            row0 = e * WROWS
            xbuf[slot, pl.ds(row0, ALIGN), :] += carry[pl.ds(e * ALIGN, ALIGN), :]
            carry[pl.ds(e * ALIGN, ALIGN), :] = xbuf[slot, pl.ds(pl.multiple_of(row0 + nxt, ALIGN), ALIGN), :]
        wait_round(g - 1)
        for e in range(ne):
            window_copy(slot, e, first[e]).start()
        nissued[0] = g + 1
        return 0

    lax.fori_loop(0, nrounds, round_body, 0)

    @pl.when(last_step)
    def _drain():
        wait_round(nissued[0] - 1)


def _gather(base, cnt, pos, aff, basev, cntv, h2, cap):
    bsz, t, d = h2.shape
    ne = pos.shape[1]
    c = TILE
    nt = t // c
    width = d + LANES
    grid_spec = pltpu.PrefetchScalarGridSpec(
        num_scalar_prefetch=2,
        grid=(bsz, nt),
        in_specs=[pl.BlockSpec((1, ne, c), lambda b, j, *_: (b, 0, j)),
                  pl.BlockSpec((1, ne, c), lambda b, j, *_: (b, 0, j)),
                  pl.BlockSpec((1, 1, ne, c), lambda b, j, *_: (b, j, 0, 0)),
                  pl.BlockSpec((1, 1, ne, c), lambda b, j, *_: (b, j, 0, 0)),
                  pl.BlockSpec((1, c, d), lambda b, j, *_: (b, j, 0))],
        out_specs=pl.BlockSpec(memory_space=pl.ANY),
        scratch_shapes=[pltpu.VMEM((2, ne * WROWS, width), BF16),
                        pltpu.VMEM((ne * ALIGN, width), BF16), pltpu.VMEM((WROWS, width), BF16),
                        pltpu.SemaphoreType.DMA((2, ne)), pltpu.SemaphoreType.DMA((ne,)),
                        pltpu.SMEM((1,), I32)],
    )
    return pl.pallas_call(
        functools.partial(_gather_kernel, cap=cap, ne=ne),
        out_shape=jax.ShapeDtypeStruct((bsz, ne, cap + WROWS, width), BF16),
        grid_spec=grid_spec,
        compiler_params=pltpu.CompilerParams(dimension_semantics=("arbitrary", "arbitrary"),
                                             vmem_limit_bytes=VMEM_LIMIT),
        name="moe_gather",
    )(base, cnt, pos, aff, basev, cntv, h2)


def _expert_kernel(xe_ref, wg_hbm, wu_hbm, wd_hbm, ye_ref,
                   wgua, wda, wgub, wdb, stg, stu, std, sem, *, d, ne, steps):
    e = pl.program_id(0)
    k = pl.program_id(1) * pl.num_programs(2) + pl.program_id(2)
    rows_in = d // steps
    rows_out = wd_hbm.shape[1] // steps

    def chunk_copies(slot, ee, kk):
        r_in = pl.ds(pl.multiple_of(kk * rows_in, ALIGN), rows_in)
        r_out = pl.ds(pl.multiple_of(kk * rows_out, ALIGN), rows_out)
        return [pltpu.make_async_copy(wg_hbm.at[ee, r_in], stg.at[slot], sem.at[slot, 0]),
                pltpu.make_async_copy(wu_hbm.at[ee, r_in], stu.at[slot], sem.at[slot, 1]),
                pltpu.make_async_copy(wd_hbm.at[ee, r_out], std.at[slot], sem.at[slot, 2])]

    def cast_chunk(slot, kk, dst):
        r_in = pl.ds(pl.multiple_of(kk * rows_in, ALIGN), rows_in)
        r_out = pl.ds(pl.multiple_of(kk * rows_out, ALIGN), rows_out)
        ff = stg.shape[2]
        dst[0][r_in, 0:ff] = stg[slot].astype(BF16)
        dst[0][r_in, ff:2 * ff] = stu[slot].astype(BF16)
        dst[1][r_out, :] = std[slot].astype(BF16)

    @pl.when((e == 0) & (k == 0))
    def _first_expert():
        for kk in range(steps):
            cps = chunk_copies(kk % 2, 0, kk)
            for cp in cps:
                cp.start()
            for cp in cps:
                cp.wait()
            cast_chunk(kk % 2, kk, (wgua, wda))
        if ne > 1:
            for cp in chunk_copies(0, 1, 0):
                cp.start()

    def step(cur, nxt):
        slot = k % 2
        last_chunk = k + 1 == steps

        @pl.when(jnp.where(last_chunk, e + 2 < ne, e + 1 < ne))
        def _start_next_chunk():
            for cp in chunk_copies(1 - slot, jnp.where(last_chunk, e + 2, e + 1), jnp.where(last_chunk, 0, k + 1)):
                cp.start()

        @pl.when(e + 1 < ne)
        def _next_weights():
            for cp in chunk_copies(slot, e + 1, k):
                cp.wait()
            cast_chunk(slot, k, nxt)

        xin = xe_ref[0, 0]
        xb = xin[:, :d]
        gate = xin[:, d:d + 1].astype(F32) + xin[:, d + LANES // 2:d + LANES // 2 + 1].astype(F32)
        ff = cur[1].shape[0]
        au = _dot(xb, cur[0][...])
        y = _dot((_silu(au[:, :ff]) * au[:, ff:]).astype(BF16), cur[1][...])
        ye_ref[0, 0] = (y * gate).astype(BF16)

    @pl.when(e % 2 == 0)
    def _even():
        step((wgua, wda), (wgub, wdb))

    @pl.when(e % 2 == 1)
    def _odd():
        step((wgub, wdb), (wgua, wda))


def _experts(xe, wg, wu, wd, cap):
    bsz, ne, _, width = xe.shape
    d = width - LANES
    ff = wg.shape[2]
    rows = min(EXPERT_ROWS, cap)
    steps = bsz * (cap // rows)
    assert steps % 2 == 0 and d % (steps * ALIGN) == 0 and ff % (steps * ALIGN) == 0
    hbm = pl.BlockSpec(memory_space=pl.ANY)
    return pl.pallas_call(
        functools.partial(_expert_kernel, d=d, ne=ne, steps=steps),
        out_shape=jax.ShapeDtypeStruct((bsz, ne, cap, d), BF16),
        grid=(ne, bsz, cap // rows),
        in_specs=[pl.BlockSpec((1, 1, rows, width), lambda e, b, r: (b, e, r, 0)), hbm, hbm, hbm],
        out_specs=pl.BlockSpec((1, 1, rows, d), lambda e, b, r: (b, e, r, 0)),
        scratch_shapes=[pltpu.VMEM((d, 2 * ff), BF16), pltpu.VMEM((ff, d), BF16),
                        pltpu.VMEM((d, 2 * ff), BF16), pltpu.VMEM((ff, d), BF16),
                        pltpu.VMEM((2, d // steps, ff), F32), pltpu.VMEM((2, d // steps, ff), F32),
                        pltpu.VMEM((2, ff // steps, d), F32), pltpu.SemaphoreType.DMA((2, 3))],
        compiler_params=pltpu.CompilerParams(dimension_semantics=("arbitrary", "arbitrary", "arbitrary"),
                                             vmem_limit_bytes=VMEM_LIMIT),
        name="moe_experts",
    )(xe, wg, wu, wd)


def _combine_kernel(base_ref, cnt_ref, pos_ref, basev_ref, cntv_ref, x1_ref, mod_ref, fnw_ref, ye_ref,
                    out_ref, stage, acc, sem, *, cap, ne):
    b = pl.program_id(0)
    j = pl.program_id(1)
    nt = pl.num_programs(1)
    step = b * nt + j
    pos = pos_ref[0]
    ones = jnp.ones(pos.shape, F32)
    basev = basev_ref[0, 0]
    cntv = cntv_ref[0, 0]
    last = cap - WROWS

    def fetch(slot, bb, jj, r):
        cps = []
        for e in range(ne):
            s, _n = _round_slots_scalar(base_ref[bb, jj, e], cnt_ref[bb, jj, e], r)
            row0 = pl.multiple_of(jnp.minimum(_align_down(s), last), ALIGN)
            cps.append(pltpu.make_async_copy(ye_ref.at[bb, e, pl.ds(row0, WROWS)],
                                             stage.at[slot, pl.ds(e * WROWS, WROWS)], sem.at[slot, e]))
        return cps

    def weights(r):
        start, num = _round_slots(basev, cntv, r)
        valid = (pos >= start) & (pos < start + num)
        return _window_select(pos - jnp.minimum(_align_down(start), last), valid, ones, ne).astype(BF16)

    def expand(w, slot):
        return _dg(w, stage[slot], _TN)

    @pl.when(step == 0)
    def _first():
        for cp in fetch(0, b, j, 0):
            cp.start()

    @pl.when(step + 1 < pl.num_programs(0) * nt)
    def _prefetch():
        wrap = j + 1 == nt
        for cp in fetch((step + 1) % 2, jnp.where(wrap, b + 1, b), jnp.where(wrap, 0, j + 1), 0):
            cp.start()

    w0 = weights(0)
    slot = step % 2
    for cp in fetch(slot, b, j, 0):
        cp.wait()
    acc[...] = expand(w0, slot)

    def round_body(r, _):
        cps = fetch(2, b, j, r)
        for cp in cps:
            cp.start()
        w = weights(r)
        for cp in cps:
            cp.wait()
        acc[...] += expand(w, 2)
        return 0

    nrounds = (_tile_counts(cnt_ref, b, j, ne) + (WIN - 1)) // WIN
    lax.fori_loop(1, nrounds, round_body, 0)
    mod = mod_ref[0]
    x2 = x1_ref[0] + mod[5:6] * acc[...]
    out_ref[0] = _rms(x2, fnw_ref[...])


def _combine(base, cnt, pos, basev, cntv, x1, modb, fnw, ye, cap):
    bsz, t, d = x1.shape
    ne = pos.shape[1]
    c = TILE
    nt = t // c
    grid_spec = pltpu.PrefetchScalarGridSpec(
        num_scalar_prefetch=2,
        grid=(bsz, nt),
        in_specs=[pl.BlockSpec((1, ne, c), lambda b, j, *_: (b, 0, j)),
                  pl.BlockSpec((1, 1, ne, c), lambda b, j, *_: (b, j, 0, 0)),
                  pl.BlockSpec((1, 1, ne, c), lambda b, j, *_: (b, j, 0, 0)),
                  pl.BlockSpec((1, c, d), lambda b, j, *_: (b, j, 0)),
                  pl.BlockSpec((1,) + modb.shape[1:], lambda b, j, *_: (b, 0, 0)),
                  pl.BlockSpec(fnw.shape, lambda b, j, *_: (0, 0)),
                  pl.BlockSpec(memory_space=pl.ANY)],
        out_specs=pl.BlockSpec((1, c, d), lambda b, j, *_: (b, j, 0)),
        scratch_shapes=[pltpu.VMEM((3, ne * WROWS, d), BF16), pltpu.VMEM((c, d), F32),
                        pltpu.SemaphoreType.DMA((3, ne))],
    )
    return pl.pallas_call(
        functools.partial(_combine_kernel, cap=cap, ne=ne),
        out_shape=jax.ShapeDtypeStruct((bsz, t, d), F32),
        grid_spec=grid_spec,
        compiler_params=pltpu.CompilerParams(dimension_semantics=("arbitrary", "arbitrary"),
                                             vmem_limit_bytes=VMEM_LIMIT),
        name="moe_combine",
    )(base, cnt, pos, basev, cntv, x1, modb, fnw, ye)


def _rope_tables(t):
    n_freq = RET_DK // 4
    inv = ROPE_BASE ** (-np.arange(n_freq, dtype=np.float64) / n_freq)
    zeros = lambda n: np.zeros((n, n_freq))

    def lanes(row_part, col_part):
        cos = np.concatenate([np.cos(row_part), np.cos(col_part)] * 2, axis=1)
        sin = np.concatenate([-np.sin(row_part), -np.sin(col_part), np.sin(row_part), np.sin(col_part)], axis=1)
        return cos, sin

    col = (np.arange(TILE) % GRID_W)[:, None] * inv
    cos_c, sin_c = lanes(zeros(TILE), col)
    cos_c[:, :n_freq] = 0.0
    cos_c[:, 2 * n_freq:3 * n_freq] = 0.0
    rows_per_tile = TILE // GRID_W
    row = np.arange(t // GRID_W)[:, None] * inv
    cos_r, sin_r = lanes(row, zeros(t // GRID_W))
    cos_r[:, n_freq:2 * n_freq] = 0.0
    cos_r[:, 3 * n_freq:] = 0.0
    row_tab = np.zeros((t // TILE, 2, 8, RET_DK))
    row_tab[:, 0, :rows_per_tile] = cos_r.reshape(t // TILE, rows_per_tile, RET_DK)
    row_tab[:, 1, :rows_per_tile] = sin_r.reshape(t // TILE, rows_per_tile, RET_DK)
    return jnp.asarray(np.stack([cos_c, sin_c]), F32), jnp.asarray(row_tab, F32)


def _mixer_weights(w_in, gate_w, gate_b):
    pts = np.cumsum(IN_WIDTHS)[:-1]
    gq, gk, gv, gz, gg, rq, rk, rv, rg = jnp.split(w_in, [int(p) for p in pts], axis=1)
    zw = 2 * GLA_RANK
    gz = jnp.concatenate([gz, gz, gz, jnp.zeros((gz.shape[0], GZ_PAD - 3 * zw), F32)], axis=1)
    wall = jnp.concatenate([gq, gk, gv, gg, rq, rk, rv, rg, gz], axis=1).astype(BF16)
    gmat = jnp.zeros((zw, 2 * GLA_QK), F32)
    gmat = gmat.at[:GLA_RANK, :GLA_QK].set(gate_w[0]).at[GLA_RANK:, GLA_QK:].set(gate_w[1])
    ghi = gmat.astype(BF16)
    glo = (gmat - ghi.astype(F32)).astype(BF16)
    gpk = jnp.concatenate([ghi, ghi, glo, jnp.zeros((GZ_PAD - 3 * zw, 2 * GLA_QK), BF16)], axis=0)
    return wall, gpk, gate_b.reshape(1, 2 * GLA_QK)


def kernel(x, c, ctx, c_ctx, w_ada, b_ada, norm1_w, w_in, gla_gate_w, gla_gate_b, ret_decay_logit, gla_norm_w,
           ret_norm_w, w_out, norm2_w, w_router, w_exp_gate, w_exp_up, w_exp_down, final_norm_w):
    bsz, t, d = x.shape
    depth = w_ada.shape[0]
    assert depth == 1 and t % TILE == 0 and ctx.shape[1] == TILE
    ne = w_router.shape[2]
    cap = EC_CAPACITY_FACTOR * t // ne
    assert cap >= WROWS and cap % ALIGN == 0 and cap % min(EXPERT_ROWS, cap) == 0
    nt = t // TILE
    nb = t // LANES
    bpt = TILE // LANES

    cs = jnp.concatenate([c, c_ctx[None, :], jnp.zeros((8 - bsz - 1, d), F32)], axis=0)
    mod = _ada(cs, w_ada[0], b_ada[0][None, :])
    mod = jnp.pad(mod.reshape(8, N_ADA, d), ((0, 0), (0, 8 - N_ADA), (0, 0)))
    modb = mod[:bsz]
    modc = mod[bsz:bsz + 1]

    wall, gpk, gb = _mixer_weights(w_in[0], gla_gate_w[0], gla_gate_b[0])
    n1w = norm1_w[0][None, :]
    rlog = jnp.broadcast_to(ret_decay_logit[0][:, :, None], (2, RET_HEADS, TILE)).astype(F32)
    cum_f = jnp.asarray(_chunk_cumsum_matrix(TILE, False), BF16)
    cum_b = jnp.asarray(_chunk_cumsum_matrix(TILE, True), BF16)
    ind = jnp.asarray(_chunk_indicator(TILE), BF16)
    lvl_f = jnp.asarray(_level_index(False))
    lvl_b = jnp.asarray(_level_index(True))
    bdm = jnp.asarray(_head_block_mask(), BF16)
    rope_col, rope_row = _rope_tables(t)

    sgf, sgb, srf, srb = _ctx_states(ctx, modc, n1w, wall, gpk, gb, rlog, cum_f, cum_b, ind, bdm)
    o_f, gqkv, gates, rqkv, lab = _fwd(x, modb, n1w, wall, gpk, gb, rope_col, rope_row, rlog, cum_f, ind, lvl_f, bdm,
                                       sgf, srf)

    wr = w_router[0].T
    wrh = wr.astype(BF16)
    wrl = (wr - wrh.astype(F32)).astype(BF16)
    x1, h2, aff = _bwd(x, o_f, gqkv, gates, rqkv, lab, modb, rlog, cum_b, ind, lvl_b, bdm, sgb, srb,
                       gla_norm_w[0][None, :], ret_norm_w[0][None, :], w_out[0].astype(BF16),
                       norm2_w[0][None, :], wrh, wrl)

    pos4, off4 = _route(aff.reshape(bsz, ne, nb, LANES), cap)
    pos = pos4.reshape(bsz, ne, t)
    boff = off4[:, :, :, 0]
    base = jnp.transpose(boff[:, :, ::bpt], (0, 2, 1))
    nxt = jnp.concatenate([base[:, 1:], jnp.full((bsz, 1, ne), cap, I32)], axis=1)
    cnt = nxt - base
    basev = jnp.broadcast_to(base[:, :, :, None], (bsz, nt, ne, TILE))
    cntv = jnp.broadcast_to(cnt[:, :, :, None], (bsz, nt, ne, TILE))

    xe = _gather(base, cnt, pos, aff, basev, cntv, h2, cap)
    ye = _experts(xe, w_exp_gate[0], w_exp_up[0], w_exp_down[0], cap)
    return _combine(base, cnt, pos, basev, cntv, x1, modb, final_norm_w[None, :], ye, cap)
```

```python
import functools

import numpy as np
import jax
import jax.numpy as jnp
from jax import lax
from jax.experimental import pallas as pl
from jax.experimental.pallas import tpu as pltpu

F32 = jnp.float32
BF16 = jnp.bfloat16
I32 = jnp.int32

GLA_HEADS = 4
GLA_DK = 64
GLA_DV = 128
GLA_RANK = 16
GLA_TAU = 16.0
RET_HEADS = 4
RET_DK = 128
RET_DV = 128
GRID_W = 64
ROPE_BASE = 10000.0
EC_CAPACITY_FACTOR = 2
N_ADA = 6
EPS = 1e-6

GLA_QK = GLA_HEADS * GLA_DK
GLA_V = GLA_HEADS * GLA_DV
RET_QK = RET_HEADS * RET_DK
RET_V = RET_HEADS * RET_DV
IN_WIDTHS = (GLA_QK, GLA_QK, GLA_V, 2 * GLA_RANK, GLA_V, RET_QK, RET_QK, RET_V, RET_V)

LANES = 128
MXU_N = 256
TILE = 256
GLA_CHUNK = 64
GLA_LEVELS = 6
WIN = 48
ALIGN = 16
WROWS = WIN + ALIGN
GZ_PAD = LANES
EXPERT_ROWS = 512
LN2 = float(np.log(2.0))
MIN_EXP = -149
EXP_STEPS = 8
MANTISSA_STEPS = 56
VMEM_LIMIT = 56 * 1024 * 1024

_NT = (((1,), (1,)), ((), ()))
_TN = (((0,), (0,)), ((), ()))


def _dot(a, b):
    return jnp.dot(a, b, preferred_element_type=F32)


def _dg(a, b, dims):
    return lax.dot_general(a, b, dims, preferred_element_type=F32)


def _split(a):
    hi = a.astype(BF16)
    lo = (a - hi.astype(F32)).astype(BF16)
    return hi, lo


def _logsig(x):
    return jnp.minimum(x, 0.0) - jnp.log(1.0 + jnp.exp(-jnp.abs(x)))


def _silu(x):
    return x / (1.0 + jnp.exp(-x))


def _rms(x, w):
    return x * lax.rsqrt(jnp.mean(x * x, axis=-1, keepdims=True) + EPS) * w


def _chunk_cumsum_matrix(c, reverse):
    i = np.arange(c)[:, None]
    t = np.arange(c)[None, :]
    same = (i // GLA_CHUNK) == (t // GLA_CHUNK)
    return (same & ((t >= i) if reverse else (t <= i))).astype(np.float32)


def _chunk_indicator(c):
    return (np.arange(c)[:, None] // GLA_CHUNK == np.arange(LANES)[None, :]).astype(np.float32)


def _level_index(reverse):
    i = np.arange(GLA_CHUNK)[:, None]
    j = np.arange(GLA_CHUNK)[None, :]
    x = i ^ j
    lvl = np.where(x > 0, np.floor(np.log2(np.maximum(x, 1))), -1).astype(np.int32)
    bad = (j < i) if reverse else (j > i)
    return np.tile(np.where(bad, 99, lvl).astype(np.int32), (1, GLA_HEADS))


def _head_block_mask():
    r = np.arange(GLA_QK)[:, None] // GLA_DK
    l = np.arange(GLA_V)[None, :] // GLA_DV
    return (r == l).astype(np.float32)


def _log_gates(gz, gpk_ref, gb_ref):
    z_hi = gz.astype(BF16).astype(F32)
    group = lax.broadcasted_iota(I32, gz.shape, 1) >> ((2 * GLA_RANK).bit_length() - 1)
    packed = jnp.where(group == 1, gz - z_hi, z_hi).astype(BF16)
    return _logsig(_dot(packed, gpk_ref[...]) + gb_ref[...]) * (1.0 / GLA_TAU)


def _project(xn, wall_ref, gpk_ref, gb_ref):
    proj = _dot(xn.astype(BF16), wall_ref[...])
    o = 0
    out = []
    for w in (GLA_QK, GLA_QK, GLA_V, GLA_V, RET_QK, RET_QK, RET_V, RET_V, GZ_PAD):
        out.append(proj[:, o:o + w])
        o += w
    gq, gk, gv, gg, rq, rk, rv, rg, gz = out
    log_a = _log_gates(gz, gpk_ref, gb_ref)
    return gq * (GLA_DK ** -0.5), gk, gv, gg, rq, rk * (RET_DK ** -0.5), rv, rg, log_a


def _rope(a, cos, sin):
    outs = []
    for h in range(a.shape[1] // RET_DK):
        ah = a[:, h * RET_DK:(h + 1) * RET_DK]
        outs.append(ah * cos + pltpu.roll(ah, RET_DK // 2, 1) * sin)
    return jnp.concatenate(outs, axis=1)


def _stack_heads(a):
    head = lax.broadcasted_iota(I32, a.shape, 1) >> 6
    zero = jnp.zeros_like(a)
    return jnp.concatenate([jnp.where(head == h, a, zero) for h in range(GLA_HEADS)], axis=0)


def _gate_sums(g):
    hi, lo = _split(g)
    return jnp.concatenate([hi, lo], axis=1)


def _level_log_decay(level, g, b, b_ref, row0, reverse, row):
    n = GLA_CHUNK
    upper = ((row >> level) & 1) == 1
    if level == 0:
        return jnp.where(upper, 0.0, g) if reverse else jnp.where(upper, g, 0.0)
    if level == 1:
        nxt = pltpu.roll(g, n - 1, 0)
        prv = pltpu.roll(g, 1, 0)
        r = row & 3
        if reverse:
            return jnp.where(r == 0, g + nxt, jnp.where(r == 1, g, jnp.where(r == 2, 0.0, prv)))
        return jnp.where(r == 0, nxt, jnp.where(r == 1, 0.0, jnp.where(r == 2, g, g + prv)))
    m = 1 << level
    anchors = [jnp.broadcast_to(b_ref[pl.ds(row0 + blk + (m if reverse else m - 1), 1), :], (2 * m, GLA_QK))
               for blk in range(0, n, 2 * m)]
    d = b - (jnp.concatenate(anchors, axis=0) if len(anchors) > 1 else anchors[0])
    return jnp.where(upper, -d, d) if reverse else jnp.where(upper, d, -d)


def _interleave(*stages):
    order = sorted((span * (k + 0.5) / n, i) for i, (_, n, span) in enumerate(stages) for k in range(n))
    for _, i in order:
        next(stages[i][0])
    for gen, _, _ in stages:
        for _ in gen:
            raise AssertionError("stage has more pieces than declared")


def _gla_steps(qkv_ref, g_ref, cum_ref, ind_ref, lvl_ref, bdm_ref, s_ref, b_ref, reverse, emit):
    c = g_ref.shape[0]
    g2 = _gate_sums(g_ref[...])
    r = _dot(cum_ref[...], g2)
    b_ref[...] = r[:, :GLA_QK] + r[:, GLA_QK:]
    cs = _dg(g2, ind_ref[...], _TN)
    tot = cs[:GLA_QK] + cs[GLA_QK:]
    yield
    nchunk = c // GLA_CHUNK
    row = lax.broadcasted_iota(I32, (GLA_CHUNK, GLA_QK), 0)
    lvl = lvl_ref[...] if emit is not None else None
    for ci in (reversed(range(nchunk)) if reverse else range(nchunk)):
        row0 = ci * GLA_CHUNK
        rows = pl.ds(row0, GLA_CHUNK)
        kc = qkv_ref[rows, GLA_QK:2 * GLA_QK].astype(F32)
        vc = qkv_ref[rows, 2 * GLA_QK:]
        gc = g_ref[rows, :]
        bc = b_ref[rows, :]
        bdm = bdm_ref[...]
        s = s_ref[...]
        if emit is not None:
            qc = qkv_ref[rows, :GLA_QK].astype(F32)
            scores = jnp.zeros((GLA_CHUNK, GLA_HEADS * GLA_CHUNK), F32)
            for level in range(GLA_LEVELS):
                e = jnp.exp(_level_log_decay(level, gc, bc, b_ref, row0, reverse, row))
                p = _dg((qc * e).astype(BF16), _stack_heads((kc * e).astype(BF16)), _NT)
                scores = jnp.where(lvl == level, p, scores)
            p = _dg(qc.astype(BF16), _stack_heads(kc.astype(BF16)), _NT)
            scores = jnp.where(lvl == -1, p, scores)
            yield
            v_bd = jnp.concatenate([vc] * GLA_HEADS, axis=0) * bdm
            s_bd = jnp.concatenate([s.astype(BF16)] * GLA_HEADS, axis=1) * bdm
            emit(row0, _dot(scores.astype(BF16), v_bd) + _dot((qc * jnp.exp(bc)).astype(BF16), s_bd))
        b_end = b_ref[pl.ds(row0 if reverse else row0 + GLA_CHUNK - 1, 1), :]
        kv = _dg((kc * jnp.exp(b_end - bc)).astype(BF16), vc, _TN)
        own = jnp.concatenate([kv[h * GLA_DK:(h + 1) * GLA_DK, h * GLA_DV:(h + 1) * GLA_DV]
                               for h in range(GLA_HEADS)], axis=0)
        s_ref[...] = jnp.exp(jnp.broadcast_to(tot[:, ci:ci + 1], (GLA_QK, GLA_DV))) * s + own
        yield


def _ret_decays(rlog_ref, c, reverse):
    lg = _logsig(rlog_ref[0])
    ii = lax.broadcasted_iota(I32, (c, c), 0)
    jj = lax.broadcasted_iota(I32, (c, c), 1)
    rel = ((jj - ii) if reverse else (ii - jj)).astype(F32)
    pos = lax.broadcasted_iota(I32, (c, RET_DK), 0).astype(F32)
    dmats, qd, kd, cd = [], [], [], []
    for h in range(RET_HEADS):
        lh = lg[h:h + 1, :]
        dmats.append(jnp.where(rel >= 0, jnp.exp(lh * jnp.maximum(rel, 0.0)), 0.0))
        l1 = lh[:, :RET_DK]
        qd.append(jnp.exp(l1 * ((c - pos) if reverse else (pos + 1.0))))
        kd.append(jnp.exp(l1 * (pos if reverse else (c - 1.0 - pos))))
        cd.append(jnp.exp(l1 * float(c)))
    return dmats, jnp.concatenate(qd, axis=1), jnp.concatenate(kd, axis=1), jnp.concatenate(cd, axis=1)


def _ret_steps(qkv_ref, dmat_ref, qdec_ref, kdec_ref, cdec_ref, s_ref, emit):
    for h in range(RET_HEADS):
        sl = slice(h * RET_DK, (h + 1) * RET_DK)
        qb = qkv_ref[:, h * RET_DK:(h + 1) * RET_DK]
        kb = qkv_ref[:, RET_QK + h * RET_DK:RET_QK + (h + 1) * RET_DK]
        vh = qkv_ref[:, 2 * RET_QK + h * RET_DV:2 * RET_QK + (h + 1) * RET_DV]
        sc = _dg(qb, kb, _NT) * dmat_ref[h]
        yield
        s = s_ref[h]
        emit(h, _dot(sc.astype(BF16), vh) + _dot((qb.astype(F32) * qdec_ref[:, sl]).astype(BF16), s.astype(BF16)))
        s_ref[h] = cdec_ref[:, sl] * s + _dg((kb.astype(F32) * kdec_ref[:, sl]).astype(BF16), vh, _TN)
        yield


def _proj_steps(x_ref, mod_ref, n1w_ref, wall_ref, gpk_ref, gb_ref, rope_col_ref, rope_row_ref,
                gqkv_ref, gates_ref, rqkv_ref, lab_ref, nxt_g, nxt_r, nxt_l):
    bsz, c, _ = x_ref.shape
    normed = []
    for b in range(bsz):
        mod = mod_ref[b]
        normed.append((_rms(x_ref[b], n1w_ref[...]) * (1.0 + mod[1:2]) + mod[0:1]).astype(BF16))
    hb = jnp.concatenate(normed, axis=0)
    yield

    def cols(o, w):
        return _dot(hb, wall_ref[:, o:o + w])

    def put(val, o, out_ref, stage_ref, post=None):
        for b in range(bsz):
            part = val[b * c:(b + 1) * c]
            part = (part if post is None else post(part)).astype(BF16)
            out_ref[b, :, o:o + part.shape[1]] = part
            if stage_ref is not None:
                stage_ref[b, :, o:o + part.shape[1]] = part

    def group(o, w, dst, out_ref, stage_ref, scale=None, post=None):
        for k in range(0, w, MXU_N):
            val = cols(o + k, MXU_N)
            put(val if scale is None else val * scale, dst + k, out_ref, stage_ref, post)
            yield

    yield from group(0, GLA_QK, 0, gqkv_ref, nxt_g, scale=GLA_DK ** -0.5)
    yield from group(GLA_QK, GLA_QK, GLA_QK, gqkv_ref, nxt_g)
    yield from group(2 * GLA_QK, GLA_V, 2 * GLA_QK, gqkv_ref, nxt_g)
    o = 2 * GLA_QK + GLA_V
    yield from group(o, GLA_V, 0, gates_ref, None)
    o += GLA_V
    rows_of = lambda i: jnp.concatenate(
        [jnp.broadcast_to(rope_row_ref[0, i, q:q + 1, :], (GRID_W, RET_DK)) for q in range(TILE // GRID_W)], axis=0)
    cos = rows_of(0) + rope_col_ref[0]
    sin = rows_of(1) + rope_col_ref[1]
    rope = lambda a: _rope(a, cos, sin)
    yield from group(o, RET_QK, 0, rqkv_ref, nxt_r, post=rope)
    o += RET_QK
    yield from group(o, RET_QK, RET_QK, rqkv_ref, nxt_r, scale=RET_DK ** -0.5, post=rope)
    o += RET_QK
    yield from group(o, RET_V, 2 * RET_QK, rqkv_ref, nxt_r)
    o += RET_V
    yield from group(o, RET_V, GLA_V, gates_ref, None)
    o += RET_V
    log_a = _log_gates(cols(o, GZ_PAD), gpk_ref, gb_ref)
    for b in range(bsz):
        nxt_l[b] = log_a[b * c:(b + 1) * c, :GLA_QK]
        lab_ref[b] = log_a[b * c:(b + 1) * c, GLA_QK:]
    yield


def _ada_kernel(c_ref, w_ref, b_ref, o_ref):
    s_hi, s_lo = _split(_silu(c_ref[...]))
    w_hi, w_lo = _split(w_ref[...])
    o_ref[...] = _dot(s_hi, w_hi) + _dot(s_lo, w_hi) + _dot(s_hi, w_lo) + b_ref[...]


def _ada(cs, w, b):
    rows, d = cs.shape
    n = w.shape[1]
    tn = 1536
    return pl.pallas_call(
        _ada_kernel,
        out_shape=jax.ShapeDtypeStruct((rows, n), F32),
        grid=(n // tn,),
        in_specs=[pl.BlockSpec((rows, d), lambda i: (0, 0)),
                  pl.BlockSpec((d, tn), lambda i: (0, i)),
                  pl.BlockSpec((1, tn), lambda i: (0, i))],
        out_specs=pl.BlockSpec((rows, tn), lambda i: (0, i)),
        compiler_params=pltpu.CompilerParams(dimension_semantics=("arbitrary",), vmem_limit_bytes=VMEM_LIMIT),
        name="ada",
    )(cs, w, b)


def _ctx_kernel(ctx_ref, mod_ref, n1w_ref, wall_ref, gpk_ref, gb_ref, rlog_ref, cumf_ref, cumb_ref,
                ind_ref, bdm_ref, sgf_ref, sgb_ref, srf_ref, srb_ref, b_scr, kv_scr, g_scr, *, c):
    mod = mod_ref[0]
    hc = _rms(ctx_ref[0], n1w_ref[...]) * (1.0 + mod[1:2]) + mod[0:1]
    _, gk, gv, _, _, rk, rv, _, log_a = _project(hc, wall_ref, gpk_ref, gb_ref)
    kv_scr[:, GLA_QK:] = jnp.concatenate([gk, gv], axis=1).astype(BF16)
    rvb = rv.astype(BF16)
    for d, (cum_ref, out_g, out_r) in enumerate(((cumf_ref, sgf_ref, srf_ref), (cumb_ref, sgb_ref, srb_ref))):
        out_g[0] = jnp.zeros((GLA_QK, GLA_DV), F32)
        g_scr[...] = log_a[:, d * GLA_QK:(d + 1) * GLA_QK]
        _interleave((_gla_steps(kv_scr, g_scr, cum_ref, ind_ref, None, bdm_ref, out_g.at[0], b_scr, bool(d), None),
                     1 + c // GLA_CHUNK, 1.0))
        _, _, kdec, _ = _ret_decays(rlog_ref.at[d:d + 1], c, reverse=bool(d))
        for h in range(RET_HEADS):
            sl = slice(h * RET_DK, (h + 1) * RET_DK)
            out_r[0, h] = _dg((rk[:, sl] * kdec[:, sl]).astype(BF16), rvb[:, h * RET_DV:(h + 1) * RET_DV], _TN)


def _ctx_states(ctx, modc, n1w, wall, gpk, gb, rlog, cum_f, cum_b, ind, bdm):
    bsz, c, d = ctx.shape
    const = lambda a: pl.BlockSpec(a.shape, lambda b: (0,) * a.ndim)
    consts = (modc, n1w, wall, gpk, gb, rlog, cum_f, cum_b, ind, bdm)
    return pl.pallas_call(
        functools.partial(_ctx_kernel, c=c),
        out_shape=(jax.ShapeDtypeStruct((bsz, GLA_QK, GLA_DV), F32),
                   jax.ShapeDtypeStruct((bsz, GLA_QK, GLA_DV), F32),
                   jax.ShapeDtypeStruct((bsz, RET_HEADS, RET_DK, RET_DV), F32),
                   jax.ShapeDtypeStruct((bsz, RET_HEADS, RET_DK, RET_DV), F32)),
        grid=(bsz,),
        in_specs=[pl.BlockSpec((1, c, d), lambda b: (b, 0, 0))] + [const(a) for a in consts],
        out_specs=(pl.BlockSpec((1, GLA_QK, GLA_DV),lambda b: (b, 0, 0)),
                   pl.BlockSpec((1, GLA_QK, GLA_DV),lambda b: (b, 0, 0)),
                   pl.BlockSpec((1, RET_HEADS, RET_DK, RET_DV), lambda b: (b, 0, 0, 0)),
                   pl.BlockSpec((1, RET_HEADS, RET_DK, RET_DV), lambda b: (b, 0, 0, 0))),
        scratch_shapes=[pltpu.VMEM((c, GLA_QK), F32), pltpu.VMEM((c, 2 * GLA_QK + GLA_V), BF16),
                        pltpu.VMEM((c, GLA_QK), F32)],
        compiler_params=pltpu.CompilerParams(dimension_semantics=("arbitrary",), vmem_limit_bytes=VMEM_LIMIT),
        name="ctx_states",
    )(ctx, *consts)


def _fwd_kernel(x_ref, mod_ref, n1w_ref, wall_ref, gpk_ref, gb_ref, rope_col_ref, rope_row_ref, rlog_ref,
                cum_ref, ind_ref, lvl_ref, bdm_ref, sg0_ref, sr0_ref,
                of_ref, gqkv_ref, gates_ref, rqkv_ref, lab_ref,
                sg_scr, sr_scr, dmat_scr, qdec_scr, kdec_scr, cdec_scr, b_scr,
                cur_g, cur_r, cur_l, nxt_g, nxt_r, nxt_l, *, c):
    j = pl.program_id(0)
    bsz = x_ref.shape[0]

    @pl.when(j == 0)
    def _first():
        sg_scr[...] = jnp.zeros(sg_scr.shape, F32)
        sr_scr[...] = jnp.zeros(sr_scr.shape, F32)
        nxt_g[...] = jnp.zeros(nxt_g.shape, BF16)
        nxt_r[...] = jnp.zeros(nxt_r.shape, BF16)
        nxt_l[...] = jnp.zeros(nxt_l.shape, F32)
        dmats, qd, kd, cd = _ret_decays(rlog_ref, c, reverse=False)
        for h in range(RET_HEADS):
            dmat_scr[h] = dmats[h]
        qdec_scr[...] = qd
        kdec_scr[...] = kd
        cdec_scr[...] = cd

    @pl.when(j == 1)
    def _seed():
        sg_scr[...] = sg0_ref[...]
        sr_scr[...] = sr0_ref[...]

    cur_g[...] = nxt_g[...]
    cur_r[...] = nxt_r[...]
    cur_l[...] = nxt_l[...]

    def emitters(b):
        def emit_gla(row0, out):
            of_ref[b, pl.ds(row0, GLA_CHUNK), 0:GLA_V] = out

        def emit_ret(h, out):
            of_ref[b, :, GLA_V + h * RET_DV:GLA_V + (h + 1) * RET_DV] = out
        return emit_gla, emit_ret

    stages = []
    for b in range(bsz):
        emit_gla, emit_ret = emitters(b)
        stages.append((_gla_steps(cur_g.at[b], cur_l.at[b], cum_ref, ind_ref, lvl_ref, bdm_ref, sg_scr.at[b],
                                  b_scr.at[b], False, emit_gla), 1 + 2 * (c // GLA_CHUNK), 1.0))
        stages.append((_ret_steps(cur_r.at[b], dmat_scr, qdec_scr, kdec_scr, cdec_scr, sr_scr.at[b], emit_ret),
                       2 * RET_HEADS, 1.0))
    proj = _proj_steps(x_ref, mod_ref, n1w_ref, wall_ref, gpk_ref, gb_ref, rope_col_ref, rope_row_ref,
                       gqkv_ref, gates_ref, rqkv_ref, lab_ref, nxt_g, nxt_r, nxt_l)
    nproj = 2 + (2 * GLA_QK + 2 * GLA_V + 2 * RET_QK + 2 * RET_V) // MXU_N
    _interleave(*stages, (proj, nproj, 1.0))


def _fwd(x, modb, n1w, wall, gpk, gb, rope_col, rope_row, rlog, cum_f, ind, lvl_f, bdm, sgf, srf):
    bsz, t, d = x.shape
    c = TILE
    nt = t // c
    const = lambda shape: pl.BlockSpec(shape, lambda j: (0,) * len(shape))
    proj_tile = lambda w: pl.BlockSpec((bsz, c, w), lambda j: (0, jnp.minimum(j, nt - 1), 0))
    scan_tile = lambda w: pl.BlockSpec((bsz, c, w), lambda j: (0, jnp.maximum(j - 1, 0), 0))
    rope_tile = pl.BlockSpec((1,) + rope_row.shape[1:], lambda j: (jnp.minimum(j, nt - 1), 0, 0, 0))
    mixw = GLA_V + RET_V
    gw, rw = 2 * GLA_QK + GLA_V, 2 * RET_QK + RET_V
    staging = [pltpu.VMEM((bsz, c, gw), BF16), pltpu.VMEM((bsz, c, rw), BF16), pltpu.VMEM((bsz, c, GLA_QK), F32)]
    return pl.pallas_call(
        functools.partial(_fwd_kernel, c=c),
        out_shape=(jax.ShapeDtypeStruct((bsz, t, mixw), F32),
                   jax.ShapeDtypeStruct((bsz, t, gw), BF16),
                   jax.ShapeDtypeStruct((bsz, t, GLA_V + RET_V), BF16),
                   jax.ShapeDtypeStruct((bsz, t, rw), BF16),
                   jax.ShapeDtypeStruct((bsz, t, GLA_QK), F32)),
        grid=(nt + 1,),
        in_specs=[proj_tile(d),
                  const(modb.shape),
                  const(n1w.shape), const(wall.shape), const(gpk.shape), const(gb.shape),
                  const(rope_col.shape), rope_tile,
                  pl.BlockSpec((1,) + rlog.shape[1:], lambda j: (0, 0, 0)),
                  const(cum_f.shape), const(ind.shape), const(lvl_f.shape), const(bdm.shape),
                  const(sgf.shape), const(srf.shape)],
        out_specs=(scan_tile(mixw), proj_tile(gw), proj_tile(GLA_V + RET_V), proj_tile(rw), proj_tile(GLA_QK)),
        scratch_shapes=[pltpu.VMEM((bsz, GLA_QK, GLA_DV), F32),
                        pltpu.VMEM((bsz, RET_HEADS, RET_DK, RET_DV), F32),
                        pltpu.VMEM((RET_HEADS, c, c), F32),
                        pltpu.VMEM((c, RET_QK), F32),
                        pltpu.VMEM((c, RET_QK), F32),
                        pltpu.VMEM((1, RET_QK), F32),
                        pltpu.VMEM((bsz, c, GLA_QK), F32)] + staging + staging,
        compiler_params=pltpu.CompilerParams(dimension_semantics=("arbitrary",),
                                             vmem_limit_bytes=VMEM_LIMIT),
        name="mixer_fwd",
    )(x, modb, n1w, wall, gpk, gb, rope_col, rope_row, rlog, cum_f, ind, lvl_f, bdm, sgf, srf)


def _bwd_kernel(x_ref, of_ref, gqkv_ref, gates_ref, rqkv_ref, lab_ref, mod_ref, rlog_ref,
                cum_ref, ind_ref, lvl_ref, bdm_ref,
                sg0_ref, sr0_ref, gnw_ref, rnw_ref, wout_ref, n2w_ref, wrh_ref, wrl_ref,
                x1_ref, h2_ref, aff_ref,
                sg_scr, sr_scr, dmat_scr, qdec_scr, kdec_scr, cdec_scr, b_scr, cur_m, nxt_m, mixb, *, c):
    j = pl.program_id(0)
    bsz = x_ref.shape[0]

    @pl.when(j == 0)
    def _first():
        sg_scr[...] = sg0_ref[...]
        sr_scr[...] = sr0_ref[...]
        nxt_m[...] = jnp.zeros(nxt_m.shape, F32)
        dmats, qd, kd, cd = _ret_decays(rlog_ref, c, reverse=True)
        for h in range(RET_HEADS):
            dmat_scr[h] = dmats[h]
        qdec_scr[...] = qd
        kdec_scr[...] = kd
        cdec_scr[...] = cd

    cur_m[...] = nxt_m[...]

    def emitters(b):
        def emit_gla(row0, out):
            nxt_m[pl.ds(b * c + row0, GLA_CHUNK), 0:GLA_V] = of_ref[b, pl.ds(row0, GLA_CHUNK), 0:GLA_V] + out

        def emit_ret(h, out):
            cols = slice(GLA_V + h * RET_DV, GLA_V + (h + 1) * RET_DV)
            nxt_m[pl.ds(b * c, c), cols] = of_ref[b, :, cols] + out
        return emit_gla, emit_ret

    stages = []
    for b in range(bsz):
        emit_gla, emit_ret = emitters(b)
        stages.append((_gla_steps(gqkv_ref.at[b], lab_ref.at[b], cum_ref, ind_ref, lvl_ref, bdm_ref, sg_scr.at[b],
                                  b_scr.at[b], True, emit_gla), 1 + 2 * (c // GLA_CHUNK), 1.0))
        stages.append((_ret_steps(rqkv_ref.at[b], dmat_scr, qdec_scr, kdec_scr, cdec_scr, sr_scr.at[b], emit_ret),
                       2 * RET_HEADS, 0.85))

    def epilogue():
        for h in range(GLA_HEADS + RET_HEADS):
            sl = slice(h * GLA_DV, (h + 1) * GLA_DV)
            oh = cur_m[:, sl]
            if h < GLA_HEADS:
                y = oh * lax.rsqrt(jnp.mean(oh * oh, axis=-1, keepdims=True) + EPS) * gnw_ref[:, sl]
            else:
                dv = oh - jnp.mean(oh, axis=-1, keepdims=True)
                y = (dv * lax.rsqrt(jnp.mean(dv * dv, axis=-1, keepdims=True) + EPS)
                     * rnw_ref[:, h * RET_DV - GLA_V:(h + 1) * RET_DV - GLA_V])
            gate = jnp.concatenate([gates_ref[b, :, sl] for b in range(bsz)], axis=0).astype(F32)
            mixb[:, sl] = (y * _silu(gate)).astype(BF16)
            yield
        d = x_ref.shape[2]
        step = d // 4
        for p in range(4):
            cs = slice(p * step, (p + 1) * step)
            out = _dot(mixb[...], wout_ref[:, cs])
            for b in range(bsz):
                x1_ref[b, :, cs] = x_ref[b, :, cs] + mod_ref[b, 2:3, cs] * out[b * c:(b + 1) * c]
            yield
        his, los = [], []
        for b in range(bsz):
            h2 = _rms(x1_ref[b], n2w_ref[...]) * (1.0 + mod_ref[b, 4:5, :]) + mod_ref[b, 3:4, :]
            h_hi, h_lo = _split(h2)
            h2_ref[b] = h_hi
            his.append(h_hi)
            los.append(h_lo)
        yield
        h_hi = jnp.concatenate(his, axis=0)
        h_lo = jnp.concatenate(los, axis=0)
        wrh = wrh_ref[...]
        logit = _dg(wrh, h_hi, _NT) + _dg(wrh, h_lo, _NT) + _dg(wrl_ref[...], h_hi, _NT)
        ex = jnp.exp(logit - jnp.max(logit, axis=0, keepdims=True))
        aff = ex / jnp.sum(ex, axis=0, keepdims=True)
        for b in range(bsz):
            aff_ref[b] = aff[:, b * c:(b + 1) * c]
        yield

    _interleave(*stages, (epilogue(), GLA_HEADS + RET_HEADS + 6, 0.7))


def _bwd(x, o_f, gqkv, gates, rqkv, lab, modb, rlog, cum_b, ind, lvl_b, bdm, sgb, srb, gnw, rnw, wout, n2w,
         wrh, wrl):
    bsz, t, d = x.shape
    c = TILE
    nt = t // c
    ne = wrh.shape[0]
    const = lambda shape: pl.BlockSpec(shape, lambda j: (0,) * len(shape))
    scan_at = lambda j: nt - 1 - jnp.minimum(j, nt - 1)
    mix_at = lambda j: nt - 1 - jnp.maximum(j - 1, 0)
    scan_tile = lambda w: pl.BlockSpec((bsz, c, w), lambda j: (0, scan_at(j), 0))
    tile = lambda w: pl.BlockSpec((bsz, c, w), lambda j: (0, mix_at(j), 0))
    mixw = GLA_V + RET_V
    return pl.pallas_call(
        functools.partial(_bwd_kernel, c=c),
        out_shape=(jax.ShapeDtypeStruct((bsz, t, d), F32),
                   jax.ShapeDtypeStruct((bsz, t, d), BF16),
                   jax.ShapeDtypeStruct((bsz, ne, t), F32)),
        grid=(nt + 1,),
        in_specs=[tile(d), scan_tile(o_f.shape[2]), scan_tile(gqkv.shape[2]), tile(gates.shape[2]),
                  scan_tile(rqkv.shape[2]), scan_tile(lab.shape[2]),
                  const(modb.shape),
                  pl.BlockSpec((1,) + rlog.shape[1:], lambda j: (1, 0, 0)),
                  const(cum_b.shape), const(ind.shape), const(lvl_b.shape), const(bdm.shape),
                  const(sgb.shape), const(srb.shape),
                  const(gnw.shape), const(rnw.shape), const(wout.shape), const(n2w.shape),
                  const(wrh.shape), const(wrl.shape)],
        out_specs=(tile(d), tile(d), pl.BlockSpec((bsz, ne, c), lambda j: (0, 0, mix_at(j)))),
        scratch_shapes=[pltpu.VMEM((bsz, GLA_QK, GLA_DV), F32),
                        pltpu.VMEM((bsz, RET_HEADS, RET_DK, RET_DV), F32),
                        pltpu.VMEM((RET_HEADS, c, c), F32),
                        pltpu.VMEM((c, RET_QK), F32),
                        pltpu.VMEM((c, RET_QK), F32),
                        pltpu.VMEM((1, RET_QK), F32),
                        pltpu.VMEM((bsz, c, GLA_QK), F32),
                        pltpu.VMEM((bsz * c, mixw), F32),
                        pltpu.VMEM((bsz * c, mixw), F32),
                        pltpu.VMEM((bsz * c, mixw), BF16)],
        compiler_params=pltpu.CompilerParams(dimension_semantics=("arbitrary",),
                                             vmem_limit_bytes=VMEM_LIMIT),
        name="mixer_bwd",
    )(x, o_f, gqkv, gates, rqkv, lab, modb, rlog, cum_b, ind, lvl_b, bdm, sgb, srb, gnw, rnw, wout, n2w, wrh, wrl)


def _route_kernel(aff_ref, pos_ref, off_ref, *, cap, nb):
    a = aff_ref[0]
    ne = a.shape[0]
    kf = float(cap)

    def count(mask):
        return jnp.sum(jnp.sum(jnp.where(mask, 1.0, 0.0), axis=1, keepdims=True), axis=2, keepdims=True)

    def bisect(lo, hi, mid, thr):
        ok = count(a >= thr(mid)) >= kf
        return jnp.where(ok, mid, lo), jnp.where(ok, hi, mid)

    pow2 = lambda e: jnp.exp(e * LN2)
    lo_e = jnp.full((ne, 1, 1), float(MIN_EXP - 1), F32)
    hi_e = jnp.full((ne, 1, 1), 1.0, F32)
    lo_e, hi_e = lax.fori_loop(0, EXP_STEPS, lambda i, c: bisect(c[0], c[1], jnp.floor((c[0] + c[1]) * 0.5), pow2),
                               (lo_e, hi_e))
    lo, hi = lax.fori_loop(0, MANTISSA_STEPS,
                           lambda i, c: bisect(c[0], c[1], c[0] + (c[1] - c[0]) * 0.5, lambda v: v),
                           (pow2(lo_e), pow2(hi_e)))
    kth = jnp.min(jnp.min(jnp.where(a >= lo, a, jnp.inf), axis=1, keepdims=True), axis=2, keepdims=True)
    gt = a > kth
    eq = a == kth
    need = kf - count(gt)

    upper = (lax.broadcasted_iota(I32, (LANES, LANES), 0) <= lax.broadcasted_iota(I32, (LANES, LANES), 1))
    upper = jnp.where(upper, 1.0, 0.0).astype(BF16)
    ones = jnp.ones((LANES, LANES), BF16)
    lower = (lax.broadcasted_iota(I32, (ne, nb, nb), 2) < lax.broadcasted_iota(I32, (ne, nb, nb), 1))
    lower = jnp.where(lower, 1.0, 0.0).astype(BF16)

    def excl_prefix(mask):
        m = jnp.where(mask, 1.0, 0.0)
        mb = m.astype(BF16).reshape(ne * nb, LANES)
        inc = _dot(mb, upper).reshape(ne, nb, LANES)
        tot = _dot(mb, ones).reshape(ne, nb, LANES)
        offs = lax.dot_general(lower, tot.astype(BF16), (((2,), (1,)), ((0,), (0,))), preferred_element_type=F32)
        return inc - m + offs, offs

    eq_rank, _ = excl_prefix(eq)
    sel = gt | (eq & (eq_rank < need))
    rank, offs = excl_prefix(sel)
    pos_ref[0] = jnp.where(sel, rank, -1.0).astype(I32)
    off_ref[0] = offs.astype(I32)


def _route(aff4, cap):
    bsz, ne, nb, _ = aff4.shape
    spec = pl.BlockSpec((1, ne, nb, LANES), lambda b: (b, 0, 0, 0))
    return pl.pallas_call(
        functools.partial(_route_kernel, cap=cap, nb=nb),
        out_shape=(jax.ShapeDtypeStruct(aff4.shape, I32), jax.ShapeDtypeStruct(aff4.shape, I32)),
        grid=(bsz,),
        in_specs=[spec],
        out_specs=(spec, spec),
        compiler_params=pltpu.CompilerParams(dimension_semantics=("arbitrary",), vmem_limit_bytes=VMEM_LIMIT),
        name="route",
    )(aff4)


def _tile_counts(cnt_ref, b, j, ne):
    m = cnt_ref[b, j, 0]
    for e in range(1, ne):
        m = jnp.maximum(m, cnt_ref[b, j, e])
    return m


def _window_select(rel, valid, val, ne):
    c = rel.shape[1]
    w = lax.broadcasted_iota(I32, (ne, WROWS, c), 1)
    relm = jnp.where(valid, rel, -1)
    sel = jnp.where(relm[:, None, :] == w, jnp.broadcast_to(val[:, None, :], (ne, WROWS, c)), 0.0)
    return sel.reshape(ne * WROWS, c)


def _round_slots(basev, cntv, r):
    start = basev + jnp.minimum(r * WIN, cntv)
    num = jnp.clip(cntv - r * WIN, 0, WIN)
    return start, num


def _round_slots_scalar(base, cnt, r):
    return base + jnp.minimum(r * WIN, cnt), jnp.clip(cnt - r * WIN, 0, WIN)


def _align_down(v):
    shift = ALIGN.bit_length() - 1
    return (v >> shift) << shift


def _gather_kernel(base_ref, cnt_ref, pos_ref, aff_ref, basev_ref, cntv_ref, h2_ref, xe_ref,
                   xbuf, carry, zbuf, sem, zsem, nissued, *, cap, ne):
    b = pl.program_id(0)
    j = pl.program_id(1)
    last_step = (b == pl.num_programs(0) - 1) & (j == pl.num_programs(1) - 1)

    def window_copy(slot, e, row0):
        return pltpu.make_async_copy(xbuf.at[slot, pl.ds(e * WROWS, WROWS)],
                                     xe_ref.at[b, e, pl.ds(row0, WROWS)], sem.at[slot, e])

    def wait_round(g):
        @pl.when(g >= 0)
        def _():
            for e in range(ne):
                window_copy(g % 2, e, 0).wait()

    @pl.when((b == 0) & (j == 0))
    def _start():
        nissued[0] = 0
        zbuf[...] = jnp.zeros(zbuf.shape, BF16)

    @pl.when(j == 0)
    def _start_sample():
        carry[...] = jnp.zeros(carry.shape, BF16)
        cps = [pltpu.make_async_copy(zbuf, xe_ref.at[b, e, pl.ds(cap, WROWS)], zsem.at[e]) for e in range(ne)]
        for cp in cps:
            cp.start()
        for cp in cps:
            cp.wait()

    pos = pos_ref[0]
    basev = basev_ref[0, 0]
    cntv = cntv_ref[0, 0]
    h2 = h2_ref[0]
    ones = jnp.ones(pos.shape, F32)
    nrounds = (_tile_counts(cnt_ref, b, j, ne) + (WIN - 1)) // WIN

    def round_body(r, _):
        g = nissued[0]
        slot = g % 2
        start, num = _round_slots(basev, cntv, r)
        valid = (pos >= start) & (pos < start + num)
        rel = pos - _align_down(start)
        onehot = _window_select(rel, valid, ones, ne).astype(BF16)
        d = h2.shape[1]
        for col0 in range(0, d, MXU_N):
            xbuf[slot, :, col0:col0 + MXU_N] = _dot(onehot, h2[:, col0:col0 + MXU_N]).astype(BF16)
        gcol = jnp.sum(_window_select(rel, valid, aff_ref[0], ne), axis=1, keepdims=True)
        gcol = jnp.broadcast_to(gcol, (ne * WROWS, LANES))
        g_hi = gcol.astype(BF16).astype(F32)
        first_half = lax.broadcasted_iota(I32, (ne * WROWS, LANES), 1) < LANES // 2
        xbuf[slot, :, d:] = jnp.where(first_half, g_hi, gcol - g_hi).astype(BF16)
        first = []
        for e in range(ne):
            s, n = _round_slots_scalar(base_ref[b, j, e], cnt_ref[b, j, e], r)
            first.append(pl.multiple_of(_align_down(s), ALIGN))
            nxt = pl.multiple_of(_align_down(s + n) - _align_down(s), ALIGN)
            row0 = e * WROWS
            xbuf[slot, pl.ds(row0, ALIGN), :] += carry[pl.ds(e * ALIGN, ALIGN), :]
            carry[pl.ds(e * ALIGN, ALIGN), :] = xbuf[slot, pl.ds(pl.multiple_of(row0 + nxt, ALIGN), ALIGN), :]
        wait_round(g - 1)
        for e in range(ne):
            window_copy(slot, e, first[e]).start()
        nissued[0] = g + 1
        return 0

    lax.fori_loop(0, nrounds, round_body, 0)

    @pl.when(last_step)
    def _drain():
        wait_round(nissued[0] - 1)


def _gather(base, cnt, pos, aff, basev, cntv, h2, cap):
    bsz, t, d = h2.shape
    ne = pos.shape[1]
    c = TILE
    nt = t // c
    width = d + LANES
    grid_spec = pltpu.PrefetchScalarGridSpec(
        num_scalar_prefetch=2,
        grid=(bsz, nt),
        in_specs=[pl.BlockSpec((1, ne, c), lambda b, j, *_: (b, 0, j)),
                  pl.BlockSpec((1, ne, c), lambda b, j, *_: (b, 0, j)),
                  pl.BlockSpec((1, 1, ne, c), lambda b, j, *_: (b, j, 0, 0)),
                  pl.BlockSpec((1, 1, ne, c), lambda b, j, *_: (b, j, 0, 0)),
                  pl.BlockSpec((1, c, d), lambda b, j, *_: (b, j, 0))],
        out_specs=pl.BlockSpec(memory_space=pl.ANY),
        scratch_shapes=[pltpu.VMEM((2, ne * WROWS, width), BF16),
                        pltpu.VMEM((ne * ALIGN, width), BF16), pltpu.VMEM((WROWS, width), BF16),
                        pltpu.SemaphoreType.DMA((2, ne)), pltpu.SemaphoreType.DMA((ne,)),
                        pltpu.SMEM((1,), I32)],
    )
    return pl.pallas_call(
        functools.partial(_gather_kernel, cap=cap, ne=ne),
        out_shape=jax.ShapeDtypeStruct((bsz, ne, cap + WROWS, width), BF16),
        grid_spec=grid_spec,
        compiler_params=pltpu.CompilerParams(dimension_semantics=("arbitrary", "arbitrary"),
                                             vmem_limit_bytes=VMEM_LIMIT),
        name="moe_gather",
    )(base, cnt, pos, aff, basev, cntv, h2)


def _expert_kernel(xe_ref, wg_hbm, wu_hbm, wd_hbm, ye_ref,
                   wgua, wda, wgub, wdb, stg, stu, std, sem, *, d, ne, steps):
    e = pl.program_id(0)
    k = pl.program_id(1) * pl.num_programs(2) + pl.program_id(2)
    rows_in = d // steps
    rows_out = wd_hbm.shape[1] // steps

    def chunk_copies(slot, ee, kk):
        r_in = pl.ds(pl.multiple_of(kk * rows_in, ALIGN), rows_in)
        r_out = pl.ds(pl.multiple_of(kk * rows_out, ALIGN), rows_out)
        return [pltpu.make_async_copy(wg_hbm.at[ee, r_in], stg.at[slot], sem.at[slot, 0]),
                pltpu.make_async_copy(wu_hbm.at[ee, r_in], stu.at[slot], sem.at[slot, 1]),
                pltpu.make_async_copy(wd_hbm.at[ee, r_out], std.at[slot], sem.at[slot, 2])]

    def cast_chunk(slot, kk, dst):
        r_in = pl.ds(pl.multiple_of(kk * rows_in, ALIGN), rows_in)
        r_out = pl.ds(pl.multiple_of(kk * rows_out, ALIGN), rows_out)
        ff = stg.shape[2]
        dst[0][r_in, 0:ff] = stg[slot].astype(BF16)
        dst[0][r_in, ff:2 * ff] = stu[slot].astype(BF16)
        dst[1][r_out, :] = std[slot].astype(BF16)

    @pl.when((e == 0) & (k == 0))
    def _first_expert():
        for kk in range(steps):
            cps = chunk_copies(kk % 2, 0, kk)
            for cp in cps:
                cp.start()
            for cp in cps:
                cp.wait()
            cast_chunk(kk % 2, kk, (wgua, wda))
        if ne > 1:
            for cp in chunk_copies(0, 1, 0):
                cp.start()

    def step(cur, nxt):
        slot = k % 2
        last_chunk = k + 1 == steps

        @pl.when(jnp.where(last_chunk, e + 2 < ne, e + 1 < ne))
        def _start_next_chunk():
            for cp in chunk_copies(1 - slot, jnp.where(last_chunk, e + 2, e + 1), jnp.where(last_chunk, 0, k + 1)):
                cp.start()

        @pl.when(e + 1 < ne)
        def _next_weights():
            for cp in chunk_copies(slot, e + 1, k):
                cp.wait()
            cast_chunk(slot, k, nxt)

        xin = xe_ref[0, 0]
        xb = xin[:, :d]
        gate = xin[:, d:d + 1].astype(F32) + xin[:, d + LANES // 2:d + LANES // 2 + 1].astype(F32)
        ff = cur[1].shape[0]
        au = _dot(xb, cur[0][...])
        y = _dot((_silu(au[:, :ff]) * au[:, ff:]).astype(BF16), cur[1][...])
        ye_ref[0, 0] = (y * gate).astype(BF16)

    @pl.when(e % 2 == 0)
    def _even():
        step((wgua, wda), (wgub, wdb))

    @pl.when(e % 2 == 1)
    def _odd():
        step((wgub, wdb), (wgua, wda))


def _experts(xe, wg, wu, wd, cap):
    bsz, ne, _, width = xe.shape
    d = width - LANES
    ff = wg.shape[2]
    rows = min(EXPERT_ROWS, cap)
    steps = bsz * (cap // rows)
    assert steps % 2 == 0 and d % (steps * ALIGN) == 0 and ff % (steps * ALIGN) == 0
    hbm = pl.BlockSpec(memory_space=pl.ANY)
    return pl.pallas_call(
        functools.partial(_expert_kernel, d=d, ne=ne, steps=steps),
        out_shape=jax.ShapeDtypeStruct((bsz, ne, cap, d), BF16),
        grid=(ne, bsz, cap // rows),
        in_specs=[pl.BlockSpec((1, 1, rows, width), lambda e, b, r: (b, e, r, 0)), hbm, hbm, hbm],
        out_specs=pl.BlockSpec((1, 1, rows, d), lambda e, b, r: (b, e, r, 0)),
        scratch_shapes=[pltpu.VMEM((d, 2 * ff), BF16), pltpu.VMEM((ff, d), BF16),
                        pltpu.VMEM((d, 2 * ff), BF16), pltpu.VMEM((ff, d), BF16),
                        pltpu.VMEM((2, d // steps, ff), F32), pltpu.VMEM((2, d // steps, ff), F32),
                        pltpu.VMEM((2, ff // steps, d), F32), pltpu.SemaphoreType.DMA((2, 3))],
        compiler_params=pltpu.CompilerParams(dimension_semantics=("arbitrary", "arbitrary", "arbitrary"),
                                             vmem_limit_bytes=VMEM_LIMIT),
        name="moe_experts",
    )(xe, wg, wu, wd)


def _combine_kernel(base_ref, cnt_ref, pos_ref, basev_ref, cntv_ref, x1_ref, mod_ref, fnw_ref, ye_ref,
                    out_ref, stage, acc, sem, *, cap, ne):
    b = pl.program_id(0)
    j = pl.program_id(1)
    nt = pl.num_programs(1)
    step = b * nt + j
    pos = pos_ref[0]
    ones = jnp.ones(pos.shape, F32)
    basev = basev_ref[0, 0]
    cntv = cntv_ref[0, 0]
    last = cap - WROWS

    def fetch(slot, bb, jj, r):
        cps = []
        for e in range(ne):
            s, _n = _round_slots_scalar(base_ref[bb, jj, e], cnt_ref[bb, jj, e], r)
            row0 = pl.multiple_of(jnp.minimum(_align_down(s), last), ALIGN)
            cps.append(pltpu.make_async_copy(ye_ref.at[bb, e, pl.ds(row0, WROWS)],
                                             stage.at[slot, pl.ds(e * WROWS, WROWS)], sem.at[slot, e]))
        return cps

    def weights(r):
        start, num = _round_slots(basev, cntv, r)
        valid = (pos >= start) & (pos < start + num)
        return _window_select(pos - jnp.minimum(_align_down(start), last), valid, ones, ne).astype(BF16)

    def expand(w, slot):
        return _dg(w, stage[slot], _TN)

    @pl.when(step == 0)
    def _first():
        for cp in fetch(0, b, j, 0):
            cp.start()

    @pl.when(step + 1 < pl.num_programs(0) * nt)
    def _prefetch():
        wrap = j + 1 == nt
        for cp in fetch((step + 1) % 2, jnp.where(wrap, b + 1, b), jnp.where(wrap, 0, j + 1), 0):
            cp.start()

    w0 = weights(0)
    slot = step % 2
    for cp in fetch(slot, b, j, 0):
        cp.wait()
    acc[...] = expand(w0, slot)

    def round_body(r, _):
        cps = fetch(2, b, j, r)
        for cp in cps:
            cp.start()
        w = weights(r)
        for cp in cps:
            cp.wait()
        acc[...] += expand(w, 2)
        return 0

    nrounds = (_tile_counts(cnt_ref, b, j, ne) + (WIN - 1)) // WIN
    lax.fori_loop(1, nrounds, round_body, 0)
    mod = mod_ref[0]
    x2 = x1_ref[0] + mod[5:6] * acc[...]
    out_ref[0] = _rms(x2, fnw_ref[...])


def _combine(base, cnt, pos, basev, cntv, x1, modb, fnw, ye, cap):
    bsz, t, d = x1.shape
    ne = pos.shape[1]
    c = TILE
    nt = t // c
    grid_spec = pltpu.PrefetchScalarGridSpec(
        num_scalar_prefetch=2,
        grid=(bsz, nt),
        in_specs=[pl.BlockSpec((1, ne, c), lambda b, j, *_: (b, 0, j)),
                  pl.BlockSpec((1, 1, ne, c), lambda b, j, *_: (b, j, 0, 0)),
                  pl.BlockSpec((1, 1, ne, c), lambda b, j, *_: (b, j, 0, 0)),
                  pl.BlockSpec((1, c, d), lambda b, j, *_: (b, j, 0)),
                  pl.BlockSpec((1,) + modb.shape[1:], lambda b, j, *_: (b, 0, 0)),
                  pl.BlockSpec(fnw.shape, lambda b, j, *_: (0, 0)),
                  pl.BlockSpec(memory_space=pl.ANY)],
        out_specs=pl.BlockSpec((1, c, d), lambda b, j, *_: (b, j, 0)),
        scratch_shapes=[pltpu.VMEM((3, ne * WROWS, d), BF16), pltpu.VMEM((c, d), F32),
                        pltpu.SemaphoreType.DMA((3, ne))],
    )
    return pl.pallas_call(
        functools.partial(_combine_kernel, cap=cap, ne=ne),
        out_shape=jax.ShapeDtypeStruct((bsz, t, d), F32),
        grid_spec=grid_spec,
        compiler_params=pltpu.CompilerParams(dimension_semantics=("arbitrary", "arbitrary"),
                                             vmem_limit_bytes=VMEM_LIMIT),
        name="moe_combine",
    )(base, cnt, pos, basev, cntv, x1, modb, fnw, ye)


def _rope_tables(t):
    n_freq = RET_DK // 4
    inv = ROPE_BASE ** (-np.arange(n_freq, dtype=np.float64) / n_freq)
    zeros = lambda n: np.zeros((n, n_freq))

    def lanes(row_part, col_part):
        cos = np.concatenate([np.cos(row_part), np.cos(col_part)] * 2, axis=1)
        sin = np.concatenate([-np.sin(row_part), -np.sin(col_part), np.sin(row_part), np.sin(col_part)], axis=1)
        return cos, sin

    col = (np.arange(TILE) % GRID_W)[:, None] * inv
    cos_c, sin_c = lanes(zeros(TILE), col)
    cos_c[:, :n_freq] = 0.0
    cos_c[:, 2 * n_freq:3 * n_freq] = 0.0
    rows_per_tile = TILE // GRID_W
    row = np.arange(t // GRID_W)[:, None] * inv
    cos_r, sin_r = lanes(row, zeros(t // GRID_W))
    cos_r[:, n_freq:2 * n_freq] = 0.0
    cos_r[:, 3 * n_freq:] = 0.0
    row_tab = np.zeros((t // TILE, 2, 8, RET_DK))
    row_tab[:, 0, :rows_per_tile] = cos_r.reshape(t // TILE, rows_per_tile, RET_DK)
    row_tab[:, 1, :rows_per_tile] = sin_r.reshape(t // TILE, rows_per_tile, RET_DK)
    return jnp.asarray(np.stack([cos_c, sin_c]), F32), jnp.asarray(row_tab, F32)


def _mixer_weights(w_in, gate_w, gate_b):
    pts = np.cumsum(IN_WIDTHS)[:-1]
    gq, gk, gv, gz, gg, rq, rk, rv, rg = jnp.split(w_in, [int(p) for p in pts], axis=1)
    zw = 2 * GLA_RANK
    gz = jnp.concatenate([gz, gz, gz, jnp.zeros((gz.shape[0], GZ_PAD - 3 * zw), F32)], axis=1)
    wall = jnp.concatenate([gq, gk, gv, gg, rq, rk, rv, rg, gz], axis=1).astype(BF16)
    gmat = jnp.zeros((zw, 2 * GLA_QK), F32)
    gmat = gmat.at[:GLA_RANK, :GLA_QK].set(gate_w[0]).at[GLA_RANK:, GLA_QK:].set(gate_w[1])
    ghi = gmat.astype(BF16)
    glo = (gmat - ghi.astype(F32)).astype(BF16)
    gpk = jnp.concatenate([ghi, ghi, glo, jnp.zeros((GZ_PAD - 3 * zw, 2 * GLA_QK), BF16)], axis=0)
    return wall, gpk, gate_b.reshape(1, 2 * GLA_QK)


def kernel(x, c, ctx, c_ctx, w_ada, b_ada, norm1_w, w_in, gla_gate_w, gla_gate_b, ret_decay_logit, gla_norm_w,
           ret_norm_w, w_out, norm2_w, w_router, w_exp_gate, w_exp_up, w_exp_down, final_norm_w):
    bsz, t, d = x.shape
    depth = w_ada.shape[0]
    assert depth == 1 and t % TILE == 0 and ctx.shape[1] == TILE
    ne = w_router.shape[2]
    cap = EC_CAPACITY_FACTOR * t // ne
    assert cap >= WROWS and cap % ALIGN == 0 and cap % min(EXPERT_ROWS, cap) == 0
    nt = t // TILE
    nb = t // LANES
    bpt = TILE // LANES

    cs = jnp.concatenate([c, c_ctx[None, :], jnp.zeros((8 - bsz - 1, d), F32)], axis=0)
    mod = _ada(cs, w_ada[0], b_ada[0][None, :])
    mod = jnp.pad(mod.reshape(8, N_ADA, d), ((0, 0), (0, 8 - N_ADA), (0, 0)))
    modb = mod[:bsz]
    modc = mod[bsz:bsz + 1]

    wall, gpk, gb = _mixer_weights(w_in[0], gla_gate_w[0], gla_gate_b[0])
    n1w = norm1_w[0][None, :]
    rlog = jnp.broadcast_to(ret_decay_logit[0][:, :, None], (2, RET_HEADS, TILE)).astype(F32)
    cum_f = jnp.asarray(_chunk_cumsum_matrix(TILE, False), BF16)
    cum_b = jnp.asarray(_chunk_cumsum_matrix(TILE, True), BF16)
    ind = jnp.asarray(_chunk_indicator(TILE), BF16)
    lvl_f = jnp.asarray(_level_index(False))
    lvl_b = jnp.asarray(_level_index(True))
    bdm = jnp.asarray(_head_block_mask(), BF16)
    rope_col, rope_row = _rope_tables(t)

    sgf, sgb, srf, srb = _ctx_states(ctx, modc, n1w, wall, gpk, gb, rlog, cum_f, cum_b, ind, bdm)
    o_f, gqkv, gates, rqkv, lab = _fwd(x, modb, n1w, wall, gpk, gb, rope_col, rope_row, rlog, cum_f, ind, lvl_f, bdm,
                                       sgf, srf)

    wr = w_router[0].T
    wrh = wr.astype(BF16)
    wrl = (wr - wrh.astype(F32)).astype(BF16)
    x1, h2, aff = _bwd(x, o_f, gqkv, gates, rqkv, lab, modb, rlog, cum_b, ind, lvl_b, bdm, sgb, srb,
                       gla_norm_w[0][None, :], ret_norm_w[0][None, :], w_out[0].astype(BF16),
                       norm2_w[0][None, :], wrh, wrl)

    pos4, off4 = _route(aff.reshape(bsz, ne, nb, LANES), cap)
    pos = pos4.reshape(bsz, ne, t)
    boff = off4[:, :, :, 0]
    base = jnp.transpose(boff[:, :, ::bpt], (0, 2, 1))
    nxt = jnp.concatenate([base[:, 1:], jnp.full((bsz, 1, ne), cap, I32)], axis=1)
    cnt = nxt - base
    basev = jnp.broadcast_to(base[:, :, :, None], (bsz, nt, ne, TILE))
    cntv = jnp.broadcast_to(cnt[:, :, :, None], (bsz, nt, ne, TILE))

    xe = _gather(base, cnt, pos, aff, basev, cntv, h2, cap)
    ye = _experts(xe, w_exp_gate[0], w_exp_up[0], w_exp_down[0], cap)
    return _combine(base, cnt, pos, basev, cntv, x1, modb, final_norm_w[None, :], ye, cap)
```

```python
import functools

import numpy as np
import jax
import jax.numpy as jnp
from jax import lax
from jax.experimental import pallas as pl
from jax.experimental.pallas import tpu as pltpu

F32 = jnp.float32
BF16 = jnp.bfloat16
I32 = jnp.int32

GLA_HEADS = 4
GLA_DK = 64
GLA_DV = 128
GLA_RANK = 16
GLA_TAU = 16.0
RET_HEADS = 4
RET_DK = 128
RET_DV = 128
GRID_W = 64
ROPE_BASE = 10000.0
EC_CAPACITY_FACTOR = 2
N_ADA = 6
EPS = 1e-6

GLA_QK = GLA_HEADS * GLA_DK
GLA_V = GLA_HEADS * GLA_DV
RET_QK = RET_HEADS * RET_DK
RET_V = RET_HEADS * RET_DV
IN_WIDTHS = (GLA_QK, GLA_QK, GLA_V, 2 * GLA_RANK, GLA_V, RET_QK, RET_QK, RET_V, RET_V)

LANES = 128
MXU_N = 256
TILE = 256
GLA_CHUNK = 64
GLA_LEVELS = 6
WIN = 48
ALIGN = 16
WROWS = WIN + ALIGN
GZ_PAD = LANES
EXPERT_ROWS = 512
LN2 = float(np.log(2.0))
MIN_EXP = -149
EXP_STEPS = 8
MANTISSA_STEPS = 56
VMEM_LIMIT = 56 * 1024 * 1024

_NT = (((1,), (1,)), ((), ()))
_TN = (((0,), (0,)), ((), ()))


def _dot(a, b):
    return jnp.dot(a, b, preferred_element_type=F32)


def _dg(a, b, dims):
    return lax.dot_general(a, b, dims, preferred_element_type=F32)


def _split(a):
    hi = a.astype(BF16)
    lo = (a - hi.astype(F32)).astype(BF16)
    return hi, lo


def _logsig(x):
    return jnp.minimum(x, 0.0) - jnp.log(1.0 + jnp.exp(-jnp.abs(x)))


def _silu(x):
    return x / (1.0 + jnp.exp(-x))


def _rms(x, w):
    return x * lax.rsqrt(jnp.mean(x * x, axis=-1, keepdims=True) + EPS) * w


def _chunk_cumsum_matrix(c, reverse):
    i = np.arange(c)[:, None]
    t = np.arange(c)[None, :]
    same = (i // GLA_CHUNK) == (t // GLA_CHUNK)
    return (same & ((t >= i) if reverse else (t <= i))).astype(np.float32)


def _chunk_indicator(c):
    return (np.arange(c)[:, None] // GLA_CHUNK == np.arange(LANES)[None, :]).astype(np.float32)


def _level_index(reverse):
    i = np.arange(GLA_CHUNK)[:, None]
    j = np.arange(GLA_CHUNK)[None, :]
    x = i ^ j
    lvl = np.where(x > 0, np.floor(np.log2(np.maximum(x, 1))), -1).astype(np.int32)
    bad = (j < i) if reverse else (j > i)
    return np.tile(np.where(bad, 99, lvl).astype(np.int32), (1, GLA_HEADS))


def _head_block_mask():
    r = np.arange(GLA_QK)[:, None] // GLA_DK
    l = np.arange(GLA_V)[None, :] // GLA_DV
    return (r == l).astype(np.float32)


def _log_gates(gz, gpk_ref, gb_ref):
    z_hi = gz.astype(BF16).astype(F32)
    group = lax.broadcasted_iota(I32, gz.shape, 1) >> ((2 * GLA_RANK).bit_length() - 1)
    packed = jnp.where(group == 1, gz - z_hi, z_hi).astype(BF16)
    return _logsig(_dot(packed, gpk_ref[...]) + gb_ref[...]) * (1.0 / GLA_TAU)


def _project(xn, wall_ref, gpk_ref, gb_ref):
    proj = _dot(xn.astype(BF16), wall_ref[...])
    o = 0
    out = []
    for w in (GLA_QK, GLA_QK, GLA_V, GLA_V, RET_QK, RET_QK, RET_V, RET_V, GZ_PAD):
        out.append(proj[:, o:o + w])
        o += w
    gq, gk, gv, gg, rq, rk, rv, rg, gz = out
    log_a = _log_gates(gz, gpk_ref, gb_ref)
    return gq * (GLA_DK ** -0.5), gk, gv, gg, rq, rk * (RET_DK ** -0.5), rv, rg, log_a


def _rope(a, cos, sin):
    outs = []
    for h in range(a.shape[1] // RET_DK):
        ah = a[:, h * RET_DK:(h + 1) * RET_DK]
        outs.append(ah * cos + pltpu.roll(ah, RET_DK // 2, 1) * sin)
    return jnp.concatenate(outs, axis=1)


def _stack_heads(a):
    head = lax.broadcasted_iota(I32, a.shape, 1) >> 6
    zero = jnp.zeros_like(a)
    return jnp.concatenate([jnp.where(head == h, a, zero) for h in range(GLA_HEADS)], axis=0)


def _gate_sums(g):
    hi, lo = _split(g)
    return jnp.concatenate([hi, lo], axis=1)


def _level_log_decay(level, g, b, b_ref, row0, reverse, row):
    n = GLA_CHUNK
    upper = ((row >> level) & 1) == 1
    if level == 0:
        return jnp.where(upper, 0.0, g) if reverse else jnp.where(upper, g, 0.0)
    if level == 1:
        nxt = pltpu.roll(g, n - 1, 0)
        prv = pltpu.roll(g, 1, 0)
        r = row & 3
        if reverse:
            return jnp.where(r == 0, g + nxt, jnp.where(r == 1, g, jnp.where(r == 2, 0.0, prv)))
        return jnp.where(r == 0, nxt, jnp.where(r == 1, 0.0, jnp.where(r == 2, g, g + prv)))
    m = 1 << level
    anchors = [jnp.broadcast_to(b_ref[pl.ds(row0 + blk + (m if reverse else m - 1), 1), :], (2 * m, GLA_QK))
               for blk in range(0, n, 2 * m)]
    d = b - (jnp.concatenate(anchors, axis=0) if len(anchors) > 1 else anchors[0])
    return jnp.where(upper, -d, d) if reverse else jnp.where(upper, d, -d)


def _interleave(*stages):
    order = sorted((span * (k + 0.5) / n, i) for i, (_, n, span) in enumerate(stages) for k in range(n))
    for _, i in order:
        next(stages[i][0])
    for gen, _, _ in stages:
        for _ in gen:
            raise AssertionError("stage has more pieces than declared")


def _gla_steps(qkv_ref, g_ref, cum_ref, ind_ref, lvl_ref, bdm_ref, s_ref, b_ref, reverse, emit):
    c = g_ref.shape[0]
    g2 = _gate_sums(g_ref[...])
    r = _dot(cum_ref[...], g2)
    b_ref[...] = r[:, :GLA_QK] + r[:, GLA_QK:]
    cs = _dg(g2, ind_ref[...], _TN)
    tot = cs[:GLA_QK] + cs[GLA_QK:]
    yield
    nchunk = c // GLA_CHUNK
    row = lax.broadcasted_iota(I32, (GLA_CHUNK, GLA_QK), 0)
    lvl = lvl_ref[...] if emit is not None else None
    for ci in (reversed(range(nchunk)) if reverse else range(nchunk)):
        row0 = ci * GLA_CHUNK
        rows = pl.ds(row0, GLA_CHUNK)
        kc = qkv_ref[rows, GLA_QK:2 * GLA_QK].astype(F32)
        vc = qkv_ref[rows, 2 * GLA_QK:]
        gc = g_ref[rows, :]
        bc = b_ref[rows, :]
        bdm = bdm_ref[...]
        s = s_ref[...]
        if emit is not None:
            qc = qkv_ref[rows, :GLA_QK].astype(F32)
            scores = jnp.zeros((GLA_CHUNK, GLA_HEADS * GLA_CHUNK), F32)
            for level in range(GLA_LEVELS):
                e = jnp.exp(_level_log_decay(level, gc, bc, b_ref, row0, reverse, row))
                p = _dg((qc * e).astype(BF16), _stack_heads((kc * e).astype(BF16)), _NT)
                scores = jnp.where(lvl == level, p, scores)
            p = _dg(qc.astype(BF16), _stack_heads(kc.astype(BF16)), _NT)
            scores = jnp.where(lvl == -1, p, scores)
            yield
            v_bd = jnp.concatenate([vc] * GLA_HEADS, axis=0) * bdm
            s_bd = jnp.concatenate([s.astype(BF16)] * GLA_HEADS, axis=1) * bdm
            emit(row0, _dot(scores.astype(BF16), v_bd) + _dot((qc * jnp.exp(bc)).astype(BF16), s_bd))
        b_end = b_ref[pl.ds(row0 if reverse else row0 + GLA_CHUNK - 1, 1), :]
        kv = _dg((kc * jnp.exp(b_end - bc)).astype(BF16), vc, _TN)
        own = jnp.concatenate([kv[h * GLA_DK:(h + 1) * GLA_DK, h * GLA_DV:(h + 1) * GLA_DV]
                               for h in range(GLA_HEADS)], axis=0)
        s_ref[...] = jnp.exp(jnp.broadcast_to(tot[:, ci:ci + 1], (GLA_QK, GLA_DV))) * s + own
        yield


def _ret_decays(rlog_ref, c, reverse):
    lg = _logsig(rlog_ref[0])
    ii = lax.broadcasted_iota(I32, (c, c), 0)
    jj = lax.broadcasted_iota(I32, (c, c), 1)
    rel = ((jj - ii) if reverse else (ii - jj)).astype(F32)
    pos = lax.broadcasted_iota(I32, (c, RET_DK), 0).astype(F32)
    dmats, qd, kd, cd = [], [], [], []
    for h in range(RET_HEADS):
        lh = lg[h:h + 1, :]
        dmats.append(jnp.where(rel >= 0, jnp.exp(lh * jnp.maximum(rel, 0.0)), 0.0))
        l1 = lh[:, :RET_DK]
        qd.append(jnp.exp(l1 * ((c - pos) if reverse else (pos + 1.0))))
        kd.append(jnp.exp(l1 * (pos if reverse else (c - 1.0 - pos))))
        cd.append(jnp.exp(l1 * float(c)))
    return dmats, jnp.concatenate(qd, axis=1), jnp.concatenate(kd, axis=1), jnp.concatenate(cd, axis=1)


def _ret_steps(qkv_ref, dmat_ref, qdec_ref, kdec_ref, cdec_ref, s_ref, emit):
    for h in range(RET_HEADS):
        sl = slice(h * RET_DK, (h + 1) * RET_DK)
        qb = qkv_ref[:, h * RET_DK:(h + 1) * RET_DK]
        kb = qkv_ref[:, RET_QK + h * RET_DK:RET_QK + (h + 1) * RET_DK]
        vh = qkv_ref[:, 2 * RET_QK + h * RET_DV:2 * RET_QK + (h + 1) * RET_DV]
        sc = _dg(qb, kb, _NT) * dmat_ref[h]
        yield
        s = s_ref[h]
        emit(h, _dot(sc.astype(BF16), vh) + _dot((qb.astype(F32) * qdec_ref[:, sl]).astype(BF16), s.astype(BF16)))
        s_ref[h] = cdec_ref[:, sl] * s + _dg((kb.astype(F32) * kdec_ref[:, sl]).astype(BF16), vh, _TN)
        yield


def _proj_steps(x_ref, mod_ref, n1w_ref, wall_ref, gpk_ref, gb_ref, rope_col_ref, rope_row_ref,
                gqkv_ref, gates_ref, rqkv_ref, lab_ref, nxt_g, nxt_r, nxt_l):
    bsz, c, _ = x_ref.shape
    normed = []
    for b in range(bsz):
        mod = mod_ref[b]
        normed.append((_rms(x_ref[b], n1w_ref[...]) * (1.0 + mod[1:2]) + mod[0:1]).astype(BF16))
    hb = jnp.concatenate(normed, axis=0)
    yield

    def cols(o, w):
        return _dot(hb, wall_ref[:, o:o + w])

    def put(val, o, out_ref, stage_ref, post=None):
        for b in range(bsz):
            part = val[b * c:(b + 1) * c]
            part = (part if post is None else post(part)).astype(BF16)
            out_ref[b, :, o:o + part.shape[1]] = part
            if stage_ref is not None:
                stage_ref[b, :, o:o + part.shape[1]] = part

    def group(o, w, dst, out_ref, stage_ref, scale=None, post=None):
        for k in range(0, w, MXU_N):
            val = cols(o + k, MXU_N)
            put(val if scale is None else val * scale, dst + k, out_ref, stage_ref, post)
            yield

    yield from group(0, GLA_QK, 0, gqkv_ref, nxt_g, scale=GLA_DK ** -0.5)
    yield from group(GLA_QK, GLA_QK, GLA_QK, gqkv_ref, nxt_g)
    yield from group(2 * GLA_QK, GLA_V, 2 * GLA_QK, gqkv_ref, nxt_g)
    o = 2 * GLA_QK + GLA_V
    yield from group(o, GLA_V, 0, gates_ref, None)
    o += GLA_V
    rows_of = lambda i: jnp.concatenate(
        [jnp.broadcast_to(rope_row_ref[0, i, q:q + 1, :], (GRID_W, RET_DK)) for q in range(TILE // GRID_W)], axis=0)
    cos = rows_of(0) + rope_col_ref[0]
    sin = rows_of(1) + rope_col_ref[1]
    rope = lambda a: _rope(a, cos, sin)
    yield from group(o, RET_QK, 0, rqkv_ref, nxt_r, post=rope)
    o += RET_QK
    yield from group(o, RET_QK, RET_QK, rqkv_ref, nxt_r, scale=RET_DK ** -0.5, post=rope)
    o += RET_QK
    yield from group(o, RET_V, 2 * RET_QK, rqkv_ref, nxt_r)
    o += RET_V
    yield from group(o, RET_V, GLA_V, gates_ref, None)
    o += RET_V
    log_a = _log_gates(cols(o, GZ_PAD), gpk_ref, gb_ref)
    for b in range(bsz):
        nxt_l[b] = log_a[b * c:(b + 1) * c, :GLA_QK]
        lab_ref[b] = log_a[b * c:(b + 1) * c, GLA_QK:]
    yield


def _ada_kernel(c_ref, w_ref, b_ref, o_ref):
    s_hi, s_lo = _split(_silu(c_ref[...]))
    w_hi, w_lo = _split(w_ref[...])
    o_ref[...] = _dot(s_hi, w_hi) + _dot(s_lo, w_hi) + _dot(s_hi, w_lo) + b_ref[...]


def _ada(cs, w, b):
    rows, d = cs.shape
    n = w.shape[1]
    tn = 1536
    return pl.pallas_call(
        _ada_kernel,
        out_shape=jax.ShapeDtypeStruct((rows, n), F32),
        grid=(n // tn,),
        in_specs=[pl.BlockSpec((rows, d), lambda i: (0, 0)),
                  pl.BlockSpec((d, tn), lambda i: (0, i)),
                  pl.BlockSpec((1, tn), lambda i: (0, i))],
        out_specs=pl.BlockSpec((rows, tn), lambda i: (0, i)),
        compiler_params=pltpu.CompilerParams(dimension_semantics=("arbitrary",), vmem_limit_bytes=VMEM_LIMIT),
        name="ada",
    )(cs, w, b)


def _ctx_kernel(ctx_ref, mod_ref, n1w_ref, wall_ref, gpk_ref, gb_ref, rlog_ref, cumf_ref, cumb_ref,
                ind_ref, bdm_ref, sgf_ref, sgb_ref, srf_ref, srb_ref, b_scr, kv_scr, g_scr, *, c):
    mod = mod_ref[0]
    hc = _rms(ctx_ref[0], n1w_ref[...]) * (1.0 + mod[1:2]) + mod[0:1]
    _, gk, gv, _, _, rk, rv, _, log_a = _project(hc, wall_ref, gpk_ref, gb_ref)
    kv_scr[:, GLA_QK:] = jnp.concatenate([gk, gv], axis=1).astype(BF16)
    rvb = rv.astype(BF16)
    for d, (cum_ref, out_g, out_r) in enumerate(((cumf_ref, sgf_ref, srf_ref), (cumb_ref, sgb_ref, srb_ref))):
        out_g[0] = jnp.zeros((GLA_QK, GLA_DV), F32)
        g_scr[...] = log_a[:, d * GLA_QK:(d + 1) * GLA_QK]
        _interleave((_gla_steps(kv_scr, g_scr, cum_ref, ind_ref, None, bdm_ref, out_g.at[0], b_scr, bool(d), None),
                     1 + c // GLA_CHUNK, 1.0))
        _, _, kdec, _ = _ret_decays(rlog_ref.at[d:d + 1], c, reverse=bool(d))
        for h in range(RET_HEADS):
            sl = slice(h * RET_DK, (h + 1) * RET_DK)
            out_r[0, h] = _dg((rk[:, sl] * kdec[:, sl]).astype(BF16), rvb[:, h * RET_DV:(h + 1) * RET_DV], _TN)


def _ctx_states(ctx, modc, n1w, wall, gpk, gb, rlog, cum_f, cum_b, ind, bdm):
    bsz, c, d = ctx.shape
    const = lambda a: pl.BlockSpec(a.shape, lambda b: (0,) * a.ndim)
    consts = (modc, n1w, wall, gpk, gb, rlog, cum_f, cum_b, ind, bdm)
    return pl.pallas_call(
        functools.partial(_ctx_kernel, c=c),
        out_shape=(jax.ShapeDtypeStruct((bsz, GLA_QK, GLA_DV), F32),
                   jax.ShapeDtypeStruct((bsz, GLA_QK, GLA_DV), F32),
                   jax.ShapeDtypeStruct((bsz, RET_HEADS, RET_DK, RET_DV), F32),
                   jax.ShapeDtypeStruct((bsz, RET_HEADS, RET_DK, RET_DV), F32)),
        grid=(bsz,),
        in_specs=[pl.BlockSpec((1, c, d), lambda b: (b, 0, 0))] + [const(a) for a in consts],
        out_specs=(pl.BlockSpec((1, GLA_QK, GLA_DV),lambda b: (b, 0, 0)),
                   pl.BlockSpec((1, GLA_QK, GLA_DV),lambda b: (b, 0, 0)),
                   pl.BlockSpec((1, RET_HEADS, RET_DK, RET_DV), lambda b: (b, 0, 0, 0)),
                   pl.BlockSpec((1, RET_HEADS, RET_DK, RET_DV), lambda b: (b, 0, 0, 0))),
        scratch_shapes=[pltpu.VMEM((c, GLA_QK), F32), pltpu.VMEM((c, 2 * GLA_QK + GLA_V), BF16),
                        pltpu.VMEM((c, GLA_QK), F32)],
        compiler_params=pltpu.CompilerParams(dimension_semantics=("arbitrary",), vmem_limit_bytes=VMEM_LIMIT),
        name="ctx_states",
    )(ctx, *consts)


def _fwd_kernel(x_ref, mod_ref, n1w_ref, wall_ref, gpk_ref, gb_ref, rope_col_ref, rope_row_ref, rlog_ref,
                cum_ref, ind_ref, lvl_ref, bdm_ref, sg0_ref, sr0_ref,
                of_ref, gqkv_ref, gates_ref, rqkv_ref, lab_ref,
                sg_scr, sr_scr, dmat_scr, qdec_scr, kdec_scr, cdec_scr, b_scr,
                cur_g, cur_r, cur_l, nxt_g, nxt_r, nxt_l, *, c):
    j = pl.program_id(0)
    bsz = x_ref.shape[0]

    @pl.when(j == 0)
    def _first():
        sg_scr[...] = jnp.zeros(sg_scr.shape, F32)
        sr_scr[...] = jnp.zeros(sr_scr.shape, F32)
        nxt_g[...] = jnp.zeros(nxt_g.shape, BF16)
        nxt_r[...] = jnp.zeros(nxt_r.shape, BF16)
        nxt_l[...] = jnp.zeros(nxt_l.shape, F32)
        dmats, qd, kd, cd = _ret_decays(rlog_ref, c, reverse=False)
        for h in range(RET_HEADS):
            dmat_scr[h] = dmats[h]
        qdec_scr[...] = qd
        kdec_scr[...] = kd
        cdec_scr[...] = cd

    @pl.when(j == 1)
    def _seed():
        sg_scr[...] = sg0_ref[...]
        sr_scr[...] = sr0_ref[...]

    cur_g[...] = nxt_g[...]
    cur_r[...] = nxt_r[...]
    cur_l[...] = nxt_l[...]

    def emitters(b):
        def emit_gla(row0, out):
            of_ref[b, pl.ds(row0, GLA_CHUNK), 0:GLA_V] = out

        def emit_ret(h, out):
            of_ref[b, :, GLA_V + h * RET_DV:GLA_V + (h + 1) * RET_DV] = out
        return emit_gla, emit_ret

    stages = []
    for b in range(bsz):
        emit_gla, emit_ret = emitters(b)
        stages.append((_gla_steps(cur_g.at[b], cur_l.at[b], cum_ref, ind_ref, lvl_ref, bdm_ref, sg_scr.at[b],
                                  b_scr.at[b], False, emit_gla), 1 + 2 * (c // GLA_CHUNK), 1.0))
        stages.append((_ret_steps(cur_r.at[b], dmat_scr, qdec_scr, kdec_scr, cdec_scr, sr_scr.at[b], emit_ret),
                       2 * RET_HEADS, 1.0))
    proj = _proj_steps(x_ref, mod_ref, n1w_ref, wall_ref, gpk_ref, gb_ref, rope_col_ref, rope_row_ref,
                       gqkv_ref, gates_ref, rqkv_ref, lab_ref, nxt_g, nxt_r, nxt_l)
    nproj = 2 + (2 * GLA_QK + 2 * GLA_V + 2 * RET_QK + 2 * RET_V) // MXU_N
    _interleave(*stages, (proj, nproj, 1.0))


def _fwd(x, modb, n1w, wall, gpk, gb, rope_col, rope_row, rlog, cum_f, ind, lvl_f, bdm, sgf, srf):
    bsz, t, d = x.shape
    c = TILE
    nt = t // c
    const = lambda shape: pl.BlockSpec(shape, lambda j: (0,) * len(shape))
    proj_tile = lambda w: pl.BlockSpec((bsz, c, w), lambda j: (0, jnp.minimum(j, nt - 1), 0))
    scan_tile = lambda w: pl.BlockSpec((bsz, c, w), lambda j: (0, jnp.maximum(j - 1, 0), 0))
    rope_tile = pl.BlockSpec((1,) + rope_row.shape[1:], lambda j: (jnp.minimum(j, nt - 1), 0, 0, 0))
    mixw = GLA_V + RET_V
    gw, rw = 2 * GLA_QK + GLA_V, 2 * RET_QK + RET_V
    staging = [pltpu.VMEM((bsz, c, gw), BF16), pltpu.VMEM((bsz, c, rw), BF16), pltpu.VMEM((bsz, c, GLA_QK), F32)]
    return pl.pallas_call(
        functools.partial(_fwd_kernel, c=c),
        out_shape=(jax.ShapeDtypeStruct((bsz, t, mixw), F32),
                   jax.ShapeDtypeStruct((bsz, t, gw), BF16),
                   jax.ShapeDtypeStruct((bsz, t, GLA_V + RET_V), BF16),
                   jax.ShapeDtypeStruct((bsz, t, rw), BF16),
                   jax.ShapeDtypeStruct((bsz, t, GLA_QK), F32)),
        grid=(nt + 1,),
        in_specs=[proj_tile(d),
                  const(modb.shape),
                  const(n1w.shape), const(wall.shape), const(gpk.shape), const(gb.shape),
                  const(rope_col.shape), rope_tile,
                  pl.BlockSpec((1,) + rlog.shape[1:], lambda j: (0, 0, 0)),
                  const(cum_f.shape), const(ind.shape), const(lvl_f.shape), const(bdm.shape),
                  const(sgf.shape), const(srf.shape)],
        out_specs=(scan_tile(mixw), proj_tile(gw), proj_tile(GLA_V + RET_V), proj_tile(rw), proj_tile(GLA_QK)),
        scratch_shapes=[pltpu.VMEM((bsz, GLA_QK, GLA_DV), F32),
                        pltpu.VMEM((bsz, RET_HEADS, RET_DK, RET_DV), F32),
                        pltpu.VMEM((RET_HEADS, c, c), F32),
                        pltpu.VMEM((c, RET_QK), F32),
                        pltpu.VMEM((c, RET_QK), F32),
                        pltpu.VMEM((1, RET_QK), F32),
                        pltpu.VMEM((bsz, c, GLA_QK), F32)] + staging + staging,
        compiler_params=pltpu.CompilerParams(dimension_semantics=("arbitrary",),
                                             vmem_limit_bytes=VMEM_LIMIT),
        name="mixer_fwd",
    )(x, modb, n1w, wall, gpk, gb, rope_col, rope_row, rlog, cum_f, ind, lvl_f, bdm, sgf, srf)


def _bwd_kernel(x_ref, of_ref, gqkv_ref, gates_ref, rqkv_ref, lab_ref, mod_ref, rlog_ref,
                cum_ref, ind_ref, lvl_ref, bdm_ref,
                sg0_ref, sr0_ref, gnw_ref, rnw_ref, wout_ref, n2w_ref, wrh_ref, wrl_ref,
                x1_ref, h2_ref, aff_ref,
                sg_scr, sr_scr, dmat_scr, qdec_scr, kdec_scr, cdec_scr, b_scr, cur_m, nxt_m, mixb, *, c):
    j = pl.program_id(0)
    bsz = x_ref.shape[0]

    @pl.when(j == 0)
    def _first():
        sg_scr[...] = sg0_ref[...]
        sr_scr[...] = sr0_ref[...]
        nxt_m[...] = jnp.zeros(nxt_m.shape, F32)
        dmats, qd, kd, cd = _ret_decays(rlog_ref, c, reverse=True)
        for h in range(RET_HEADS):
            dmat_scr[h] = dmats[h]
        qdec_scr[...] = qd
        kdec_scr[...] = kd
        cdec_scr[...] = cd

    cur_m[...] = nxt_m[...]

    def emitters(b):
        def emit_gla(row0, out):
            nxt_m[pl.ds(b * c + row0, GLA_CHUNK), 0:GLA_V] = of_ref[b, pl.ds(row0, GLA_CHUNK), 0:GLA_V] + out

        def emit_ret(h, out):
            cols = slice(GLA_V + h * RET_DV, GLA_V + (h + 1) * RET_DV)
            nxt_m[pl.ds(b * c, c), cols] = of_ref[b, :, cols] + out
        return emit_gla, emit_ret

    stages = []
    for b in range(bsz):
        emit_gla, emit_ret = emitters(b)
        stages.append((_gla_steps(gqkv_ref.at[b], lab_ref.at[b], cum_ref, ind_ref, lvl_ref, bdm_ref, sg_scr.at[b],
                                  b_scr.at[b], True, emit_gla), 1 + 2 * (c // GLA_CHUNK), 1.0))
        stages.append((_ret_steps(rqkv_ref.at[b], dmat_scr, qdec_scr, kdec_scr, cdec_scr, sr_scr.at[b], emit_ret),
                       2 * RET_HEADS, 0.85))

    def epilogue():
        for h in range(GLA_HEADS + RET_HEADS):
            sl = slice(h * GLA_DV, (h + 1) * GLA_DV)
            oh = cur_m[:, sl]
            if h < GLA_HEADS:
                y = oh * lax.rsqrt(jnp.mean(oh * oh, axis=-1, keepdims=True) + EPS) * gnw_ref[:, sl]
            else:
                dv = oh - jnp.mean(oh, axis=-1, keepdims=True)
                y = (dv * lax.rsqrt(jnp.mean(dv * dv, axis=-1, keepdims=True) + EPS)
                     * rnw_ref[:, h * RET_DV - GLA_V:(h + 1) * RET_DV - GLA_V])
            gate = jnp.concatenate([gates_ref[b, :, sl] for b in range(bsz)], axis=0).astype(F32)
            mixb[:, sl] = (y * _silu(gate)).astype(BF16)
            yield
        d = x_ref.shape[2]
        step = d // 4
        for p in range(4):
            cs = slice(p * step, (p + 1) * step)
            out = _dot(mixb[...], wout_ref[:, cs])
            for b in range(bsz):
                x1_ref[b, :, cs] = x_ref[b, :, cs] + mod_ref[b, 2:3, cs] * out[b * c:(b + 1) * c]
            yield
        his, los = [], []
        for b in range(bsz):
            h2 = _rms(x1_ref[b], n2w_ref[...]) * (1.0 + mod_ref[b, 4:5, :]) + mod_ref[b, 3:4, :]
            h_hi, h_lo = _split(h2)
            h2_ref[b] = h_hi
            his.append(h_hi)
            los.append(h_lo)
        yield
        h_hi = jnp.concatenate(his, axis=0)
        h_lo = jnp.concatenate(los, axis=0)
        wrh = wrh_ref[...]
        logit = _dg(wrh, h_hi, _NT) + _dg(wrh, h_lo, _NT) + _dg(wrl_ref[...], h_hi, _NT)
        ex = jnp.exp(logit - jnp.max(logit, axis=0, keepdims=True))
        aff = ex / jnp.sum(ex, axis=0, keepdims=True)
        for b in range(bsz):
            aff_ref[b] = aff[:, b * c:(b + 1) * c]
        yield

    _interleave(*stages, (epilogue(), GLA_HEADS + RET_HEADS + 6, 0.7))


def _bwd(x, o_f, gqkv, gates, rqkv, lab, modb, rlog, cum_b, ind, lvl_b, bdm, sgb, srb, gnw, rnw, wout, n2w,
         wrh, wrl):
    bsz, t, d = x.shape
    c = TILE
    nt = t // c
    ne = wrh.shape[0]
    const = lambda shape: pl.BlockSpec(shape, lambda j: (0,) * len(shape))
    scan_at = lambda j: nt - 1 - jnp.minimum(j, nt - 1)
    mix_at = lambda j: nt - 1 - jnp.maximum(j - 1, 0)
    scan_tile = lambda w: pl.BlockSpec((bsz, c, w), lambda j: (0, scan_at(j), 0))
    tile = lambda w: pl.BlockSpec((bsz, c, w), lambda j: (0, mix_at(j), 0))
    mixw = GLA_V + RET_V
    return pl.pallas_call(
        functools.partial(_bwd_kernel, c=c),
        out_shape=(jax.ShapeDtypeStruct((bsz, t, d), F32),
                   jax.ShapeDtypeStruct((bsz, t, d), BF16),
                   jax.ShapeDtypeStruct((bsz, ne, t), F32)),
        grid=(nt + 1,),
        in_specs=[tile(d), scan_tile(o_f.shape[2]), scan_tile(gqkv.shape[2]), tile(gates.shape[2]),
                  scan_tile(rqkv.shape[2]), scan_tile(lab.shape[2]),
                  const(modb.shape),
                  pl.BlockSpec((1,) + rlog.shape[1:], lambda j: (1, 0, 0)),
                  const(cum_b.shape), const(ind.shape), const(lvl_b.shape), const(bdm.shape),
                  const(sgb.shape), const(srb.shape),
                  const(gnw.shape), const(rnw.shape), const(wout.shape), const(n2w.shape),
                  const(wrh.shape), const(wrl.shape)],
        out_specs=(tile(d), tile(d), pl.BlockSpec((bsz, ne, c), lambda j: (0, 0, mix_at(j)))),
        scratch_shapes=[pltpu.VMEM((bsz, GLA_QK, GLA_DV), F32),
                        pltpu.VMEM((bsz, RET_HEADS, RET_DK, RET_DV), F32),
                        pltpu.VMEM((RET_HEADS, c, c), F32),
                        pltpu.VMEM((c, RET_QK), F32),
                        pltpu.VMEM((c, RET_QK), F32),
                        pltpu.VMEM((1, RET_QK), F32),
                        pltpu.VMEM((bsz, c, GLA_QK), F32),
                        pltpu.VMEM((bsz * c, mixw), F32),
                        pltpu.VMEM((bsz * c, mixw), F32),
                        pltpu.VMEM((bsz * c, mixw), BF16)],
        compiler_params=pltpu.CompilerParams(dimension_semantics=("arbitrary",),
                                             vmem_limit_bytes=VMEM_LIMIT),
        name="mixer_bwd",
    )(x, o_f, gqkv, gates, rqkv, lab, modb, rlog, cum_b, ind, lvl_b, bdm, sgb, srb, gnw, rnw, wout, n2w, wrh, wrl)


def _route_kernel(aff_ref, pos_ref, off_ref, *, cap, nb):
    a = aff_ref[0]
    ne = a.shape[0]
    kf = float(cap)

    def count(mask):
        return jnp.sum(jnp.sum(jnp.where(mask, 1.0, 0.0), axis=1, keepdims=True), axis=2, keepdims=True)

    def bisect(lo, hi, mid, thr):
        ok = count(a >= thr(mid)) >= kf
        return jnp.where(ok, mid, lo), jnp.where(ok, hi, mid)

    pow2 = lambda e: jnp.exp(e * LN2)
    lo_e = jnp.full((ne, 1, 1), float(MIN_EXP - 1), F32)
    hi_e = jnp.full((ne, 1, 1), 1.0, F32)
    lo_e, hi_e = lax.fori_loop(0, EXP_STEPS, lambda i, c: bisect(c[0], c[1], jnp.floor((c[0] + c[1]) * 0.5), pow2),
                               (lo_e, hi_e))
    lo, hi = lax.fori_loop(0, MANTISSA_STEPS,
                           lambda i, c: bisect(c[0], c[1], c[0] + (c[1] - c[0]) * 0.5, lambda v: v),
                           (pow2(lo_e), pow2(hi_e)))
    kth = jnp.min(jnp.min(jnp.where(a >= lo, a, jnp.inf), axis=1, keepdims=True), axis=2, keepdims=True)
    gt = a > kth
    eq = a == kth
    need = kf - count(gt)

    upper = (lax.broadcasted_iota(I32, (LANES, LANES), 0) <= lax.broadcasted_iota(I32, (LANES, LANES), 1))
    upper = jnp.where(upper, 1.0, 0.0).astype(BF16)
    ones = jnp.ones((LANES, LANES), BF16)
    lower = (lax.broadcasted_iota(I32, (ne, nb, nb), 2) < lax.broadcasted_iota(I32, (ne, nb, nb), 1))
    lower = jnp.where(lower, 1.0, 0.0).astype(BF16)

    def excl_prefix(mask):
        m = jnp.where(mask, 1.0, 0.0)
        mb = m.astype(BF16).reshape(ne * nb, LANES)
        inc = _dot(mb, upper).reshape(ne, nb, LANES)
        tot = _dot(mb, ones).reshape(ne, nb, LANES)
        offs = lax.dot_general(lower, tot.astype(BF16), (((2,), (1,)), ((0,), (0,))), preferred_element_type=F32)
        return inc - m + offs, offs

    eq_rank, _ = excl_prefix(eq)
    sel = gt | (eq & (eq_rank < need))
    rank, offs = excl_prefix(sel)
    pos_ref[0] = jnp.where(sel, rank, -1.0).astype(I32)
    off_ref[0] = offs.astype(I32)


def _route(aff4, cap):
    bsz, ne, nb, _ = aff4.shape
    spec = pl.BlockSpec((1, ne, nb, LANES), lambda b: (b, 0, 0, 0))
    return pl.pallas_call(
        functools.partial(_route_kernel, cap=cap, nb=nb),
        out_shape=(jax.ShapeDtypeStruct(aff4.shape, I32), jax.ShapeDtypeStruct(aff4.shape, I32)),
        grid=(bsz,),
        in_specs=[spec],
        out_specs=(spec, spec),
        compiler_params=pltpu.CompilerParams(dimension_semantics=("arbitrary",), vmem_limit_bytes=VMEM_LIMIT),
        name="route",
    )(aff4)


def _tile_counts(cnt_ref, b, j, ne):
    m = cnt_ref[b, j, 0]
    for e in range(1, ne):
        m = jnp.maximum(m, cnt_ref[b, j, e])
    return m


def _window_select(rel, valid, val, ne):
    c = rel.shape[1]
    w = lax.broadcasted_iota(I32, (ne, WROWS, c), 1)
    relm = jnp.where(valid, rel, -1)
    sel = jnp.where(relm[:, None, :] == w, jnp.broadcast_to(val[:, None, :], (ne, WROWS, c)), 0.0)
    return sel.reshape(ne * WROWS, c)


def _round_slots(basev, cntv, r):
    start = basev + jnp.minimum(r * WIN, cntv)
    num = jnp.clip(cntv - r * WIN, 0, WIN)
    return start, num


def _round_slots_scalar(base, cnt, r):
    return base + jnp.minimum(r * WIN, cnt), jnp.clip(cnt - r * WIN, 0, WIN)


def _align_down(v):
    shift = ALIGN.bit_length() - 1
    return (v >> shift) << shift


def _gather_kernel(base_ref, cnt_ref, pos_ref, aff_ref, basev_ref, cntv_ref, h2_ref, xe_ref,
                   xbuf, carry, zbuf, sem, zsem, nissued, *, cap, ne):
    b = pl.program_id(0)
    j = pl.program_id(1)
    last_step = (b == pl.num_programs(0) - 1) & (j == pl.num_programs(1) - 1)

    def window_copy(slot, e, row0):
        return pltpu.make_async_copy(xbuf.at[slot, pl.ds(e * WROWS, WROWS)],
                                     xe_ref.at[b, e, pl.ds(row0, WROWS)], sem.at[slot, e])

    def wait_round(g):
        @pl.when(g >= 0)
        def _():
            for e in range(ne):
                window_copy(g % 2, e, 0).wait()

    @pl.when((b == 0) & (j == 0))
    def _start():
        nissued[0] = 0
        zbuf[...] = jnp.zeros(zbuf.shape, BF16)

    @pl.when(j == 0)
    def _start_sample():
        carry[...] = jnp.zeros(carry.shape, BF16)
        cps = [pltpu.make_async_copy(zbuf, xe_ref.at[b, e, pl.ds(cap, WROWS)], zsem.at[e]) for e in range(ne)]
        for cp in cps:
            cp.start()
        for cp in cps:
            cp.wait()

    pos = pos_ref[0]
    basev = basev_ref[0, 0]
    cntv = cntv_ref[0, 0]
    h2 = h2_ref[0]
    ones = jnp.ones(pos.shape, F32)
    nrounds = (_tile_counts(cnt_ref, b, j, ne) + (WIN - 1)) // WIN

    def round_body(r, _):
        g = nissued[0]
        slot = g % 2
        start, num = _round_slots(basev, cntv, r)
        valid = (pos >= start) & (pos < start + num)
        rel = pos - _align_down(start)
        onehot = _window_select(rel, valid, ones, ne).astype(BF16)
        d = h2.shape[1]
        for col0 in range(0, d, MXU_N):
            xbuf[slot, :, col0:col0 + MXU_N] = _dot(onehot, h2[:, col0:col0 + MXU_N]).astype(BF16)
        gcol = jnp.sum(_window_select(rel, valid, aff_ref[0], ne), axis=1, keepdims=True)
        gcol = jnp.broadcast_to(gcol, (ne * WROWS, LANES))
        g_hi = gcol.astype(BF16).astype(F32)
        first_half = lax.broadcasted_iota(I32, (ne * WROWS, LANES), 1) < LANES // 2
        xbuf[slot, :, d:] = jnp.where(first_half, g_hi, gcol - g_hi).astype(BF16)
        first = []
        for e in range(ne):
            s, n = _round_slots_scalar(base_ref[b, j, e], cnt_ref[b, j, e], r)
            first.append(pl.multiple_of(_align_down(s), ALIGN))
            nxt = pl.multiple_of(_align_down(s + n) - _align_down(s), ALIGN)
            row0 = e * WROWS
            xbuf[slot, pl.ds(row0, ALIGN), :] += carry[pl.ds(e * ALIGN, ALIGN), :]
            carry[pl.ds(e * ALIGN, ALIGN), :] = xbuf[slot, pl.ds(pl.multiple_of(row0 + nxt, ALIGN), ALIGN), :]
        wait_round(g - 1)
        for e in range(ne):
            window_copy(slot, e, first[e]).start(priority=e % 2)
        nissued[0] = g + 1
        return 0

    lax.fori_loop(0, nrounds, round_body, 0)

    @pl.when(last_step)
    def _drain():
        wait_round(nissued[0] - 1)


def _gather(base, cnt, pos, aff, basev, cntv, h2, cap):
    bsz, t, d = h2.shape
    ne = pos.shape[1]
    c = TILE
    nt = t // c
    width = d + LANES
    grid_spec = pltpu.PrefetchScalarGridSpec(
        num_scalar_prefetch=2,
        grid=(bsz, nt),
        in_specs=[pl.BlockSpec((1, ne, c), lambda b, j, *_: (b, 0, j)),
                  pl.BlockSpec((1, ne, c), lambda b, j, *_: (b, 0, j)),
                  pl.BlockSpec((1, 1, ne, c), lambda b, j, *_: (b, j, 0, 0)),
                  pl.BlockSpec((1, 1, ne, c), lambda b, j, *_: (b, j, 0, 0)),
                  pl.BlockSpec((1, c, d), lambda b, j, *_: (b, j, 0))],
        out_specs=pl.BlockSpec(memory_space=pl.ANY),
        scratch_shapes=[pltpu.VMEM((2, ne * WROWS, width), BF16),
                        pltpu.VMEM((ne * ALIGN, width), BF16), pltpu.VMEM((WROWS, width), BF16),
                        pltpu.SemaphoreType.DMA((2, ne)), pltpu.SemaphoreType.DMA((ne,)),
                        pltpu.SMEM((1,), I32)],
    )
    return pl.pallas_call(
        functools.partial(_gather_kernel, cap=cap, ne=ne),
        out_shape=jax.ShapeDtypeStruct((bsz, ne, cap + WROWS, width), BF16),
        grid_spec=grid_spec,
        compiler_params=pltpu.CompilerParams(dimension_semantics=("arbitrary", "arbitrary"),
                                             vmem_limit_bytes=VMEM_LIMIT),
        name="moe_gather",
    )(base, cnt, pos, aff, basev, cntv, h2)


def _expert_kernel(xe_ref, wg_hbm, wu_hbm, wd_hbm, ye_ref,
                   wgua, wda, wgub, wdb, stg, stu, std, sem, *, d, ne, steps):
    e = pl.program_id(0)
    k = pl.program_id(1) * pl.num_programs(2) + pl.program_id(2)
    rows_in = d // steps
    rows_out = wd_hbm.shape[1] // steps

    def chunk_copies(slot, ee, kk):
        r_in = pl.ds(pl.multiple_of(kk * rows_in, ALIGN), rows_in)
        r_out = pl.ds(pl.multiple_of(kk * rows_out, ALIGN), rows_out)
        return [pltpu.make_async_copy(wg_hbm.at[ee, r_in], stg.at[slot], sem.at[slot, 0]),
                pltpu.make_async_copy(wu_hbm.at[ee, r_in], stu.at[slot], sem.at[slot, 1]),
                pltpu.make_async_copy(wd_hbm.at[ee, r_out], std.at[slot], sem.at[slot, 2])]

    def cast_chunk(slot, kk, dst):
        r_in = pl.ds(pl.multiple_of(kk * rows_in, ALIGN), rows_in)
        r_out = pl.ds(pl.multiple_of(kk * rows_out, ALIGN), rows_out)
        ff = stg.shape[2]
        dst[0][r_in, 0:ff] = stg[slot].astype(BF16)
        dst[0][r_in, ff:2 * ff] = stu[slot].astype(BF16)
        dst[1][r_out, :] = std[slot].astype(BF16)

    @pl.when((e == 0) & (k == 0))
    def _first_expert():
        for kk in range(steps):
            cps = chunk_copies(kk % 2, 0, kk)
            for cp in cps:
                cp.start()
            for cp in cps:
                cp.wait()
            cast_chunk(kk % 2, kk, (wgua, wda))
        if ne > 1:
            for cp in chunk_copies(0, 1, 0):
                cp.start()

    def step(cur, nxt):
        slot = k % 2
        last_chunk = k + 1 == steps

        @pl.when(jnp.where(last_chunk, e + 2 < ne, e + 1 < ne))
        def _start_next_chunk():
            for cp in chunk_copies(1 - slot, jnp.where(last_chunk, e + 2, e + 1), jnp.where(last_chunk, 0, k + 1)):
                cp.start()

        @pl.when(e + 1 < ne)
        def _next_weights():
            for cp in chunk_copies(slot, e + 1, k):
                cp.wait()
            cast_chunk(slot, k, nxt)

        xin = xe_ref[0, 0]
        xb = xin[:, :d]
        gate = xin[:, d:d + 1].astype(F32) + xin[:, d + LANES // 2:d + LANES // 2 + 1].astype(F32)
        ff = cur[1].shape[0]
        au = _dot(xb, cur[0][...])
        y = _dot((_silu(au[:, :ff]) * au[:, ff:]).astype(BF16), cur[1][...])
        ye_ref[0, 0] = (y * gate).astype(BF16)

    @pl.when(e % 2 == 0)
    def _even():
        step((wgua, wda), (wgub, wdb))

    @pl.when(e % 2 == 1)
    def _odd():
        step((wgub, wdb), (wgua, wda))


def _experts(xe, wg, wu, wd, cap):
    bsz, ne, _, width = xe.shape
    d = width - LANES
    ff = wg.shape[2]
    rows = min(EXPERT_ROWS, cap)
    steps = bsz * (cap // rows)
    assert steps % 2 == 0 and d % (steps * ALIGN) == 0 and ff % (steps * ALIGN) == 0
    hbm = pl.BlockSpec(memory_space=pl.ANY)
    return pl.pallas_call(
        functools.partial(_expert_kernel, d=d, ne=ne, steps=steps),
        out_shape=jax.ShapeDtypeStruct((bsz, ne, cap, d), BF16),
        grid=(ne, bsz, cap // rows),
        in_specs=[pl.BlockSpec((1, 1, rows, width), lambda e, b, r: (b, e, r, 0)), hbm, hbm, hbm],
        out_specs=pl.BlockSpec((1, 1, rows, d), lambda e, b, r: (b, e, r, 0)),
        scratch_shapes=[pltpu.VMEM((d, 2 * ff), BF16), pltpu.VMEM((ff, d), BF16),
                        pltpu.VMEM((d, 2 * ff), BF16), pltpu.VMEM((ff, d), BF16),
                        pltpu.VMEM((2, d // steps, ff), F32), pltpu.VMEM((2, d // steps, ff), F32),
                        pltpu.VMEM((2, ff // steps, d), F32), pltpu.SemaphoreType.DMA((2, 3))],
        compiler_params=pltpu.CompilerParams(dimension_semantics=("arbitrary", "arbitrary", "arbitrary"),
                                             vmem_limit_bytes=VMEM_LIMIT),
        name="moe_experts",
    )(xe, wg, wu, wd)


def _combine_kernel(base_ref, cnt_ref, pos_ref, basev_ref, cntv_ref, x1_ref, mod_ref, fnw_ref, ye_ref,
                    out_ref, stage, acc, sem, *, cap, ne):
    b = pl.program_id(0)
    j = pl.program_id(1)
    nt = pl.num_programs(1)
    step = b * nt + j
    pos = pos_ref[0]
    ones = jnp.ones(pos.shape, F32)
    basev = basev_ref[0, 0]
    cntv = cntv_ref[0, 0]
    last = cap - WROWS

    def fetch(slot, bb, jj, r):
        cps = []
        for e in range(ne):
            s, _n = _round_slots_scalar(base_ref[bb, jj, e], cnt_ref[bb, jj, e], r)
            row0 = pl.multiple_of(jnp.minimum(_align_down(s), last), ALIGN)
            cps.append(pltpu.make_async_copy(ye_ref.at[bb, e, pl.ds(row0, WROWS)],
                                             stage.at[slot, pl.ds(e * WROWS, WROWS)], sem.at[slot, e]))
        return cps

    def weights(r):
        start, num = _round_slots(basev, cntv, r)
        valid = (pos >= start) & (pos < start + num)
        return _window_select(pos - jnp.minimum(_align_down(start), last), valid, ones, ne).astype(BF16)

    def expand(w, slot):
        return _dg(w, stage[slot], _TN)

    @pl.when(step == 0)
    def _first():
        for cp in fetch(0, b, j, 0):
            cp.start()

    @pl.when(step + 1 < pl.num_programs(0) * nt)
    def _prefetch():
        wrap = j + 1 == nt
        nxt = fetch((step + 1) % 2, jnp.where(wrap, b + 1, b), jnp.where(wrap, 0, j + 1), 0)
        for e, cp in enumerate(nxt):
            cp.start(priority=e % 2)

    w0 = weights(0)
    slot = step % 2
    for cp in fetch(slot, b, j, 0):
        cp.wait()
    acc[...] = expand(w0, slot)

    def round_body(r, _):
        cps = fetch(2, b, j, r)
        for cp in cps:
            cp.start()
        w = weights(r)
        for cp in cps:
            cp.wait()
        acc[...] += expand(w, 2)
        return 0

    nrounds = (_tile_counts(cnt_ref, b, j, ne) + (WIN - 1)) // WIN
    lax.fori_loop(1, nrounds, round_body, 0)
    mod = mod_ref[0]
    x2 = x1_ref[0] + mod[5:6] * acc[...]
    out_ref[0] = _rms(x2, fnw_ref[...])


def _combine(base, cnt, pos, basev, cntv, x1, modb, fnw, ye, cap):
    bsz, t, d = x1.shape
    ne = pos.shape[1]
    c = TILE
    nt = t // c
    grid_spec = pltpu.PrefetchScalarGridSpec(
        num_scalar_prefetch=2,
        grid=(bsz, nt),
        in_specs=[pl.BlockSpec((1, ne, c), lambda b, j, *_: (b, 0, j)),
                  pl.BlockSpec((1, 1, ne, c), lambda b, j, *_: (b, j, 0, 0)),
                  pl.BlockSpec((1, 1, ne, c), lambda b, j, *_: (b, j, 0, 0)),
                  pl.BlockSpec((1, c, d), lambda b, j, *_: (b, j, 0)),
                  pl.BlockSpec((1,) + modb.shape[1:], lambda b, j, *_: (b, 0, 0)),
                  pl.BlockSpec(fnw.shape, lambda b, j, *_: (0, 0)),
                  pl.BlockSpec(memory_space=pl.ANY)],
        out_specs=pl.BlockSpec((1, c, d), lambda b, j, *_: (b, j, 0)),
        scratch_shapes=[pltpu.VMEM((3, ne * WROWS, d), BF16), pltpu.VMEM((c, d), F32),
                        pltpu.SemaphoreType.DMA((3, ne))],
    )
    return pl.pallas_call(
        functools.partial(_combine_kernel, cap=cap, ne=ne),
        out_shape=jax.ShapeDtypeStruct((bsz, t, d), F32),
        grid_spec=grid_spec,
        compiler_params=pltpu.CompilerParams(dimension_semantics=("arbitrary", "arbitrary"),
                                             vmem_limit_bytes=VMEM_LIMIT),
        name="moe_combine",
    )(base, cnt, pos, basev, cntv, x1, modb, fnw, ye)


def _rope_tables(t):
    n_freq = RET_DK // 4
    inv = ROPE_BASE ** (-np.arange(n_freq, dtype=np.float64) / n_freq)
    zeros = lambda n: np.zeros((n, n_freq))

    def lanes(row_part, col_part):
        cos = np.concatenate([np.cos(row_part), np.cos(col_part)] * 2, axis=1)
        sin = np.concatenate([-np.sin(row_part), -np.sin(col_part), np.sin(row_part), np.sin(col_part)], axis=1)
        return cos, sin

    col = (np.arange(TILE) % GRID_W)[:, None] * inv
    cos_c, sin_c = lanes(zeros(TILE), col)
    cos_c[:, :n_freq] = 0.0
    cos_c[:, 2 * n_freq:3 * n_freq] = 0.0
    rows_per_tile = TILE // GRID_W
    row = np.arange(t // GRID_W)[:, None] * inv
    cos_r, sin_r = lanes(row, zeros(t // GRID_W))
    cos_r[:, n_freq:2 * n_freq] = 0.0
    cos_r[:, 3 * n_freq:] = 0.0
    row_tab = np.zeros((t // TILE, 2, 8, RET_DK))
    row_tab[:, 0, :rows_per_tile] = cos_r.reshape(t // TILE, rows_per_tile, RET_DK)
    row_tab[:, 1, :rows_per_tile] = sin_r.reshape(t // TILE, rows_per_tile, RET_DK)
    return jnp.asarray(np.stack([cos_c, sin_c]), F32), jnp.asarray(row_tab, F32)


def _mixer_weights(w_in, gate_w, gate_b):
    pts = np.cumsum(IN_WIDTHS)[:-1]
    gq, gk, gv, gz, gg, rq, rk, rv, rg = jnp.split(w_in, [int(p) for p in pts], axis=1)
    zw = 2 * GLA_RANK
    gz = jnp.concatenate([gz, gz, gz, jnp.zeros((gz.shape[0], GZ_PAD - 3 * zw), F32)], axis=1)
    wall = jnp.concatenate([gq, gk, gv, gg, rq, rk, rv, rg, gz], axis=1).astype(BF16)
    gmat = jnp.zeros((zw, 2 * GLA_QK), F32)
    gmat = gmat.at[:GLA_RANK, :GLA_QK].set(gate_w[0]).at[GLA_RANK:, GLA_QK:].set(gate_w[1])
    ghi = gmat.astype(BF16)
    glo = (gmat - ghi.astype(F32)).astype(BF16)
    gpk = jnp.concatenate([ghi, ghi, glo, jnp.zeros((GZ_PAD - 3 * zw, 2 * GLA_QK), BF16)], axis=0)
    return wall, gpk, gate_b.reshape(1, 2 * GLA_QK)


def kernel(x, c, ctx, c_ctx, w_ada, b_ada, norm1_w, w_in, gla_gate_w, gla_gate_b, ret_decay_logit, gla_norm_w,
           ret_norm_w, w_out, norm2_w, w_router, w_exp_gate, w_exp_up, w_exp_down, final_norm_w):
    bsz, t, d = x.shape
    depth = w_ada.shape[0]
    assert depth == 1 and t % TILE == 0 and ctx.shape[1] == TILE
    ne = w_router.shape[2]
    cap = EC_CAPACITY_FACTOR * t // ne
    assert cap >= WROWS and cap % ALIGN == 0 and cap % min(EXPERT_ROWS, cap) == 0
    nt = t // TILE
    nb = t // LANES
    bpt = TILE // LANES

    cs = jnp.concatenate([c, c_ctx[None, :], jnp.zeros((8 - bsz - 1, d), F32)], axis=0)
    mod = _ada(cs, w_ada[0], b_ada[0][None, :])
    mod = jnp.pad(mod.reshape(8, N_ADA, d), ((0, 0), (0, 8 - N_ADA), (0, 0)))
    modb = mod[:bsz]
    modc = mod[bsz:bsz + 1]

    wall, gpk, gb = _mixer_weights(w_in[0], gla_gate_w[0], gla_gate_b[0])
    n1w = norm1_w[0][None, :]
    rlog = jnp.broadcast_to(ret_decay_logit[0][:, :, None], (2, RET_HEADS, TILE)).astype(F32)
    cum_f = jnp.asarray(_chunk_cumsum_matrix(TILE, False), BF16)
    cum_b = jnp.asarray(_chunk_cumsum_matrix(TILE, True), BF16)
    ind = jnp.asarray(_chunk_indicator(TILE), BF16)
    lvl_f = jnp.asarray(_level_index(False))
    lvl_b = jnp.asarray(_level_index(True))
    bdm = jnp.asarray(_head_block_mask(), BF16)
    rope_col, rope_row = _rope_tables(t)

    sgf, sgb, srf, srb = _ctx_states(ctx, modc, n1w, wall, gpk, gb, rlog, cum_f, cum_b, ind, bdm)
    o_f, gqkv, gates, rqkv, lab = _fwd(x, modb, n1w, wall, gpk, gb, rope_col, rope_row, rlog, cum_f, ind, lvl_f, bdm,
                                       sgf, srf)

    wr = w_router[0].T
    wrh = wr.astype(BF16)
    wrl = (wr - wrh.astype(F32)).astype(BF16)
    x1, h2, aff = _bwd(x, o_f, gqkv, gates, rqkv, lab, modb, rlog, cum_b, ind, lvl_b, bdm, sgb, srb,
                       gla_norm_w[0][None, :], ret_norm_w[0][None, :], w_out[0].astype(BF16),
                       norm2_w[0][None, :], wrh, wrl)

    pos4, off4 = _route(aff.reshape(bsz, ne, nb, LANES), cap)
    pos = pos4.reshape(bsz, ne, t)
    boff = off4[:, :, :, 0]
    base = jnp.transpose(boff[:, :, ::bpt], (0, 2, 1))
    nxt = jnp.concatenate([base[:, 1:], jnp.full((bsz, 1, ne), cap, I32)], axis=1)
    cnt = nxt - base
    basev = jnp.broadcast_to(base[:, :, :, None], (bsz, nt, ne, TILE))
    cntv = jnp.broadcast_to(cnt[:, :, :, None], (bsz, nt, ne, TILE))

    xe = _gather(base, cnt, pos, aff, basev, cntv, h2, cap)
    ye = _experts(xe, w_exp_gate[0], w_exp_up[0], w_exp_down[0], cap)
    return _combine(base, cnt, pos, basev, cntv, x1, modb, final_norm_w[None, :], ye, cap)
```
